```python
import jax
import jax.numpy as jnp
from jax import lax
import numpy as np

D_MODEL = 1024
BATCH = 8
SEQ = 4096
DEPTH = 1
DEC_BATCH = 128
DEC_SEQ = 8
PAST_LEN = 16384
PAGE_SIZE = 128

MIX_WIDTH = D_MODEL
ATTN_WIDTH = MIX_WIDTH // 2
CONV_CH = MIX_WIDTH - ATTN_WIDTH
HEAD_DIM = 64
N_HEADS = ATTN_WIDTH // HEAD_DIM
N_KV_HEADS = 2
Q_PER_KV = N_HEADS // N_KV_HEADS
KV_WIDTH = N_KV_HEADS * HEAD_DIM
WINDOW = 128
ROT_DIM = HEAD_DIM // 4
ROPE_THETA = 500000.0
CONV_K = 3
IN_COLS = ATTN_WIDTH + 2 * KV_WIDTH + 3 * CONV_CH
N_EXPERTS = 64
TOP_K = 6
EXPERT_FF = 256
SHARED_FF = 256
N_EXPERT_GROUPS = 8
TOPK_GROUPS = 4
EXPERTS_PER_GROUP = N_EXPERTS // N_EXPERT_GROUPS
ROUTED_SCALE = 2.5
MOE_BLOCK = 128
LN_EPS = 1e-5
RMS_EPS = 1e-6
ALPHA = (2.0 * DEPTH) ** 0.25
BETA = (8.0 * DEPTH) ** -0.25

kernel_name = 'hybrid_swa_sink_shortconv_moe_deepnorm_step'


def _layer_norm(x, g, b):
    x32 = x.astype(jnp.float32)
    mu = jnp.mean(x32, axis=-1, keepdims=True)
    var = jnp.mean(jnp.square(x32 - mu), axis=-1, keepdims=True)
    y = (x32 - mu) * lax.rsqrt(var + LN_EPS)
    return (y * g.astype(jnp.float32) + b.astype(jnp.float32)).astype(x.dtype)


def _rms_norm(x, g):
    x32 = x.astype(jnp.float32)
    y = x32 * lax.rsqrt(jnp.mean(jnp.square(x32), axis=-1, keepdims=True) + RMS_EPS)
    return (y * g.astype(jnp.float32)).astype(x.dtype)


def _partial_rope(x, pos):
    half = ROT_DIM // 2
    inv_freq = ROPE_THETA ** (-jnp.arange(0, ROT_DIM, 2, dtype=jnp.float32) / ROT_DIM)
    ang = pos.astype(jnp.float32)[:, None] * inv_freq[None, :]
    bshape = (ang.shape[0],) + (1,) * (x.ndim - 3) + (half,)
    cos = jnp.cos(ang).reshape(bshape)
    sin = jnp.sin(ang).reshape(bshape)
    x32 = x.astype(jnp.float32)
    x1 = x32[..., :half]
    x2 = x32[..., half:ROT_DIM]
    out = jnp.concatenate([x1 * cos - x2 * sin, x2 * cos + x1 * sin, x32[..., ROT_DIM:]], axis=-1)
    return out.astype(x.dtype)


def _split_projection(x, w_in):
    B, S, _ = x.shape
    p = x @ w_in
    o1 = ATTN_WIDTH
    o2 = o1 + KV_WIDTH
    o3 = o2 + KV_WIDTH
    o4 = o3 + CONV_CH
    o5 = o4 + CONV_CH
    q = p[..., :o1].reshape(B, S, N_KV_HEADS, Q_PER_KV, HEAD_DIM)
    k = p[..., o1:o2].reshape(B, S, N_KV_HEADS, HEAD_DIM)
    v = p[..., o2:o3].reshape(B, S, N_KV_HEADS, HEAD_DIM)
    u = p[..., o3:o4]
    b_gate = p[..., o4:o5]
    c_gate = p[..., o5:]
    return q, k, v, u, b_gate, c_gate


def _sink_attention(q, k, v, mask, sinks):
    s = jnp.einsum('...qhgd,...khd->...hgqk', q, k).astype(jnp.float32) * (HEAD_DIM ** -0.5)
    s = jnp.where(mask, s, -jnp.inf)
    sink = sinks.astype(jnp.float32)[:, :, None, None]
    m = jnp.maximum(jnp.max(s, axis=-1, keepdims=True), sink)
    p = jnp.exp(s - m)
    denom = jnp.sum(p, axis=-1, keepdims=True) + jnp.exp(sink - m)
    p = (p / denom).astype(v.dtype)
    return jnp.einsum('...hgqk,...khd->...qhgd', p, v)


def _short_conv(u, b_gate, c_gate, conv_w, conv_buf):
    gated = c_gate * u
    ext = jnp.concatenate([conv_buf.astype(gated.dtype), gated], axis=1)
    S = u.shape[1]
    y = conv_w[0] * ext[:, 0:S]
    for tap in range(1, CONV_K):
        y = y + conv_w[tap] * ext[:, tap:tap + S]
    return b_gate * y, ext[:, S:]


def _merge(attn_o, conv_o, g_attn, g_conv, w_out):
    cat = jnp.concatenate([_rms_norm(attn_o, g_attn), _rms_norm(conv_o, g_conv)], axis=-1)
    return cat @ w_out


def _mixer_prompt(x, w_in, sinks, conv_w, g_attn, g_conv, w_out):
    B, S, _ = x.shape
    q, k, v, u, b_gate, c_gate = _split_projection(x, w_in)
    pos = jnp.arange(S)
    q = _partial_rope(q, pos)
    k = _partial_rope(k, pos)
    nb = S // WINDOW
    qb = q.reshape(B, nb, WINDOW, N_KV_HEADS, Q_PER_KV, HEAD_DIM)
    kb = k.reshape(B, nb, WINDOW, N_KV_HEADS, HEAD_DIM)
    vb = v.reshape(B, nb, WINDOW, N_KV_HEADS, HEAD_DIM)

    def with_prev(t):
        prev = jnp.concatenate([jnp.zeros_like(t[:, :1]), t[:, :-1]], axis=1)
        return jnp.concatenate([prev, t], axis=2)

    kk = with_prev(kb)
    vv = with_prev(vb)
    blk = jnp.arange(nb)[:, None] * WINDOW
    qpos = blk + jnp.arange(WINDOW)[None, :]
    kpos = blk - WINDOW + jnp.arange(2 * WINDOW)[None, :]
    diff = qpos[:, :, None] - kpos[:, None, :]
    mask = (diff >= 0) & (diff < WINDOW) & (kpos[:, None, :] >= 0)
    o = _sink_attention(qb, kk, vv, mask[:, None, None], sinks.reshape(N_KV_HEADS, Q_PER_KV))
    o = o.reshape(B, S, ATTN_WIDTH)
    conv_o, conv_state = _short_conv(u, b_gate, c_gate, conv_w, jnp.zeros((B, CONV_K - 1, CONV_CH), x.dtype))
    out = _merge(o, conv_o, g_attn, g_conv, w_out)
    win = min(WINDOW, PAST_LEN)
    return out, k[:, S - win:], v[:, S - win:], conv_state


def _mixer_sample(x, k_buf, v_buf, conv_buf, w_in, sinks, conv_w, g_attn, g_conv, w_out):
    B, S, _ = x.shape
    win = k_buf.shape[1]
    q, k, v, u, b_gate, c_gate = _split_projection(x, w_in)
    pos = PAST_LEN + jnp.arange(S)
    q = _partial_rope(q, pos)
    k = _partial_rope(k, pos)
    kk = jnp.concatenate([k_buf.astype(k.dtype), k], axis=1)
    vv = jnp.concatenate([v_buf.astype(v.dtype), v], axis=1)
    kpos = PAST_LEN - win + jnp.arange(win + S)
    diff = pos[:, None] - kpos[None, :]
    mask = (diff >= 0) & (diff < WINDOW)
    o = _sink_attention(q, kk, vv, mask, sinks.reshape(N_KV_HEADS, Q_PER_KV))
    o = o.reshape(B, S, ATTN_WIDTH)
    conv_o, conv_state = _short_conv(u, b_gate, c_gate, conv_w, conv_buf)
    out = _merge(o, conv_o, g_attn, g_conv, w_out)
    return out, kk[:, S:], vv[:, S:], conv_state


def _route(xt, router_w, router_bias):
    T = xt.shape[0]
    scores = jax.nn.sigmoid((xt @ router_w).astype(jnp.float32))
    sel = scores + router_bias.astype(jnp.float32)
    grp = sel.reshape(T, N_EXPERT_GROUPS, EXPERTS_PER_GROUP)
    gscore = jnp.sum(lax.top_k(grp, 2)[0], axis=-1)
    _, gidx = lax.top_k(gscore, TOPK_GROUPS)
    gmask = jnp.sum(jax.nn.one_hot(gidx, N_EXPERT_GROUPS, dtype=jnp.float32), axis=-2) > 0
    emask = jnp.repeat(gmask, EXPERTS_PER_GROUP, axis=-1)
    _, eidx = lax.top_k(jnp.where(emask, sel, -jnp.inf), TOP_K)
    w = jnp.take_along_axis(scores, eidx, axis=-1)
    w = w / jnp.sum(w, axis=-1, keepdims=True) * ROUTED_SCALE
    return eidx, w


def _routed_experts(xt, eidx, gates, w_gate, w_up, w_down):
    T = xt.shape[0]
    A = T * TOP_K
    n_blocks = -(-A // MOE_BLOCK) + N_EXPERTS
    flat_e = eidx.reshape(A)
    flat_tok = jnp.arange(A, dtype=jnp.int32) // TOP_K
    flat_g = gates.reshape(A).astype(xt.dtype)
    order = jnp.argsort(flat_e)
    sorted_e = flat_e[order]
    counts = jnp.zeros((N_EXPERTS,), jnp.int32).at[flat_e].add(1)
    blocks_per_e = (counts + MOE_BLOCK - 1) // MOE_BLOCK
    block_end = jnp.cumsum(blocks_per_e)
    pad_start = (block_end - blocks_per_e) * MOE_BLOCK
    start = jnp.cumsum(counts) - counts
    rank = jnp.arange(A, dtype=jnp.int32) - start[sorted_e]
    dest = pad_start[sorted_e] + rank
    rows = n_blocks * MOE_BLOCK
    row_tok = jnp.full((rows,), T, jnp.int32).at[dest].set(flat_tok[order])
    row_gate = jnp.zeros((rows,), xt.dtype).at[dest].set(flat_g[order])
    block_expert = jnp.minimum(jnp.searchsorted(block_end, jnp.arange(n_blocks), side='right'), N_EXPERTS - 1)
    xpad = jnp.concatenate([xt, jnp.zeros((1, D_MODEL), xt.dtype)], axis=0)

    def step(acc, blk):
        tok, g, e = blk
        xb = xpad[tok]
        hid = jax.nn.silu(xb @ w_gate[e]) * (xb @ w_up[e])
        yb = (hid @ w_down[e]) * g[:, None]
        return acc.at[tok].add(yb.astype(acc.dtype)), None

    acc0 = jnp.zeros((T + 1, D_MODEL), xt.dtype)
    acc, _ = lax.scan(step, acc0, (row_tok.reshape(n_blocks, MOE_BLOCK), row_gate.reshape(n_blocks, MOE_BLOCK), block_expert))
    return acc[:T]


def _moe(h, router_w, router_bias, w_gate, w_up, w_down, ws_gate, ws_up, ws_down):
    shp = h.shape
    xt = h.reshape(-1, D_MODEL)
    eidx, gates = _route(xt, router_w, router_bias)
    routed = _routed_experts(xt, eidx, gates, w_gate, w_up, w_down)
    shared = (jax.nn.silu(xt @ ws_gate) * (xt @ ws_up)) @ ws_down
    return (routed + shared).reshape(shp)


def setup_inputs(seed: int = 0) -> dict:
    key = jax.random.key(seed)
    ks = jax.random.split(key, 24)

    def nrm(k, shape, scale):
        return jax.random.normal(k, shape, jnp.float32) * scale

    win = min(WINDOW, PAST_LEN)
    v_lo = ATTN_WIDTH + KV_WIDTH
    col_scale = jnp.ones((IN_COLS,), jnp.float32).at[v_lo:v_lo + KV_WIDTH].set(BETA)
    return {
        'x_prompt': nrm(ks[0], (BATCH, SEQ, D_MODEL), 1.0),
        'x_sample': nrm(ks[1], (DEC_BATCH, DEC_SEQ, D_MODEL), 1.0),
        'cache_k': nrm(ks[2], (DEPTH, DEC_BATCH, win, N_KV_HEADS, HEAD_DIM), 1.0),
        'cache_v': nrm(ks[3], (DEPTH, DEC_BATCH, win, N_KV_HEADS, HEAD_DIM), BETA),
        'state_conv': nrm(ks[4], (DEPTH, DEC_BATCH, CONV_K - 1, CONV_CH), 1.0),
        'w_in': nrm(ks[5], (DEPTH, D_MODEL, IN_COLS), D_MODEL ** -0.5) * col_scale,
        'attn_sinks': nrm(ks[6], (DEPTH, N_HEADS), 0.5),
        'conv_w': nrm(ks[7], (DEPTH, CONV_K, CONV_CH), CONV_K ** -0.5),
        'g_attn_out': 1.0 + nrm(ks[8], (DEPTH, ATTN_WIDTH), 0.02),
        'g_conv_out': 1.0 + nrm(ks[9], (DEPTH, CONV_CH), 0.02),
        'w_out': nrm(ks[10], (DEPTH, MIX_WIDTH, D_MODEL), MIX_WIDTH ** -0.5) * BETA,
        'ln1_g': 1.0 + nrm(ks[11], (DEPTH, D_MODEL), 0.02),
        'ln1_b': nrm(ks[12], (DEPTH, D_MODEL), 0.02),
        'router_w': nrm(ks[13], (DEPTH, D_MODEL, N_EXPERTS), D_MODEL ** -0.5),
        'router_bias': nrm(ks[14], (DEPTH, N_EXPERTS), 0.01),
        'w_gate': nrm(ks[15], (DEPTH, N_EXPERTS, D_MODEL, EXPERT_FF), D_MODEL ** -0.5),
        'w_up': nrm(ks[16], (DEPTH, N_EXPERTS, D_MODEL, EXPERT_FF), D_MODEL ** -0.5),
        'w_down': nrm(ks[17], (DEPTH, N_EXPERTS, EXPERT_FF, D_MODEL), EXPERT_FF ** -0.5) * BETA,
        'ws_gate': nrm(ks[18], (DEPTH, D_MODEL, SHARED_FF), D_MODEL ** -0.5),
        'ws_up': nrm(ks[19], (DEPTH, D_MODEL, SHARED_FF), D_MODEL ** -0.5),
        'ws_down': nrm(ks[20], (DEPTH, SHARED_FF, D_MODEL), SHARED_FF ** -0.5) * BETA,
        'ln2_g': 1.0 + nrm(ks[21], (DEPTH, D_MODEL), 0.02),
        'ln2_b': nrm(ks[22], (DEPTH, D_MODEL), 0.02),
    }


def reference(x_prompt, x_sample, cache_k, cache_v, state_conv, w_in, attn_sinks, conv_w, g_attn_out, g_conv_out, w_out, ln1_g, ln1_b, router_w, router_bias, w_gate, w_up, w_down, ws_gate, ws_up, ws_down, ln2_g, ln2_b):
    xp = x_prompt
    xs = x_sample
    kp_l, vp_l, cp_l, ks_l, vs_l, cs_l = [], [], [], [], [], []
    for l in range(DEPTH):
        mp, kp, vp, cp = _mixer_prompt(xp, w_in[l], attn_sinks[l], conv_w[l], g_attn_out[l], g_conv_out[l], w_out[l])
        ms, kn, vn, cn = _mixer_sample(xs, cache_k[l], cache_v[l], state_conv[l], w_in[l], attn_sinks[l], conv_w[l], g_attn_out[l], g_conv_out[l], w_out[l])
        kp_l.append(kp); vp_l.append(vp); cp_l.append(cp)
        ks_l.append(kn); vs_l.append(vn); cs_l.append(cn)
        xp = _layer_norm(ALPHA * xp + mp, ln1_g[l], ln1_b[l])
        xs = _layer_norm(ALPHA * xs + ms, ln1_g[l], ln1_b[l])
        xp = _layer_norm(ALPHA * xp + _moe(xp, router_w[l], router_bias[l], w_gate[l], w_up[l], w_down[l], ws_gate[l], ws_up[l], ws_down[l]), ln2_g[l], ln2_b[l])
        xs = _layer_norm(ALPHA * xs + _moe(xs, router_w[l], router_bias[l], w_gate[l], w_up[l], w_down[l], ws_gate[l], ws_up[l], ws_down[l]), ln2_g[l], ln2_b[l])
    y_prompt = xp
    y_sample = xs
    k_prompt = jnp.stack(kp_l, axis=0)
    v_prompt = jnp.stack(vp_l, axis=0)
    conv_prompt = jnp.stack(cp_l, axis=0)
    k_sample = jnp.stack(ks_l, axis=0)
    v_sample = jnp.stack(vs_l, axis=0)
    conv_sample = jnp.stack(cs_l, axis=0)
    return (y_prompt, y_sample, k_prompt, v_prompt, conv_prompt, k_sample, v_sample, conv_sample)
```

```python
import functools

import jax
import jax.numpy as jnp
from jax import lax
from jax.experimental import pallas as pl
from jax.experimental.pallas import tpu as pltpu

PAST_LEN = 16384
WINDOW = 128
HEAD_DIM = 64
N_KV_HEADS = 2
ROT_DIM = HEAD_DIM // 4
ROPE_THETA = 500000.0
CONV_K = 3
TOP_K = 6
N_EXPERT_GROUPS = 8
TOPK_GROUPS = 4
ROUTED_SCALE = 2.5
LN_EPS = 1e-5
RMS_EPS = 1e-6

LANES = 128
SUBLANES = 8
VMEM_LIMIT_BYTES = 56 * 1024 * 1024

PROMPT_TILE = 512
SAMPLE_SEQS = 16
ROUTE_TILE = 512
COMBINE_TILE = 256
EXPERT_BLOCK = 256

F32 = jnp.float32
BF16 = jnp.bfloat16
NEG_INF = float("-inf")


def _rope_tables(positions):
    half = ROT_DIM // 2
    inv_freq = ROPE_THETA ** (-jnp.arange(0, ROT_DIM, 2, dtype=F32) / ROT_DIM)
    ang = positions.astype(F32)[:, None] * inv_freq[None, :]
    cos, sin = jnp.cos(ang), jnp.sin(ang)
    n = positions.shape[0]
    rest = HEAD_DIM - ROT_DIM
    c = jnp.concatenate([cos, cos, jnp.ones((n, rest), F32)], axis=-1)
    sa = jnp.concatenate([-sin, jnp.zeros((n, half + rest), F32)], axis=-1)
    sb = jnp.concatenate([jnp.zeros((n, half), F32), sin, jnp.zeros((n, rest), F32)], axis=-1)
    reps = LANES // HEAD_DIM
    return jnp.tile(c, (1, reps)), jnp.tile(sa, (1, reps)), jnp.tile(sb, (1, reps))


def _rope(x, c, sa, sb):
    half = ROT_DIM // 2
    return x * c + pltpu.roll(x, LANES - half, 1) * sa + pltpu.roll(x, half, 1) * sb


def _rms_norm(x, g):
    return x * lax.rsqrt(jnp.mean(jnp.square(x), axis=-1, keepdims=True) + RMS_EPS) * g


def _layer_norm(x, g, b):
    mu = jnp.mean(x, axis=-1, keepdims=True)
    var = jnp.mean(jnp.square(x - mu), axis=-1, keepdims=True)
    return (x - mu) * lax.rsqrt(var + LN_EPS) * g + b


def _short_conv(gated, prev2, prev1, row, conv_w, b_gate):
    g1 = pltpu.roll(gated, 1, 0)
    g2 = pltpu.roll(gated, 2, 0)
    g1 = jnp.where(row == 0, prev1, g1)
    g2 = jnp.where(row == 0, prev2, jnp.where(row == 1, prev1, g2))
    y = conv_w[0:1, :] * g2 + conv_w[1:2, :] * g1 + conv_w[2:3, :] * gated
    return b_gate * y


def _merge_norm(x, attn_o, conv_o, gattn, gconv, wout_ref, ln_g, ln_b, alpha):
    cat = jnp.concatenate([_rms_norm(attn_o, gattn), _rms_norm(conv_o, gconv)], axis=-1)
    mix = jnp.dot(cat.astype(BF16), wout_ref[...], preferred_element_type=F32)
    return _layer_norm(alpha * x + mix, ln_g, ln_b)


def _sink_softmax(parts, sink):
    m = sink
    for s in parts:
        m = jnp.maximum(m, jnp.max(s, axis=-1, keepdims=True))
    es = [jnp.exp(s - m) for s in parts]
    den = jnp.exp(sink - m)
    for e in es:
        den = den + jnp.sum(e, axis=-1, keepdims=True)
    return [(e / den).astype(BF16) for e in es]


def _prompt_mixer_kernel(x_ref, win_ref, c_ref, sa_ref, sb_ref, sinks_ref, convw_ref, gattn_ref, gconv_ref,
                         wout_ref, lng_ref, lnb_ref,
                         h_ref, ko_ref, vo_ref, co_ref,
                         q_s, k_s, v_s, o_s, gc_s, *, alpha, n_heads, attn_w, kv_w, conv_ch):
    s = pl.program_id(1)
    last = pl.num_programs(1) - 1
    tq = x_ref.shape[0]
    q_per_kv = n_heads // N_KV_HEADS

    @pl.when(s == 0)
    def _():
        k_s[0:WINDOW, :] = jnp.zeros((WINDOW, kv_w), BF16)
        v_s[0:WINDOW, :] = jnp.zeros((WINDOW, kv_w), BF16)
        gc_s[...] = jnp.zeros(gc_s.shape, F32)

    @pl.when(s > 0)
    def _():
        k_s[0:WINDOW, :] = k_s[tq:tq + WINDOW, :]
        v_s[0:WINDOW, :] = v_s[tq:tq + WINDOW, :]

    x = x_ref[...]
    xb = x.astype(BF16)
    c, sa, sb = c_ref[...], sa_ref[...], sb_ref[...]

    def proj(lo, width):
        return jnp.dot(xb, win_ref[:, lo:lo + width], preferred_element_type=F32)

    scale = HEAD_DIM ** -0.5
    for j in range(attn_w // LANES):
        qj = _rope(proj(j * LANES, LANES), c, sa, sb)
        q_s[:, j * LANES:(j + 1) * LANES] = (qj * scale).astype(BF16)
    k = _rope(proj(attn_w, kv_w), c, sa, sb)
    v = proj(attn_w + kv_w, kv_w)
    k_s[WINDOW:WINDOW + tq, :] = k.astype(BF16)
    v_s[WINDOW:WINDOW + tq, :] = v.astype(BF16)

    @pl.when(s == last)
    def _():
        ko_ref[...] = k[tq - WINDOW:, :]
        vo_ref[...] = v[tq - WINDOW:, :]

    qi = lax.broadcasted_iota(jnp.int32, (WINDOW, 2 * WINDOW), 0)
    ci = lax.broadcasted_iota(jnp.int32, (WINDOW, 2 * WINDOW), 1)
    band = (ci > qi) & (ci <= qi + WINDOW)

    def sub_block(j, carry):
        r0 = pl.multiple_of(j * WINDOW, WINDOW)
        has_prev = (s * tq + r0) > 0
        mask = band & ((ci >= WINDOW) | has_prev)
        kk = k_s[pl.ds(r0, 2 * WINDOW), :]
        vv = v_s[pl.ds(r0, 2 * WINDOW), :]
        for hd in range(n_heads):
            kvh = hd // q_per_kv
            qh = q_s[pl.ds(r0, WINDOW), hd * HEAD_DIM:(hd + 1) * HEAD_DIM]
            kh = kk[:, kvh * HEAD_DIM:(kvh + 1) * HEAD_DIM]
            vh = vv[:, kvh * HEAD_DIM:(kvh + 1) * HEAD_DIM]
            sc = lax.dot_general(qh, kh, (((1,), (1,)), ((), ())), preferred_element_type=F32)
            sc = jnp.where(mask, sc, NEG_INF)
            (p,) = _sink_softmax([sc], sinks_ref[hd])
            o_s[pl.ds(r0, WINDOW), hd * HEAD_DIM:(hd + 1) * HEAD_DIM] = jnp.dot(
                p, vh, preferred_element_type=F32)
        return carry

    lax.fori_loop(0, tq // WINDOW, sub_block, 0)

    o3 = attn_w + 2 * kv_w
    gated = proj(o3 + 2 * conv_ch, conv_ch) * proj(o3, conv_ch)
    row = lax.broadcasted_iota(jnp.int32, (tq, 1), 0)
    conv_o = _short_conv(gated, gc_s[0:1, :], gc_s[1:2, :], row, convw_ref[...], proj(o3 + conv_ch, conv_ch))
    gc_s[0:CONV_K - 1, :] = gated[tq - (CONV_K - 1):, :]

    @pl.when(s == last)
    def _():
        co_ref[...] = gated[tq - (CONV_K - 1):, :]

    h_ref[...] = _merge_norm(x, o_s[...], conv_o, gattn_ref[...], gconv_ref[...], wout_ref,
                             lng_ref[...], lnb_ref[...], alpha)


def _prompt_mixer(x, win_b, tabs, sinks, conv_w, g_attn, g_conv, wout_b, ln_g, ln_b, *, alpha):
    bsz, seq, d = x.shape
    tq = PROMPT_TILE
    ns = seq // tq
    attn_w = g_attn.shape[-1]
    conv_ch = g_conv.shape[-1]
    n_heads = attn_w // HEAD_DIM
    kv_w = N_KV_HEADS * HEAD_DIM
    in_cols = win_b.shape[-1]
    const2 = lambda b, s: (0, 0)
    kern = functools.partial(_prompt_mixer_kernel, alpha=alpha, n_heads=n_heads, attn_w=attn_w, kv_w=kv_w,
                             conv_ch=conv_ch)
    return pl.pallas_call(
        kern,
        grid=(bsz, ns),
        in_specs=[
            pl.BlockSpec((None, tq, d), lambda b, s: (b, s, 0)),
            pl.BlockSpec((d, in_cols), const2),
            pl.BlockSpec((tq, LANES), lambda b, s: (s, 0)),
            pl.BlockSpec((tq, LANES), lambda b, s: (s, 0)),
            pl.BlockSpec((tq, LANES), lambda b, s: (s, 0)),
            pl.BlockSpec(memory_space=pltpu.SMEM),
            pl.BlockSpec((CONV_K, conv_ch), const2),
            pl.BlockSpec((1, attn_w), const2),
            pl.BlockSpec((1, conv_ch), const2),
            pl.BlockSpec((attn_w + conv_ch, d), const2),
            pl.BlockSpec((1, d), const2),
            pl.BlockSpec((1, d), const2),
        ],
        out_specs=[
            pl.BlockSpec((tq, d), lambda b, s: (b * ns + s, 0)),
            pl.BlockSpec((None, WINDOW, kv_w), lambda b, s: (b, 0, 0)),
            pl.BlockSpec((None, WINDOW, kv_w), lambda b, s: (b, 0, 0)),
            pl.BlockSpec((None, CONV_K - 1, conv_ch), lambda b, s: (b, 0, 0)),
        ],
        out_shape=[
            jax.ShapeDtypeStruct((bsz * seq, d), F32),
            jax.ShapeDtypeStruct((bsz, WINDOW, kv_w), F32),
            jax.ShapeDtypeStruct((bsz, WINDOW, kv_w), F32),
            jax.ShapeDtypeStruct((bsz, CONV_K - 1, conv_ch), F32),
        ],
        scratch_shapes=[
            pltpu.VMEM((tq, attn_w), BF16),
            pltpu.VMEM((tq + WINDOW, kv_w), BF16),
            pltpu.VMEM((tq + WINDOW, kv_w), BF16),
            pltpu.VMEM((tq, attn_w), F32),
            pltpu.VMEM((SUBLANES, conv_ch), F32),
        ],
        compiler_params=pltpu.CompilerParams(dimension_semantics=("arbitrary", "arbitrary"),
                                             vmem_limit_bytes=VMEM_LIMIT_BYTES),
        name="prompt_mixer",
    )(x, win_b, *tabs, sinks, conv_w, g_attn, g_conv, wout_b, ln_g, ln_b)


def _sample_mixer_kernel(x_ref, ck_ref, cv_ref, st_ref, win_ref, c_ref, sa_ref, sb_ref, sinks_ref,
                         convw_ref, gattn_ref, gconv_ref, wout_ref, lng_ref, lnb_ref,
                         h_ref, ko_ref, vo_ref, co_ref, *, alpha, n_heads, attn_w, kv_w, conv_ch, dec_seq):
    nb, win = ck_ref.shape[0], ck_ref.shape[1]
    rows = nb * dec_seq
    q_per_kv = n_heads // N_KV_HEADS
    x = x_ref[...]
    xb = x.astype(BF16)
    c, sa, sb = c_ref[...], sa_ref[...], sb_ref[...]

    def proj(lo, width):
        return jnp.dot(xb, win_ref[:, lo:lo + width], preferred_element_type=F32)

    scale = HEAD_DIM ** -0.5
    k = _rope(proj(attn_w, kv_w), c, sa, sb)
    v = proj(attn_w + kv_w, kv_w)
    k3 = k.reshape(nb, dec_seq, kv_w)
    v3 = v.reshape(nb, dec_seq, kv_w)
    ck = ck_ref[...]
    cv = cv_ref[...]
    ko_ref[:, 0:win - dec_seq, :] = ck[:, dec_seq:, :]
    ko_ref[:, win - dec_seq:, :] = k3
    vo_ref[:, 0:win - dec_seq, :] = cv[:, dec_seq:, :]
    vo_ref[:, win - dec_seq:, :] = v3
    ckb, cvb, k3b, v3b = ck.astype(BF16), cv.astype(BF16), k3.astype(BF16), v3.astype(BF16)

    qrows = q_per_kv * dec_seq
    qi = lax.broadcasted_iota(jnp.int32, (nb, qrows, win), 1) % dec_seq
    mask_c = lax.broadcasted_iota(jnp.int32, (nb, qrows, win), 2) > qi + (win - WINDOW)
    qn = lax.broadcasted_iota(jnp.int32, (nb, qrows, dec_seq), 1) % dec_seq
    mask_n = lax.broadcasted_iota(jnp.int32, (nb, qrows, dec_seq), 2) <= qn
    sink_row = lax.broadcasted_iota(jnp.int32, (nb, qrows, 1), 1) // dec_seq

    q_chunks = [_rope(proj(j * LANES, LANES), c, sa, sb) * scale for j in range(attn_w // LANES)]
    heads_out = []
    for kvh in range(N_KV_HEADS):
        qs = []
        for g in range(q_per_kv):
            lo = (kvh * q_per_kv + g) * HEAD_DIM
            qh = q_chunks[lo // LANES][:, lo % LANES:lo % LANES + HEAD_DIM]
            qs.append(qh.reshape(nb, dec_seq, HEAD_DIM))
        qg = jnp.concatenate(qs, axis=1).astype(BF16)
        sl = slice(kvh * HEAD_DIM, (kvh + 1) * HEAD_DIM)
        sc_c = jnp.einsum("bqd,bkd->bqk", qg, ckb[:, :, sl], preferred_element_type=F32)
        sc_n = jnp.einsum("bqd,bkd->bqk", qg, k3b[:, :, sl], preferred_element_type=F32)
        sc_c = jnp.where(mask_c, sc_c, NEG_INF)
        sc_n = jnp.where(mask_n, sc_n, NEG_INF)
        sink = jnp.zeros((nb, qrows, 1), F32)
        for g in range(q_per_kv):
            sink = jnp.where(sink_row == g, sinks_ref[kvh * q_per_kv + g], sink)
        p_c, p_n = _sink_softmax([sc_c, sc_n], sink)
        og = (jnp.einsum("bqk,bkd->bqd", p_c, cvb[:, :, sl], preferred_element_type=F32)
              + jnp.einsum("bqk,bkd->bqd", p_n, v3b[:, :, sl], preferred_element_type=F32))
        for g in range(q_per_kv):
            heads_out.append(og[:, g * dec_seq:(g + 1) * dec_seq, :].reshape(rows, HEAD_DIM))
    attn_o = jnp.concatenate(heads_out, axis=-1)

    o3 = attn_w + 2 * kv_w
    gated = proj(o3 + 2 * conv_ch, conv_ch) * proj(o3, conv_ch)
    st = st_ref[...]
    prev2 = jnp.broadcast_to(st[:, 0:1, :], (nb, dec_seq, conv_ch)).reshape(rows, conv_ch)
    prev1 = jnp.broadcast_to(st[:, 1:2, :], (nb, dec_seq, conv_ch)).reshape(rows, conv_ch)
    row = lax.broadcasted_iota(jnp.int32, (rows, 1), 0) % dec_seq
    conv_o = _short_conv(gated, prev2, prev1, row, convw_ref[...], proj(o3 + conv_ch, conv_ch))
    co_ref[...] = gated.reshape(nb, dec_seq, conv_ch)[:, dec_seq - (CONV_K - 1):, :]

    h_ref[...] = _merge_norm(x, attn_o, conv_o, gattn_ref[...], gconv_ref[...], wout_ref,
                             lng_ref[...], lnb_ref[...], alpha)


def _sample_mixer(x, ck, cv, st, win_b, tabs, sinks, conv_w, g_attn, g_conv, wout_b, ln_g, ln_b, *, alpha):
    dec_b, dec_seq, d = x.shape
    assert dec_seq >= CONV_K - 1 and dec_seq % SUBLANES == 0
    nb = SAMPLE_SEQS
    rows = nb * dec_seq
    win = ck.shape[1]
    attn_w = g_attn.shape[-1]
    conv_ch = g_conv.shape[-1]
    n_heads = attn_w // HEAD_DIM
    kv_w = N_KV_HEADS * HEAD_DIM
    in_cols = win_b.shape[-1]
    const2 = lambda i: (0, 0)
    kern = functools.partial(_sample_mixer_kernel, alpha=alpha, n_heads=n_heads, attn_w=attn_w, kv_w=kv_w,
                             conv_ch=conv_ch, dec_seq=dec_seq)
    return pl.pallas_call(
        kern,
        grid=(dec_b // nb,),
        in_specs=[
            pl.BlockSpec((rows, d), lambda i: (i, 0)),
            pl.BlockSpec((nb, win, kv_w), lambda i: (i, 0, 0)),
            pl.BlockSpec((nb, win, kv_w), lambda i: (i, 0, 0)),
            pl.BlockSpec((nb, CONV_K - 1, conv_ch), lambda i: (i, 0, 0)),
            pl.BlockSpec((d, in_cols), const2),
            pl.BlockSpec((rows, LANES), const2),
            pl.BlockSpec((rows, LANES), const2),
            pl.BlockSpec((rows, LANES), const2),
            pl.BlockSpec(memory_space=pltpu.SMEM),
            pl.BlockSpec((CONV_K, conv_ch), const2),
            pl.BlockSpec((1, attn_w), const2),
            pl.BlockSpec((1, conv_ch), const2),
            pl.BlockSpec((attn_w + conv_ch, d), const2),
            pl.BlockSpec((1, d), const2),
            pl.BlockSpec((1, d), const2),
        ],
        out_specs=[
            pl.BlockSpec((rows, d), lambda i: (i, 0)),
            pl.BlockSpec((nb, win, kv_w), lambda i: (i, 0, 0)),
            pl.BlockSpec((nb, win, kv_w), lambda i: (i, 0, 0)),
            pl.BlockSpec((nb, CONV_K - 1, conv_ch), lambda i: (i, 0, 0)),
        ],
        out_shape=[
            jax.ShapeDtypeStruct((dec_b * dec_seq, d), F32),
            jax.ShapeDtypeStruct((dec_b, win, kv_w), F32),
            jax.ShapeDtypeStruct((dec_b, win, kv_w), F32),
            jax.ShapeDtypeStruct((dec_b, CONV_K - 1, conv_ch), F32),
        ],
        compiler_params=pltpu.CompilerParams(dimension_semantics=("arbitrary",),
                                             vmem_limit_bytes=VMEM_LIMIT_BYTES),
        name="sample_mixer",
    )(x.reshape(dec_b * dec_seq, d), ck, cv, st, win_b, *tabs, sinks, conv_w, g_attn, g_conv, wout_b,
      ln_g, ln_b)


def _over_experts(fn, x):
    return fn(fn(x, axis=0, keepdims=True), axis=1, keepdims=True)


def _two_group_specs(tm, d, n_prompt_tiles):
    return [pl.BlockSpec((tm, d), lambda i, *_: (jnp.minimum(i, n_prompt_tiles - 1), 0)),
            pl.BlockSpec((tm, d), lambda i, *_: (jnp.maximum(i - n_prompt_tiles, 0), 0))]


def _route_kernel(hp_ref, hs_ref, rwt_ref, bias_ref, eidx_ref, rank_ref, gate_ref, cnt_ref, cnt_s, *, n_experts,
                  n_prompt_tiles):
    i = pl.program_id(0)
    tm = hp_ref.shape[0]
    per_group = n_experts // N_EXPERT_GROUPS
    shape3 = (N_EXPERT_GROUPS, per_group, tm)

    @pl.when(i == 0)
    def _():
        cnt_s[...] = jnp.zeros(cnt_s.shape, F32)

    h = jnp.where(i < n_prompt_tiles, hp_ref[...], hs_ref[...])
    logits = lax.dot_general(rwt_ref[...], h.astype(BF16), (((1,), (1,)), ((), ())),
                             preferred_element_type=F32)
    scores = jax.nn.sigmoid(logits)
    sel = scores + bias_ref[...]
    scores3 = scores.reshape(shape3)
    grp = sel.reshape(shape3)
    member = lax.broadcasted_iota(jnp.int32, shape3, 1).astype(F32)
    group = lax.broadcasted_iota(jnp.int32, shape3, 0).astype(F32)
    expert = group * per_group + member

    m1 = jnp.max(grp, axis=1, keepdims=True)
    f1 = jnp.min(jnp.where(grp == m1, member, float(per_group)), axis=1, keepdims=True)
    m2 = jnp.max(jnp.where(member == f1, NEG_INF, grp), axis=1, keepdims=True)
    gscore = m1 + m2

    gid = lax.broadcasted_iota(jnp.int32, gscore.shape, 0).astype(F32)
    gmask = jnp.zeros(gscore.shape, F32)
    cur = gscore
    for _ in range(TOPK_GROUPS):
        mx = jnp.max(cur, axis=0, keepdims=True)
        pick = gid == jnp.min(jnp.where(cur == mx, gid, float(N_EXPERT_GROUPS)), axis=0, keepdims=True)
        gmask = jnp.where(pick, 1.0, gmask)
        cur = jnp.where(pick, NEG_INF, cur)

    cand = jnp.where(gmask > 0.0, grp, NEG_INF)
    chosen = jnp.zeros(shape3, F32)
    picks, firsts, weights = [], [], []
    for _ in range(TOP_K):
        mx = _over_experts(jnp.max, cand)
        first = _over_experts(jnp.min, jnp.where(cand == mx, expert, float(n_experts)))
        pick = expert == first
        picks.append(pick)
        firsts.append(first)
        weights.append(_over_experts(jnp.sum, jnp.where(pick, scores3, 0.0)))
        chosen = jnp.where(pick, 1.0, chosen)
        cand = jnp.where(pick, NEG_INF, cand)
    wsum = weights[0]
    for w in weights[1:]:
        wsum = wsum + w

    chosen2 = chosen.reshape(n_experts, tm)
    earlier = (lax.broadcasted_iota(jnp.int32, (tm, tm), 0)
               < lax.broadcasted_iota(jnp.int32, (tm, tm), 1)).astype(BF16)
    before = jnp.dot(chosen2.astype(BF16), earlier, preferred_element_type=F32) + cnt_s[:, 0:1]
    before3 = before.reshape(shape3)

    pad = SUBLANES - TOP_K
    eidx = [f.reshape(1, tm).astype(jnp.int32) for f in firsts]
    rank = [_over_experts(jnp.sum, jnp.where(p, before3, 0.0)).reshape(1, tm).astype(jnp.int32) for p in picks]
    gate = [(w / wsum * ROUTED_SCALE).reshape(1, tm) for w in weights]
    eidx_ref[...] = jnp.concatenate(eidx + [jnp.zeros((pad, tm), jnp.int32)], axis=0)
    rank_ref[...] = jnp.concatenate(rank + [jnp.zeros((pad, tm), jnp.int32)], axis=0)
    gate_ref[...] = jnp.concatenate(gate + [jnp.zeros((pad, tm), F32)], axis=0)

    cnt_s[...] = cnt_s[...] + jnp.sum(chosen2, axis=1, keepdims=True)
    cnt_ref[...] = cnt_s[...]


def _route(h_p, h_s, rwt_b, bias_col):
    d = h_p.shape[1]
    t_all = h_p.shape[0] + h_s.shape[0]
    n_experts = rwt_b.shape[0]
    tm = ROUTE_TILE
    nt = t_all // tm
    npt = h_p.shape[0] // tm
    row_spec = pl.BlockSpec((SUBLANES, tm), lambda i: (0, i))
    return pl.pallas_call(
        functools.partial(_route_kernel, n_experts=n_experts, n_prompt_tiles=npt),
        grid=(nt,),
        in_specs=_two_group_specs(tm, d, npt) + [
            pl.BlockSpec((n_experts, d), lambda i: (0, 0)),
            pl.BlockSpec((n_experts, 1), lambda i: (0, 0)),
        ],
        out_specs=[row_spec, row_spec, row_spec, pl.BlockSpec((n_experts, LANES), lambda i: (0, 0))],
        out_shape=[
            jax.ShapeDtypeStruct((SUBLANES, t_all), jnp.int32),
            jax.ShapeDtypeStruct((SUBLANES, t_all), jnp.int32),
            jax.ShapeDtypeStruct((SUBLANES, t_all), F32),
            jax.ShapeDtypeStruct((n_experts, LANES), F32),
        ],
        scratch_shapes=[pltpu.VMEM((n_experts, LANES), F32)],
        compiler_params=pltpu.CompilerParams(dimension_semantics=("arbitrary",),
                                             vmem_limit_bytes=VMEM_LIMIT_BYTES),
        name="route",
    )(h_p, h_s, rwt_b, bias_col)


def _dispatch_kernel(zrow_ref, pos_hbm, hp_ref, hs_ref, xs_hbm, pos_s, zero_s, pos_sem, row_sem, *, n_experts, blk,
                     n_prompt_tiles):
    i = pl.program_id(0)
    tm = hp_ref.shape[0]
    n = TOP_K * tm

    @pl.when(i == 0)
    def _():
        zero_s[...] = jnp.zeros(zero_s.shape, F32)

        def zcopy(e):
            return pltpu.make_async_copy(zero_s, xs_hbm.at[pl.ds(pl.multiple_of(zrow_ref[e], blk), blk)], row_sem)

        def start(e, c):
            zcopy(e).start()
            return c

        def wait(e, c):
            zcopy(e).wait()
            return c

        lax.fori_loop(0, n_experts, start, 0)
        lax.fori_loop(0, n_experts, wait, 0)

    pos_cp = pltpu.make_async_copy(pos_hbm.at[pl.ds(i * n, n)], pos_s, pos_sem)
    pos_cp.start()
    pos_cp.wait()

    def scatter_rows(h_ref):
        def row_copy(a):
            return pltpu.make_async_copy(h_ref.at[pl.ds(a % tm, 1)], xs_hbm.at[pl.ds(pos_s[a], 1)], row_sem)

        def start(a, c):
            row_copy(a).start()
            return c

        def wait(a, c):
            row_copy(a).wait()
            return c

        lax.fori_loop(0, n, start, 0)
        lax.fori_loop(0, n, wait, 0)

    @pl.when(i < n_prompt_tiles)
    def _():
        scatter_rows(hp_ref)

    @pl.when(i >= n_prompt_tiles)
    def _():
        scatter_rows(hs_ref)


def _dispatch(zrow, pos_flat, h_p, h_s, *, n_rows, blk):
    d = h_p.shape[1]
    t_all = h_p.shape[0] + h_s.shape[0]
    tm = ROUTE_TILE
    npt = h_p.shape[0] // tm
    n_experts = zrow.shape[0]
    return pl.pallas_call(
        functools.partial(_dispatch_kernel, n_experts=n_experts, blk=blk, n_prompt_tiles=npt),
        grid_spec=pltpu.PrefetchScalarGridSpec(
            num_scalar_prefetch=1,
            grid=(t_all // tm,),
            in_specs=[pl.BlockSpec(memory_space=pl.ANY)] + _two_group_specs(tm, d, npt),
            out_specs=pl.BlockSpec(memory_space=pl.ANY),
            scratch_shapes=[
                pltpu.SMEM((TOP_K * tm,), jnp.int32),
                pltpu.VMEM((blk, d), F32),
                pltpu.SemaphoreType.DMA,
                pltpu.SemaphoreType.DMA,
            ],
        ),
        out_shape=jax.ShapeDtypeStruct((n_rows, d), F32),
        compiler_params=pltpu.CompilerParams(dimension_semantics=("arbitrary",),
                                             vmem_limit_bytes=VMEM_LIMIT_BYTES),
        name="dispatch",
    )(zrow, pos_flat, h_p, h_s)


def _silu(x):
    return x * jax.nn.sigmoid(x)


def _expert_kernel(be_ref, nact_ref, xs_ref, wg_ref, wu_ref, wd_ref, ys_ref, wg_s, wu_s, wd_s):
    b = pl.program_id(0)

    @pl.when(b < nact_ref[0])
    def _():
        @pl.when((b == 0) | (be_ref[b] != be_ref[jnp.maximum(b - 1, 0)]))
        def _():
            wg_s[...] = wg_ref[...].astype(BF16)
            wu_s[...] = wu_ref[...].astype(BF16)
            wd_s[...] = wd_ref[...].astype(BF16)

        xb = xs_ref[...].astype(BF16)
        hid = _silu(jnp.dot(xb, wg_s[...], preferred_element_type=F32)) * jnp.dot(
            xb, wu_s[...], preferred_element_type=F32)
        ys_ref[...] = jnp.dot(hid.astype(BF16), wd_s[...], preferred_element_type=F32)


def _experts(block_expert, nact, xs, w_gate, w_up, w_down, *, blk):
    n_rows, d = xs.shape
    ff = w_gate.shape[-1]
    n_blocks = n_rows // blk

    def active(b, be, na):
        return jnp.minimum(b, na[0] - 1)

    return pl.pallas_call(
        _expert_kernel,
        grid_spec=pltpu.PrefetchScalarGridSpec(
            num_scalar_prefetch=2,
            grid=(n_blocks,),
            in_specs=[
                pl.BlockSpec((blk, d), lambda b, be, na: (active(b, be, na), 0)),
                pl.BlockSpec((None, d, ff), lambda b, be, na: (be[active(b, be, na)], 0, 0)),
                pl.BlockSpec((None, d, ff), lambda b, be, na: (be[active(b, be, na)], 0, 0)),
                pl.BlockSpec((None, ff, d), lambda b, be, na: (be[active(b, be, na)], 0, 0)),
            ],
            out_specs=pl.BlockSpec((blk, d), lambda b, be, na: (active(b, be, na), 0)),
            scratch_shapes=[
                pltpu.VMEM((d, ff), BF16),
                pltpu.VMEM((d, ff), BF16),
                pltpu.VMEM((ff, d), BF16),
            ],
        ),
        out_shape=jax.ShapeDtypeStruct((n_rows, d), F32),
        compiler_params=pltpu.CompilerParams(dimension_semantics=("arbitrary",),
                                             vmem_limit_bytes=VMEM_LIMIT_BYTES),
        name="experts",
    )(block_expert, nact, xs, w_gate, w_up, w_down)


def _combine_kernel(pos_hbm, ys_hbm, gate_ref, hp_ref, hs_ref, wsg_ref, wsu_ref, wsd_ref, lng_ref, lnb_ref,
                    yp_ref, ysm_ref, pos_s, ybuf, pos_sem, row_sem, *, alpha, n_prompt_tiles):
    i = pl.program_id(0)
    tm = hp_ref.shape[0]
    n = TOP_K * tm

    pos_cp = pltpu.make_async_copy(pos_hbm.at[pl.ds(i * n, n)], pos_s, pos_sem)
    pos_cp.start()
    pos_cp.wait()

    def row_copy(a):
        return pltpu.make_async_copy(ys_hbm.at[pl.ds(pos_s[a], 1)], ybuf.at[pl.ds(a, 1)], row_sem)

    def start(a, c):
        row_copy(a).start()
        return c

    def wait(a, c):
        row_copy(a).wait()
        return c

    lax.fori_loop(0, n, start, 0)

    h = jnp.where(i < n_prompt_tiles, hp_ref[...], hs_ref[...])
    hb = h.astype(BF16)
    hid = _silu(jnp.dot(hb, wsg_ref[...], preferred_element_type=F32)) * jnp.dot(
        hb, wsu_ref[...], preferred_element_type=F32)
    shared = jnp.dot(hid.astype(BF16), wsd_ref[...], preferred_element_type=F32)

    lax.fori_loop(0, n, wait, 0)

    gates = gate_ref[...]
    routed = gates[:, 0:1] * ybuf[0:tm, :]
    for k in range(1, TOP_K):
        routed = routed + gates[:, k:k + 1] * ybuf[k * tm:(k + 1) * tm, :]
    y = _layer_norm(alpha * h + (routed + shared), lng_ref[...], lnb_ref[...])

    @pl.when(i < n_prompt_tiles)
    def _():
        yp_ref[...] = y

    @pl.when(i >= n_prompt_tiles)
    def _():
        ysm_ref[...] = y


def _combine(pos_flat, ys, gates_t, h_p, h_s, wsg_b, wsu_b, wsd_b, ln_g, ln_b, *, alpha):
    t_prompt, d = h_p.shape
    t_all = t_prompt + h_s.shape[0]
    tm = COMBINE_TILE
    ff = wsg_b.shape[-1]
    npt = t_prompt // tm
    const2 = lambda i: (0, 0)
    return pl.pallas_call(
        functools.partial(_combine_kernel, alpha=alpha, n_prompt_tiles=npt),
        grid=(t_all // tm,),
        in_specs=[
            pl.BlockSpec(memory_space=pl.ANY),
            pl.BlockSpec(memory_space=pl.ANY),
            pl.BlockSpec((tm, SUBLANES), lambda i: (i, 0)),
            *_two_group_specs(tm, d, npt),
            pl.BlockSpec((d, ff), const2),
            pl.BlockSpec((d, ff), const2),
            pl.BlockSpec((ff, d), const2),
            pl.BlockSpec((1, d), const2),
            pl.BlockSpec((1, d), const2),
        ],
        out_specs=[
            pl.BlockSpec((tm, d), lambda i: (jnp.minimum(i, npt - 1), 0)),
            pl.BlockSpec((tm, d), lambda i: (jnp.maximum(i - npt, 0), 0)),
        ],
        out_shape=[
            jax.ShapeDtypeStruct((t_prompt, d), F32),
            jax.ShapeDtypeStruct((t_all - t_prompt, d), F32),
        ],
        scratch_shapes=[
            pltpu.SMEM((TOP_K * tm,), jnp.int32),
            pltpu.VMEM((TOP_K * tm, d), F32),
            pltpu.SemaphoreType.DMA,
            pltpu.SemaphoreType.DMA,
        ],
        compiler_params=pltpu.CompilerParams(dimension_semantics=("arbitrary",),
                                             vmem_limit_bytes=VMEM_LIMIT_BYTES),
        name="combine",
    )(pos_flat, ys, gates_t, h_p, h_s, wsg_b, wsu_b, wsd_b, ln_g, ln_b)


def _tile_major(a, tm):
    k, t = a.shape
    return a.reshape(k, t // tm, tm).transpose(1, 0, 2).reshape(-1)


def _moe(h_p, h_s, router_w, router_bias, w_gate, w_up, w_down, ws_gate, ws_up, ws_down, ln_g, ln_b, *, alpha):
    t_all = h_p.shape[0] + h_s.shape[0]
    n_experts = router_w.shape[-1]
    blk = EXPERT_BLOCK
    eidx8, rank8, gate8, cnt = _route(h_p, h_s, router_w.T.astype(BF16), router_bias.reshape(n_experts, 1))
    eidx, rank = eidx8[:TOP_K], rank8[:TOP_K]

    counts = cnt[:, 0].astype(jnp.int32)
    blocks_per_e = (counts + blk - 1) // blk
    block_end = jnp.cumsum(blocks_per_e)
    pad_start = (block_end - blocks_per_e) * blk
    n_blocks = -(-(t_all * TOP_K) // blk) + n_experts
    block_expert = jnp.minimum(jnp.searchsorted(block_end, jnp.arange(n_blocks), side="right"),
                               n_experts - 1).astype(jnp.int32)
    nact = block_end[-1:].astype(jnp.int32)
    zrow = (jnp.maximum(block_end - 1, 0) * blk).astype(jnp.int32)
    base = jnp.sum(jnp.where(eidx[..., None] == jnp.arange(n_experts), pad_start, 0), axis=-1)
    pos = (base + rank).astype(jnp.int32)

    xs = _dispatch(zrow, _tile_major(pos, ROUTE_TILE), h_p, h_s, n_rows=n_blocks * blk, blk=blk)
    ys = _experts(block_expert, nact, xs, w_gate, w_up, w_down, blk=blk)
    return _combine(_tile_major(pos, COMBINE_TILE), ys, gate8.T, h_p, h_s, ws_gate.astype(BF16),
                    ws_up.astype(BF16), ws_down.astype(BF16), ln_g, ln_b, alpha=alpha)


def kernel(x_prompt, x_sample, cache_k, cache_v, state_conv, w_in, attn_sinks, conv_w, g_attn_out, g_conv_out, w_out, ln1_g, ln1_b, router_w, router_bias, w_gate, w_up, w_down, ws_gate, ws_up, ws_down, ln2_g, ln2_b):
    depth = w_in.shape[0]
    bsz, seq, d = x_prompt.shape
    dec_b, dec_seq, _ = x_sample.shape
    win = cache_k.shape[2]
    kv_w = N_KV_HEADS * HEAD_DIM
    t_prompt = bsz * seq
    t_all = t_prompt + dec_b * dec_seq
    alpha = (2.0 * depth) ** 0.25
    assert win == WINDOW and seq % PROMPT_TILE == 0 and dec_b % SAMPLE_SEQS == 0
    for tile in (ROUTE_TILE, COMBINE_TILE):
        assert t_prompt % tile == 0 and (t_all - t_prompt) % tile == 0

    tabs_p = _rope_tables(jnp.arange(seq))
    tabs_s = tuple(jnp.tile(t, (SAMPLE_SEQS, 1)) for t in _rope_tables(PAST_LEN + jnp.arange(dec_seq)))
    row = lambda a: a.reshape(1, -1)

    xp, xs = x_prompt, x_sample
    outs = [[] for _ in range(6)]
    for l in range(depth):
        win_b, wout_b = w_in[l].astype(BF16), w_out[l].astype(BF16)
        shared = (attn_sinks[l], conv_w[l], row(g_attn_out[l]), row(g_conv_out[l]), wout_b, row(ln1_g[l]),
                  row(ln1_b[l]))
        h_p, kp, vp, cp = _prompt_mixer(xp, win_b, tabs_p, *shared, alpha=alpha)
        h_s, kn, vn, cn = _sample_mixer(xs, cache_k[l].reshape(dec_b, win, kv_w),
                                        cache_v[l].reshape(dec_b, win, kv_w), state_conv[l], win_b, tabs_s,
                                        *shared, alpha=alpha)
        yp, ys = _moe(h_p, h_s, router_w[l], router_bias[l], w_gate[l], w_up[l], w_down[l], ws_gate[l], ws_up[l],
                      ws_down[l], row(ln2_g[l]), row(ln2_b[l]), alpha=alpha)
        xp, xs = yp.reshape(bsz, seq, d), ys.reshape(dec_b, dec_seq, d)
        heads = lambda a: a.reshape(a.shape[0], win, N_KV_HEADS, HEAD_DIM)
        for o, a in zip(outs, (heads(kp), heads(vp), cp, heads(kn), heads(vn), cn)):
            o.append(a)
    return (xp, xs) + tuple(jnp.stack(o, axis=0) for o in outs)
```

```python
import functools

import jax
import jax.numpy as jnp
from jax import lax
from jax.experimental import pallas as pl
from jax.experimental.pallas import tpu as pltpu

PAST_LEN = 16384
WINDOW = 128
HEAD_DIM = 64
N_KV_HEADS = 2
ROT_DIM = HEAD_DIM // 4
ROPE_THETA = 500000.0
CONV_K = 3
TOP_K = 6
N_EXPERT_GROUPS = 8
TOPK_GROUPS = 4
ROUTED_SCALE = 2.5
LN_EPS = 1e-5
RMS_EPS = 1e-6

LANES = 128
SUBLANES = 8
VMEM_LIMIT_BYTES = 56 * 1024 * 1024

PROMPT_TILE = 512
SAMPLE_SEQS = 16
ROUTE_TILE = 512
COMBINE_TILE = 256
EXPERT_BLOCK = 256
DMA_UNROLL = 4

F32 = jnp.float32
BF16 = jnp.bfloat16
NEG_INF = float("-inf")


def _rope_tables(positions):
    half = ROT_DIM // 2
    inv_freq = ROPE_THETA ** (-jnp.arange(0, ROT_DIM, 2, dtype=F32) / ROT_DIM)
    ang = positions.astype(F32)[:, None] * inv_freq[None, :]
    cos, sin = jnp.cos(ang), jnp.sin(ang)
    n = positions.shape[0]
    rest = HEAD_DIM - ROT_DIM
    c = jnp.concatenate([cos, cos, jnp.ones((n, rest), F32)], axis=-1)
    sa = jnp.concatenate([-sin, jnp.zeros((n, half + rest), F32)], axis=-1)
    sb = jnp.concatenate([jnp.zeros((n, half), F32), sin, jnp.zeros((n, rest), F32)], axis=-1)
    reps = LANES // HEAD_DIM
    return jnp.tile(c, (1, reps)), jnp.tile(sa, (1, reps)), jnp.tile(sb, (1, reps))


def _rope(x, c, sa, sb):
    half = ROT_DIM // 2
    return x * c + pltpu.roll(x, LANES - half, 1) * sa + pltpu.roll(x, half, 1) * sb


def _rms_norm(x, g):
    return x * lax.rsqrt(jnp.mean(jnp.square(x), axis=-1, keepdims=True) + RMS_EPS) * g


def _layer_norm(x, g, b):
    mu = jnp.mean(x, axis=-1, keepdims=True)
    var = jnp.mean(jnp.square(x - mu), axis=-1, keepdims=True)
    return (x - mu) * lax.rsqrt(var + LN_EPS) * g + b


def _short_conv(gated, prev2, prev1, row, conv_w, b_gate):
    g1 = pltpu.roll(gated, 1, 0)
    g2 = pltpu.roll(gated, 2, 0)
    g1 = jnp.where(row == 0, prev1, g1)
    g2 = jnp.where(row == 0, prev2, jnp.where(row == 1, prev1, g2))
    y = conv_w[0:1, :] * g2 + conv_w[1:2, :] * g1 + conv_w[2:3, :] * gated
    return b_gate * y


def _merge_norm(x, attn_o, conv_o, gattn, gconv, wout_ref, ln_g, ln_b, alpha):
    cat = jnp.concatenate([_rms_norm(attn_o, gattn), _rms_norm(conv_o, gconv)], axis=-1)
    mix = jnp.dot(cat.astype(BF16), wout_ref[...], preferred_element_type=F32)
    return _layer_norm(alpha * x + mix, ln_g, ln_b)


def _sink_softmax(parts, sink):
    m = sink
    for s in parts:
        m = jnp.maximum(m, jnp.max(s, axis=-1, keepdims=True))
    es = [jnp.exp(s - m) for s in parts]
    den = jnp.exp(sink - m)
    for e in es:
        den = den + jnp.sum(e, axis=-1, keepdims=True)
    return [(e / den).astype(BF16) for e in es]


def _prompt_mixer_kernel(x_ref, win_ref, c_ref, sa_ref, sb_ref, sinks_ref, convw_ref, gattn_ref, gconv_ref,
                         wout_ref, lng_ref, lnb_ref,
                         h_ref, ko_ref, vo_ref, co_ref,
                         q_s, k_s, v_s, o_s, gc_s, *, alpha, n_heads, attn_w, kv_w, conv_ch):
    s = pl.program_id(1)
    last = pl.num_programs(1) - 1
    tq = x_ref.shape[0]
    q_per_kv = n_heads // N_KV_HEADS

    @pl.when(s == 0)
    def _():
        k_s[0:WINDOW, :] = jnp.zeros((WINDOW, kv_w), BF16)
        v_s[0:WINDOW, :] = jnp.zeros((WINDOW, kv_w), BF16)
        gc_s[...] = jnp.zeros(gc_s.shape, F32)

    @pl.when(s > 0)
    def _():
        k_s[0:WINDOW, :] = k_s[tq:tq + WINDOW, :]
        v_s[0:WINDOW, :] = v_s[tq:tq + WINDOW, :]

    x = x_ref[...]
    xb = x.astype(BF16)
    c, sa, sb = c_ref[...], sa_ref[...], sb_ref[...]

    def proj(lo, width):
        return jnp.dot(xb, win_ref[:, lo:lo + width], preferred_element_type=F32)

    scale = HEAD_DIM ** -0.5
    for j in range(attn_w // LANES):
        qj = _rope(proj(j * LANES, LANES), c, sa, sb)
        q_s[:, j * LANES:(j + 1) * LANES] = (qj * scale).astype(BF16)
    k = _rope(proj(attn_w, kv_w), c, sa, sb)
    v = proj(attn_w + kv_w, kv_w)
    k_s[WINDOW:WINDOW + tq, :] = k.astype(BF16)
    v_s[WINDOW:WINDOW + tq, :] = v.astype(BF16)

    @pl.when(s == last)
    def _():
        ko_ref[...] = k[tq - WINDOW:, :]
        vo_ref[...] = v[tq - WINDOW:, :]

    qi = lax.broadcasted_iota(jnp.int32, (WINDOW, 2 * WINDOW), 0)
    ci = lax.broadcasted_iota(jnp.int32, (WINDOW, 2 * WINDOW), 1)
    band = (ci > qi) & (ci <= qi + WINDOW)

    def sub_block(j, carry):
        r0 = pl.multiple_of(j * WINDOW, WINDOW)
        has_prev = (s * tq + r0) > 0
        mask = band & ((ci >= WINDOW) | has_prev)
        kk = k_s[pl.ds(r0, 2 * WINDOW), :]
        vv = v_s[pl.ds(r0, 2 * WINDOW), :]
        for hd in range(n_heads):
            kvh = hd // q_per_kv
            qh = q_s[pl.ds(r0, WINDOW), hd * HEAD_DIM:(hd + 1) * HEAD_DIM]
            kh = kk[:, kvh * HEAD_DIM:(kvh + 1) * HEAD_DIM]
            vh = vv[:, kvh * HEAD_DIM:(kvh + 1) * HEAD_DIM]
            sc = lax.dot_general(qh, kh, (((1,), (1,)), ((), ())), preferred_element_type=F32)
            sc = jnp.where(mask, sc, NEG_INF)
            (p,) = _sink_softmax([sc], sinks_ref[hd])
            o_s[pl.ds(r0, WINDOW), hd * HEAD_DIM:(hd + 1) * HEAD_DIM] = jnp.dot(
                p, vh, preferred_element_type=F32)
        return carry

    lax.fori_loop(0, tq // WINDOW, sub_block, 0)

    o3 = attn_w + 2 * kv_w
    gated = proj(o3 + 2 * conv_ch, conv_ch) * proj(o3, conv_ch)
    row = lax.broadcasted_iota(jnp.int32, (tq, 1), 0)
    conv_o = _short_conv(gated, gc_s[0:1, :], gc_s[1:2, :], row, convw_ref[...], proj(o3 + conv_ch, conv_ch))
    gc_s[0:CONV_K - 1, :] = gated[tq - (CONV_K - 1):, :]

    @pl.when(s == last)
    def _():
        co_ref[...] = gated[tq - (CONV_K - 1):, :]

    h_ref[...] = _merge_norm(x, o_s[...], conv_o, gattn_ref[...], gconv_ref[...], wout_ref,
                             lng_ref[...], lnb_ref[...], alpha)


def _prompt_mixer(x, win_b, tabs, sinks, conv_w, g_attn, g_conv, wout_b, ln_g, ln_b, *, alpha):
    bsz, seq, d = x.shape
    tq = PROMPT_TILE
    ns = seq // tq
    attn_w = g_attn.shape[-1]
    conv_ch = g_conv.shape[-1]
    n_heads = attn_w // HEAD_DIM
    kv_w = N_KV_HEADS * HEAD_DIM
    in_cols = win_b.shape[-1]
    const2 = lambda b, s: (0, 0)
    kern = functools.partial(_prompt_mixer_kernel, alpha=alpha, n_heads=n_heads, attn_w=attn_w, kv_w=kv_w,
                             conv_ch=conv_ch)
    return pl.pallas_call(
        kern,
        grid=(bsz, ns),
        in_specs=[
            pl.BlockSpec((None, tq, d), lambda b, s: (b, s, 0)),
            pl.BlockSpec((d, in_cols), const2),
            pl.BlockSpec((tq, LANES), lambda b, s: (s, 0)),
            pl.BlockSpec((tq, LANES), lambda b, s: (s, 0)),
            pl.BlockSpec((tq, LANES), lambda b, s: (s, 0)),
            pl.BlockSpec(memory_space=pltpu.SMEM),
            pl.BlockSpec((CONV_K, conv_ch), const2),
            pl.BlockSpec((1, attn_w), const2),
            pl.BlockSpec((1, conv_ch), const2),
            pl.BlockSpec((attn_w + conv_ch, d), const2),
            pl.BlockSpec((1, d), const2),
            pl.BlockSpec((1, d), const2),
        ],
        out_specs=[
            pl.BlockSpec((tq, d), lambda b, s: (b * ns + s, 0)),
            pl.BlockSpec((None, WINDOW, kv_w), lambda b, s: (b, 0, 0)),
            pl.BlockSpec((None, WINDOW, kv_w), lambda b, s: (b, 0, 0)),
            pl.BlockSpec((None, CONV_K - 1, conv_ch), lambda b, s: (b, 0, 0)),
        ],
        out_shape=[
            jax.ShapeDtypeStruct((bsz * seq, d), F32),
            jax.ShapeDtypeStruct((bsz, WINDOW, kv_w), F32),
            jax.ShapeDtypeStruct((bsz, WINDOW, kv_w), F32),
            jax.ShapeDtypeStruct((bsz, CONV_K - 1, conv_ch), F32),
        ],
        scratch_shapes=[
            pltpu.VMEM((tq, attn_w), BF16),
            pltpu.VMEM((tq + WINDOW, kv_w), BF16),
            pltpu.VMEM((tq + WINDOW, kv_w), BF16),
            pltpu.VMEM((tq, attn_w), F32),
            pltpu.VMEM((SUBLANES, conv_ch), F32),
        ],
        compiler_params=pltpu.CompilerParams(dimension_semantics=("arbitrary", "arbitrary"),
                                             vmem_limit_bytes=VMEM_LIMIT_BYTES),
        name="prompt_mixer",
    )(x, win_b, *tabs, sinks, conv_w, g_attn, g_conv, wout_b, ln_g, ln_b)


def _sample_mixer_kernel(x_ref, ck_ref, cv_ref, st_ref, win_ref, c_ref, sa_ref, sb_ref, sinks_ref,
                         convw_ref, gattn_ref, gconv_ref, wout_ref, lng_ref, lnb_ref,
                         h_ref, ko_ref, vo_ref, co_ref, *, alpha, n_heads, attn_w, kv_w, conv_ch, dec_seq):
    nb, win = ck_ref.shape[0], ck_ref.shape[1]
    rows = nb * dec_seq
    q_per_kv = n_heads // N_KV_HEADS
    x = x_ref[...]
    xb = x.astype(BF16)
    c, sa, sb = c_ref[...], sa_ref[...], sb_ref[...]

    def proj(lo, width):
        return jnp.dot(xb, win_ref[:, lo:lo + width], preferred_element_type=F32)

    scale = HEAD_DIM ** -0.5
    k = _rope(proj(attn_w, kv_w), c, sa, sb)
    v = proj(attn_w + kv_w, kv_w)
    k3 = k.reshape(nb, dec_seq, kv_w)
    v3 = v.reshape(nb, dec_seq, kv_w)
    ck = ck_ref[...]
    cv = cv_ref[...]
    ko_ref[:, 0:win - dec_seq, :] = ck[:, dec_seq:, :]
    ko_ref[:, win - dec_seq:, :] = k3
    vo_ref[:, 0:win - dec_seq, :] = cv[:, dec_seq:, :]
    vo_ref[:, win - dec_seq:, :] = v3
    ckb, cvb, k3b, v3b = ck.astype(BF16), cv.astype(BF16), k3.astype(BF16), v3.astype(BF16)

    qrows = q_per_kv * dec_seq
    qi = lax.broadcasted_iota(jnp.int32, (nb, qrows, win), 1) % dec_seq
    mask_c = lax.broadcasted_iota(jnp.int32, (nb, qrows, win), 2) > qi + (win - WINDOW)
    qn = lax.broadcasted_iota(jnp.int32, (nb, qrows, dec_seq), 1) % dec_seq
    mask_n = lax.broadcasted_iota(jnp.int32, (nb, qrows, dec_seq), 2) <= qn
    sink_row = lax.broadcasted_iota(jnp.int32, (nb, qrows, 1), 1) // dec_seq

    q_chunks = [_rope(proj(j * LANES, LANES), c, sa, sb) * scale for j in range(attn_w // LANES)]
    heads_out = []
    for kvh in range(N_KV_HEADS):
        qs = []
        for g in range(q_per_kv):
            lo = (kvh * q_per_kv + g) * HEAD_DIM
            qh = q_chunks[lo // LANES][:, lo % LANES:lo % LANES + HEAD_DIM]
            qs.append(qh.reshape(nb, dec_seq, HEAD_DIM))
        qg = jnp.concatenate(qs, axis=1).astype(BF16)
        sl = slice(kvh * HEAD_DIM, (kvh + 1) * HEAD_DIM)
        sc_c = jnp.einsum("bqd,bkd->bqk", qg, ckb[:, :, sl], preferred_element_type=F32)
        sc_n = jnp.einsum("bqd,bkd->bqk", qg, k3b[:, :, sl], preferred_element_type=F32)
        sc_c = jnp.where(mask_c, sc_c, NEG_INF)
        sc_n = jnp.where(mask_n, sc_n, NEG_INF)
        sink = jnp.zeros((nb, qrows, 1), F32)
        for g in range(q_per_kv):
            sink = jnp.where(sink_row == g, sinks_ref[kvh * q_per_kv + g], sink)
        p_c, p_n = _sink_softmax([sc_c, sc_n], sink)
        og = (jnp.einsum("bqk,bkd->bqd", p_c, cvb[:, :, sl], preferred_element_type=F32)
              + jnp.einsum("bqk,bkd->bqd", p_n, v3b[:, :, sl], preferred_element_type=F32))
        for g in range(q_per_kv):
            heads_out.append(og[:, g * dec_seq:(g + 1) * dec_seq, :].reshape(rows, HEAD_DIM))
    attn_o = jnp.concatenate(heads_out, axis=-1)

    o3 = attn_w + 2 * kv_w
    gated = proj(o3 + 2 * conv_ch, conv_ch) * proj(o3, conv_ch)
    st = st_ref[...]
    prev2 = jnp.broadcast_to(st[:, 0:1, :], (nb, dec_seq, conv_ch)).reshape(rows, conv_ch)
    prev1 = jnp.broadcast_to(st[:, 1:2, :], (nb, dec_seq, conv_ch)).reshape(rows, conv_ch)
    row = lax.broadcasted_iota(jnp.int32, (rows, 1), 0) % dec_seq
    conv_o = _short_conv(gated, prev2, prev1, row, convw_ref[...], proj(o3 + conv_ch, conv_ch))
    co_ref[...] = gated.reshape(nb, dec_seq, conv_ch)[:, dec_seq - (CONV_K - 1):, :]

    h_ref[...] = _merge_norm(x, attn_o, conv_o, gattn_ref[...], gconv_ref[...], wout_ref,
                             lng_ref[...], lnb_ref[...], alpha)


def _sample_mixer(x, ck, cv, st, win_b, tabs, sinks, conv_w, g_attn, g_conv, wout_b, ln_g, ln_b, *, alpha):
    dec_b, dec_seq, d = x.shape
    assert dec_seq >= CONV_K - 1 and dec_seq % SUBLANES == 0
    nb = SAMPLE_SEQS
    rows = nb * dec_seq
    win = ck.shape[1]
    attn_w = g_attn.shape[-1]
    conv_ch = g_conv.shape[-1]
    n_heads = attn_w // HEAD_DIM
    kv_w = N_KV_HEADS * HEAD_DIM
    in_cols = win_b.shape[-1]
    const2 = lambda i: (0, 0)
    kern = functools.partial(_sample_mixer_kernel, alpha=alpha, n_heads=n_heads, attn_w=attn_w, kv_w=kv_w,
                             conv_ch=conv_ch, dec_seq=dec_seq)
    return pl.pallas_call(
        kern,
        grid=(dec_b // nb,),
        in_specs=[
            pl.BlockSpec((rows, d), lambda i: (i, 0)),
            pl.BlockSpec((nb, win, kv_w), lambda i: (i, 0, 0)),
            pl.BlockSpec((nb, win, kv_w), lambda i: (i, 0, 0)),
            pl.BlockSpec((nb, CONV_K - 1, conv_ch), lambda i: (i, 0, 0)),
            pl.BlockSpec((d, in_cols), const2),
            pl.BlockSpec((rows, LANES), const2),
            pl.BlockSpec((rows, LANES), const2),
            pl.BlockSpec((rows, LANES), const2),
            pl.BlockSpec(memory_space=pltpu.SMEM),
            pl.BlockSpec((CONV_K, conv_ch), const2),
            pl.BlockSpec((1, attn_w), const2),
            pl.BlockSpec((1, conv_ch), const2),
            pl.BlockSpec((attn_w + conv_ch, d), const2),
            pl.BlockSpec((1, d), const2),
            pl.BlockSpec((1, d), const2),
        ],
        out_specs=[
            pl.BlockSpec((rows, d), lambda i: (i, 0)),
            pl.BlockSpec((nb, win, kv_w), lambda i: (i, 0, 0)),
            pl.BlockSpec((nb, win, kv_w), lambda i: (i, 0, 0)),
            pl.BlockSpec((nb, CONV_K - 1, conv_ch), lambda i: (i, 0, 0)),
        ],
        out_shape=[
            jax.ShapeDtypeStruct((dec_b * dec_seq, d), F32),
            jax.ShapeDtypeStruct((dec_b, win, kv_w), F32),
            jax.ShapeDtypeStruct((dec_b, win, kv_w), F32),
            jax.ShapeDtypeStruct((dec_b, CONV_K - 1, conv_ch), F32),
        ],
        compiler_params=pltpu.CompilerParams(dimension_semantics=("arbitrary",),
                                             vmem_limit_bytes=VMEM_LIMIT_BYTES),
        name="sample_mixer",
    )(x.reshape(dec_b * dec_seq, d), ck, cv, st, win_b, *tabs, sinks, conv_w, g_attn, g_conv, wout_b,
      ln_g, ln_b)


def _over_experts(fn, x):
    return fn(fn(x, axis=0, keepdims=True), axis=1, keepdims=True)


def _two_group_specs(tm, d, n_prompt_tiles):
    return [pl.BlockSpec((tm, d), lambda i, *_: (jnp.minimum(i, n_prompt_tiles - 1), 0)),
            pl.BlockSpec((tm, d), lambda i, *_: (jnp.maximum(i - n_prompt_tiles, 0), 0))]


def _route_kernel(hp_ref, hs_ref, rwt_ref, bias_ref, eidx_ref, rank_ref, gate_ref, cnt_ref, cnt_s, *, n_experts,
                  n_prompt_tiles):
    i = pl.program_id(0)
    tm = hp_ref.shape[0]
    per_group = n_experts // N_EXPERT_GROUPS
    shape3 = (N_EXPERT_GROUPS, per_group, tm)

    @pl.when(i == 0)
    def _():
        cnt_s[...] = jnp.zeros(cnt_s.shape, F32)

    h = jnp.where(i < n_prompt_tiles, hp_ref[...], hs_ref[...])
    logits = lax.dot_general(rwt_ref[...], h.astype(BF16), (((1,), (1,)), ((), ())),
                             preferred_element_type=F32)
    scores = jax.nn.sigmoid(logits)
    sel = scores + bias_ref[...]
    scores3 = scores.reshape(shape3)
    grp = sel.reshape(shape3)
    member = lax.broadcasted_iota(jnp.int32, shape3, 1).astype(F32)
    group = lax.broadcasted_iota(jnp.int32, shape3, 0).astype(F32)
    expert = group * per_group + member

    m1 = jnp.max(grp, axis=1, keepdims=True)
    f1 = jnp.min(jnp.where(grp == m1, member, float(per_group)), axis=1, keepdims=True)
    m2 = jnp.max(jnp.where(member == f1, NEG_INF, grp), axis=1, keepdims=True)
    gscore = m1 + m2

    gid = lax.broadcasted_iota(jnp.int32, gscore.shape, 0).astype(F32)
    gmask = jnp.zeros(gscore.shape, F32)
    cur = gscore
    for _ in range(TOPK_GROUPS):
        mx = jnp.max(cur, axis=0, keepdims=True)
        pick = gid == jnp.min(jnp.where(cur == mx, gid, float(N_EXPERT_GROUPS)), axis=0, keepdims=True)
        gmask = jnp.where(pick, 1.0, gmask)
        cur = jnp.where(pick, NEG_INF, cur)

    cand = jnp.where(gmask > 0.0, grp, NEG_INF)
    chosen = jnp.zeros(shape3, F32)
    picks, firsts, weights = [], [], []
    for _ in range(TOP_K):
        mx = _over_experts(jnp.max, cand)
        first = _over_experts(jnp.min, jnp.where(cand == mx, expert, float(n_experts)))
        pick = expert == first
        picks.append(pick)
        firsts.append(first)
        weights.append(_over_experts(jnp.sum, jnp.where(pick, scores3, 0.0)))
        chosen = jnp.where(pick, 1.0, chosen)
        cand = jnp.where(pick, NEG_INF, cand)
    wsum = weights[0]
    for w in weights[1:]:
        wsum = wsum + w

    chosen2 = chosen.reshape(n_experts, tm)
    earlier = (lax.broadcasted_iota(jnp.int32, (tm, tm), 0)
               < lax.broadcasted_iota(jnp.int32, (tm, tm), 1)).astype(BF16)
    before = jnp.dot(chosen2.astype(BF16), earlier, preferred_element_type=F32) + cnt_s[:, 0:1]
    before3 = before.reshape(shape3)

    pad = SUBLANES - TOP_K
    eidx = [f.reshape(1, tm).astype(jnp.int32) for f in firsts]
    rank = [_over_experts(jnp.sum, jnp.where(p, before3, 0.0)).reshape(1, tm).astype(jnp.int32) for p in picks]
    gate = [(w / wsum * ROUTED_SCALE).reshape(1, tm) for w in weights]
    eidx_ref[...] = jnp.concatenate(eidx + [jnp.zeros((pad, tm), jnp.int32)], axis=0)
    rank_ref[...] = jnp.concatenate(rank + [jnp.zeros((pad, tm), jnp.int32)], axis=0)
    gate_ref[...] = jnp.concatenate(gate + [jnp.zeros((pad, tm), F32)], axis=0)

    cnt_s[...] = cnt_s[...] + jnp.sum(chosen2, axis=1, keepdims=True)
    cnt_ref[...] = cnt_s[...]


def _route(h_p, h_s, rwt_b, bias_col):
    d = h_p.shape[1]
    t_all = h_p.shape[0] + h_s.shape[0]
    n_experts = rwt_b.shape[0]
    tm = ROUTE_TILE
    nt = t_all // tm
    npt = h_p.shape[0] // tm
    row_spec = pl.BlockSpec((SUBLANES, tm), lambda i: (0, i))
    return pl.pallas_call(
        functools.partial(_route_kernel, n_experts=n_experts, n_prompt_tiles=npt),
        grid=(nt,),
        in_specs=_two_group_specs(tm, d, npt) + [
            pl.BlockSpec((n_experts, d), lambda i: (0, 0)),
            pl.BlockSpec((n_experts, 1), lambda i: (0, 0)),
        ],
        out_specs=[row_spec, row_spec, row_spec, pl.BlockSpec((n_experts, LANES), lambda i: (0, 0))],
        out_shape=[
            jax.ShapeDtypeStruct((SUBLANES, t_all), jnp.int32),
            jax.ShapeDtypeStruct((SUBLANES, t_all), jnp.int32),
            jax.ShapeDtypeStruct((SUBLANES, t_all), F32),
            jax.ShapeDtypeStruct((n_experts, LANES), F32),
        ],
        scratch_shapes=[pltpu.VMEM((n_experts, LANES), F32)],
        compiler_params=pltpu.CompilerParams(dimension_semantics=("arbitrary",),
                                             vmem_limit_bytes=VMEM_LIMIT_BYTES),
        name="route",
    )(h_p, h_s, rwt_b, bias_col)


def _dispatch_kernel(zrow_ref, pos_hbm, hp_ref, hs_ref, xs_hbm, pos_s, zero_s, pos_sem, row_sem, *, n_experts, blk,
                     n_prompt_tiles):
    i = pl.program_id(0)
    tm = hp_ref.shape[0]
    n = TOP_K * tm

    @pl.when(i == 0)
    def _():
        zero_s[...] = jnp.zeros(zero_s.shape, F32)

        def zcopy(e):
            return pltpu.make_async_copy(zero_s, xs_hbm.at[pl.ds(pl.multiple_of(zrow_ref[e], blk), blk)], row_sem)

        def start(e, c):
            zcopy(e).start()
            return c

        def wait(e, c):
            zcopy(e).wait()
            return c

        lax.fori_loop(0, n_experts, start, 0)
        lax.fori_loop(0, n_experts, wait, 0)

    pos_cp = pltpu.make_async_copy(pos_hbm.at[pl.ds(i * n, n)], pos_s, pos_sem)
    pos_cp.start()
    pos_cp.wait()

    def scatter_rows(h_ref):
        def start(t, c):
            for k in range(TOP_K):
                pltpu.make_async_copy(h_ref.at[pl.ds(t, 1)], xs_hbm.at[pl.ds(pos_s[k * tm + t], 1)],
                                      row_sem).start()
            return c

        lax.fori_loop(0, tm, start, 0, unroll=DMA_UNROLL)
        for k in range(TOP_K):
            pltpu.make_async_copy(h_ref, xs_hbm.at[pl.ds(0, tm)], row_sem).wait()

    @pl.when(i < n_prompt_tiles)
    def _():
        scatter_rows(hp_ref)

    @pl.when(i >= n_prompt_tiles)
    def _():
        scatter_rows(hs_ref)


def _dispatch(zrow, pos_flat, h_p, h_s, *, n_rows, blk):
    d = h_p.shape[1]
    t_all = h_p.shape[0] + h_s.shape[0]
    tm = ROUTE_TILE
    npt = h_p.shape[0] // tm
    n_experts = zrow.shape[0]
    return pl.pallas_call(
        functools.partial(_dispatch_kernel, n_experts=n_experts, blk=blk, n_prompt_tiles=npt),
        grid_spec=pltpu.PrefetchScalarGridSpec(
            num_scalar_prefetch=1,
            grid=(t_all // tm,),
            in_specs=[pl.BlockSpec(memory_space=pl.ANY)] + _two_group_specs(tm, d, npt),
            out_specs=pl.BlockSpec(memory_space=pl.ANY),
            scratch_shapes=[
                pltpu.SMEM((TOP_K * tm,), jnp.int32),
                pltpu.VMEM((blk, d), F32),
                pltpu.SemaphoreType.DMA,
                pltpu.SemaphoreType.DMA,
            ],
        ),
        out_shape=jax.ShapeDtypeStruct((n_rows, d), F32),
        compiler_params=pltpu.CompilerParams(dimension_semantics=("arbitrary",),
                                             vmem_limit_bytes=VMEM_LIMIT_BYTES),
        name="dispatch",
    )(zrow, pos_flat, h_p, h_s)


def _silu(x):
    return x * jax.nn.sigmoid(x)


def _expert_kernel(be_ref, nact_ref, xs_ref, wg_ref, wu_ref, wd_ref, ys_ref, wg_s, wu_s, wd_s):
    b = pl.program_id(0)

    @pl.when(b < nact_ref[0])
    def _():
        @pl.when((b == 0) | (be_ref[b] != be_ref[jnp.maximum(b - 1, 0)]))
        def _():
            wg_s[...] = wg_ref[...].astype(BF16)
            wu_s[...] = wu_ref[...].astype(BF16)
            wd_s[...] = wd_ref[...].astype(BF16)

        xb = xs_ref[...].astype(BF16)
        hid = _silu(jnp.dot(xb, wg_s[...], preferred_element_type=F32)) * jnp.dot(
            xb, wu_s[...], preferred_element_type=F32)
        ys_ref[...] = jnp.dot(hid.astype(BF16), wd_s[...], preferred_element_type=F32)


def _experts(block_expert, nact, xs, w_gate, w_up, w_down, *, blk):
    n_rows, d = xs.shape
    ff = w_gate.shape[-1]
    n_blocks = n_rows // blk

    def active(b, be, na):
        return jnp.minimum(b, na[0] - 1)

    return pl.pallas_call(
        _expert_kernel,
        grid_spec=pltpu.PrefetchScalarGridSpec(
            num_scalar_prefetch=2,
            grid=(n_blocks,),
            in_specs=[
                pl.BlockSpec((blk, d), lambda b, be, na: (active(b, be, na), 0)),
                pl.BlockSpec((None, d, ff), lambda b, be, na: (be[active(b, be, na)], 0, 0)),
                pl.BlockSpec((None, d, ff), lambda b, be, na: (be[active(b, be, na)], 0, 0)),
                pl.BlockSpec((None, ff, d), lambda b, be, na: (be[active(b, be, na)], 0, 0)),
            ],
            out_specs=pl.BlockSpec((blk, d), lambda b, be, na: (active(b, be, na), 0)),
            scratch_shapes=[
                pltpu.VMEM((d, ff), BF16),
                pltpu.VMEM((d, ff), BF16),
                pltpu.VMEM((ff, d), BF16),
            ],
        ),
        out_shape=jax.ShapeDtypeStruct((n_rows, d), F32),
        compiler_params=pltpu.CompilerParams(dimension_semantics=("arbitrary",),
                                             vmem_limit_bytes=VMEM_LIMIT_BYTES),
        name="experts",
    )(block_expert, nact, xs, w_gate, w_up, w_down)


def _combine_kernel(pos_hbm, ys_hbm, gate_ref, hp_ref, hs_ref, wsg_ref, wsu_ref, wsd_ref, lng_ref, lnb_ref,
                    yp_ref, ysm_ref, pos_s, ybuf, pos_sem, row_sem, *, alpha, n_prompt_tiles):
    i = pl.program_id(0)
    tm = hp_ref.shape[0]
    n = TOP_K * tm

    pos_cp = pltpu.make_async_copy(pos_hbm.at[pl.ds(i * n, n)], pos_s, pos_sem)
    pos_cp.start()
    pos_cp.wait()

    def start(t, c):
        for k in range(TOP_K):
            a = k * tm + t
            pltpu.make_async_copy(ys_hbm.at[pl.ds(pos_s[a], 1)], ybuf.at[pl.ds(a, 1)], row_sem).start()
        return c

    lax.fori_loop(0, tm, start, 0, unroll=DMA_UNROLL)

    h = jnp.where(i < n_prompt_tiles, hp_ref[...], hs_ref[...])
    hb = h.astype(BF16)
    hid = _silu(jnp.dot(hb, wsg_ref[...], preferred_element_type=F32)) * jnp.dot(
        hb, wsu_ref[...], preferred_element_type=F32)
    shared = jnp.dot(hid.astype(BF16), wsd_ref[...], preferred_element_type=F32)

    pltpu.make_async_copy(ys_hbm.at[pl.ds(0, n)], ybuf, row_sem).wait()

    gates = gate_ref[...]
    routed = gates[:, 0:1] * ybuf[0:tm, :]
    for k in range(1, TOP_K):
        routed = routed + gates[:, k:k + 1] * ybuf[k * tm:(k + 1) * tm, :]
    y = _layer_norm(alpha * h + (routed + shared), lng_ref[...], lnb_ref[...])

    @pl.when(i < n_prompt_tiles)
    def _():
        yp_ref[...] = y

    @pl.when(i >= n_prompt_tiles)
    def _():
        ysm_ref[...] = y


def _combine(pos_flat, ys, gates_t, h_p, h_s, wsg_b, wsu_b, wsd_b, ln_g, ln_b, *, alpha):
    t_prompt, d = h_p.shape
    t_all = t_prompt + h_s.shape[0]
    tm = COMBINE_TILE
    ff = wsg_b.shape[-1]
    npt = t_prompt // tm
    const2 = lambda i: (0, 0)
    return pl.pallas_call(
        functools.partial(_combine_kernel, alpha=alpha, n_prompt_tiles=npt),
        grid=(t_all // tm,),
        in_specs=[
            pl.BlockSpec(memory_space=pl.ANY),
            pl.BlockSpec(memory_space=pl.ANY),
            pl.BlockSpec((tm, SUBLANES), lambda i: (i, 0)),
            *_two_group_specs(tm, d, npt),
            pl.BlockSpec((d, ff), const2),
            pl.BlockSpec((d, ff), const2),
            pl.BlockSpec((ff, d), const2),
            pl.BlockSpec((1, d), const2),
            pl.BlockSpec((1, d), const2),
        ],
        out_specs=[
            pl.BlockSpec((tm, d), lambda i: (jnp.minimum(i, npt - 1), 0)),
            pl.BlockSpec((tm, d), lambda i: (jnp.maximum(i - npt, 0), 0)),
        ],
        out_shape=[
            jax.ShapeDtypeStruct((t_prompt, d), F32),
            jax.ShapeDtypeStruct((t_all - t_prompt, d), F32),
        ],
        scratch_shapes=[
            pltpu.SMEM((TOP_K * tm,), jnp.int32),
            pltpu.VMEM((TOP_K * tm, d), F32),
            pltpu.SemaphoreType.DMA,
            pltpu.SemaphoreType.DMA,
        ],
        compiler_params=pltpu.CompilerParams(dimension_semantics=("arbitrary",),
                                             vmem_limit_bytes=VMEM_LIMIT_BYTES),
        name="combine",
    )(pos_flat, ys, gates_t, h_p, h_s, wsg_b, wsu_b, wsd_b, ln_g, ln_b)


def _tile_major(a, tm):
    k, t = a.shape
    return a.reshape(k, t // tm, tm).transpose(1, 0, 2).reshape(-1)


def _moe(h_p, h_s, router_w, router_bias, w_gate, w_up, w_down, ws_gate, ws_up, ws_down, ln_g, ln_b, *, alpha):
    t_all = h_p.shape[0] + h_s.shape[0]
    n_experts = router_w.shape[-1]
    blk = EXPERT_BLOCK
    eidx8, rank8, gate8, cnt = _route(h_p, h_s, router_w.T.astype(BF16), router_bias.reshape(n_experts, 1))
    eidx, rank = eidx8[:TOP_K], rank8[:TOP_K]

    counts = cnt[:, 0].astype(jnp.int32)
    blocks_per_e = (counts + blk - 1) // blk
    block_end = jnp.cumsum(blocks_per_e)
    pad_start = (block_end - blocks_per_e) * blk
    n_blocks = -(-(t_all * TOP_K) // blk) + n_experts
    block_expert = jnp.minimum(jnp.sum(block_end[None, :] <= jnp.arange(n_blocks)[:, None], axis=1),
                               n_experts - 1).astype(jnp.int32)
    nact = block_end[-1:].astype(jnp.int32)
    zrow = (jnp.maximum(block_end - 1, 0) * blk).astype(jnp.int32)
    base = jnp.sum(jnp.where(eidx[..., None] == jnp.arange(n_experts), pad_start, 0), axis=-1)
    pos = (base + rank).astype(jnp.int32)

    xs = _dispatch(zrow, _tile_major(pos, ROUTE_TILE), h_p, h_s, n_rows=n_blocks * blk, blk=blk)
    ys = _experts(block_expert, nact, xs, w_gate, w_up, w_down, blk=blk)
    return _combine(_tile_major(pos, COMBINE_TILE), ys, gate8.T, h_p, h_s, ws_gate.astype(BF16),
                    ws_up.astype(BF16), ws_down.astype(BF16), ln_g, ln_b, alpha=alpha)


def kernel(x_prompt, x_sample, cache_k, cache_v, state_conv, w_in, attn_sinks, conv_w, g_attn_out, g_conv_out, w_out, ln1_g, ln1_b, router_w, router_bias, w_gate, w_up, w_down, ws_gate, ws_up, ws_down, ln2_g, ln2_b):
    depth = w_in.shape[0]
    bsz, seq, d = x_prompt.shape
    dec_b, dec_seq, _ = x_sample.shape
    win = cache_k.shape[2]
    kv_w = N_KV_HEADS * HEAD_DIM
    t_prompt = bsz * seq
    t_all = t_prompt + dec_b * dec_seq
    alpha = (2.0 * depth) ** 0.25
    assert win == WINDOW and seq % PROMPT_TILE == 0 and dec_b % SAMPLE_SEQS == 0
    for tile in (ROUTE_TILE, COMBINE_TILE):
        assert t_prompt % tile == 0 and (t_all - t_prompt) % tile == 0

    tabs_p = _rope_tables(jnp.arange(seq))
    tabs_s = tuple(jnp.tile(t, (SAMPLE_SEQS, 1)) for t in _rope_tables(PAST_LEN + jnp.arange(dec_seq)))
    row = lambda a: a.reshape(1, -1)

    xp, xs = x_prompt, x_sample
    outs = [[] for _ in range(6)]
    for l in range(depth):
        win_b, wout_b = w_in[l].astype(BF16), w_out[l].astype(BF16)
        shared = (attn_sinks[l], conv_w[l], row(g_attn_out[l]), row(g_conv_out[l]), wout_b, row(ln1_g[l]),
                  row(ln1_b[l]))
        h_p, kp, vp, cp = _prompt_mixer(xp, win_b, tabs_p, *shared, alpha=alpha)
        h_s, kn, vn, cn = _sample_mixer(xs, cache_k[l].reshape(dec_b, win, kv_w),
                                        cache_v[l].reshape(dec_b, win, kv_w), state_conv[l], win_b, tabs_s,
                                        *shared, alpha=alpha)
        yp, ys = _moe(h_p, h_s, router_w[l], router_bias[l], w_gate[l], w_up[l], w_down[l], ws_gate[l], ws_up[l],
                      ws_down[l], row(ln2_g[l]), row(ln2_b[l]), alpha=alpha)
        xp, xs = yp.reshape(bsz, seq, d), ys.reshape(dec_b, dec_seq, d)
        heads = lambda a: a.reshape(a.shape[0], win, N_KV_HEADS, HEAD_DIM)
        for o, a in zip(outs, (heads(kp), heads(vp), cp, heads(kn), heads(vn), cn)):
            o.append(a)
    return (xp, xs) + tuple(jnp.stack(o, axis=0) for o in outs)
```

```python
import functools

import jax
import jax.numpy as jnp
from jax import lax
from jax.experimental import pallas as pl
from jax.experimental.pallas import tpu as pltpu

PAST_LEN = 16384
WINDOW = 128
HEAD_DIM = 64
N_KV_HEADS = 2
ROT_DIM = HEAD_DIM // 4
ROPE_THETA = 500000.0
CONV_K = 3
TOP_K = 6
N_EXPERT_GROUPS = 8
TOPK_GROUPS = 4
ROUTED_SCALE = 2.5
LN_EPS = 1e-5
RMS_EPS = 1e-6

LANES = 128
SUBLANES = 8
VMEM_LIMIT_BYTES = 56 * 1024 * 1024

PROMPT_TILE = 512
SAMPLE_SEQS = 16
MOE_TILE = 256
SORT_CHUNK = 256
EXPERT_BLOCK = 256

F32 = jnp.float32
BF16 = jnp.bfloat16
NEG_INF = float("-inf")


def _rope_tables(positions):
    half = ROT_DIM // 2
    inv_freq = ROPE_THETA ** (-jnp.arange(0, ROT_DIM, 2, dtype=F32) / ROT_DIM)
    ang = positions.astype(F32)[:, None] * inv_freq[None, :]
    cos, sin = jnp.cos(ang), jnp.sin(ang)
    n = positions.shape[0]
    rest = HEAD_DIM - ROT_DIM
    c = jnp.concatenate([cos, cos, jnp.ones((n, rest), F32)], axis=-1)
    sa = jnp.concatenate([-sin, jnp.zeros((n, half + rest), F32)], axis=-1)
    sb = jnp.concatenate([jnp.zeros((n, half), F32), sin, jnp.zeros((n, rest), F32)], axis=-1)
    reps = LANES // HEAD_DIM
    return jnp.tile(c, (1, reps)), jnp.tile(sa, (1, reps)), jnp.tile(sb, (1, reps))


def _rope(x, c, sa, sb):
    half = ROT_DIM // 2
    return x * c + pltpu.roll(x, LANES - half, 1) * sa + pltpu.roll(x, half, 1) * sb


def _rms_norm(x, g):
    return x * lax.rsqrt(jnp.mean(jnp.square(x), axis=-1, keepdims=True) + RMS_EPS) * g


def _layer_norm(x, g, b):
    mu = jnp.mean(x, axis=-1, keepdims=True)
    var = jnp.mean(jnp.square(x - mu), axis=-1, keepdims=True)
    return (x - mu) * lax.rsqrt(var + LN_EPS) * g + b


def _short_conv(gated, prev2, prev1, row, conv_w, b_gate):
    g1 = pltpu.roll(gated, 1, 0)
    g2 = pltpu.roll(gated, 2, 0)
    g1 = jnp.where(row == 0, prev1, g1)
    g2 = jnp.where(row == 0, prev2, jnp.where(row == 1, prev1, g2))
    y = conv_w[0:1, :] * g2 + conv_w[1:2, :] * g1 + conv_w[2:3, :] * gated
    return b_gate * y


def _merge_norm(x, attn_o, conv_o, gattn, gconv, wout_ref, ln_g, ln_b, alpha):
    cat = jnp.concatenate([_rms_norm(attn_o, gattn), _rms_norm(conv_o, gconv)], axis=-1)
    mix = jnp.dot(cat.astype(BF16), wout_ref[...], preferred_element_type=F32)
    return _layer_norm(alpha * x + mix, ln_g, ln_b)


def _sink_softmax(parts, sink):
    m = sink
    for s in parts:
        m = jnp.maximum(m, jnp.max(s, axis=-1, keepdims=True))
    es = [jnp.exp(s - m) for s in parts]
    den = jnp.exp(sink - m)
    for e in es:
        den = den + jnp.sum(e, axis=-1, keepdims=True)
    return [(e / den).astype(BF16) for e in es]


def _prompt_mixer_kernel(x_ref, win_ref, c_ref, sa_ref, sb_ref, sinks_ref, convw_ref, gattn_ref, gconv_ref,
                         wout_ref, lng_ref, lnb_ref,
                         h_ref, ko_ref, vo_ref, co_ref,
                         q_s, k_s, v_s, o_s, gc_s, *, alpha, n_heads, attn_w, kv_w, conv_ch):
    s = pl.program_id(1)
    last = pl.num_programs(1) - 1
    tq = x_ref.shape[0]
    q_per_kv = n_heads // N_KV_HEADS

    @pl.when(s == 0)
    def _():
        k_s[0:WINDOW, :] = jnp.zeros((WINDOW, kv_w), BF16)
        v_s[0:WINDOW, :] = jnp.zeros((WINDOW, kv_w), BF16)
        gc_s[...] = jnp.zeros(gc_s.shape, F32)

    @pl.when(s > 0)
    def _():
        k_s[0:WINDOW, :] = k_s[tq:tq + WINDOW, :]
        v_s[0:WINDOW, :] = v_s[tq:tq + WINDOW, :]

    x = x_ref[...]
    xb = x.astype(BF16)
    c, sa, sb = c_ref[...], sa_ref[...], sb_ref[...]

    def proj(lo, width):
        return jnp.dot(xb, win_ref[:, lo:lo + width], preferred_element_type=F32)

    scale = HEAD_DIM ** -0.5
    for j in range(attn_w // LANES):
        qj = _rope(proj(j * LANES, LANES), c, sa, sb)
        q_s[:, j * LANES:(j + 1) * LANES] = (qj * scale).astype(BF16)
    k = _rope(proj(attn_w, kv_w), c, sa, sb)
    v = proj(attn_w + kv_w, kv_w)
    k_s[WINDOW:WINDOW + tq, :] = k.astype(BF16)
    v_s[WINDOW:WINDOW + tq, :] = v.astype(BF16)

    @pl.when(s == last)
    def _():
        ko_ref[...] = k[tq - WINDOW:, :]
        vo_ref[...] = v[tq - WINDOW:, :]

    qi = lax.broadcasted_iota(jnp.int32, (WINDOW, 2 * WINDOW), 0)
    ci = lax.broadcasted_iota(jnp.int32, (WINDOW, 2 * WINDOW), 1)
    band = (ci > qi) & (ci <= qi + WINDOW)

    def sub_block(j, carry):
        r0 = pl.multiple_of(j * WINDOW, WINDOW)
        has_prev = (s * tq + r0) > 0
        mask = band & ((ci >= WINDOW) | has_prev)
        kk = k_s[pl.ds(r0, 2 * WINDOW), :]
        vv = v_s[pl.ds(r0, 2 * WINDOW), :]
        for hd in range(n_heads):
            kvh = hd // q_per_kv
            qh = q_s[pl.ds(r0, WINDOW), hd * HEAD_DIM:(hd + 1) * HEAD_DIM]
            kh = kk[:, kvh * HEAD_DIM:(kvh + 1) * HEAD_DIM]
            vh = vv[:, kvh * HEAD_DIM:(kvh + 1) * HEAD_DIM]
            sc = lax.dot_general(qh, kh, (((1,), (1,)), ((), ())), preferred_element_type=F32)
            sc = jnp.where(mask, sc, NEG_INF)
            (p,) = _sink_softmax([sc], sinks_ref[hd])
            o_s[pl.ds(r0, WINDOW), hd * HEAD_DIM:(hd + 1) * HEAD_DIM] = jnp.dot(
                p, vh, preferred_element_type=F32)
        return carry

    lax.fori_loop(0, tq // WINDOW, sub_block, 0)

    o3 = attn_w + 2 * kv_w
    gated = proj(o3 + 2 * conv_ch, conv_ch) * proj(o3, conv_ch)
    row = lax.broadcasted_iota(jnp.int32, (tq, 1), 0)
    conv_o = _short_conv(gated, gc_s[0:1, :], gc_s[1:2, :], row, convw_ref[...], proj(o3 + conv_ch, conv_ch))
    gc_s[0:CONV_K - 1, :] = gated[tq - (CONV_K - 1):, :]

    @pl.when(s == last)
    def _():
        co_ref[...] = gated[tq - (CONV_K - 1):, :]

    h_ref[...] = _merge_norm(x, o_s[...], conv_o, gattn_ref[...], gconv_ref[...], wout_ref,
                             lng_ref[...], lnb_ref[...], alpha)


def _prompt_mixer(x, win_b, tabs, sinks, conv_w, g_attn, g_conv, wout_b, ln_g, ln_b, *, alpha):
    bsz, seq, d = x.shape
    tq = PROMPT_TILE
    ns = seq // tq
    attn_w = g_attn.shape[-1]
    conv_ch = g_conv.shape[-1]
    n_heads = attn_w // HEAD_DIM
    kv_w = N_KV_HEADS * HEAD_DIM
    in_cols = win_b.shape[-1]
    const2 = lambda b, s: (0, 0)
    kern = functools.partial(_prompt_mixer_kernel, alpha=alpha, n_heads=n_heads, attn_w=attn_w, kv_w=kv_w,
                             conv_ch=conv_ch)
    return pl.pallas_call(
        kern,
        grid=(bsz, ns),
        in_specs=[
            pl.BlockSpec((None, tq, d), lambda b, s: (b, s, 0)),
            pl.BlockSpec((d, in_cols), const2),
            pl.BlockSpec((tq, LANES), lambda b, s: (s, 0)),
            pl.BlockSpec((tq, LANES), lambda b, s: (s, 0)),
            pl.BlockSpec((tq, LANES), lambda b, s: (s, 0)),
            pl.BlockSpec(memory_space=pltpu.SMEM),
            pl.BlockSpec((CONV_K, conv_ch), const2),
            pl.BlockSpec((1, attn_w), const2),
            pl.BlockSpec((1, conv_ch), const2),
            pl.BlockSpec((attn_w + conv_ch, d), const2),
            pl.BlockSpec((1, d), const2),
            pl.BlockSpec((1, d), const2),
        ],
        out_specs=[
            pl.BlockSpec((tq, d), lambda b, s: (b * ns + s, 0)),
            pl.BlockSpec((None, WINDOW, kv_w), lambda b, s: (b, 0, 0)),
            pl.BlockSpec((None, WINDOW, kv_w), lambda b, s: (b, 0, 0)),
            pl.BlockSpec((None, CONV_K - 1, conv_ch), lambda b, s: (b, 0, 0)),
        ],
        out_shape=[
            jax.ShapeDtypeStruct((bsz * seq, d), F32),
            jax.ShapeDtypeStruct((bsz, WINDOW, kv_w), F32),
            jax.ShapeDtypeStruct((bsz, WINDOW, kv_w), F32),
            jax.ShapeDtypeStruct((bsz, CONV_K - 1, conv_ch), F32),
        ],
        scratch_shapes=[
            pltpu.VMEM((tq, attn_w), BF16),
            pltpu.VMEM((tq + WINDOW, kv_w), BF16),
            pltpu.VMEM((tq + WINDOW, kv_w), BF16),
            pltpu.VMEM((tq, attn_w), F32),
            pltpu.VMEM((SUBLANES, conv_ch), F32),
        ],
        compiler_params=pltpu.CompilerParams(dimension_semantics=("arbitrary", "arbitrary"),
                                             vmem_limit_bytes=VMEM_LIMIT_BYTES),
        name="prompt_mixer",
    )(x, win_b, *tabs, sinks, conv_w, g_attn, g_conv, wout_b, ln_g, ln_b)


def _sample_mixer_kernel(x_ref, ck_ref, cv_ref, st_ref, win_ref, c_ref, sa_ref, sb_ref, sinks_ref,
                         convw_ref, gattn_ref, gconv_ref, wout_ref, lng_ref, lnb_ref,
                         h_ref, ko_ref, vo_ref, co_ref, *, alpha, n_heads, attn_w, kv_w, conv_ch, dec_seq):
    nb, win = ck_ref.shape[0], ck_ref.shape[1]
    rows = nb * dec_seq
    q_per_kv = n_heads // N_KV_HEADS
    x = x_ref[...]
    xb = x.astype(BF16)
    c, sa, sb = c_ref[...], sa_ref[...], sb_ref[...]

    def proj(lo, width):
        return jnp.dot(xb, win_ref[:, lo:lo + width], preferred_element_type=F32)

    scale = HEAD_DIM ** -0.5
    k = _rope(proj(attn_w, kv_w), c, sa, sb)
    v = proj(attn_w + kv_w, kv_w)
    k3 = k.reshape(nb, dec_seq, kv_w)
    v3 = v.reshape(nb, dec_seq, kv_w)
    ck = ck_ref[...]
    cv = cv_ref[...]
    ko_ref[:, 0:win - dec_seq, :] = ck[:, dec_seq:, :]
    ko_ref[:, win - dec_seq:, :] = k3
    vo_ref[:, 0:win - dec_seq, :] = cv[:, dec_seq:, :]
    vo_ref[:, win - dec_seq:, :] = v3
    ckb, cvb, k3b, v3b = ck.astype(BF16), cv.astype(BF16), k3.astype(BF16), v3.astype(BF16)

    qrows = q_per_kv * dec_seq
    qi = lax.broadcasted_iota(jnp.int32, (nb, qrows, win), 1) % dec_seq
    mask_c = lax.broadcasted_iota(jnp.int32, (nb, qrows, win), 2) > qi + (win - WINDOW)
    qn = lax.broadcasted_iota(jnp.int32, (nb, qrows, dec_seq), 1) % dec_seq
    mask_n = lax.broadcasted_iota(jnp.int32, (nb, qrows, dec_seq), 2) <= qn
    sink_row = lax.broadcasted_iota(jnp.int32, (nb, qrows, 1), 1) // dec_seq

    q_chunks = [_rope(proj(j * LANES, LANES), c, sa, sb) * scale for j in range(attn_w // LANES)]
    heads_out = []
    for kvh in range(N_KV_HEADS):
        qs = []
        for g in range(q_per_kv):
            lo = (kvh * q_per_kv + g) * HEAD_DIM
            qh = q_chunks[lo // LANES][:, lo % LANES:lo % LANES + HEAD_DIM]
            qs.append(qh.reshape(nb, dec_seq, HEAD_DIM))
        qg = jnp.concatenate(qs, axis=1).astype(BF16)
        sl = slice(kvh * HEAD_DIM, (kvh + 1) * HEAD_DIM)
        sc_c = jnp.einsum("bqd,bkd->bqk", qg, ckb[:, :, sl], preferred_element_type=F32)
        sc_n = jnp.einsum("bqd,bkd->bqk", qg, k3b[:, :, sl], preferred_element_type=F32)
        sc_c = jnp.where(mask_c, sc_c, NEG_INF)
        sc_n = jnp.where(mask_n, sc_n, NEG_INF)
        sink = jnp.zeros((nb, qrows, 1), F32)
        for g in range(q_per_kv):
            sink = jnp.where(sink_row == g, sinks_ref[kvh * q_per_kv + g], sink)
        p_c, p_n = _sink_softmax([sc_c, sc_n], sink)
        og = (jnp.einsum("bqk,bkd->bqd", p_c, cvb[:, :, sl], preferred_element_type=F32)
              + jnp.einsum("bqk,bkd->bqd", p_n, v3b[:, :, sl], preferred_element_type=F32))
        for g in range(q_per_kv):
            heads_out.append(og[:, g * dec_seq:(g + 1) * dec_seq, :].reshape(rows, HEAD_DIM))
    attn_o = jnp.concatenate(heads_out, axis=-1)

    o3 = attn_w + 2 * kv_w
    gated = proj(o3 + 2 * conv_ch, conv_ch) * proj(o3, conv_ch)
    st = st_ref[...]
    prev2 = jnp.broadcast_to(st[:, 0:1, :], (nb, dec_seq, conv_ch)).reshape(rows, conv_ch)
    prev1 = jnp.broadcast_to(st[:, 1:2, :], (nb, dec_seq, conv_ch)).reshape(rows, conv_ch)
    row = lax.broadcasted_iota(jnp.int32, (rows, 1), 0) % dec_seq
    conv_o = _short_conv(gated, prev2, prev1, row, convw_ref[...], proj(o3 + conv_ch, conv_ch))
    co_ref[...] = gated.reshape(nb, dec_seq, conv_ch)[:, dec_seq - (CONV_K - 1):, :]

    h_ref[...] = _merge_norm(x, attn_o, conv_o, gattn_ref[...], gconv_ref[...], wout_ref,
                             lng_ref[...], lnb_ref[...], alpha)


def _sample_mixer(x, ck, cv, st, win_b, tabs, sinks, conv_w, g_attn, g_conv, wout_b, ln_g, ln_b, *, alpha):
    dec_b, dec_seq, d = x.shape
    assert dec_seq >= CONV_K - 1 and dec_seq % SUBLANES == 0
    nb = SAMPLE_SEQS
    rows = nb * dec_seq
    win = ck.shape[1]
    attn_w = g_attn.shape[-1]
    conv_ch = g_conv.shape[-1]
    n_heads = attn_w // HEAD_DIM
    kv_w = N_KV_HEADS * HEAD_DIM
    in_cols = win_b.shape[-1]
    const2 = lambda i: (0, 0)
    kern = functools.partial(_sample_mixer_kernel, alpha=alpha, n_heads=n_heads, attn_w=attn_w, kv_w=kv_w,
                             conv_ch=conv_ch, dec_seq=dec_seq)
    return pl.pallas_call(
        kern,
        grid=(dec_b // nb,),
        in_specs=[
            pl.BlockSpec((rows, d), lambda i: (i, 0)),
            pl.BlockSpec((nb, win, kv_w), lambda i: (i, 0, 0)),
            pl.BlockSpec((nb, win, kv_w), lambda i: (i, 0, 0)),
            pl.BlockSpec((nb, CONV_K - 1, conv_ch), lambda i: (i, 0, 0)),
            pl.BlockSpec((d, in_cols), const2),
            pl.BlockSpec((rows, LANES), const2),
            pl.BlockSpec((rows, LANES), const2),
            pl.BlockSpec((rows, LANES), const2),
            pl.BlockSpec(memory_space=pltpu.SMEM),
            pl.BlockSpec((CONV_K, conv_ch), const2),
            pl.BlockSpec((1, attn_w), const2),
            pl.BlockSpec((1, conv_ch), const2),
            pl.BlockSpec((attn_w + conv_ch, d), const2),
            pl.BlockSpec((1, d), const2),
            pl.BlockSpec((1, d), const2),
        ],
        out_specs=[
            pl.BlockSpec((rows, d), lambda i: (i, 0)),
            pl.BlockSpec((nb, win, kv_w), lambda i: (i, 0, 0)),
            pl.BlockSpec((nb, win, kv_w), lambda i: (i, 0, 0)),
            pl.BlockSpec((nb, CONV_K - 1, conv_ch), lambda i: (i, 0, 0)),
        ],
        out_shape=[
            jax.ShapeDtypeStruct((dec_b * dec_seq, d), F32),
            jax.ShapeDtypeStruct((dec_b, win, kv_w), F32),
            jax.ShapeDtypeStruct((dec_b, win, kv_w), F32),
            jax.ShapeDtypeStruct((dec_b, CONV_K - 1, conv_ch), F32),
        ],
        compiler_params=pltpu.CompilerParams(dimension_semantics=("arbitrary",),
                                             vmem_limit_bytes=VMEM_LIMIT_BYTES),
        name="sample_mixer",
    )(x.reshape(dec_b * dec_seq, d), ck, cv, st, win_b, *tabs, sinks, conv_w, g_attn, g_conv, wout_b,
      ln_g, ln_b)


def _over_experts(fn, x):
    return fn(fn(x, axis=0, keepdims=True), axis=1, keepdims=True)


def _two_group_specs(tm, d, n_prompt_tiles):
    return [pl.BlockSpec((tm, d), lambda i, *_: (jnp.minimum(i, n_prompt_tiles - 1), 0)),
            pl.BlockSpec((tm, d), lambda i, *_: (jnp.maximum(i - n_prompt_tiles, 0), 0))]


def _route_kernel(hp_ref, hs_ref, rwt_ref, bias_ref, eidx_ref, rank_ref, gate_ref, cnt_ref, *, n_experts,
                  n_prompt_tiles):
    i = pl.program_id(0)
    tm = hp_ref.shape[0]
    per_group = n_experts // N_EXPERT_GROUPS
    shape3 = (N_EXPERT_GROUPS, per_group, tm)

    h = jnp.where(i < n_prompt_tiles, hp_ref[...], hs_ref[...])
    logits = lax.dot_general(rwt_ref[...], h.astype(BF16), (((1,), (1,)), ((), ())),
                             preferred_element_type=F32)
    scores = jax.nn.sigmoid(logits)
    sel = scores + bias_ref[...]
    scores3 = scores.reshape(shape3)
    grp = sel.reshape(shape3)
    member = lax.broadcasted_iota(jnp.int32, shape3, 1).astype(F32)
    group = lax.broadcasted_iota(jnp.int32, shape3, 0).astype(F32)
    expert = group * per_group + member

    m1 = jnp.max(grp, axis=1, keepdims=True)
    f1 = jnp.min(jnp.where(grp == m1, member, float(per_group)), axis=1, keepdims=True)
    m2 = jnp.max(jnp.where(member == f1, NEG_INF, grp), axis=1, keepdims=True)
    gscore = m1 + m2

    gid = lax.broadcasted_iota(jnp.int32, gscore.shape, 0).astype(F32)
    gmask = jnp.zeros(gscore.shape, F32)
    cur = gscore
    for _ in range(TOPK_GROUPS):
        mx = jnp.max(cur, axis=0, keepdims=True)
        pick = gid == jnp.min(jnp.where(cur == mx, gid, float(N_EXPERT_GROUPS)), axis=0, keepdims=True)
        gmask = jnp.where(pick, 1.0, gmask)
        cur = jnp.where(pick, NEG_INF, cur)

    cand = jnp.where(gmask > 0.0, grp, NEG_INF)
    chosen = jnp.zeros(shape3, F32)
    picks, firsts, weights = [], [], []
    for _ in range(TOP_K):
        mx = _over_experts(jnp.max, cand)
        first = _over_experts(jnp.min, jnp.where(cand == mx, expert, float(n_experts)))
        pick = expert == first
        picks.append(pick)
        firsts.append(first)
        weights.append(_over_experts(jnp.sum, jnp.where(pick, scores3, 0.0)))
        chosen = jnp.where(pick, 1.0, chosen)
        cand = jnp.where(pick, NEG_INF, cand)
    wsum = weights[0]
    for w in weights[1:]:
        wsum = wsum + w

    chosen_b = chosen.reshape(n_experts, tm).astype(BF16)
    earlier = (lax.broadcasted_iota(jnp.int32, (tm, tm), 0)
               < lax.broadcasted_iota(jnp.int32, (tm, tm), 1)).astype(BF16)
    before3 = jnp.dot(chosen_b, earlier, preferred_element_type=F32).reshape(shape3)

    pad = SUBLANES - TOP_K
    eidx = [f.reshape(1, tm).astype(jnp.int32) for f in firsts]
    rank = [_over_experts(jnp.sum, jnp.where(p, before3, 0.0)).reshape(1, tm).astype(jnp.int32) for p in picks]
    gate = [(w / wsum * ROUTED_SCALE).reshape(1, tm) for w in weights]
    eidx_ref[...] = jnp.concatenate(eidx + [jnp.zeros((pad, tm), jnp.int32)], axis=0)
    rank_ref[...] = jnp.concatenate(rank + [jnp.zeros((pad, tm), jnp.int32)], axis=0)
    gate_ref[...] = jnp.concatenate(gate + [jnp.zeros((pad, tm), F32)], axis=0)

    cnt_ref[...] = lax.dot_general(jnp.ones((SUBLANES, tm), BF16), chosen_b, (((1,), (1,)), ((), ())),
                                   preferred_element_type=F32)


def _route(h_p, h_s, rwt_b, bias_col):
    d = h_p.shape[1]
    t_all = h_p.shape[0] + h_s.shape[0]
    n_experts = rwt_b.shape[0]
    tm = MOE_TILE
    nt = t_all // tm
    npt = h_p.shape[0] // tm
    row_spec = pl.BlockSpec((SUBLANES, tm), lambda i: (0, i))
    return pl.pallas_call(
        functools.partial(_route_kernel, n_experts=n_experts, n_prompt_tiles=npt),
        grid=(nt,),
        in_specs=_two_group_specs(tm, d, npt) + [
            pl.BlockSpec((n_experts, d), lambda i: (0, 0)),
            pl.BlockSpec((n_experts, 1), lambda i: (0, 0)),
        ],
        out_specs=[row_spec, row_spec, row_spec, pl.BlockSpec((SUBLANES, n_experts), lambda i: (i, 0))],
        out_shape=[
            jax.ShapeDtypeStruct((SUBLANES, t_all), jnp.int32),
            jax.ShapeDtypeStruct((SUBLANES, t_all), jnp.int32),
            jax.ShapeDtypeStruct((SUBLANES, t_all), F32),
            jax.ShapeDtypeStruct((nt * SUBLANES, n_experts), F32),
        ],
        compiler_params=pltpu.CompilerParams(dimension_semantics=("arbitrary",),
                                             vmem_limit_bytes=VMEM_LIMIT_BYTES),
        name="route",
    )(h_p, h_s, rwt_b, bias_col)


def _for_each_part(rows, max_rows, fn):
    off = jnp.int32(0)
    size = max_rows
    while size >= SUBLANES:
        part = rows & size

        @pl.when(part != 0)
        def _(off=off, size=size):
            fn(pl.multiple_of(off, SUBLANES), size)

        off = off + part
        size //= 2


def _top_bit(n):
    return 1 << (n.bit_length() - 1)


def _local_rows(tm, n_experts):
    n = TOP_K * tm + n_experts * (SUBLANES - 1)
    return -(-n // SORT_CHUNK) * SORT_CHUNK


def _dispatch_kernel(cnt_ref, ls_ref, gs_ref, tot_ref, zrow_ref, lp_ref, hp_ref, hs_ref, xs_hbm,
                     lbuf, zero_s, sem, *, n_experts, blk, n_prompt_tiles):
    i = pl.program_id(0)
    tm, d = hp_ref.shape
    n = TOP_K * tm
    nl = lbuf.shape[1]
    slot = i % 2

    def tile_wait(tile):
        pltpu.make_async_copy(lbuf.at[0, pl.ds(0, n)], xs_hbm.at[pl.ds(0, n)], sem).wait()
        _for_each_part(tot_ref[tile] - n, _top_bit(n_experts * (SUBLANES - 1)), lambda off, size:
                       pltpu.make_async_copy(lbuf.at[0, pl.ds(0, size)], xs_hbm.at[pl.ds(0, size)], sem).wait())

    @pl.when(i == 0)
    def _():
        zero_s[...] = jnp.zeros(zero_s.shape, F32)

        def zcopy(e):
            return pltpu.make_async_copy(zero_s, xs_hbm.at[pl.ds(pl.multiple_of(zrow_ref[e], blk), blk)], sem)

        def start(e, c):
            zcopy(e).start()
            return c

        def wait(e, c):
            zcopy(e).wait()
            return c

        lax.fori_loop(0, n_experts, start, 0)
        lax.fori_loop(0, n_experts, wait, 0)

    hb = jnp.where(i < n_prompt_tiles, hp_ref[...], hs_ref[...]).astype(BF16)
    lp = lp_ref[...]
    for r in range(nl // SORT_CHUNK):
        rows = r * SORT_CHUNK + lax.broadcasted_iota(jnp.int32, (SORT_CHUNK, tm), 0)
        hit = jnp.zeros((SORT_CHUNK, tm), F32)
        for k in range(TOP_K):
            hit = jnp.where(rows == lp[k:k + 1, :], 1.0, hit)
        lbuf[slot, pl.ds(r * SORT_CHUNK, SORT_CHUNK), :] = jnp.dot(hit.astype(BF16), hb,
                                                                   preferred_element_type=F32)

    @pl.when(i > 0)
    def _():
        tile_wait(i - 1)

    def per_expert(e, c):
        idx = i * n_experts + e
        l0, g0 = ls_ref[idx], gs_ref[idx]
        _for_each_part(cnt_ref[idx], tm, lambda off, size: pltpu.make_async_copy(
            lbuf.at[slot, pl.ds(pl.multiple_of(l0 + off, SUBLANES), size)],
            xs_hbm.at[pl.ds(pl.multiple_of(g0 + off, SUBLANES), size)], sem).start())
        return c

    lax.fori_loop(0, n_experts, per_expert, 0)

    @pl.when(i == pl.num_programs(0) - 1)
    def _():
        tile_wait(i)


def _dispatch(tables, zrow, lp, h_p, h_s, *, n_rows, blk):
    d = h_p.shape[1]
    t_all = h_p.shape[0] + h_s.shape[0]
    tm = MOE_TILE
    npt = h_p.shape[0] // tm
    n_experts = zrow.shape[0]
    return pl.pallas_call(
        functools.partial(_dispatch_kernel, n_experts=n_experts, blk=blk, n_prompt_tiles=npt),
        grid_spec=pltpu.PrefetchScalarGridSpec(
            num_scalar_prefetch=5,
            grid=(t_all // tm,),
            in_specs=[pl.BlockSpec((SUBLANES, tm), lambda i, *_: (0, i))] + _two_group_specs(tm, d, npt),
            out_specs=pl.BlockSpec(memory_space=pl.ANY),
            scratch_shapes=[
                pltpu.VMEM((2, _local_rows(tm, n_experts), d), F32),
                pltpu.VMEM((blk, d), F32),
                pltpu.SemaphoreType.DMA,
            ],
        ),
        out_shape=jax.ShapeDtypeStruct((n_rows, d), F32),
        compiler_params=pltpu.CompilerParams(dimension_semantics=("arbitrary",),
                                             vmem_limit_bytes=VMEM_LIMIT_BYTES),
        name="dispatch",
    )(*tables, zrow, lp, h_p, h_s)


def _silu(x):
    return x * jax.nn.sigmoid(x)


def _expert_kernel(be_ref, nact_ref, xs_ref, wg_ref, wu_ref, wd_ref, ys_ref, wg_s, wu_s, wd_s):
    b = pl.program_id(0)

    @pl.when(b < nact_ref[0])
    def _():
        @pl.when((b == 0) | (be_ref[b] != be_ref[jnp.maximum(b - 1, 0)]))
        def _():
            wg_s[...] = wg_ref[...].astype(BF16)
            wu_s[...] = wu_ref[...].astype(BF16)
            wd_s[...] = wd_ref[...].astype(BF16)

        xb = xs_ref[...].astype(BF16)
        hid = _silu(jnp.dot(xb, wg_s[...], preferred_element_type=F32)) * jnp.dot(
            xb, wu_s[...], preferred_element_type=F32)
        ys_ref[...] = jnp.dot(hid.astype(BF16), wd_s[...], preferred_element_type=F32)


def _experts(block_expert, nact, xs, w_gate, w_up, w_down, *, blk):
    n_rows, d = xs.shape
    ff = w_gate.shape[-1]
    n_blocks = n_rows // blk

    def active(b, be, na):
        return jnp.minimum(b, na[0] - 1)

    return pl.pallas_call(
        _expert_kernel,
        grid_spec=pltpu.PrefetchScalarGridSpec(
            num_scalar_prefetch=2,
            grid=(n_blocks,),
            in_specs=[
                pl.BlockSpec((blk, d), lambda b, be, na: (active(b, be, na), 0)),
                pl.BlockSpec((None, d, ff), lambda b, be, na: (be[active(b, be, na)], 0, 0)),
                pl.BlockSpec((None, d, ff), lambda b, be, na: (be[active(b, be, na)], 0, 0)),
                pl.BlockSpec((None, ff, d), lambda b, be, na: (be[active(b, be, na)], 0, 0)),
            ],
            out_specs=pl.BlockSpec((blk, d), lambda b, be, na: (active(b, be, na), 0)),
            scratch_shapes=[
                pltpu.VMEM((d, ff), BF16),
                pltpu.VMEM((d, ff), BF16),
                pltpu.VMEM((ff, d), BF16),
            ],
        ),
        out_shape=jax.ShapeDtypeStruct((n_rows, d), F32),
        compiler_params=pltpu.CompilerParams(dimension_semantics=("arbitrary",),
                                             vmem_limit_bytes=VMEM_LIMIT_BYTES),
        name="experts",
    )(block_expert, nact, xs, w_gate, w_up, w_down)


def _combine_kernel(cnt_ref, ls_ref, gs_ref, tot_ref, ys_hbm, lpt_ref, gate_ref, hp_ref, hs_ref, wsg_ref, wsu_ref,
                    wsd_ref, lng_ref, lnb_ref, yp_ref, ysm_ref, ybuf, sems, *, alpha, n_experts, n_prompt_tiles):
    i = pl.program_id(0)
    tm, d = hp_ref.shape
    n = TOP_K * tm
    nl = ybuf.shape[1]
    slot = i % 2

    def fetch(tile, to):
        def per_expert(e, c):
            idx = tile * n_experts + e
            l0, g0 = ls_ref[idx], gs_ref[idx]
            _for_each_part(cnt_ref[idx], tm, lambda off, size: pltpu.make_async_copy(
                ys_hbm.at[pl.ds(pl.multiple_of(g0 + off, SUBLANES), size)],
                ybuf.at[to, pl.ds(pl.multiple_of(l0 + off, SUBLANES), size)], sems.at[to]).start())
            return c

        lax.fori_loop(0, n_experts, per_expert, 0)

    @pl.when(i == 0)
    def _():
        ybuf[...] = jnp.zeros(ybuf.shape, F32)
        fetch(0, 0)

    @pl.when(i + 1 < pl.num_programs(0))
    def _():
        fetch(i + 1, 1 - slot)

    h = jnp.where(i < n_prompt_tiles, hp_ref[...], hs_ref[...])
    hb = h.astype(BF16)
    hid = _silu(jnp.dot(hb, wsg_ref[...], preferred_element_type=F32)) * jnp.dot(
        hb, wsu_ref[...], preferred_element_type=F32)
    shared = jnp.dot(hid.astype(BF16), wsd_ref[...], preferred_element_type=F32)

    pltpu.make_async_copy(ys_hbm.at[pl.ds(0, n)], ybuf.at[slot, pl.ds(0, n)], sems.at[slot]).wait()
    _for_each_part(tot_ref[i] - n, _top_bit(n_experts * (SUBLANES - 1)), lambda off, size: pltpu.make_async_copy(
        ys_hbm.at[pl.ds(0, size)], ybuf.at[slot, pl.ds(0, size)], sems.at[slot]).wait())

    lpt = lpt_ref[...]
    gate = gate_ref[...]
    routed = jnp.zeros((tm, d), F32)
    for r in range(nl // SORT_CHUNK):
        cols = r * SORT_CHUNK + lax.broadcasted_iota(jnp.int32, (tm, SORT_CHUNK), 1)
        g = jnp.zeros((tm, SORT_CHUNK), F32)
        for k in range(TOP_K):
            g = jnp.where(cols == lpt[:, k:k + 1], gate[:, k:k + 1], g)
        y = ybuf[slot, pl.ds(r * SORT_CHUNK, SORT_CHUNK), :]
        yb = y.astype(BF16)
        g_hi = g.astype(BF16)
        g_lo = (g - g_hi.astype(F32)).astype(BF16)
        routed = routed + (jnp.dot(g_hi, yb, preferred_element_type=F32)
                           + jnp.dot(g_lo, yb, preferred_element_type=F32))
    y = _layer_norm(alpha * h + (routed + shared), lng_ref[...], lnb_ref[...])

    @pl.when(i < n_prompt_tiles)
    def _():
        yp_ref[...] = y

    @pl.when(i >= n_prompt_tiles)
    def _():
        ysm_ref[...] = y


def _combine(tables, ys, lp_t, gates_t, h_p, h_s, wsg_b, wsu_b, wsd_b, ln_g, ln_b, *, alpha):
    t_prompt, d = h_p.shape
    t_all = t_prompt + h_s.shape[0]
    tm = MOE_TILE
    ff = wsg_b.shape[-1]
    npt = t_prompt // tm
    n_experts = tables[0].shape[0] // (t_all // tm)
    const2 = lambda i, *_: (0, 0)
    return pl.pallas_call(
        functools.partial(_combine_kernel, alpha=alpha, n_experts=n_experts, n_prompt_tiles=npt),
        grid_spec=pltpu.PrefetchScalarGridSpec(
            num_scalar_prefetch=4,
            grid=(t_all // tm,),
            in_specs=[
                pl.BlockSpec(memory_space=pl.ANY),
                pl.BlockSpec((tm, SUBLANES), lambda i, *_: (i, 0)),
                pl.BlockSpec((tm, SUBLANES), lambda i, *_: (i, 0)),
                *_two_group_specs(tm, d, npt),
                pl.BlockSpec((d, ff), const2),
                pl.BlockSpec((d, ff), const2),
                pl.BlockSpec((ff, d), const2),
                pl.BlockSpec((1, d), const2),
                pl.BlockSpec((1, d), const2),
            ],
            out_specs=[
                pl.BlockSpec((tm, d), lambda i, *_: (jnp.minimum(i, npt - 1), 0)),
                pl.BlockSpec((tm, d), lambda i, *_: (jnp.maximum(i - npt, 0), 0)),
            ],
            scratch_shapes=[
                pltpu.VMEM((2, _local_rows(tm, n_experts), d), F32),
                pltpu.SemaphoreType.DMA((2,)),
            ],
        ),
        out_shape=[
            jax.ShapeDtypeStruct((t_prompt, d), F32),
            jax.ShapeDtypeStruct((t_all - t_prompt, d), F32),
        ],
        compiler_params=pltpu.CompilerParams(dimension_semantics=("arbitrary",),
                                             vmem_limit_bytes=VMEM_LIMIT_BYTES),
        name="combine",
    )(*tables, ys, lp_t, gates_t, h_p, h_s, wsg_b, wsu_b, wsd_b, ln_g, ln_b)


def _moe(h_p, h_s, router_w, router_bias, w_gate, w_up, w_down, ws_gate, ws_up, ws_down, ln_g, ln_b, *, alpha):
    t_all = h_p.shape[0] + h_s.shape[0]
    n_experts = router_w.shape[-1]
    blk = EXPERT_BLOCK
    tm = MOE_TILE
    nt = t_all // tm
    eidx8, lrank8, gate8, cnt8 = _route(h_p, h_s, router_w.T.astype(BF16), router_bias.reshape(n_experts, 1))

    cnt = cnt8.reshape(nt, SUBLANES, n_experts)[:, 0, :].astype(jnp.int32)
    cnt = (cnt + SUBLANES - 1) // SUBLANES * SUBLANES
    counts = jnp.sum(cnt, axis=0)
    blocks_per_e = (counts + blk - 1) // blk
    block_end = jnp.cumsum(blocks_per_e)
    pad_start = (block_end - blocks_per_e) * blk
    n_blocks = -(-(t_all * TOP_K + nt * n_experts * (SUBLANES - 1)) // blk) + n_experts
    block_expert = jnp.minimum(jnp.sum(block_end[None, :] <= jnp.arange(n_blocks)[:, None], axis=1),
                               n_experts - 1).astype(jnp.int32)
    nact = block_end[-1:].astype(jnp.int32)
    zrow = (jnp.maximum(block_end - 1, 0) * blk).astype(jnp.int32)
    gstart = pad_start[None, :] + jnp.cumsum(cnt, axis=0) - cnt
    lstart = jnp.cumsum(cnt, axis=1) - cnt
    tables = tuple(a.reshape(-1).astype(jnp.int32) for a in (cnt, lstart, gstart, jnp.sum(cnt, axis=1)))
    lstart_tok = jnp.repeat(lstart, tm, axis=0)
    lp8 = jnp.sum(jnp.where(eidx8[..., None] == jnp.arange(n_experts), lstart_tok[None], 0), axis=-1) + lrank8
    lp8 = lp8.astype(jnp.int32)

    xs = _dispatch(tables, zrow, lp8, h_p, h_s, n_rows=n_blocks * blk, blk=blk)
    ys = _experts(block_expert, nact, xs, w_gate, w_up, w_down, blk=blk)
    return _combine(tables, ys, lp8.T, gate8.T, h_p, h_s, ws_gate.astype(BF16), ws_up.astype(BF16), ws_down.astype(BF16),
                    ln_g, ln_b, alpha=alpha)


def kernel(x_prompt, x_sample, cache_k, cache_v, state_conv, w_in, attn_sinks, conv_w, g_attn_out, g_conv_out, w_out, ln1_g, ln1_b, router_w, router_bias, w_gate, w_up, w_down, ws_gate, ws_up, ws_down, ln2_g, ln2_b):
    depth = w_in.shape[0]
    bsz, seq, d = x_prompt.shape
    dec_b, dec_seq, _ = x_sample.shape
    win = cache_k.shape[2]
    kv_w = N_KV_HEADS * HEAD_DIM
    t_prompt = bsz * seq
    t_all = t_prompt + dec_b * dec_seq
    alpha = (2.0 * depth) ** 0.25
    assert win == WINDOW and seq % PROMPT_TILE == 0 and dec_b % SAMPLE_SEQS == 0
    assert t_prompt % MOE_TILE == 0 and (t_all - t_prompt) % MOE_TILE == 0
    assert MOE_TILE & (MOE_TILE - 1) == 0 and (TOP_K * MOE_TILE) % SORT_CHUNK == 0

    tabs_p = _rope_tables(jnp.arange(seq))
    tabs_s = tuple(jnp.tile(t, (SAMPLE_SEQS, 1)) for t in _rope_tables(PAST_LEN + jnp.arange(dec_seq)))
    row = lambda a: a.reshape(1, -1)

    xp, xs = x_prompt, x_sample
    outs = [[] for _ in range(6)]
    for l in range(depth):
        win_b, wout_b = w_in[l].astype(BF16), w_out[l].astype(BF16)
        shared = (attn_sinks[l], conv_w[l], row(g_attn_out[l]), row(g_conv_out[l]), wout_b, row(ln1_g[l]),
                  row(ln1_b[l]))
        h_p, kp, vp, cp = _prompt_mixer(xp, win_b, tabs_p, *shared, alpha=alpha)
        h_s, kn, vn, cn = _sample_mixer(xs, cache_k[l].reshape(dec_b, win, kv_w),
                                        cache_v[l].reshape(dec_b, win, kv_w), state_conv[l], win_b, tabs_s,
                                        *shared, alpha=alpha)
        yp, ys = _moe(h_p, h_s, router_w[l], router_bias[l], w_gate[l], w_up[l], w_down[l], ws_gate[l], ws_up[l],
                      ws_down[l], row(ln2_g[l]), row(ln2_b[l]), alpha=alpha)
        xp, xs = yp.reshape(bsz, seq, d), ys.reshape(dec_b, dec_seq, d)
        heads = lambda a: a.reshape(a.shape[0], win, N_KV_HEADS, HEAD_DIM)
        for o, a in zip(outs, (heads(kp), heads(vp), cp, heads(kn), heads(vn), cn)):
            o.append(a)
    return (xp, xs) + tuple(jnp.stack(o, axis=0) for o in outs)
```

```python
import functools

import jax
import jax.numpy as jnp
from jax import lax
from jax.experimental import pallas as pl
from jax.experimental.pallas import tpu as pltpu

PAST_LEN = 16384
WINDOW = 128
HEAD_DIM = 64
N_KV_HEADS = 2
ROT_DIM = HEAD_DIM // 4
ROPE_THETA = 500000.0
CONV_K = 3
TOP_K = 6
N_EXPERT_GROUPS = 8
TOPK_GROUPS = 4
ROUTED_SCALE = 2.5
LN_EPS = 1e-5
RMS_EPS = 1e-6

LANES = 128
SUBLANES = 8
VMEM_LIMIT_BYTES = 56 * 1024 * 1024

PROMPT_TILE = 512
SAMPLE_SEQS = 16
MOE_TILE = 256
SORT_CHUNK = 256
EXPERT_BLOCK = 512

F32 = jnp.float32
BF16 = jnp.bfloat16
NEG_INF = float("-inf")


def _rope_tables(positions):
    half = ROT_DIM // 2
    inv_freq = ROPE_THETA ** (-jnp.arange(0, ROT_DIM, 2, dtype=F32) / ROT_DIM)
    ang = positions.astype(F32)[:, None] * inv_freq[None, :]
    cos, sin = jnp.cos(ang), jnp.sin(ang)
    n = positions.shape[0]
    rest = HEAD_DIM - ROT_DIM
    c = jnp.concatenate([cos, cos, jnp.ones((n, rest), F32)], axis=-1)
    sa = jnp.concatenate([-sin, jnp.zeros((n, half + rest), F32)], axis=-1)
    sb = jnp.concatenate([jnp.zeros((n, half), F32), sin, jnp.zeros((n, rest), F32)], axis=-1)
    reps = LANES // HEAD_DIM
    return jnp.tile(c, (1, reps)), jnp.tile(sa, (1, reps)), jnp.tile(sb, (1, reps))


def _rope(x, c, sa, sb):
    half = ROT_DIM // 2
    return x * c + pltpu.roll(x, LANES - half, 1) * sa + pltpu.roll(x, half, 1) * sb


def _rms_norm(x, g):
    return x * lax.rsqrt(jnp.mean(jnp.square(x), axis=-1, keepdims=True) + RMS_EPS) * g


def _layer_norm(x, g, b):
    mu = jnp.mean(x, axis=-1, keepdims=True)
    var = jnp.mean(jnp.square(x - mu), axis=-1, keepdims=True)
    return (x - mu) * lax.rsqrt(var + LN_EPS) * g + b


def _short_conv(gated, prev2, prev1, row, conv_w, b_gate):
    g1 = pltpu.roll(gated, 1, 0)
    g2 = pltpu.roll(gated, 2, 0)
    g1 = jnp.where(row == 0, prev1, g1)
    g2 = jnp.where(row == 0, prev2, jnp.where(row == 1, prev1, g2))
    y = conv_w[0:1, :] * g2 + conv_w[1:2, :] * g1 + conv_w[2:3, :] * gated
    return b_gate * y


def _merge_norm(x, attn_o, conv_o, gattn, gconv, wout_ref, ln_g, ln_b, alpha):
    cat = jnp.concatenate([_rms_norm(attn_o, gattn), _rms_norm(conv_o, gconv)], axis=-1)
    mix = jnp.dot(cat.astype(BF16), wout_ref[...], preferred_element_type=F32)
    return _layer_norm(alpha * x + mix, ln_g, ln_b)


def _sink_softmax(parts, sink):
    m = sink
    for s in parts:
        m = jnp.maximum(m, jnp.max(s, axis=-1, keepdims=True))
    es = [jnp.exp(s - m) for s in parts]
    den = jnp.exp(sink - m)
    for e in es:
        den = den + jnp.sum(e, axis=-1, keepdims=True)
    return [(e / den).astype(BF16) for e in es]


def _prompt_mixer_kernel(x_ref, win_ref, c_ref, sa_ref, sb_ref, sinks_ref, convw_ref, gattn_ref, gconv_ref,
                         wout_ref, lng_ref, lnb_ref,
                         h_ref, ko_ref, vo_ref, co_ref,
                         q_s, k_s, v_s, o_s, gc_s, *, alpha, n_heads, attn_w, kv_w, conv_ch):
    s = pl.program_id(1)
    last = pl.num_programs(1) - 1
    tq = x_ref.shape[0]
    q_per_kv = n_heads // N_KV_HEADS

    @pl.when(s == 0)
    def _():
        k_s[0:WINDOW, :] = jnp.zeros((WINDOW, kv_w), BF16)
        v_s[0:WINDOW, :] = jnp.zeros((WINDOW, kv_w), BF16)
        gc_s[...] = jnp.zeros(gc_s.shape, F32)

    @pl.when(s > 0)
    def _():
        k_s[0:WINDOW, :] = k_s[tq:tq + WINDOW, :]
        v_s[0:WINDOW, :] = v_s[tq:tq + WINDOW, :]

    x = x_ref[...]
    xb = x.astype(BF16)
    c, sa, sb = c_ref[...], sa_ref[...], sb_ref[...]

    def proj(lo, width):
        return jnp.dot(xb, win_ref[:, lo:lo + width], preferred_element_type=F32)

    scale = HEAD_DIM ** -0.5
    for j in range(attn_w // LANES):
        qj = _rope(proj(j * LANES, LANES), c, sa, sb)
        q_s[:, j * LANES:(j + 1) * LANES] = (qj * scale).astype(BF16)
    k = _rope(proj(attn_w, kv_w), c, sa, sb)
    v = proj(attn_w + kv_w, kv_w)
    k_s[WINDOW:WINDOW + tq, :] = k.astype(BF16)
    v_s[WINDOW:WINDOW + tq, :] = v.astype(BF16)

    @pl.when(s == last)
    def _():
        ko_ref[...] = k[tq - WINDOW:, :]
        vo_ref[...] = v[tq - WINDOW:, :]

    qi = lax.broadcasted_iota(jnp.int32, (WINDOW, 2 * WINDOW), 0)
    ci = lax.broadcasted_iota(jnp.int32, (WINDOW, 2 * WINDOW), 1)
    band = (ci > qi) & (ci <= qi + WINDOW)

    def sub_block(j, carry):
        r0 = pl.multiple_of(j * WINDOW, WINDOW)
        has_prev = (s * tq + r0) > 0
        mask = band & ((ci >= WINDOW) | has_prev)
        kk = k_s[pl.ds(r0, 2 * WINDOW), :]
        vv = v_s[pl.ds(r0, 2 * WINDOW), :]
        for hd in range(n_heads):
            kvh = hd // q_per_kv
            qh = q_s[pl.ds(r0, WINDOW), hd * HEAD_DIM:(hd + 1) * HEAD_DIM]
            kh = kk[:, kvh * HEAD_DIM:(kvh + 1) * HEAD_DIM]
            vh = vv[:, kvh * HEAD_DIM:(kvh + 1) * HEAD_DIM]
            sc = lax.dot_general(qh, kh, (((1,), (1,)), ((), ())), preferred_element_type=F32)
            sc = jnp.where(mask, sc, NEG_INF)
            (p,) = _sink_softmax([sc], sinks_ref[hd])
            o_s[pl.ds(r0, WINDOW), hd * HEAD_DIM:(hd + 1) * HEAD_DIM] = jnp.dot(
                p, vh, preferred_element_type=F32)
        return carry

    lax.fori_loop(0, tq // WINDOW, sub_block, 0)

    o3 = attn_w + 2 * kv_w
    gated = proj(o3 + 2 * conv_ch, conv_ch) * proj(o3, conv_ch)
    row = lax.broadcasted_iota(jnp.int32, (tq, 1), 0)
    conv_o = _short_conv(gated, gc_s[0:1, :], gc_s[1:2, :], row, convw_ref[...], proj(o3 + conv_ch, conv_ch))
    gc_s[0:CONV_K - 1, :] = gated[tq - (CONV_K - 1):, :]

    @pl.when(s == last)
    def _():
        co_ref[...] = gated[tq - (CONV_K - 1):, :]

    h_ref[...] = _merge_norm(x, o_s[...], conv_o, gattn_ref[...], gconv_ref[...], wout_ref,
                             lng_ref[...], lnb_ref[...], alpha)


def _prompt_mixer(x, win_b, tabs, sinks, conv_w, g_attn, g_conv, wout_b, ln_g, ln_b, *, alpha):
    bsz, seq, d = x.shape
    tq = PROMPT_TILE
    ns = seq // tq
    attn_w = g_attn.shape[-1]
    conv_ch = g_conv.shape[-1]
    n_heads = attn_w // HEAD_DIM
    kv_w = N_KV_HEADS * HEAD_DIM
    in_cols = win_b.shape[-1]
    const2 = lambda b, s: (0, 0)
    kern = functools.partial(_prompt_mixer_kernel, alpha=alpha, n_heads=n_heads, attn_w=attn_w, kv_w=kv_w,
                             conv_ch=conv_ch)
    return pl.pallas_call(
        kern,
        grid=(bsz, ns),
        in_specs=[
            pl.BlockSpec((None, tq, d), lambda b, s: (b, s, 0)),
            pl.BlockSpec((d, in_cols), const2),
            pl.BlockSpec((tq, LANES), lambda b, s: (s, 0)),
            pl.BlockSpec((tq, LANES), lambda b, s: (s, 0)),
            pl.BlockSpec((tq, LANES), lambda b, s: (s, 0)),
            pl.BlockSpec(memory_space=pltpu.SMEM),
            pl.BlockSpec((CONV_K, conv_ch), const2),
            pl.BlockSpec((1, attn_w), const2),
            pl.BlockSpec((1, conv_ch), const2),
            pl.BlockSpec((attn_w + conv_ch, d), const2),
            pl.BlockSpec((1, d), const2),
            pl.BlockSpec((1, d), const2),
        ],
        out_specs=[
            pl.BlockSpec((tq, d), lambda b, s: (b * ns + s, 0)),
            pl.BlockSpec((None, WINDOW, kv_w), lambda b, s: (b, 0, 0)),
            pl.BlockSpec((None, WINDOW, kv_w), lambda b, s: (b, 0, 0)),
            pl.BlockSpec((None, CONV_K - 1, conv_ch), lambda b, s: (b, 0, 0)),
        ],
        out_shape=[
            jax.ShapeDtypeStruct((bsz * seq, d), F32),
            jax.ShapeDtypeStruct((bsz, WINDOW, kv_w), F32),
            jax.ShapeDtypeStruct((bsz, WINDOW, kv_w), F32),
            jax.ShapeDtypeStruct((bsz, CONV_K - 1, conv_ch), F32),
        ],
        scratch_shapes=[
            pltpu.VMEM((tq, attn_w), BF16),
            pltpu.VMEM((tq + WINDOW, kv_w), BF16),
            pltpu.VMEM((tq + WINDOW, kv_w), BF16),
            pltpu.VMEM((tq, attn_w), F32),
            pltpu.VMEM((SUBLANES, conv_ch), F32),
        ],
        compiler_params=pltpu.CompilerParams(dimension_semantics=("arbitrary", "arbitrary"),
                                             vmem_limit_bytes=VMEM_LIMIT_BYTES),
        name="prompt_mixer",
    )(x, win_b, *tabs, sinks, conv_w, g_attn, g_conv, wout_b, ln_g, ln_b)


def _sample_mixer_kernel(x_ref, ck_ref, cv_ref, st_ref, win_ref, c_ref, sa_ref, sb_ref, sinks_ref,
                         convw_ref, gattn_ref, gconv_ref, wout_ref, lng_ref, lnb_ref,
                         h_ref, ko_ref, vo_ref, co_ref, *, alpha, n_heads, attn_w, kv_w, conv_ch, dec_seq):
    nb, win = ck_ref.shape[0], ck_ref.shape[1]
    rows = nb * dec_seq
    q_per_kv = n_heads // N_KV_HEADS
    x = x_ref[...]
    xb = x.astype(BF16)
    c, sa, sb = c_ref[...], sa_ref[...], sb_ref[...]

    def proj(lo, width):
        return jnp.dot(xb, win_ref[:, lo:lo + width], preferred_element_type=F32)

    scale = HEAD_DIM ** -0.5
    k = _rope(proj(attn_w, kv_w), c, sa, sb)
    v = proj(attn_w + kv_w, kv_w)
    k3 = k.reshape(nb, dec_seq, kv_w)
    v3 = v.reshape(nb, dec_seq, kv_w)
    ck = ck_ref[...]
    cv = cv_ref[...]
    ko_ref[:, 0:win - dec_seq, :] = ck[:, dec_seq:, :]
    ko_ref[:, win - dec_seq:, :] = k3
    vo_ref[:, 0:win - dec_seq, :] = cv[:, dec_seq:, :]
    vo_ref[:, win - dec_seq:, :] = v3
    ckb, cvb, k3b, v3b = ck.astype(BF16), cv.astype(BF16), k3.astype(BF16), v3.astype(BF16)

    qrows = q_per_kv * dec_seq
    qi = lax.broadcasted_iota(jnp.int32, (nb, qrows, win), 1) % dec_seq
    mask_c = lax.broadcasted_iota(jnp.int32, (nb, qrows, win), 2) > qi + (win - WINDOW)
    qn = lax.broadcasted_iota(jnp.int32, (nb, qrows, dec_seq), 1) % dec_seq
    mask_n = lax.broadcasted_iota(jnp.int32, (nb, qrows, dec_seq), 2) <= qn
    sink_row = lax.broadcasted_iota(jnp.int32, (nb, qrows, 1), 1) // dec_seq

    q_chunks = [_rope(proj(j * LANES, LANES), c, sa, sb) * scale for j in range(attn_w // LANES)]
    heads_out = []
    for kvh in range(N_KV_HEADS):
        qs = []
        for g in range(q_per_kv):
            lo = (kvh * q_per_kv + g) * HEAD_DIM
            qh = q_chunks[lo // LANES][:, lo % LANES:lo % LANES + HEAD_DIM]
            qs.append(qh.reshape(nb, dec_seq, HEAD_DIM))
        qg = jnp.concatenate(qs, axis=1).astype(BF16)
        sl = slice(kvh * HEAD_DIM, (kvh + 1) * HEAD_DIM)
        sc_c = jnp.einsum("bqd,bkd->bqk", qg, ckb[:, :, sl], preferred_element_type=F32)
        sc_n = jnp.einsum("bqd,bkd->bqk", qg, k3b[:, :, sl], preferred_element_type=F32)
        sc_c = jnp.where(mask_c, sc_c, NEG_INF)
        sc_n = jnp.where(mask_n, sc_n, NEG_INF)
        sink = jnp.zeros((nb, qrows, 1), F32)
        for g in range(q_per_kv):
            sink = jnp.where(sink_row == g, sinks_ref[kvh * q_per_kv + g], sink)
        p_c, p_n = _sink_softmax([sc_c, sc_n], sink)
        og = (jnp.einsum("bqk,bkd->bqd", p_c, cvb[:, :, sl], preferred_element_type=F32)
              + jnp.einsum("bqk,bkd->bqd", p_n, v3b[:, :, sl], preferred_element_type=F32))
        for g in range(q_per_kv):
            heads_out.append(og[:, g * dec_seq:(g + 1) * dec_seq, :].reshape(rows, HEAD_DIM))
    attn_o = jnp.concatenate(heads_out, axis=-1)

    o3 = attn_w + 2 * kv_w
    gated = proj(o3 + 2 * conv_ch, conv_ch) * proj(o3, conv_ch)
    st = st_ref[...]
    prev2 = jnp.broadcast_to(st[:, 0:1, :], (nb, dec_seq, conv_ch)).reshape(rows, conv_ch)
    prev1 = jnp.broadcast_to(st[:, 1:2, :], (nb, dec_seq, conv_ch)).reshape(rows, conv_ch)
    row = lax.broadcasted_iota(jnp.int32, (rows, 1), 0) % dec_seq
    conv_o = _short_conv(gated, prev2, prev1, row, convw_ref[...], proj(o3 + conv_ch, conv_ch))
    co_ref[...] = gated.reshape(nb, dec_seq, conv_ch)[:, dec_seq - (CONV_K - 1):, :]

    h_ref[...] = _merge_norm(x, attn_o, conv_o, gattn_ref[...], gconv_ref[...], wout_ref,
                             lng_ref[...], lnb_ref[...], alpha)


def _sample_mixer(x, ck, cv, st, win_b, tabs, sinks, conv_w, g_attn, g_conv, wout_b, ln_g, ln_b, *, alpha):
    dec_b, dec_seq, d = x.shape
    assert dec_seq >= CONV_K - 1 and dec_seq % SUBLANES == 0
    nb = SAMPLE_SEQS
    rows = nb * dec_seq
    win = ck.shape[1]
    attn_w = g_attn.shape[-1]
    conv_ch = g_conv.shape[-1]
    n_heads = attn_w // HEAD_DIM
    kv_w = N_KV_HEADS * HEAD_DIM
    in_cols = win_b.shape[-1]
    const2 = lambda i: (0, 0)
    kern = functools.partial(_sample_mixer_kernel, alpha=alpha, n_heads=n_heads, attn_w=attn_w, kv_w=kv_w,
                             conv_ch=conv_ch, dec_seq=dec_seq)
    return pl.pallas_call(
        kern,
        grid=(dec_b // nb,),
        in_specs=[
            pl.BlockSpec((rows, d), lambda i: (i, 0)),
            pl.BlockSpec((nb, win, kv_w), lambda i: (i, 0, 0)),
            pl.BlockSpec((nb, win, kv_w), lambda i: (i, 0, 0)),
            pl.BlockSpec((nb, CONV_K - 1, conv_ch), lambda i: (i, 0, 0)),
            pl.BlockSpec((d, in_cols), const2),
            pl.BlockSpec((rows, LANES), const2),
            pl.BlockSpec((rows, LANES), const2),
            pl.BlockSpec((rows, LANES), const2),
            pl.BlockSpec(memory_space=pltpu.SMEM),
            pl.BlockSpec((CONV_K, conv_ch), const2),
            pl.BlockSpec((1, attn_w), const2),
            pl.BlockSpec((1, conv_ch), const2),
            pl.BlockSpec((attn_w + conv_ch, d), const2),
            pl.BlockSpec((1, d), const2),
            pl.BlockSpec((1, d), const2),
        ],
        out_specs=[
            pl.BlockSpec((rows, d), lambda i: (i, 0)),
            pl.BlockSpec((nb, win, kv_w), lambda i: (i, 0, 0)),
            pl.BlockSpec((nb, win, kv_w), lambda i: (i, 0, 0)),
            pl.BlockSpec((nb, CONV_K - 1, conv_ch), lambda i: (i, 0, 0)),
        ],
        out_shape=[
            jax.ShapeDtypeStruct((dec_b * dec_seq, d), F32),
            jax.ShapeDtypeStruct((dec_b, win, kv_w), F32),
            jax.ShapeDtypeStruct((dec_b, win, kv_w), F32),
            jax.ShapeDtypeStruct((dec_b, CONV_K - 1, conv_ch), F32),
        ],
        compiler_params=pltpu.CompilerParams(dimension_semantics=("arbitrary",),
                                             vmem_limit_bytes=VMEM_LIMIT_BYTES),
        name="sample_mixer",
    )(x.reshape(dec_b * dec_seq, d), ck, cv, st, win_b, *tabs, sinks, conv_w, g_attn, g_conv, wout_b,
      ln_g, ln_b)


def _over_experts(fn, x):
    return fn(fn(x, axis=0, keepdims=True), axis=1, keepdims=True)


def _two_group_specs(tm, d, n_prompt_tiles):
    return [pl.BlockSpec((tm, d), lambda i, *_: (jnp.minimum(i, n_prompt_tiles - 1), 0)),
            pl.BlockSpec((tm, d), lambda i, *_: (jnp.maximum(i - n_prompt_tiles, 0), 0))]


def _route_kernel(hp_ref, hs_ref, rwt_ref, bias_ref, eidx_ref, rank_ref, gate_ref, cnt_ref, *, n_experts,
                  n_prompt_tiles):
    i = pl.program_id(0)
    tm = hp_ref.shape[0]
    per_group = n_experts // N_EXPERT_GROUPS
    shape3 = (N_EXPERT_GROUPS, per_group, tm)

    h = jnp.where(i < n_prompt_tiles, hp_ref[...], hs_ref[...])
    logits = lax.dot_general(rwt_ref[...], h.astype(BF16), (((1,), (1,)), ((), ())),
                             preferred_element_type=F32)
    scores = jax.nn.sigmoid(logits)
    sel = scores + bias_ref[...]
    scores3 = scores.reshape(shape3)
    grp = sel.reshape(shape3)
    member = lax.broadcasted_iota(jnp.int32, shape3, 1).astype(F32)
    group = lax.broadcasted_iota(jnp.int32, shape3, 0).astype(F32)
    expert = group * per_group + member

    m1 = jnp.max(grp, axis=1, keepdims=True)
    f1 = jnp.min(jnp.where(grp == m1, member, float(per_group)), axis=1, keepdims=True)
    m2 = jnp.max(jnp.where(member == f1, NEG_INF, grp), axis=1, keepdims=True)
    gscore = m1 + m2

    gid = lax.broadcasted_iota(jnp.int32, gscore.shape, 0).astype(F32)
    gmask = jnp.zeros(gscore.shape, F32)
    cur = gscore
    for _ in range(TOPK_GROUPS):
        mx = jnp.max(cur, axis=0, keepdims=True)
        pick = gid == jnp.min(jnp.where(cur == mx, gid, float(N_EXPERT_GROUPS)), axis=0, keepdims=True)
        gmask = jnp.where(pick, 1.0, gmask)
        cur = jnp.where(pick, NEG_INF, cur)

    cand = jnp.where(gmask > 0.0, grp, NEG_INF)
    chosen = jnp.zeros(shape3, F32)
    picks, firsts, weights = [], [], []
    for _ in range(TOP_K):
        mx = _over_experts(jnp.max, cand)
        first = _over_experts(jnp.min, jnp.where(cand == mx, expert, float(n_experts)))
        pick = expert == first
        picks.append(pick)
        firsts.append(first)
        weights.append(_over_experts(jnp.sum, jnp.where(pick, scores3, 0.0)))
        chosen = jnp.where(pick, 1.0, chosen)
        cand = jnp.where(pick, NEG_INF, cand)
    wsum = weights[0]
    for w in weights[1:]:
        wsum = wsum + w

    chosen_b = chosen.reshape(n_experts, tm).astype(BF16)
    earlier = (lax.broadcasted_iota(jnp.int32, (tm, tm), 0)
               < lax.broadcasted_iota(jnp.int32, (tm, tm), 1)).astype(BF16)
    before3 = jnp.dot(chosen_b, earlier, preferred_element_type=F32).reshape(shape3)

    pad = SUBLANES - TOP_K
    eidx = [f.reshape(1, tm).astype(jnp.int32) for f in firsts]
    rank = [_over_experts(jnp.sum, jnp.where(p, before3, 0.0)).reshape(1, tm).astype(jnp.int32) for p in picks]
    gate = [(w / wsum * ROUTED_SCALE).reshape(1, tm) for w in weights]
    eidx_ref[...] = jnp.concatenate(eidx + [jnp.zeros((pad, tm), jnp.int32)], axis=0)
    rank_ref[...] = jnp.concatenate(rank + [jnp.zeros((pad, tm), jnp.int32)], axis=0)
    gate_ref[...] = jnp.concatenate(gate + [jnp.zeros((pad, tm), F32)], axis=0)

    cnt_ref[...] = lax.dot_general(jnp.ones((SUBLANES, tm), BF16), chosen_b, (((1,), (1,)), ((), ())),
                                   preferred_element_type=F32)


def _route(h_p, h_s, rwt_b, bias_col):
    d = h_p.shape[1]
    t_all = h_p.shape[0] + h_s.shape[0]
    n_experts = rwt_b.shape[0]
    tm = MOE_TILE
    nt = t_all // tm
    npt = h_p.shape[0] // tm
    row_spec = pl.BlockSpec((SUBLANES, tm), lambda i: (0, i))
    return pl.pallas_call(
        functools.partial(_route_kernel, n_experts=n_experts, n_prompt_tiles=npt),
        grid=(nt,),
        in_specs=_two_group_specs(tm, d, npt) + [
            pl.BlockSpec((n_experts, d), lambda i: (0, 0)),
            pl.BlockSpec((n_experts, 1), lambda i: (0, 0)),
        ],
        out_specs=[row_spec, row_spec, row_spec, pl.BlockSpec((SUBLANES, n_experts), lambda i: (i, 0))],
        out_shape=[
            jax.ShapeDtypeStruct((SUBLANES, t_all), jnp.int32),
            jax.ShapeDtypeStruct((SUBLANES, t_all), jnp.int32),
            jax.ShapeDtypeStruct((SUBLANES, t_all), F32),
            jax.ShapeDtypeStruct((nt * SUBLANES, n_experts), F32),
        ],
        compiler_params=pltpu.CompilerParams(dimension_semantics=("arbitrary",),
                                             vmem_limit_bytes=VMEM_LIMIT_BYTES),
        name="route",
    )(h_p, h_s, rwt_b, bias_col)


def _for_each_part(rows, max_rows, fn):
    off = jnp.int32(0)
    size = max_rows
    while size >= SUBLANES:
        part = rows & size

        @pl.when(part != 0)
        def _(off=off, size=size):
            fn(pl.multiple_of(off, SUBLANES), size)

        off = off + part
        size //= 2


def _top_bit(n):
    return 1 << (n.bit_length() - 1)


def _local_rows(tm, n_experts):
    n = TOP_K * tm + n_experts * (SUBLANES - 1)
    return -(-n // SORT_CHUNK) * SORT_CHUNK


def _dispatch_kernel(cnt_ref, ls_ref, gs_ref, tot_ref, zrow_ref, lp_ref, hp_ref, hs_ref, xs_hbm,
                     lbuf, zero_s, sem, *, n_experts, blk, n_prompt_tiles):
    i = pl.program_id(0)
    tm, d = hp_ref.shape
    n = TOP_K * tm
    nl = lbuf.shape[1]
    slot = i % 2

    def tile_wait(tile):
        pltpu.make_async_copy(lbuf.at[0, pl.ds(0, n)], xs_hbm.at[pl.ds(0, n)], sem).wait()
        _for_each_part(tot_ref[tile] - n, _top_bit(n_experts * (SUBLANES - 1)), lambda off, size:
                       pltpu.make_async_copy(lbuf.at[0, pl.ds(0, size)], xs_hbm.at[pl.ds(0, size)], sem).wait())

    @pl.when(i == 0)
    def _():
        zero_s[...] = jnp.zeros(zero_s.shape, F32)

        def zcopy(e):
            return pltpu.make_async_copy(zero_s, xs_hbm.at[pl.ds(pl.multiple_of(zrow_ref[e], blk), blk)], sem)

        def start(e, c):
            zcopy(e).start()
            return c

        def wait(e, c):
            zcopy(e).wait()
            return c

        lax.fori_loop(0, n_experts, start, 0)
        lax.fori_loop(0, n_experts, wait, 0)

    hb = jnp.where(i < n_prompt_tiles, hp_ref[...], hs_ref[...]).astype(BF16)
    lp = lp_ref[...]
    for r in range(nl // SORT_CHUNK):
        rows = r * SORT_CHUNK + lax.broadcasted_iota(jnp.int32, (SORT_CHUNK, tm), 0)
        hit = jnp.zeros((SORT_CHUNK, tm), F32)
        for k in range(TOP_K):
            hit = jnp.where(rows == lp[k:k + 1, :], 1.0, hit)
        lbuf[slot, pl.ds(r * SORT_CHUNK, SORT_CHUNK), :] = jnp.dot(hit.astype(BF16), hb,
                                                                   preferred_element_type=F32)

    @pl.when(i > 0)
    def _():
        tile_wait(i - 1)

    def per_expert(e, c):
        idx = i * n_experts + e
        l0, g0 = ls_ref[idx], gs_ref[idx]
        _for_each_part(cnt_ref[idx], tm, lambda off, size: pltpu.make_async_copy(
            lbuf.at[slot, pl.ds(pl.multiple_of(l0 + off, SUBLANES), size)],
            xs_hbm.at[pl.ds(pl.multiple_of(g0 + off, SUBLANES), size)], sem).start())
        return c

    lax.fori_loop(0, n_experts, per_expert, 0)

    @pl.when(i == pl.num_programs(0) - 1)
    def _():
        tile_wait(i)


def _dispatch(tables, zrow, lp, h_p, h_s, *, n_rows, blk):
    d = h_p.shape[1]
    t_all = h_p.shape[0] + h_s.shape[0]
    tm = MOE_TILE
    npt = h_p.shape[0] // tm
    n_experts = zrow.shape[0]
    return pl.pallas_call(
        functools.partial(_dispatch_kernel, n_experts=n_experts, blk=blk, n_prompt_tiles=npt),
        grid_spec=pltpu.PrefetchScalarGridSpec(
            num_scalar_prefetch=5,
            grid=(t_all // tm,),
            in_specs=[pl.BlockSpec((SUBLANES, tm), lambda i, *_: (0, i))] + _two_group_specs(tm, d, npt),
            out_specs=pl.BlockSpec(memory_space=pl.ANY),
            scratch_shapes=[
                pltpu.VMEM((2, _local_rows(tm, n_experts), d), F32),
                pltpu.VMEM((blk, d), F32),
                pltpu.SemaphoreType.DMA,
            ],
        ),
        out_shape=jax.ShapeDtypeStruct((n_rows, d), F32),
        compiler_params=pltpu.CompilerParams(dimension_semantics=("arbitrary",),
                                             vmem_limit_bytes=VMEM_LIMIT_BYTES),
        name="dispatch",
    )(*tables, zrow, lp, h_p, h_s)


def _silu(x):
    return x * jax.nn.sigmoid(x)


def _expert_kernel(be_ref, nact_ref, xs_ref, wg_ref, wu_ref, wd_ref, ys_ref, wg_s, wu_s, wd_s):
    b = pl.program_id(0)

    @pl.when(b < nact_ref[0])
    def _():
        @pl.when((b == 0) | (be_ref[b] != be_ref[jnp.maximum(b - 1, 0)]))
        def _():
            wg_s[...] = wg_ref[...].astype(BF16)
            wu_s[...] = wu_ref[...].astype(BF16)
            wd_s[...] = wd_ref[...].astype(BF16)

        xb = xs_ref[...].astype(BF16)
        hid = _silu(jnp.dot(xb, wg_s[...], preferred_element_type=F32)) * jnp.dot(
            xb, wu_s[...], preferred_element_type=F32)
        ys_ref[...] = jnp.dot(hid.astype(BF16), wd_s[...], preferred_element_type=F32)


def _experts(block_expert, nact, xs, w_gate, w_up, w_down, *, blk):
    n_rows, d = xs.shape
    ff = w_gate.shape[-1]
    n_blocks = n_rows // blk

    def active(b, be, na):
        return jnp.minimum(b, na[0] - 1)

    return pl.pallas_call(
        _expert_kernel,
        grid_spec=pltpu.PrefetchScalarGridSpec(
            num_scalar_prefetch=2,
            grid=(n_blocks,),
            in_specs=[
                pl.BlockSpec((blk, d), lambda b, be, na: (active(b, be, na), 0)),
                pl.BlockSpec((None, d, ff), lambda b, be, na: (be[active(b, be, na)], 0, 0)),
                pl.BlockSpec((None, d, ff), lambda b, be, na: (be[active(b, be, na)], 0, 0)),
                pl.BlockSpec((None, ff, d), lambda b, be, na: (be[active(b, be, na)], 0, 0)),
            ],
            out_specs=pl.BlockSpec((blk, d), lambda b, be, na: (active(b, be, na), 0)),
            scratch_shapes=[
                pltpu.VMEM((d, ff), BF16),
                pltpu.VMEM((d, ff), BF16),
                pltpu.VMEM((ff, d), BF16),
            ],
        ),
        out_shape=jax.ShapeDtypeStruct((n_rows, d), F32),
        compiler_params=pltpu.CompilerParams(dimension_semantics=("arbitrary",),
                                             vmem_limit_bytes=VMEM_LIMIT_BYTES),
        name="experts",
    )(block_expert, nact, xs, w_gate, w_up, w_down)


def _combine_kernel(cnt_ref, ls_ref, gs_ref, tot_ref, ys_hbm, lpt_ref, gate_ref, hp_ref, hs_ref, wsg_ref, wsu_ref,
                    wsd_ref, lng_ref, lnb_ref, yp_ref, ysm_ref, ybuf, sems, *, alpha, n_experts, n_prompt_tiles):
    i = pl.program_id(0)
    tm, d = hp_ref.shape
    n = TOP_K * tm
    nl = ybuf.shape[1]
    slot = i % 2

    def fetch(tile, to):
        def per_expert(e, c):
            idx = tile * n_experts + e
            l0, g0 = ls_ref[idx], gs_ref[idx]
            _for_each_part(cnt_ref[idx], tm, lambda off, size: pltpu.make_async_copy(
                ys_hbm.at[pl.ds(pl.multiple_of(g0 + off, SUBLANES), size)],
                ybuf.at[to, pl.ds(pl.multiple_of(l0 + off, SUBLANES), size)], sems.at[to]).start())
            return c

        lax.fori_loop(0, n_experts, per_expert, 0)

    @pl.when(i == 0)
    def _():
        ybuf[...] = jnp.zeros(ybuf.shape, F32)
        fetch(0, 0)

    @pl.when(i + 1 < pl.num_programs(0))
    def _():
        fetch(i + 1, 1 - slot)

    h = jnp.where(i < n_prompt_tiles, hp_ref[...], hs_ref[...])
    hb = h.astype(BF16)
    hid = _silu(jnp.dot(hb, wsg_ref[...], preferred_element_type=F32)) * jnp.dot(
        hb, wsu_ref[...], preferred_element_type=F32)
    shared = jnp.dot(hid.astype(BF16), wsd_ref[...], preferred_element_type=F32)

    pltpu.make_async_copy(ys_hbm.at[pl.ds(0, n)], ybuf.at[slot, pl.ds(0, n)], sems.at[slot]).wait()
    _for_each_part(tot_ref[i] - n, _top_bit(n_experts * (SUBLANES - 1)), lambda off, size: pltpu.make_async_copy(
        ys_hbm.at[pl.ds(0, size)], ybuf.at[slot, pl.ds(0, size)], sems.at[slot]).wait())

    lpt = lpt_ref[...]
    gate = gate_ref[...]
    routed = jnp.zeros((tm, d), F32)
    for r in range(nl // SORT_CHUNK):
        cols = r * SORT_CHUNK + lax.broadcasted_iota(jnp.int32, (tm, SORT_CHUNK), 1)
        g = jnp.zeros((tm, SORT_CHUNK), F32)
        for k in range(TOP_K):
            g = jnp.where(cols == lpt[:, k:k + 1], gate[:, k:k + 1], g)
        y = ybuf[slot, pl.ds(r * SORT_CHUNK, SORT_CHUNK), :]
        yb = y.astype(BF16)
        g_hi = g.astype(BF16)
        g_lo = (g - g_hi.astype(F32)).astype(BF16)
        routed = routed + (jnp.dot(g_hi, yb, preferred_element_type=F32)
                           + jnp.dot(g_lo, yb, preferred_element_type=F32))
    y = _layer_norm(alpha * h + (routed + shared), lng_ref[...], lnb_ref[...])

    @pl.when(i < n_prompt_tiles)
    def _():
        yp_ref[...] = y

    @pl.when(i >= n_prompt_tiles)
    def _():
        ysm_ref[...] = y


def _combine(tables, ys, lp_t, gates_t, h_p, h_s, wsg_b, wsu_b, wsd_b, ln_g, ln_b, *, alpha):
    t_prompt, d = h_p.shape
    t_all = t_prompt + h_s.shape[0]
    tm = MOE_TILE
    ff = wsg_b.shape[-1]
    npt = t_prompt // tm
    n_experts = tables[0].shape[0] // (t_all // tm)
    const2 = lambda i, *_: (0, 0)
    return pl.pallas_call(
        functools.partial(_combine_kernel, alpha=alpha, n_experts=n_experts, n_prompt_tiles=npt),
        grid_spec=pltpu.PrefetchScalarGridSpec(
            num_scalar_prefetch=4,
            grid=(t_all // tm,),
            in_specs=[
                pl.BlockSpec(memory_space=pl.ANY),
                pl.BlockSpec((tm, SUBLANES), lambda i, *_: (i, 0)),
                pl.BlockSpec((tm, SUBLANES), lambda i, *_: (i, 0)),
                *_two_group_specs(tm, d, npt),
                pl.BlockSpec((d, ff), const2),
                pl.BlockSpec((d, ff), const2),
                pl.BlockSpec((ff, d), const2),
                pl.BlockSpec((1, d), const2),
                pl.BlockSpec((1, d), const2),
            ],
            out_specs=[
                pl.BlockSpec((tm, d), lambda i, *_: (jnp.minimum(i, npt - 1), 0)),
                pl.BlockSpec((tm, d), lambda i, *_: (jnp.maximum(i - npt, 0), 0)),
            ],
            scratch_shapes=[
                pltpu.VMEM((2, _local_rows(tm, n_experts), d), F32),
                pltpu.SemaphoreType.DMA((2,)),
            ],
        ),
        out_shape=[
            jax.ShapeDtypeStruct((t_prompt, d), F32),
            jax.ShapeDtypeStruct((t_all - t_prompt, d), F32),
        ],
        compiler_params=pltpu.CompilerParams(dimension_semantics=("arbitrary",),
                                             vmem_limit_bytes=VMEM_LIMIT_BYTES),
        name="combine",
    )(*tables, ys, lp_t, gates_t, h_p, h_s, wsg_b, wsu_b, wsd_b, ln_g, ln_b)


def _moe(h_p, h_s, router_w, router_bias, w_gate, w_up, w_down, ws_gate, ws_up, ws_down, ln_g, ln_b, *, alpha):
    t_all = h_p.shape[0] + h_s.shape[0]
    n_experts = router_w.shape[-1]
    blk = EXPERT_BLOCK
    tm = MOE_TILE
    nt = t_all // tm
    eidx8, lrank8, gate8, cnt8 = _route(h_p, h_s, router_w.T.astype(BF16), router_bias.reshape(n_experts, 1))

    cnt = cnt8.reshape(nt, SUBLANES, n_experts)[:, 0, :].astype(jnp.int32)
    cnt = (cnt + SUBLANES - 1) // SUBLANES * SUBLANES
    counts = jnp.sum(cnt, axis=0)
    blocks_per_e = (counts + blk - 1) // blk
    block_end = jnp.cumsum(blocks_per_e)
    pad_start = (block_end - blocks_per_e) * blk
    n_blocks = -(-(t_all * TOP_K + nt * n_experts * (SUBLANES - 1)) // blk) + n_experts
    block_expert = jnp.minimum(jnp.sum(block_end[None, :] <= jnp.arange(n_blocks)[:, None], axis=1),
                               n_experts - 1).astype(jnp.int32)
    nact = block_end[-1:].astype(jnp.int32)
    zrow = (jnp.maximum(block_end - 1, 0) * blk).astype(jnp.int32)
    gstart = pad_start[None, :] + jnp.cumsum(cnt, axis=0) - cnt
    lstart = jnp.cumsum(cnt, axis=1) - cnt
    tables = tuple(a.reshape(-1).astype(jnp.int32) for a in (cnt, lstart, gstart, jnp.sum(cnt, axis=1)))
    lstart_tok = jnp.repeat(lstart, tm, axis=0)
    lp8 = jnp.sum(jnp.where(eidx8[..., None] == jnp.arange(n_experts), lstart_tok[None], 0), axis=-1) + lrank8
    lp8 = lp8.astype(jnp.int32)

    xs = _dispatch(tables, zrow, lp8, h_p, h_s, n_rows=n_blocks * blk, blk=blk)
    ys = _experts(block_expert, nact, xs, w_gate, w_up, w_down, blk=blk)
    return _combine(tables, ys, lp8.T, gate8.T, h_p, h_s, ws_gate.astype(BF16), ws_up.astype(BF16), ws_down.astype(BF16),
                    ln_g, ln_b, alpha=alpha)


def kernel(x_prompt, x_sample, cache_k, cache_v, state_conv, w_in, attn_sinks, conv_w, g_attn_out, g_conv_out, w_out, ln1_g, ln1_b, router_w, router_bias, w_gate, w_up, w_down, ws_gate, ws_up, ws_down, ln2_g, ln2_b):
    depth = w_in.shape[0]
    bsz, seq, d = x_prompt.shape
    dec_b, dec_seq, _ = x_sample.shape
    win = cache_k.shape[2]
    kv_w = N_KV_HEADS * HEAD_DIM
    t_prompt = bsz * seq
    t_all = t_prompt + dec_b * dec_seq
    alpha = (2.0 * depth) ** 0.25
    assert win == WINDOW and seq % PROMPT_TILE == 0 and dec_b % SAMPLE_SEQS == 0
    assert t_prompt % MOE_TILE == 0 and (t_all - t_prompt) % MOE_TILE == 0
    assert MOE_TILE & (MOE_TILE - 1) == 0 and (TOP_K * MOE_TILE) % SORT_CHUNK == 0

    tabs_p = _rope_tables(jnp.arange(seq))
    tabs_s = tuple(jnp.tile(t, (SAMPLE_SEQS, 1)) for t in _rope_tables(PAST_LEN + jnp.arange(dec_seq)))
    row = lambda a: a.reshape(1, -1)

    xp, xs = x_prompt, x_sample
    outs = [[] for _ in range(6)]
    for l in range(depth):
        win_b, wout_b = w_in[l].astype(BF16), w_out[l].astype(BF16)
        shared = (attn_sinks[l], conv_w[l], row(g_attn_out[l]), row(g_conv_out[l]), wout_b, row(ln1_g[l]),
                  row(ln1_b[l]))
        h_p, kp, vp, cp = _prompt_mixer(xp, win_b, tabs_p, *shared, alpha=alpha)
        h_s, kn, vn, cn = _sample_mixer(xs, cache_k[l].reshape(dec_b, win, kv_w),
                                        cache_v[l].reshape(dec_b, win, kv_w), state_conv[l], win_b, tabs_s,
                                        *shared, alpha=alpha)
        yp, ys = _moe(h_p, h_s, router_w[l], router_bias[l], w_gate[l], w_up[l], w_down[l], ws_gate[l], ws_up[l],
                      ws_down[l], row(ln2_g[l]), row(ln2_b[l]), alpha=alpha)
        xp, xs = yp.reshape(bsz, seq, d), ys.reshape(dec_b, dec_seq, d)
        heads = lambda a: a.reshape(a.shape[0], win, N_KV_HEADS, HEAD_DIM)
        for o, a in zip(outs, (heads(kp), heads(vp), cp, heads(kn), heads(vn), cn)):
            o.append(a)
    return (xp, xs) + tuple(jnp.stack(o, axis=0) for o in outs)
```

```python
import functools

import jax
import jax.numpy as jnp
from jax import lax
from jax.experimental import pallas as pl
from jax.experimental.pallas import tpu as pltpu

PAST_LEN = 16384
WINDOW = 128
HEAD_DIM = 64
N_KV_HEADS = 2
ROT_DIM = HEAD_DIM // 4
ROPE_THETA = 500000.0
CONV_K = 3
TOP_K = 6
N_EXPERT_GROUPS = 8
TOPK_GROUPS = 4
ROUTED_SCALE = 2.5
LN_EPS = 1e-5
RMS_EPS = 1e-6

LANES = 128
SUBLANES = 8
VMEM_LIMIT_BYTES = 56 * 1024 * 1024

PROMPT_TILE = 512
SAMPLE_SEQS = 16
MOE_TILE = 256
SORT_CHUNK = 256
EXPERT_BLOCK = 512

F32 = jnp.float32
BF16 = jnp.bfloat16
NEG_INF = float("-inf")


def _rope_tables(positions):
    half = ROT_DIM // 2
    inv_freq = ROPE_THETA ** (-jnp.arange(0, ROT_DIM, 2, dtype=F32) / ROT_DIM)
    ang = positions.astype(F32)[:, None] * inv_freq[None, :]
    cos, sin = jnp.cos(ang), jnp.sin(ang)
    n = positions.shape[0]
    rest = HEAD_DIM - ROT_DIM
    c = jnp.concatenate([cos, cos, jnp.ones((n, rest), F32)], axis=-1)
    sa = jnp.concatenate([-sin, jnp.zeros((n, half + rest), F32)], axis=-1)
    sb = jnp.concatenate([jnp.zeros((n, half), F32), sin, jnp.zeros((n, rest), F32)], axis=-1)
    reps = LANES // HEAD_DIM
    return jnp.tile(c, (1, reps)), jnp.tile(sa, (1, reps)), jnp.tile(sb, (1, reps))


def _rope(x, c, sa, sb):
    half = ROT_DIM // 2
    return x * c + pltpu.roll(x, LANES - half, 1) * sa + pltpu.roll(x, half, 1) * sb


def _rms_norm(x, g):
    return x * lax.rsqrt(jnp.mean(jnp.square(x), axis=-1, keepdims=True) + RMS_EPS) * g


def _layer_norm(x, g, b):
    mu = jnp.mean(x, axis=-1, keepdims=True)
    var = jnp.mean(jnp.square(x - mu), axis=-1, keepdims=True)
    return (x - mu) * lax.rsqrt(var + LN_EPS) * g + b


def _short_conv(gated, prev2, prev1, row, conv_w, b_gate):
    g1 = pltpu.roll(gated, 1, 0)
    g2 = pltpu.roll(gated, 2, 0)
    g1 = jnp.where(row == 0, prev1, g1)
    g2 = jnp.where(row == 0, prev2, jnp.where(row == 1, prev1, g2))
    y = conv_w[0:1, :] * g2 + conv_w[1:2, :] * g1 + conv_w[2:3, :] * gated
    return b_gate * y


def _merge_norm(x, attn_o, conv_o, gattn, gconv, wout_ref, ln_g, ln_b, alpha):
    cat = jnp.concatenate([_rms_norm(attn_o, gattn), _rms_norm(conv_o, gconv)], axis=-1)
    mix = jnp.dot(cat.astype(BF16), wout_ref[...], preferred_element_type=F32)
    return _layer_norm(alpha * x + mix, ln_g, ln_b)


def _sink_softmax(parts, sink):
    m = sink
    for s in parts:
        m = jnp.maximum(m, jnp.max(s, axis=-1, keepdims=True))
    es = [jnp.exp(s - m) for s in parts]
    den = jnp.exp(sink - m)
    for e in es:
        den = den + jnp.sum(e, axis=-1, keepdims=True)
    return [(e / den).astype(BF16) for e in es]


def _prompt_mixer_kernel(x_ref, win_ref, c_ref, sa_ref, sb_ref, sinks_ref, convw_ref, gattn_ref, gconv_ref,
                         wout_ref, lng_ref, lnb_ref,
                         h_ref, ko_ref, vo_ref, co_ref,
                         q_s, k_s, v_s, o_s, gc_s, *, alpha, n_heads, attn_w, kv_w, conv_ch):
    s = pl.program_id(1)
    last = pl.num_programs(1) - 1
    tq = x_ref.shape[0]
    q_per_kv = n_heads // N_KV_HEADS

    @pl.when(s == 0)
    def _():
        k_s[0:WINDOW, :] = jnp.zeros((WINDOW, kv_w), BF16)
        v_s[0:WINDOW, :] = jnp.zeros((WINDOW, kv_w), BF16)
        gc_s[...] = jnp.zeros(gc_s.shape, F32)

    @pl.when(s > 0)
    def _():
        k_s[0:WINDOW, :] = k_s[tq:tq + WINDOW, :]
        v_s[0:WINDOW, :] = v_s[tq:tq + WINDOW, :]

    x = x_ref[...]
    xb = x.astype(BF16)
    c, sa, sb = c_ref[...], sa_ref[...], sb_ref[...]

    def proj(lo, width):
        return jnp.dot(xb, win_ref[:, lo:lo + width], preferred_element_type=F32)

    scale = HEAD_DIM ** -0.5
    for j in range(attn_w // LANES):
        qj = _rope(proj(j * LANES, LANES), c, sa, sb)
        q_s[:, j * LANES:(j + 1) * LANES] = (qj * scale).astype(BF16)
    k = _rope(proj(attn_w, kv_w), c, sa, sb)
    v = proj(attn_w + kv_w, kv_w)
    k_s[WINDOW:WINDOW + tq, :] = k.astype(BF16)
    v_s[WINDOW:WINDOW + tq, :] = v.astype(BF16)

    @pl.when(s == last)
    def _():
        ko_ref[...] = k[tq - WINDOW:, :]
        vo_ref[...] = v[tq - WINDOW:, :]

    qi = lax.broadcasted_iota(jnp.int32, (WINDOW, 2 * WINDOW), 0)
    ci = lax.broadcasted_iota(jnp.int32, (WINDOW, 2 * WINDOW), 1)
    band = (ci > qi) & (ci <= qi + WINDOW)

    def sub_block(j, carry):
        r0 = pl.multiple_of(j * WINDOW, WINDOW)
        has_prev = (s * tq + r0) > 0
        mask = band & ((ci >= WINDOW) | has_prev)
        kk = k_s[pl.ds(r0, 2 * WINDOW), :]
        vv = v_s[pl.ds(r0, 2 * WINDOW), :]
        for hd in range(n_heads):
            kvh = hd // q_per_kv
            qh = q_s[pl.ds(r0, WINDOW), hd * HEAD_DIM:(hd + 1) * HEAD_DIM]
            kh = kk[:, kvh * HEAD_DIM:(kvh + 1) * HEAD_DIM]
            vh = vv[:, kvh * HEAD_DIM:(kvh + 1) * HEAD_DIM]
            sc = lax.dot_general(qh, kh, (((1,), (1,)), ((), ())), preferred_element_type=F32)
            sc = jnp.where(mask, sc, NEG_INF)
            (p,) = _sink_softmax([sc], sinks_ref[hd])
            o_s[pl.ds(r0, WINDOW), hd * HEAD_DIM:(hd + 1) * HEAD_DIM] = jnp.dot(
                p, vh, preferred_element_type=F32)
        return carry

    lax.fori_loop(0, tq // WINDOW, sub_block, 0)

    o3 = attn_w + 2 * kv_w
    gated = proj(o3 + 2 * conv_ch, conv_ch) * proj(o3, conv_ch)
    row = lax.broadcasted_iota(jnp.int32, (tq, 1), 0)
    conv_o = _short_conv(gated, gc_s[0:1, :], gc_s[1:2, :], row, convw_ref[...], proj(o3 + conv_ch, conv_ch))
    gc_s[0:CONV_K - 1, :] = gated[tq - (CONV_K - 1):, :]

    @pl.when(s == last)
    def _():
        co_ref[...] = gated[tq - (CONV_K - 1):, :]

    h_ref[...] = _merge_norm(x, o_s[...], conv_o, gattn_ref[...], gconv_ref[...], wout_ref,
                             lng_ref[...], lnb_ref[...], alpha)


def _prompt_mixer(x, win_b, tabs, sinks, conv_w, g_attn, g_conv, wout_b, ln_g, ln_b, *, alpha):
    bsz, seq, d = x.shape
    tq = PROMPT_TILE
    ns = seq // tq
    attn_w = g_attn.shape[-1]
    conv_ch = g_conv.shape[-1]
    n_heads = attn_w // HEAD_DIM
    kv_w = N_KV_HEADS * HEAD_DIM
    in_cols = win_b.shape[-1]
    const2 = lambda b, s: (0, 0)
    kern = functools.partial(_prompt_mixer_kernel, alpha=alpha, n_heads=n_heads, attn_w=attn_w, kv_w=kv_w,
                             conv_ch=conv_ch)
    return pl.pallas_call(
        kern,
        grid=(bsz, ns),
        in_specs=[
            pl.BlockSpec((None, tq, d), lambda b, s: (b, s, 0)),
            pl.BlockSpec((d, in_cols), const2),
            pl.BlockSpec((tq, LANES), lambda b, s: (s, 0)),
            pl.BlockSpec((tq, LANES), lambda b, s: (s, 0)),
            pl.BlockSpec((tq, LANES), lambda b, s: (s, 0)),
            pl.BlockSpec(memory_space=pltpu.SMEM),
            pl.BlockSpec((CONV_K, conv_ch), const2),
            pl.BlockSpec((1, attn_w), const2),
            pl.BlockSpec((1, conv_ch), const2),
            pl.BlockSpec((attn_w + conv_ch, d), const2),
            pl.BlockSpec((1, d), const2),
            pl.BlockSpec((1, d), const2),
        ],
        out_specs=[
            pl.BlockSpec((tq, d), lambda b, s: (b * ns + s, 0)),
            pl.BlockSpec((None, WINDOW, kv_w), lambda b, s: (b, 0, 0)),
            pl.BlockSpec((None, WINDOW, kv_w), lambda b, s: (b, 0, 0)),
            pl.BlockSpec((None, CONV_K - 1, conv_ch), lambda b, s: (b, 0, 0)),
        ],
        out_shape=[
            jax.ShapeDtypeStruct((bsz * seq, d), F32),
            jax.ShapeDtypeStruct((bsz, WINDOW, kv_w), F32),
            jax.ShapeDtypeStruct((bsz, WINDOW, kv_w), F32),
            jax.ShapeDtypeStruct((bsz, CONV_K - 1, conv_ch), F32),
        ],
        scratch_shapes=[
            pltpu.VMEM((tq, attn_w), BF16),
            pltpu.VMEM((tq + WINDOW, kv_w), BF16),
            pltpu.VMEM((tq + WINDOW, kv_w), BF16),
            pltpu.VMEM((tq, attn_w), F32),
            pltpu.VMEM((SUBLANES, conv_ch), F32),
        ],
        compiler_params=pltpu.CompilerParams(dimension_semantics=("arbitrary", "arbitrary"),
                                             vmem_limit_bytes=VMEM_LIMIT_BYTES),
        name="prompt_mixer",
    )(x, win_b, *tabs, sinks, conv_w, g_attn, g_conv, wout_b, ln_g, ln_b)


def _sample_mixer_kernel(x_ref, ck_ref, cv_ref, st_ref, win_ref, c_ref, sa_ref, sb_ref, sinks_ref,
                         convw_ref, gattn_ref, gconv_ref, wout_ref, lng_ref, lnb_ref,
                         h_ref, ko_ref, vo_ref, co_ref, *, alpha, n_heads, attn_w, kv_w, conv_ch, dec_seq):
    nb, win = ck_ref.shape[0], ck_ref.shape[1]
    rows = nb * dec_seq
    q_per_kv = n_heads // N_KV_HEADS
    x = x_ref[...]
    xb = x.astype(BF16)
    c, sa, sb = c_ref[...], sa_ref[...], sb_ref[...]

    def proj(lo, width):
        return jnp.dot(xb, win_ref[:, lo:lo + width], preferred_element_type=F32)

    scale = HEAD_DIM ** -0.5
    k = _rope(proj(attn_w, kv_w), c, sa, sb)
    v = proj(attn_w + kv_w, kv_w)
    k3 = k.reshape(nb, dec_seq, kv_w)
    v3 = v.reshape(nb, dec_seq, kv_w)
    ck = ck_ref[...]
    cv = cv_ref[...]
    ko_ref[:, 0:win - dec_seq, :] = ck[:, dec_seq:, :]
    ko_ref[:, win - dec_seq:, :] = k3
    vo_ref[:, 0:win - dec_seq, :] = cv[:, dec_seq:, :]
    vo_ref[:, win - dec_seq:, :] = v3
    ckb, cvb, k3b, v3b = ck.astype(BF16), cv.astype(BF16), k3.astype(BF16), v3.astype(BF16)

    qrows = q_per_kv * dec_seq
    qi = lax.broadcasted_iota(jnp.int32, (nb, qrows, win), 1) % dec_seq
    mask_c = lax.broadcasted_iota(jnp.int32, (nb, qrows, win), 2) > qi + (win - WINDOW)
    qn = lax.broadcasted_iota(jnp.int32, (nb, qrows, dec_seq), 1) % dec_seq
    mask_n = lax.broadcasted_iota(jnp.int32, (nb, qrows, dec_seq), 2) <= qn
    sink_row = lax.broadcasted_iota(jnp.int32, (nb, qrows, 1), 1) // dec_seq

    q_chunks = [_rope(proj(j * LANES, LANES), c, sa, sb) * scale for j in range(attn_w // LANES)]
    heads_out = []
    for kvh in range(N_KV_HEADS):
        qs = []
        for g in range(q_per_kv):
            lo = (kvh * q_per_kv + g) * HEAD_DIM
            qh = q_chunks[lo // LANES][:, lo % LANES:lo % LANES + HEAD_DIM]
            qs.append(qh.reshape(nb, dec_seq, HEAD_DIM))
        qg = jnp.concatenate(qs, axis=1).astype(BF16)
        sl = slice(kvh * HEAD_DIM, (kvh + 1) * HEAD_DIM)
        sc_c = jnp.einsum("bqd,bkd->bqk", qg, ckb[:, :, sl], preferred_element_type=F32)
        sc_n = jnp.einsum("bqd,bkd->bqk", qg, k3b[:, :, sl], preferred_element_type=F32)
        sc_c = jnp.where(mask_c, sc_c, NEG_INF)
        sc_n = jnp.where(mask_n, sc_n, NEG_INF)
        sink = jnp.zeros((nb, qrows, 1), F32)
        for g in range(q_per_kv):
            sink = jnp.where(sink_row == g, sinks_ref[kvh * q_per_kv + g], sink)
        p_c, p_n = _sink_softmax([sc_c, sc_n], sink)
        og = (jnp.einsum("bqk,bkd->bqd", p_c, cvb[:, :, sl], preferred_element_type=F32)
              + jnp.einsum("bqk,bkd->bqd", p_n, v3b[:, :, sl], preferred_element_type=F32))
        for g in range(q_per_kv):
            heads_out.append(og[:, g * dec_seq:(g + 1) * dec_seq, :].reshape(rows, HEAD_DIM))
    attn_o = jnp.concatenate(heads_out, axis=-1)

    o3 = attn_w + 2 * kv_w
    gated = proj(o3 + 2 * conv_ch, conv_ch) * proj(o3, conv_ch)
    st = st_ref[...]
    prev2 = jnp.broadcast_to(st[:, 0:1, :], (nb, dec_seq, conv_ch)).reshape(rows, conv_ch)
    prev1 = jnp.broadcast_to(st[:, 1:2, :], (nb, dec_seq, conv_ch)).reshape(rows, conv_ch)
    row = lax.broadcasted_iota(jnp.int32, (rows, 1), 0) % dec_seq
    conv_o = _short_conv(gated, prev2, prev1, row, convw_ref[...], proj(o3 + conv_ch, conv_ch))
    co_ref[...] = gated.reshape(nb, dec_seq, conv_ch)[:, dec_seq - (CONV_K - 1):, :]

    h_ref[...] = _merge_norm(x, attn_o, conv_o, gattn_ref[...], gconv_ref[...], wout_ref,
                             lng_ref[...], lnb_ref[...], alpha)


def _sample_mixer(x, ck, cv, st, win_b, tabs, sinks, conv_w, g_attn, g_conv, wout_b, ln_g, ln_b, *, alpha):
    dec_b, dec_seq, d = x.shape
    assert dec_seq >= CONV_K - 1 and dec_seq % SUBLANES == 0
    nb = SAMPLE_SEQS
    rows = nb * dec_seq
    win = ck.shape[1]
    attn_w = g_attn.shape[-1]
    conv_ch = g_conv.shape[-1]
    n_heads = attn_w // HEAD_DIM
    kv_w = N_KV_HEADS * HEAD_DIM
    in_cols = win_b.shape[-1]
    const2 = lambda i: (0, 0)
    kern = functools.partial(_sample_mixer_kernel, alpha=alpha, n_heads=n_heads, attn_w=attn_w, kv_w=kv_w,
                             conv_ch=conv_ch, dec_seq=dec_seq)
    return pl.pallas_call(
        kern,
        grid=(dec_b // nb,),
        in_specs=[
            pl.BlockSpec((rows, d), lambda i: (i, 0)),
            pl.BlockSpec((nb, win, kv_w), lambda i: (i, 0, 0)),
            pl.BlockSpec((nb, win, kv_w), lambda i: (i, 0, 0)),
            pl.BlockSpec((nb, CONV_K - 1, conv_ch), lambda i: (i, 0, 0)),
            pl.BlockSpec((d, in_cols), const2),
            pl.BlockSpec((rows, LANES), const2),
            pl.BlockSpec((rows, LANES), const2),
            pl.BlockSpec((rows, LANES), const2),
            pl.BlockSpec(memory_space=pltpu.SMEM),
            pl.BlockSpec((CONV_K, conv_ch), const2),
            pl.BlockSpec((1, attn_w), const2),
            pl.BlockSpec((1, conv_ch), const2),
            pl.BlockSpec((attn_w + conv_ch, d), const2),
            pl.BlockSpec((1, d), const2),
            pl.BlockSpec((1, d), const2),
        ],
        out_specs=[
            pl.BlockSpec((rows, d), lambda i: (i, 0)),
            pl.BlockSpec((nb, win, kv_w), lambda i: (i, 0, 0)),
            pl.BlockSpec((nb, win, kv_w), lambda i: (i, 0, 0)),
            pl.BlockSpec((nb, CONV_K - 1, conv_ch), lambda i: (i, 0, 0)),
        ],
        out_shape=[
            jax.ShapeDtypeStruct((dec_b * dec_seq, d), F32),
            jax.ShapeDtypeStruct((dec_b, win, kv_w), F32),
            jax.ShapeDtypeStruct((dec_b, win, kv_w), F32),
            jax.ShapeDtypeStruct((dec_b, CONV_K - 1, conv_ch), F32),
        ],
        compiler_params=pltpu.CompilerParams(dimension_semantics=("arbitrary",),
                                             vmem_limit_bytes=VMEM_LIMIT_BYTES),
        name="sample_mixer",
    )(x.reshape(dec_b * dec_seq, d), ck, cv, st, win_b, *tabs, sinks, conv_w, g_attn, g_conv, wout_b,
      ln_g, ln_b)


def _over_experts(fn, x):
    return fn(fn(x, axis=0, keepdims=True), axis=1, keepdims=True)


def _two_group_specs(tm, d, n_prompt_tiles):
    return [pl.BlockSpec((tm, d), lambda i, *_: (jnp.minimum(i, n_prompt_tiles - 1), 0)),
            pl.BlockSpec((tm, d), lambda i, *_: (jnp.maximum(i - n_prompt_tiles, 0), 0))]


def _route_kernel(hp_ref, hs_ref, rwt_ref, bias_ref, eidx_ref, rank_ref, gate_ref, cnt_ref, *, n_experts,
                  n_prompt_tiles):
    i = pl.program_id(0)
    tm = hp_ref.shape[0]
    per_group = n_experts // N_EXPERT_GROUPS
    shape3 = (N_EXPERT_GROUPS, per_group, tm)

    h = jnp.where(i < n_prompt_tiles, hp_ref[...], hs_ref[...])
    logits = lax.dot_general(rwt_ref[...], h.astype(BF16), (((1,), (1,)), ((), ())),
                             preferred_element_type=F32)
    scores = jax.nn.sigmoid(logits)
    sel = scores + bias_ref[...]
    scores3 = scores.reshape(shape3)
    grp = sel.reshape(shape3)
    member = lax.broadcasted_iota(jnp.int32, shape3, 1).astype(F32)
    group = lax.broadcasted_iota(jnp.int32, shape3, 0).astype(F32)
    expert = group * per_group + member

    m1 = jnp.max(grp, axis=1, keepdims=True)
    f1 = jnp.min(jnp.where(grp == m1, member, float(per_group)), axis=1, keepdims=True)
    m2 = jnp.max(jnp.where(member == f1, NEG_INF, grp), axis=1, keepdims=True)
    gscore = m1 + m2

    gid = lax.broadcasted_iota(jnp.int32, gscore.shape, 0).astype(F32)
    gmask = jnp.zeros(gscore.shape, F32)
    cur = gscore
    for _ in range(TOPK_GROUPS):
        mx = jnp.max(cur, axis=0, keepdims=True)
        pick = gid == jnp.min(jnp.where(cur == mx, gid, float(N_EXPERT_GROUPS)), axis=0, keepdims=True)
        gmask = jnp.where(pick, 1.0, gmask)
        cur = jnp.where(pick, NEG_INF, cur)

    cand = jnp.where(gmask > 0.0, grp, NEG_INF)
    chosen = jnp.zeros(shape3, F32)
    picks, firsts, weights = [], [], []
    for _ in range(TOP_K):
        mx = _over_experts(jnp.max, cand)
        first = _over_experts(jnp.min, jnp.where(cand == mx, expert, float(n_experts)))
        pick = expert == first
        picks.append(pick)
        firsts.append(first)
        weights.append(_over_experts(jnp.sum, jnp.where(pick, scores3, 0.0)))
        chosen = jnp.where(pick, 1.0, chosen)
        cand = jnp.where(pick, NEG_INF, cand)
    wsum = weights[0]
    for w in weights[1:]:
        wsum = wsum + w

    chosen_b = chosen.reshape(n_experts, tm).astype(BF16)
    earlier = (lax.broadcasted_iota(jnp.int32, (tm, tm), 0)
               < lax.broadcasted_iota(jnp.int32, (tm, tm), 1)).astype(BF16)
    before3 = jnp.dot(chosen_b, earlier, preferred_element_type=F32).reshape(shape3)

    pad = SUBLANES - TOP_K
    eidx = [f.reshape(1, tm).astype(jnp.int32) for f in firsts]
    rank = [_over_experts(jnp.sum, jnp.where(p, before3, 0.0)).reshape(1, tm).astype(jnp.int32) for p in picks]
    gate = [(w / wsum * ROUTED_SCALE).reshape(1, tm) for w in weights]
    eidx_ref[...] = jnp.concatenate(eidx + [jnp.zeros((pad, tm), jnp.int32)], axis=0)
    rank_ref[...] = jnp.concatenate(rank + [jnp.zeros((pad, tm), jnp.int32)], axis=0)
    gate_ref[...] = jnp.concatenate(gate + [jnp.zeros((pad, tm), F32)], axis=0)

    cnt_ref[...] = lax.dot_general(jnp.ones((SUBLANES, tm), BF16), chosen_b, (((1,), (1,)), ((), ())),
                                   preferred_element_type=F32)


def _route(h_p, h_s, rwt_b, bias_col):
    d = h_p.shape[1]
    t_all = h_p.shape[0] + h_s.shape[0]
    n_experts = rwt_b.shape[0]
    tm = MOE_TILE
    nt = t_all // tm
    npt = h_p.shape[0] // tm
    row_spec = pl.BlockSpec((SUBLANES, tm), lambda i: (0, i))
    return pl.pallas_call(
        functools.partial(_route_kernel, n_experts=n_experts, n_prompt_tiles=npt),
        grid=(nt,),
        in_specs=_two_group_specs(tm, d, npt) + [
            pl.BlockSpec((n_experts, d), lambda i: (0, 0)),
            pl.BlockSpec((n_experts, 1), lambda i: (0, 0)),
        ],
        out_specs=[row_spec, row_spec, row_spec, pl.BlockSpec((SUBLANES, n_experts), lambda i: (i, 0))],
        out_shape=[
            jax.ShapeDtypeStruct((SUBLANES, t_all), jnp.int32),
            jax.ShapeDtypeStruct((SUBLANES, t_all), jnp.int32),
            jax.ShapeDtypeStruct((SUBLANES, t_all), F32),
            jax.ShapeDtypeStruct((nt * SUBLANES, n_experts), F32),
        ],
        compiler_params=pltpu.CompilerParams(dimension_semantics=("arbitrary",),
                                             vmem_limit_bytes=VMEM_LIMIT_BYTES),
        name="route",
    )(h_p, h_s, rwt_b, bias_col)


def _for_each_part(rows, max_rows, fn):
    off = jnp.int32(0)
    size = max_rows
    while size >= SUBLANES:
        part = rows & size

        @pl.when(part != 0)
        def _(off=off, size=size):
            fn(pl.multiple_of(off, SUBLANES), size)

        off = off + part
        size //= 2


def _top_bit(n):
    return 1 << (n.bit_length() - 1)


def _pack_rows(x):
    rows, two_w = x.shape
    w = two_w // 2
    x3 = x.reshape(rows // SUBLANES, SUBLANES, two_w)
    halves = jnp.concatenate([x3[:, :, :w], x3[:, :, w:]], axis=1).astype(BF16)
    return pltpu.bitcast(halves, jnp.uint32).reshape(rows, w)


def _unpack_rows(p):
    rows, w = p.shape
    halves = pltpu.bitcast(p.reshape(rows // SUBLANES, SUBLANES, w), BF16).astype(F32)
    return jnp.concatenate([halves[:, :SUBLANES, :], halves[:, SUBLANES:, :]], axis=-1).reshape(rows, 2 * w)


def _local_rows(tm, n_experts):
    n = TOP_K * tm + n_experts * (SUBLANES - 1)
    return -(-n // SORT_CHUNK) * SORT_CHUNK


def _dispatch_kernel(cnt_ref, ls_ref, gs_ref, tot_ref, zrow_ref, lp_ref, hp_ref, hs_ref, xs_hbm,
                     lbuf, zero_s, sem, *, n_experts, blk, n_prompt_tiles):
    i = pl.program_id(0)
    tm, d = hp_ref.shape
    n = TOP_K * tm
    nl = lbuf.shape[1]
    slot = i % 2

    def tile_wait(tile):
        pltpu.make_async_copy(lbuf.at[0, pl.ds(0, n)], xs_hbm.at[pl.ds(0, n)], sem).wait()
        _for_each_part(tot_ref[tile] - n, _top_bit(n_experts * (SUBLANES - 1)), lambda off, size:
                       pltpu.make_async_copy(lbuf.at[0, pl.ds(0, size)], xs_hbm.at[pl.ds(0, size)], sem).wait())

    @pl.when(i == 0)
    def _():
        zero_s[...] = jnp.zeros(zero_s.shape, zero_s.dtype)

        def zcopy(e):
            return pltpu.make_async_copy(zero_s, xs_hbm.at[pl.ds(pl.multiple_of(zrow_ref[e], blk), blk)], sem)

        def start(e, c):
            zcopy(e).start()
            return c

        def wait(e, c):
            zcopy(e).wait()
            return c

        lax.fori_loop(0, n_experts, start, 0)
        lax.fori_loop(0, n_experts, wait, 0)

    hb = jnp.where(i < n_prompt_tiles, hp_ref[...], hs_ref[...]).astype(BF16)
    lp = lp_ref[...]
    for r in range(nl // SORT_CHUNK):
        rows = r * SORT_CHUNK + lax.broadcasted_iota(jnp.int32, (SORT_CHUNK, tm), 0)
        hit = jnp.zeros((SORT_CHUNK, tm), F32)
        for k in range(TOP_K):
            hit = jnp.where(rows == lp[k:k + 1, :], 1.0, hit)
        lbuf[slot, pl.ds(r * SORT_CHUNK, SORT_CHUNK), :] = _pack_rows(
            jnp.dot(hit.astype(BF16), hb, preferred_element_type=F32))

    @pl.when(i > 0)
    def _():
        tile_wait(i - 1)

    def per_expert(e, c):
        idx = i * n_experts + e
        l0, g0 = ls_ref[idx], gs_ref[idx]
        _for_each_part(cnt_ref[idx], tm, lambda off, size: pltpu.make_async_copy(
            lbuf.at[slot, pl.ds(pl.multiple_of(l0 + off, SUBLANES), size)],
            xs_hbm.at[pl.ds(pl.multiple_of(g0 + off, SUBLANES), size)], sem).start())
        return c

    lax.fori_loop(0, n_experts, per_expert, 0)

    @pl.when(i == pl.num_programs(0) - 1)
    def _():
        tile_wait(i)


def _dispatch(tables, zrow, lp, h_p, h_s, *, n_rows, blk):
    d = h_p.shape[1]
    t_all = h_p.shape[0] + h_s.shape[0]
    tm = MOE_TILE
    npt = h_p.shape[0] // tm
    n_experts = zrow.shape[0]
    return pl.pallas_call(
        functools.partial(_dispatch_kernel, n_experts=n_experts, blk=blk, n_prompt_tiles=npt),
        grid_spec=pltpu.PrefetchScalarGridSpec(
            num_scalar_prefetch=5,
            grid=(t_all // tm,),
            in_specs=[pl.BlockSpec((SUBLANES, tm), lambda i, *_: (0, i))] + _two_group_specs(tm, d, npt),
            out_specs=pl.BlockSpec(memory_space=pl.ANY),
            scratch_shapes=[
                pltpu.VMEM((2, _local_rows(tm, n_experts), d // 2), jnp.uint32),
                pltpu.VMEM((blk, d // 2), jnp.uint32),
                pltpu.SemaphoreType.DMA,
            ],
        ),
        out_shape=jax.ShapeDtypeStruct((n_rows, d // 2), jnp.uint32),
        compiler_params=pltpu.CompilerParams(dimension_semantics=("arbitrary",),
                                             vmem_limit_bytes=VMEM_LIMIT_BYTES),
        name="dispatch",
    )(*tables, zrow, lp, h_p, h_s)


def _silu(x):
    return x * jax.nn.sigmoid(x)


def _expert_kernel(be_ref, nact_ref, xs_ref, wg_ref, wu_ref, wd_ref, ys_ref, wg_s, wu_s, wd_s):
    b = pl.program_id(0)

    @pl.when(b < nact_ref[0])
    def _():
        @pl.when((b == 0) | (be_ref[b] != be_ref[jnp.maximum(b - 1, 0)]))
        def _():
            wg_s[...] = wg_ref[...].astype(BF16)
            wu_s[...] = wu_ref[...].astype(BF16)
            wd_s[...] = wd_ref[...].astype(BF16)

        xb = _unpack_rows(xs_ref[...]).astype(BF16)
        hid = _silu(jnp.dot(xb, wg_s[...], preferred_element_type=F32)) * jnp.dot(
            xb, wu_s[...], preferred_element_type=F32)
        ys_ref[...] = _pack_rows(jnp.dot(hid.astype(BF16), wd_s[...], preferred_element_type=F32))


def _experts(block_expert, nact, xs, w_gate, w_up, w_down, *, blk):
    n_rows, dw = xs.shape
    d = w_gate.shape[-2]
    ff = w_gate.shape[-1]
    n_blocks = n_rows // blk

    def active(b, be, na):
        return jnp.minimum(b, na[0] - 1)

    return pl.pallas_call(
        _expert_kernel,
        grid_spec=pltpu.PrefetchScalarGridSpec(
            num_scalar_prefetch=2,
            grid=(n_blocks,),
            in_specs=[
                pl.BlockSpec((blk, dw), lambda b, be, na: (active(b, be, na), 0)),
                pl.BlockSpec((None, d, ff), lambda b, be, na: (be[active(b, be, na)], 0, 0)),
                pl.BlockSpec((None, d, ff), lambda b, be, na: (be[active(b, be, na)], 0, 0)),
                pl.BlockSpec((None, ff, d), lambda b, be, na: (be[active(b, be, na)], 0, 0)),
            ],
            out_specs=pl.BlockSpec((blk, dw), lambda b, be, na: (active(b, be, na), 0)),
            scratch_shapes=[
                pltpu.VMEM((d, ff), BF16),
                pltpu.VMEM((d, ff), BF16),
                pltpu.VMEM((ff, d), BF16),
            ],
        ),
        out_shape=jax.ShapeDtypeStruct((n_rows, dw), jnp.uint32),
        compiler_params=pltpu.CompilerParams(dimension_semantics=("arbitrary",),
                                             vmem_limit_bytes=VMEM_LIMIT_BYTES),
        name="experts",
    )(block_expert, nact, xs, w_gate, w_up, w_down)


def _combine_kernel(cnt_ref, ls_ref, gs_ref, tot_ref, ys_hbm, lpt_ref, gate_ref, hp_ref, hs_ref, wsg_ref, wsu_ref,
                    wsd_ref, lng_ref, lnb_ref, yp_ref, ysm_ref, ybuf, sems, *, alpha, n_experts, n_prompt_tiles):
    i = pl.program_id(0)
    tm, d = hp_ref.shape
    n = TOP_K * tm
    nl = ybuf.shape[1]
    slot = i % 2

    def fetch(tile, to):
        def per_expert(e, c):
            idx = tile * n_experts + e
            l0, g0 = ls_ref[idx], gs_ref[idx]
            _for_each_part(cnt_ref[idx], tm, lambda off, size: pltpu.make_async_copy(
                ys_hbm.at[pl.ds(pl.multiple_of(g0 + off, SUBLANES), size)],
                ybuf.at[to, pl.ds(pl.multiple_of(l0 + off, SUBLANES), size)], sems.at[to]).start())
            return c

        lax.fori_loop(0, n_experts, per_expert, 0)

    @pl.when(i == 0)
    def _():
        ybuf[...] = jnp.zeros(ybuf.shape, ybuf.dtype)
        fetch(0, 0)

    @pl.when(i + 1 < pl.num_programs(0))
    def _():
        fetch(i + 1, 1 - slot)

    h = jnp.where(i < n_prompt_tiles, hp_ref[...], hs_ref[...])
    hb = h.astype(BF16)
    hid = _silu(jnp.dot(hb, wsg_ref[...], preferred_element_type=F32)) * jnp.dot(
        hb, wsu_ref[...], preferred_element_type=F32)
    shared = jnp.dot(hid.astype(BF16), wsd_ref[...], preferred_element_type=F32)

    pltpu.make_async_copy(ys_hbm.at[pl.ds(0, n)], ybuf.at[slot, pl.ds(0, n)], sems.at[slot]).wait()
    _for_each_part(tot_ref[i] - n, _top_bit(n_experts * (SUBLANES - 1)), lambda off, size: pltpu.make_async_copy(
        ys_hbm.at[pl.ds(0, size)], ybuf.at[slot, pl.ds(0, size)], sems.at[slot]).wait())

    lpt = lpt_ref[...]
    gate = gate_ref[...]
    routed = jnp.zeros((tm, d), F32)
    for r in range(nl // SORT_CHUNK):
        cols = r * SORT_CHUNK + lax.broadcasted_iota(jnp.int32, (tm, SORT_CHUNK), 1)
        g = jnp.zeros((tm, SORT_CHUNK), F32)
        for k in range(TOP_K):
            g = jnp.where(cols == lpt[:, k:k + 1], gate[:, k:k + 1], g)
        yb = _unpack_rows(ybuf[slot, pl.ds(r * SORT_CHUNK, SORT_CHUNK), :]).astype(BF16)
        g_hi = g.astype(BF16)
        g_lo = (g - g_hi.astype(F32)).astype(BF16)
        routed = routed + (jnp.dot(g_hi, yb, preferred_element_type=F32)
                           + jnp.dot(g_lo, yb, preferred_element_type=F32))
    y = _layer_norm(alpha * h + (routed + shared), lng_ref[...], lnb_ref[...])

    @pl.when(i < n_prompt_tiles)
    def _():
        yp_ref[...] = y

    @pl.when(i >= n_prompt_tiles)
    def _():
        ysm_ref[...] = y


def _combine(tables, ys, lp_t, gates_t, h_p, h_s, wsg_b, wsu_b, wsd_b, ln_g, ln_b, *, alpha):
    t_prompt, d = h_p.shape
    t_all = t_prompt + h_s.shape[0]
    tm = MOE_TILE
    ff = wsg_b.shape[-1]
    npt = t_prompt // tm
    n_experts = tables[0].shape[0] // (t_all // tm)
    const2 = lambda i, *_: (0, 0)
    return pl.pallas_call(
        functools.partial(_combine_kernel, alpha=alpha, n_experts=n_experts, n_prompt_tiles=npt),
        grid_spec=pltpu.PrefetchScalarGridSpec(
            num_scalar_prefetch=4,
            grid=(t_all // tm,),
            in_specs=[
                pl.BlockSpec(memory_space=pl.ANY),
                pl.BlockSpec((tm, SUBLANES), lambda i, *_: (i, 0)),
                pl.BlockSpec((tm, SUBLANES), lambda i, *_: (i, 0)),
                *_two_group_specs(tm, d, npt),
                pl.BlockSpec((d, ff), const2),
                pl.BlockSpec((d, ff), const2),
                pl.BlockSpec((ff, d), const2),
                pl.BlockSpec((1, d), const2),
                pl.BlockSpec((1, d), const2),
            ],
            out_specs=[
                pl.BlockSpec((tm, d), lambda i, *_: (jnp.minimum(i, npt - 1), 0)),
                pl.BlockSpec((tm, d), lambda i, *_: (jnp.maximum(i - npt, 0), 0)),
            ],
            scratch_shapes=[
                pltpu.VMEM((2, _local_rows(tm, n_experts), d // 2), jnp.uint32),
                pltpu.SemaphoreType.DMA((2,)),
            ],
        ),
        out_shape=[
            jax.ShapeDtypeStruct((t_prompt, d), F32),
            jax.ShapeDtypeStruct((t_all - t_prompt, d), F32),
        ],
        compiler_params=pltpu.CompilerParams(dimension_semantics=("arbitrary",),
                                             vmem_limit_bytes=VMEM_LIMIT_BYTES),
        name="combine",
    )(*tables, ys, lp_t, gates_t, h_p, h_s, wsg_b, wsu_b, wsd_b, ln_g, ln_b)


def _moe(h_p, h_s, router_w, router_bias, w_gate, w_up, w_down, ws_gate, ws_up, ws_down, ln_g, ln_b, *, alpha):
    t_all = h_p.shape[0] + h_s.shape[0]
    n_experts = router_w.shape[-1]
    blk = EXPERT_BLOCK
    tm = MOE_TILE
    nt = t_all // tm
    eidx8, lrank8, gate8, cnt8 = _route(h_p, h_s, router_w.T.astype(BF16), router_bias.reshape(n_experts, 1))

    cnt = cnt8.reshape(nt, SUBLANES, n_experts)[:, 0, :].astype(jnp.int32)
    cnt = (cnt + SUBLANES - 1) // SUBLANES * SUBLANES
    counts = jnp.sum(cnt, axis=0)
    blocks_per_e = (counts + blk - 1) // blk
    block_end = jnp.cumsum(blocks_per_e)
    pad_start = (block_end - blocks_per_e) * blk
    n_blocks = -(-(t_all * TOP_K + nt * n_experts * (SUBLANES - 1)) // blk) + n_experts
    block_expert = jnp.minimum(jnp.sum(block_end[None, :] <= jnp.arange(n_blocks)[:, None], axis=1),
                               n_experts - 1).astype(jnp.int32)
    nact = block_end[-1:].astype(jnp.int32)
    zrow = (jnp.maximum(block_end - 1, 0) * blk).astype(jnp.int32)
    gstart = pad_start[None, :] + jnp.cumsum(cnt, axis=0) - cnt
    lstart = jnp.cumsum(cnt, axis=1) - cnt
    tables = tuple(a.reshape(-1).astype(jnp.int32) for a in (cnt, lstart, gstart, jnp.sum(cnt, axis=1)))
    lstart_tok = jnp.repeat(lstart, tm, axis=0)
    lp8 = jnp.sum(jnp.where(eidx8[..., None] == jnp.arange(n_experts), lstart_tok[None], 0), axis=-1) + lrank8
    lp8 = lp8.astype(jnp.int32)

    xs = _dispatch(tables, zrow, lp8, h_p, h_s, n_rows=n_blocks * blk, blk=blk)
    ys = _experts(block_expert, nact, xs, w_gate, w_up, w_down, blk=blk)
    return _combine(tables, ys, lp8.T, gate8.T, h_p, h_s, ws_gate.astype(BF16), ws_up.astype(BF16), ws_down.astype(BF16),
                    ln_g, ln_b, alpha=alpha)


def kernel(x_prompt, x_sample, cache_k, cache_v, state_conv, w_in, attn_sinks, conv_w, g_attn_out, g_conv_out, w_out, ln1_g, ln1_b, router_w, router_bias, w_gate, w_up, w_down, ws_gate, ws_up, ws_down, ln2_g, ln2_b):
    depth = w_in.shape[0]
    bsz, seq, d = x_prompt.shape
    dec_b, dec_seq, _ = x_sample.shape
    win = cache_k.shape[2]
    kv_w = N_KV_HEADS * HEAD_DIM
    t_prompt = bsz * seq
    t_all = t_prompt + dec_b * dec_seq
    alpha = (2.0 * depth) ** 0.25
    assert win == WINDOW and seq % PROMPT_TILE == 0 and dec_b % SAMPLE_SEQS == 0
    assert t_prompt % MOE_TILE == 0 and (t_all - t_prompt) % MOE_TILE == 0
    assert MOE_TILE & (MOE_TILE - 1) == 0 and (TOP_K * MOE_TILE) % SORT_CHUNK == 0

    tabs_p = _rope_tables(jnp.arange(seq))
    tabs_s = tuple(jnp.tile(t, (SAMPLE_SEQS, 1)) for t in _rope_tables(PAST_LEN + jnp.arange(dec_seq)))
    row = lambda a: a.reshape(1, -1)

    xp, xs = x_prompt, x_sample
    outs = [[] for _ in range(6)]
    for l in range(depth):
        win_b, wout_b = w_in[l].astype(BF16), w_out[l].astype(BF16)
        shared = (attn_sinks[l], conv_w[l], row(g_attn_out[l]), row(g_conv_out[l]), wout_b, row(ln1_g[l]),
                  row(ln1_b[l]))
        h_p, kp, vp, cp = _prompt_mixer(xp, win_b, tabs_p, *shared, alpha=alpha)
        h_s, kn, vn, cn = _sample_mixer(xs, cache_k[l].reshape(dec_b, win, kv_w),
                                        cache_v[l].reshape(dec_b, win, kv_w), state_conv[l], win_b, tabs_s,
                                        *shared, alpha=alpha)
        yp, ys = _moe(h_p, h_s, router_w[l], router_bias[l], w_gate[l], w_up[l], w_down[l], ws_gate[l], ws_up[l],
                      ws_down[l], row(ln2_g[l]), row(ln2_b[l]), alpha=alpha)
        xp, xs = yp.reshape(bsz, seq, d), ys.reshape(dec_b, dec_seq, d)
        heads = lambda a: a.reshape(a.shape[0], win, N_KV_HEADS, HEAD_DIM)
        for o, a in zip(outs, (heads(kp), heads(vp), cp, heads(kn), heads(vn), cn)):
            o.append(a)
    return (xp, xs) + tuple(jnp.stack(o, axis=0) for o in outs)
```

```python
import functools

import jax
import jax.numpy as jnp
from jax import lax
from jax.experimental import pallas as pl
from jax.experimental.pallas import tpu as pltpu

PAST_LEN = 16384
WINDOW = 128
HEAD_DIM = 64
N_KV_HEADS = 2
ROT_DIM = HEAD_DIM // 4
ROPE_THETA = 500000.0
CONV_K = 3
TOP_K = 6
N_EXPERT_GROUPS = 8
TOPK_GROUPS = 4
ROUTED_SCALE = 2.5
LN_EPS = 1e-5
RMS_EPS = 1e-6

LANES = 128
SUBLANES = 8
VMEM_LIMIT_BYTES = 56 * 1024 * 1024

PROMPT_TILE = 512
SAMPLE_SEQS = 16
MOE_TILE = 256
SORT_CHUNK = 256
EXPERT_BLOCK = 512
RUN_CHUNK = 64
RUN_UNROLL = 4

F32 = jnp.float32
BF16 = jnp.bfloat16
NEG_INF = float("-inf")


def _rope_tables(positions):
    half = ROT_DIM // 2
    inv_freq = ROPE_THETA ** (-jnp.arange(0, ROT_DIM, 2, dtype=F32) / ROT_DIM)
    ang = positions.astype(F32)[:, None] * inv_freq[None, :]
    cos, sin = jnp.cos(ang), jnp.sin(ang)
    n = positions.shape[0]
    rest = HEAD_DIM - ROT_DIM
    c = jnp.concatenate([cos, cos, jnp.ones((n, rest), F32)], axis=-1)
    sa = jnp.concatenate([-sin, jnp.zeros((n, half + rest), F32)], axis=-1)
    sb = jnp.concatenate([jnp.zeros((n, half), F32), sin, jnp.zeros((n, rest), F32)], axis=-1)
    reps = LANES // HEAD_DIM
    return jnp.tile(c, (1, reps)), jnp.tile(sa, (1, reps)), jnp.tile(sb, (1, reps))


def _rope(x, c, sa, sb):
    half = ROT_DIM // 2
    return x * c + pltpu.roll(x, LANES - half, 1) * sa + pltpu.roll(x, half, 1) * sb


def _rms_norm(x, g):
    return x * lax.rsqrt(jnp.mean(jnp.square(x), axis=-1, keepdims=True) + RMS_EPS) * g


def _layer_norm(x, g, b):
    mu = jnp.mean(x, axis=-1, keepdims=True)
    var = jnp.mean(jnp.square(x - mu), axis=-1, keepdims=True)
    return (x - mu) * lax.rsqrt(var + LN_EPS) * g + b


def _short_conv(gated, prev2, prev1, row, conv_w, b_gate):
    g1 = pltpu.roll(gated, 1, 0)
    g2 = pltpu.roll(gated, 2, 0)
    g1 = jnp.where(row == 0, prev1, g1)
    g2 = jnp.where(row == 0, prev2, jnp.where(row == 1, prev1, g2))
    y = conv_w[0:1, :] * g2 + conv_w[1:2, :] * g1 + conv_w[2:3, :] * gated
    return b_gate * y


def _merge_norm(x, attn_o, conv_o, gattn, gconv, wout_ref, ln_g, ln_b, alpha):
    cat = jnp.concatenate([_rms_norm(attn_o, gattn), _rms_norm(conv_o, gconv)], axis=-1)
    mix = jnp.dot(cat.astype(BF16), wout_ref[...], preferred_element_type=F32)
    return _layer_norm(alpha * x + mix, ln_g, ln_b)


def _sink_softmax(parts, sink):
    m = sink
    for s in parts:
        m = jnp.maximum(m, jnp.max(s, axis=-1, keepdims=True))
    es = [jnp.exp(s - m) for s in parts]
    den = jnp.exp(sink - m)
    for e in es:
        den = den + jnp.sum(e, axis=-1, keepdims=True)
    return [(e / den).astype(BF16) for e in es]


def _prompt_mixer_kernel(x_ref, win_ref, c_ref, sa_ref, sb_ref, sinks_ref, convw_ref, gattn_ref, gconv_ref,
                         wout_ref, lng_ref, lnb_ref,
                         h_ref, ko_ref, vo_ref, co_ref,
                         q_s, k_s, v_s, o_s, gc_s, *, alpha, n_heads, attn_w, kv_w, conv_ch):
    s = pl.program_id(1)
    last = pl.num_programs(1) - 1
    tq = x_ref.shape[0]
    q_per_kv = n_heads // N_KV_HEADS

    @pl.when(s == 0)
    def _():
        k_s[0:WINDOW, :] = jnp.zeros((WINDOW, kv_w), BF16)
        v_s[0:WINDOW, :] = jnp.zeros((WINDOW, kv_w), BF16)
        gc_s[...] = jnp.zeros(gc_s.shape, F32)

    @pl.when(s > 0)
    def _():
        k_s[0:WINDOW, :] = k_s[tq:tq + WINDOW, :]
        v_s[0:WINDOW, :] = v_s[tq:tq + WINDOW, :]

    x = x_ref[...]
    xb = x.astype(BF16)
    c, sa, sb = c_ref[...], sa_ref[...], sb_ref[...]

    def proj(lo, width):
        return jnp.dot(xb, win_ref[:, lo:lo + width], preferred_element_type=F32)

    scale = HEAD_DIM ** -0.5
    for j in range(attn_w // LANES):
        qj = _rope(proj(j * LANES, LANES), c, sa, sb)
        q_s[:, j * LANES:(j + 1) * LANES] = (qj * scale).astype(BF16)
    k = _rope(proj(attn_w, kv_w), c, sa, sb)
    v = proj(attn_w + kv_w, kv_w)
    k_s[WINDOW:WINDOW + tq, :] = k.astype(BF16)
    v_s[WINDOW:WINDOW + tq, :] = v.astype(BF16)

    @pl.when(s == last)
    def _():
        ko_ref[...] = k[tq - WINDOW:, :]
        vo_ref[...] = v[tq - WINDOW:, :]

    qi = lax.broadcasted_iota(jnp.int32, (WINDOW, 2 * WINDOW), 0)
    ci = lax.broadcasted_iota(jnp.int32, (WINDOW, 2 * WINDOW), 1)
    band = (ci > qi) & (ci <= qi + WINDOW)

    def sub_block(j, carry):
        r0 = pl.multiple_of(j * WINDOW, WINDOW)
        has_prev = (s * tq + r0) > 0
        mask = band & ((ci >= WINDOW) | has_prev)
        kk = k_s[pl.ds(r0, 2 * WINDOW), :]
        vv = v_s[pl.ds(r0, 2 * WINDOW), :]
        for hd in range(n_heads):
            kvh = hd // q_per_kv
            qh = q_s[pl.ds(r0, WINDOW), hd * HEAD_DIM:(hd + 1) * HEAD_DIM]
            kh = kk[:, kvh * HEAD_DIM:(kvh + 1) * HEAD_DIM]
            vh = vv[:, kvh * HEAD_DIM:(kvh + 1) * HEAD_DIM]
            sc = lax.dot_general(qh, kh, (((1,), (1,)), ((), ())), preferred_element_type=F32)
            sc = jnp.where(mask, sc, NEG_INF)
            (p,) = _sink_softmax([sc], sinks_ref[hd])
            o_s[pl.ds(r0, WINDOW), hd * HEAD_DIM:(hd + 1) * HEAD_DIM] = jnp.dot(
                p, vh, preferred_element_type=F32)
        return carry

    lax.fori_loop(0, tq // WINDOW, sub_block, 0)

    o3 = attn_w + 2 * kv_w
    gated = proj(o3 + 2 * conv_ch, conv_ch) * proj(o3, conv_ch)
    row = lax.broadcasted_iota(jnp.int32, (tq, 1), 0)
    conv_o = _short_conv(gated, gc_s[0:1, :], gc_s[1:2, :], row, convw_ref[...], proj(o3 + conv_ch, conv_ch))
    gc_s[0:CONV_K - 1, :] = gated[tq - (CONV_K - 1):, :]

    @pl.when(s == last)
    def _():
        co_ref[...] = gated[tq - (CONV_K - 1):, :]

    h_ref[...] = _merge_norm(x, o_s[...], conv_o, gattn_ref[...], gconv_ref[...], wout_ref,
                             lng_ref[...], lnb_ref[...], alpha)


def _prompt_mixer(x, win_b, tabs, sinks, conv_w, g_attn, g_conv, wout_b, ln_g, ln_b, *, alpha):
    bsz, seq, d = x.shape
    tq = PROMPT_TILE
    ns = seq // tq
    attn_w = g_attn.shape[-1]
    conv_ch = g_conv.shape[-1]
    n_heads = attn_w // HEAD_DIM
    kv_w = N_KV_HEADS * HEAD_DIM
    in_cols = win_b.shape[-1]
    const2 = lambda b, s: (0, 0)
    kern = functools.partial(_prompt_mixer_kernel, alpha=alpha, n_heads=n_heads, attn_w=attn_w, kv_w=kv_w,
                             conv_ch=conv_ch)
    return pl.pallas_call(
        kern,
        grid=(bsz, ns),
        in_specs=[
            pl.BlockSpec((None, tq, d), lambda b, s: (b, s, 0)),
            pl.BlockSpec((d, in_cols), const2),
            pl.BlockSpec((tq, LANES), lambda b, s: (s, 0)),
            pl.BlockSpec((tq, LANES), lambda b, s: (s, 0)),
            pl.BlockSpec((tq, LANES), lambda b, s: (s, 0)),
            pl.BlockSpec(memory_space=pltpu.SMEM),
            pl.BlockSpec((CONV_K, conv_ch), const2),
            pl.BlockSpec((1, attn_w), const2),
            pl.BlockSpec((1, conv_ch), const2),
            pl.BlockSpec((attn_w + conv_ch, d), const2),
            pl.BlockSpec((1, d), const2),
            pl.BlockSpec((1, d), const2),
        ],
        out_specs=[
            pl.BlockSpec((tq, d), lambda b, s: (b * ns + s, 0)),
            pl.BlockSpec((None, WINDOW, kv_w), lambda b, s: (b, 0, 0)),
            pl.BlockSpec((None, WINDOW, kv_w), lambda b, s: (b, 0, 0)),
            pl.BlockSpec((None, CONV_K - 1, conv_ch), lambda b, s: (b, 0, 0)),
        ],
        out_shape=[
            jax.ShapeDtypeStruct((bsz * seq, d), F32),
            jax.ShapeDtypeStruct((bsz, WINDOW, kv_w), F32),
            jax.ShapeDtypeStruct((bsz, WINDOW, kv_w), F32),
            jax.ShapeDtypeStruct((bsz, CONV_K - 1, conv_ch), F32),
        ],
        scratch_shapes=[
            pltpu.VMEM((tq, attn_w), BF16),
            pltpu.VMEM((tq + WINDOW, kv_w), BF16),
            pltpu.VMEM((tq + WINDOW, kv_w), BF16),
            pltpu.VMEM((tq, attn_w), F32),
            pltpu.VMEM((SUBLANES, conv_ch), F32),
        ],
        compiler_params=pltpu.CompilerParams(dimension_semantics=("arbitrary", "arbitrary"),
                                             vmem_limit_bytes=VMEM_LIMIT_BYTES),
        name="prompt_mixer",
    )(x, win_b, *tabs, sinks, conv_w, g_attn, g_conv, wout_b, ln_g, ln_b)


def _sample_mixer_kernel(x_ref, ck_ref, cv_ref, st_ref, win_ref, c_ref, sa_ref, sb_ref, sinks_ref,
                         convw_ref, gattn_ref, gconv_ref, wout_ref, lng_ref, lnb_ref,
                         h_ref, ko_ref, vo_ref, co_ref, *, alpha, n_heads, attn_w, kv_w, conv_ch, dec_seq):
    nb, win = ck_ref.shape[0], ck_ref.shape[1]
    rows = nb * dec_seq
    q_per_kv = n_heads // N_KV_HEADS
    x = x_ref[...]
    xb = x.astype(BF16)
    c, sa, sb = c_ref[...], sa_ref[...], sb_ref[...]

    def proj(lo, width):
        return jnp.dot(xb, win_ref[:, lo:lo + width], preferred_element_type=F32)

    scale = HEAD_DIM ** -0.5
    k = _rope(proj(attn_w, kv_w), c, sa, sb)
    v = proj(attn_w + kv_w, kv_w)
    k3 = k.reshape(nb, dec_seq, kv_w)
    v3 = v.reshape(nb, dec_seq, kv_w)
    ck = ck_ref[...]
    cv = cv_ref[...]
    ko_ref[:, 0:win - dec_seq, :] = ck[:, dec_seq:, :]
    ko_ref[:, win - dec_seq:, :] = k3
    vo_ref[:, 0:win - dec_seq, :] = cv[:, dec_seq:, :]
    vo_ref[:, win - dec_seq:, :] = v3
    ckb, cvb, k3b, v3b = ck.astype(BF16), cv.astype(BF16), k3.astype(BF16), v3.astype(BF16)

    qrows = q_per_kv * dec_seq
    qi = lax.broadcasted_iota(jnp.int32, (nb, qrows, win), 1) % dec_seq
    mask_c = lax.broadcasted_iota(jnp.int32, (nb, qrows, win), 2) > qi + (win - WINDOW)
    qn = lax.broadcasted_iota(jnp.int32, (nb, qrows, dec_seq), 1) % dec_seq
    mask_n = lax.broadcasted_iota(jnp.int32, (nb, qrows, dec_seq), 2) <= qn
    sink_row = lax.broadcasted_iota(jnp.int32, (nb, qrows, 1), 1) // dec_seq

    q_chunks = [_rope(proj(j * LANES, LANES), c, sa, sb) * scale for j in range(attn_w // LANES)]
    heads_out = []
    for kvh in range(N_KV_HEADS):
        qs = []
        for g in range(q_per_kv):
            lo = (kvh * q_per_kv + g) * HEAD_DIM
            qh = q_chunks[lo // LANES][:, lo % LANES:lo % LANES + HEAD_DIM]
            qs.append(qh.reshape(nb, dec_seq, HEAD_DIM))
        qg = jnp.concatenate(qs, axis=1).astype(BF16)
        sl = slice(kvh * HEAD_DIM, (kvh + 1) * HEAD_DIM)
        sc_c = jnp.einsum("bqd,bkd->bqk", qg, ckb[:, :, sl], preferred_element_type=F32)
        sc_n = jnp.einsum("bqd,bkd->bqk", qg, k3b[:, :, sl], preferred_element_type=F32)
        sc_c = jnp.where(mask_c, sc_c, NEG_INF)
        sc_n = jnp.where(mask_n, sc_n, NEG_INF)
        sink = jnp.zeros((nb, qrows, 1), F32)
        for g in range(q_per_kv):
            sink = jnp.where(sink_row == g, sinks_ref[kvh * q_per_kv + g], sink)
        p_c, p_n = _sink_softmax([sc_c, sc_n], sink)
        og = (jnp.einsum("bqk,bkd->bqd", p_c, cvb[:, :, sl], preferred_element_type=F32)
              + jnp.einsum("bqk,bkd->bqd", p_n, v3b[:, :, sl], preferred_element_type=F32))
        for g in range(q_per_kv):
            heads_out.append(og[:, g * dec_seq:(g + 1) * dec_seq, :].reshape(rows, HEAD_DIM))
    attn_o = jnp.concatenate(heads_out, axis=-1)

    o3 = attn_w + 2 * kv_w
    gated = proj(o3 + 2 * conv_ch, conv_ch) * proj(o3, conv_ch)
    st = st_ref[...]
    prev2 = jnp.broadcast_to(st[:, 0:1, :], (nb, dec_seq, conv_ch)).reshape(rows, conv_ch)
    prev1 = jnp.broadcast_to(st[:, 1:2, :], (nb, dec_seq, conv_ch)).reshape(rows, conv_ch)
    row = lax.broadcasted_iota(jnp.int32, (rows, 1), 0) % dec_seq
    conv_o = _short_conv(gated, prev2, prev1, row, convw_ref[...], proj(o3 + conv_ch, conv_ch))
    co_ref[...] = gated.reshape(nb, dec_seq, conv_ch)[:, dec_seq - (CONV_K - 1):, :]

    h_ref[...] = _merge_norm(x, attn_o, conv_o, gattn_ref[...], gconv_ref[...], wout_ref,
                             lng_ref[...], lnb_ref[...], alpha)


def _sample_mixer(x, ck, cv, st, win_b, tabs, sinks, conv_w, g_attn, g_conv, wout_b, ln_g, ln_b, *, alpha):
    dec_b, dec_seq, d = x.shape
    assert dec_seq >= CONV_K - 1 and dec_seq % SUBLANES == 0
    nb = SAMPLE_SEQS
    rows = nb * dec_seq
    win = ck.shape[1]
    attn_w = g_attn.shape[-1]
    conv_ch = g_conv.shape[-1]
    n_heads = attn_w // HEAD_DIM
    kv_w = N_KV_HEADS * HEAD_DIM
    in_cols = win_b.shape[-1]
    const2 = lambda i: (0, 0)
    kern = functools.partial(_sample_mixer_kernel, alpha=alpha, n_heads=n_heads, attn_w=attn_w, kv_w=kv_w,
                             conv_ch=conv_ch, dec_seq=dec_seq)
    return pl.pallas_call(
        kern,
        grid=(dec_b // nb,),
        in_specs=[
            pl.BlockSpec((rows, d), lambda i: (i, 0)),
            pl.BlockSpec((nb, win, kv_w), lambda i: (i, 0, 0)),
            pl.BlockSpec((nb, win, kv_w), lambda i: (i, 0, 0)),
            pl.BlockSpec((nb, CONV_K - 1, conv_ch), lambda i: (i, 0, 0)),
            pl.BlockSpec((d, in_cols), const2),
            pl.BlockSpec((rows, LANES), const2),
            pl.BlockSpec((rows, LANES), const2),
            pl.BlockSpec((rows, LANES), const2),
            pl.BlockSpec(memory_space=pltpu.SMEM),
            pl.BlockSpec((CONV_K, conv_ch), const2),
            pl.BlockSpec((1, attn_w), const2),
            pl.BlockSpec((1, conv_ch), const2),
            pl.BlockSpec((attn_w + conv_ch, d), const2),
            pl.BlockSpec((1, d), const2),
            pl.BlockSpec((1, d), const2),
        ],
        out_specs=[
            pl.BlockSpec((rows, d), lambda i: (i, 0)),
            pl.BlockSpec((nb, win, kv_w), lambda i: (i, 0, 0)),
            pl.BlockSpec((nb, win, kv_w), lambda i: (i, 0, 0)),
            pl.BlockSpec((nb, CONV_K - 1, conv_ch), lambda i: (i, 0, 0)),
        ],
        out_shape=[
            jax.ShapeDtypeStruct((dec_b * dec_seq, d), F32),
            jax.ShapeDtypeStruct((dec_b, win, kv_w), F32),
            jax.ShapeDtypeStruct((dec_b, win, kv_w), F32),
            jax.ShapeDtypeStruct((dec_b, CONV_K - 1, conv_ch), F32),
        ],
        compiler_params=pltpu.CompilerParams(dimension_semantics=("arbitrary",),
                                             vmem_limit_bytes=VMEM_LIMIT_BYTES),
        name="sample_mixer",
    )(x.reshape(dec_b * dec_seq, d), ck, cv, st, win_b, *tabs, sinks, conv_w, g_attn, g_conv, wout_b,
      ln_g, ln_b)


def _over_experts(fn, x):
    return fn(fn(x, axis=0, keepdims=True), axis=1, keepdims=True)


def _two_group_specs(tm, d, n_prompt_tiles):
    return [pl.BlockSpec((tm, d), lambda i, *_: (jnp.minimum(i, n_prompt_tiles - 1), 0)),
            pl.BlockSpec((tm, d), lambda i, *_: (jnp.maximum(i - n_prompt_tiles, 0), 0))]


def _route_kernel(hp_ref, hs_ref, rwt_ref, bias_ref, eidx_ref, rank_ref, gate_ref, cnt_ref, *, n_experts,
                  n_prompt_tiles):
    i = pl.program_id(0)
    tm = hp_ref.shape[0]
    per_group = n_experts // N_EXPERT_GROUPS
    shape3 = (N_EXPERT_GROUPS, per_group, tm)

    h = jnp.where(i < n_prompt_tiles, hp_ref[...], hs_ref[...])
    logits = lax.dot_general(rwt_ref[...], h.astype(BF16), (((1,), (1,)), ((), ())),
                             preferred_element_type=F32)
    scores = jax.nn.sigmoid(logits)
    sel = scores + bias_ref[...]
    scores3 = scores.reshape(shape3)
    grp = sel.reshape(shape3)
    member = lax.broadcasted_iota(jnp.int32, shape3, 1).astype(F32)
    group = lax.broadcasted_iota(jnp.int32, shape3, 0).astype(F32)
    expert = group * per_group + member

    m1 = jnp.max(grp, axis=1, keepdims=True)
    f1 = jnp.min(jnp.where(grp == m1, member, float(per_group)), axis=1, keepdims=True)
    m2 = jnp.max(jnp.where(member == f1, NEG_INF, grp), axis=1, keepdims=True)
    gscore = m1 + m2

    gid = lax.broadcasted_iota(jnp.int32, gscore.shape, 0).astype(F32)
    gmask = jnp.zeros(gscore.shape, F32)
    cur = gscore
    for _ in range(TOPK_GROUPS):
        mx = jnp.max(cur, axis=0, keepdims=True)
        pick = gid == jnp.min(jnp.where(cur == mx, gid, float(N_EXPERT_GROUPS)), axis=0, keepdims=True)
        gmask = jnp.where(pick, 1.0, gmask)
        cur = jnp.where(pick, NEG_INF, cur)

    cand = jnp.where(gmask > 0.0, grp, NEG_INF)
    chosen = jnp.zeros(shape3, F32)
    picks, firsts, weights = [], [], []
    for _ in range(TOP_K):
        mx = _over_experts(jnp.max, cand)
        first = _over_experts(jnp.min, jnp.where(cand == mx, expert, float(n_experts)))
        pick = expert == first
        picks.append(pick)
        firsts.append(first)
        weights.append(_over_experts(jnp.sum, jnp.where(pick, scores3, 0.0)))
        chosen = jnp.where(pick, 1.0, chosen)
        cand = jnp.where(pick, NEG_INF, cand)
    wsum = weights[0]
    for w in weights[1:]:
        wsum = wsum + w

    chosen_b = chosen.reshape(n_experts, tm).astype(BF16)
    earlier = (lax.broadcasted_iota(jnp.int32, (tm, tm), 0)
               < lax.broadcasted_iota(jnp.int32, (tm, tm), 1)).astype(BF16)
    before3 = jnp.dot(chosen_b, earlier, preferred_element_type=F32).reshape(shape3)

    pad = SUBLANES - TOP_K
    eidx = [f.reshape(1, tm).astype(jnp.int32) for f in firsts]
    rank = [_over_experts(jnp.sum, jnp.where(p, before3, 0.0)).reshape(1, tm).astype(jnp.int32) for p in picks]
    gate = [(w / wsum * ROUTED_SCALE).reshape(1, tm) for w in weights]
    eidx_ref[...] = jnp.concatenate(eidx + [jnp.zeros((pad, tm), jnp.int32)], axis=0)
    rank_ref[...] = jnp.concatenate(rank + [jnp.zeros((pad, tm), jnp.int32)], axis=0)
    gate_ref[...] = jnp.concatenate(gate + [jnp.zeros((pad, tm), F32)], axis=0)

    cnt_ref[...] = lax.dot_general(jnp.ones((SUBLANES, tm), BF16), chosen_b, (((1,), (1,)), ((), ())),
                                   preferred_element_type=F32)


def _route(h_p, h_s, rwt_b, bias_col):
    d = h_p.shape[1]
    t_all = h_p.shape[0] + h_s.shape[0]
    n_experts = rwt_b.shape[0]
    tm = MOE_TILE
    nt = t_all // tm
    npt = h_p.shape[0] // tm
    row_spec = pl.BlockSpec((SUBLANES, tm), lambda i: (0, i))
    return pl.pallas_call(
        functools.partial(_route_kernel, n_experts=n_experts, n_prompt_tiles=npt),
        grid=(nt,),
        in_specs=_two_group_specs(tm, d, npt) + [
            pl.BlockSpec((n_experts, d), lambda i: (0, 0)),
            pl.BlockSpec((n_experts, 1), lambda i: (0, 0)),
        ],
        out_specs=[row_spec, row_spec, row_spec, pl.BlockSpec((SUBLANES, n_experts), lambda i: (i, 0))],
        out_shape=[
            jax.ShapeDtypeStruct((SUBLANES, t_all), jnp.int32),
            jax.ShapeDtypeStruct((SUBLANES, t_all), jnp.int32),
            jax.ShapeDtypeStruct((SUBLANES, t_all), F32),
            jax.ShapeDtypeStruct((nt * SUBLANES, n_experts), F32),
        ],
        compiler_params=pltpu.CompilerParams(dimension_semantics=("arbitrary",),
                                             vmem_limit_bytes=VMEM_LIMIT_BYTES),
        name="route",
    )(h_p, h_s, rwt_b, bias_col)


def _for_each_part(rows, max_rows, fn, chunk_rows=None):
    top = max_rows if chunk_rows is None else chunk_rows // 2
    for size in [SUBLANES << b for b in range((top // SUBLANES).bit_length())]:
        @pl.when((rows & size) != 0)
        def _(size=size):
            fn(pl.multiple_of(rows & (size - 1), SUBLANES), size)
    if chunk_rows is not None:
        base = rows & (chunk_rows - 1)

        def chunk(j, c):
            fn(pl.multiple_of(base + j * chunk_rows, SUBLANES), chunk_rows)
            return c

        lax.fori_loop(0, lax.shift_right_logical(rows, chunk_rows.bit_length() - 1), chunk, 0)


def _top_bit(n):
    return 1 << (n.bit_length() - 1)


def _pack_rows(x):
    rows, two_w = x.shape
    w = two_w // 2
    x3 = x.reshape(rows // SUBLANES, SUBLANES, two_w)
    halves = jnp.concatenate([x3[:, :, :w], x3[:, :, w:]], axis=1).astype(BF16)
    return pltpu.bitcast(halves, jnp.uint32).reshape(rows, w)


def _unpack_rows(p):
    rows, w = p.shape
    halves = pltpu.bitcast(p.reshape(rows // SUBLANES, SUBLANES, w), BF16).astype(F32)
    return jnp.concatenate([halves[:, :SUBLANES, :], halves[:, SUBLANES:, :]], axis=-1).reshape(rows, 2 * w)


def _local_rows(tm, n_experts):
    n = TOP_K * tm + n_experts * (SUBLANES - 1)
    return -(-n // SORT_CHUNK) * SORT_CHUNK


def _dispatch_kernel(cnt_ref, ls_ref, gs_ref, tot_ref, zrow_ref, lp_ref, hp_ref, hs_ref, xs_hbm,
                     lbuf, zero_s, sem, *, n_experts, blk, n_prompt_tiles):
    i = pl.program_id(0)
    tm, d = hp_ref.shape
    n = TOP_K * tm
    nl = lbuf.shape[1]
    slot = i % 2

    def tile_wait(tile):
        pltpu.make_async_copy(lbuf.at[0, pl.ds(0, n)], xs_hbm.at[pl.ds(0, n)], sem).wait()
        _for_each_part(tot_ref[tile] - n, _top_bit(n_experts * (SUBLANES - 1)), lambda off, size:
                       pltpu.make_async_copy(lbuf.at[0, pl.ds(0, size)], xs_hbm.at[pl.ds(0, size)], sem).wait())

    @pl.when(i == 0)
    def _():
        zero_s[...] = jnp.zeros(zero_s.shape, zero_s.dtype)

        def zcopy(e):
            return pltpu.make_async_copy(zero_s, xs_hbm.at[pl.ds(pl.multiple_of(zrow_ref[e], blk), blk)], sem)

        def start(e, c):
            zcopy(e).start()
            return c

        def wait(e, c):
            zcopy(e).wait()
            return c

        lax.fori_loop(0, n_experts, start, 0)
        lax.fori_loop(0, n_experts, wait, 0)

    hb = jnp.where(i < n_prompt_tiles, hp_ref[...], hs_ref[...]).astype(BF16)
    lp = lp_ref[...]
    def sort_chunk(r):
        rows = r * SORT_CHUNK + lax.broadcasted_iota(jnp.int32, (SORT_CHUNK, tm), 0)
        hit = jnp.zeros((SORT_CHUNK, tm), F32)
        for k in range(TOP_K):
            hit = jnp.where(rows == lp[k:k + 1, :], 1.0, hit)
        lbuf[slot, pl.ds(r * SORT_CHUNK, SORT_CHUNK), :] = _pack_rows(
            jnp.dot(hit.astype(BF16), hb, preferred_element_type=F32))

    for r in range(nl // SORT_CHUNK):
        if (r + 1) * SORT_CHUNK <= n:
            sort_chunk(r)
        else:
            pl.when(tot_ref[i] > r * SORT_CHUNK)(functools.partial(sort_chunk, r))

    @pl.when(i > 0)
    def _():
        tile_wait(i - 1)

    def per_expert(e, c):
        idx = i * n_experts + e
        l0, g0 = ls_ref[idx], gs_ref[idx]
        _for_each_part(cnt_ref[idx], tm, lambda off, size: pltpu.make_async_copy(
            lbuf.at[slot, pl.ds(pl.multiple_of(l0 + off, SUBLANES), size)],
            xs_hbm.at[pl.ds(pl.multiple_of(g0 + off, SUBLANES), size)], sem).start(), RUN_CHUNK)
        return c

    lax.fori_loop(0, n_experts, per_expert, 0, unroll=RUN_UNROLL)

    @pl.when(i == pl.num_programs(0) - 1)
    def _():
        tile_wait(i)


def _dispatch(tables, zrow, lp, h_p, h_s, *, n_rows, blk):
    d = h_p.shape[1]
    t_all = h_p.shape[0] + h_s.shape[0]
    tm = MOE_TILE
    npt = h_p.shape[0] // tm
    n_experts = zrow.shape[0]
    return pl.pallas_call(
        functools.partial(_dispatch_kernel, n_experts=n_experts, blk=blk, n_prompt_tiles=npt),
        grid_spec=pltpu.PrefetchScalarGridSpec(
            num_scalar_prefetch=5,
            grid=(t_all // tm,),
            in_specs=[pl.BlockSpec((SUBLANES, tm), lambda i, *_: (0, i))] + _two_group_specs(tm, d, npt),
            out_specs=pl.BlockSpec(memory_space=pl.ANY),
            scratch_shapes=[
                pltpu.VMEM((2, _local_rows(tm, n_experts), d // 2), jnp.uint32),
                pltpu.VMEM((blk, d // 2), jnp.uint32),
                pltpu.SemaphoreType.DMA,
            ],
        ),
        out_shape=jax.ShapeDtypeStruct((n_rows, d // 2), jnp.uint32),
        compiler_params=pltpu.CompilerParams(dimension_semantics=("arbitrary",),
                                             vmem_limit_bytes=VMEM_LIMIT_BYTES),
        name="dispatch",
    )(*tables, zrow, lp, h_p, h_s)


def _silu(x):
    return x * jax.nn.sigmoid(x)


def _expert_kernel(be_ref, nact_ref, xs_ref, wg_ref, wu_ref, wd_ref, ys_ref, wg_s, wu_s, wd_s):
    b = pl.program_id(0)

    @pl.when(b < nact_ref[0])
    def _():
        @pl.when((b == 0) | (be_ref[b] != be_ref[jnp.maximum(b - 1, 0)]))
        def _():
            wg_s[...] = wg_ref[...].astype(BF16)
            wu_s[...] = wu_ref[...].astype(BF16)
            wd_s[...] = wd_ref[...].astype(BF16)

        xb = _unpack_rows(xs_ref[...]).astype(BF16)
        hid = _silu(jnp.dot(xb, wg_s[...], preferred_element_type=F32)) * jnp.dot(
            xb, wu_s[...], preferred_element_type=F32)
        ys_ref[...] = _pack_rows(jnp.dot(hid.astype(BF16), wd_s[...], preferred_element_type=F32))


def _experts(block_expert, nact, xs, w_gate, w_up, w_down, *, blk):
    n_rows, dw = xs.shape
    d = w_gate.shape[-2]
    ff = w_gate.shape[-1]
    n_blocks = n_rows // blk

    def active(b, be, na):
        return jnp.minimum(b, na[0] - 1)

    return pl.pallas_call(
        _expert_kernel,
        grid_spec=pltpu.PrefetchScalarGridSpec(
            num_scalar_prefetch=2,
            grid=(n_blocks,),
            in_specs=[
                pl.BlockSpec((blk, dw), lambda b, be, na: (active(b, be, na), 0)),
                pl.BlockSpec((None, d, ff), lambda b, be, na: (be[active(b, be, na)], 0, 0)),
                pl.BlockSpec((None, d, ff), lambda b, be, na: (be[active(b, be, na)], 0, 0)),
                pl.BlockSpec((None, ff, d), lambda b, be, na: (be[active(b, be, na)], 0, 0)),
            ],
            out_specs=pl.BlockSpec((blk, dw), lambda b, be, na: (active(b, be, na), 0)),
            scratch_shapes=[
                pltpu.VMEM((d, ff), BF16),
                pltpu.VMEM((d, ff), BF16),
                pltpu.VMEM((ff, d), BF16),
            ],
        ),
        out_shape=jax.ShapeDtypeStruct((n_rows, dw), jnp.uint32),
        compiler_params=pltpu.CompilerParams(dimension_semantics=("arbitrary",),
                                             vmem_limit_bytes=VMEM_LIMIT_BYTES),
        name="experts",
    )(block_expert, nact, xs, w_gate, w_up, w_down)


def _combine_kernel(cnt_ref, ls_ref, gs_ref, tot_ref, ys_hbm, lpt_ref, gate_ref, hp_ref, hs_ref, wsg_ref, wsu_ref,
                    wsd_ref, lng_ref, lnb_ref, yp_ref, ysm_ref, ybuf, moe_s, sems, *, alpha, n_experts,
                    n_prompt_tiles):
    i = pl.program_id(0)
    tm, d = hp_ref.shape
    n = TOP_K * tm
    nl = ybuf.shape[1]
    slot = i % 2

    def fetch(tile, to):
        def per_expert(e, c):
            idx = tile * n_experts + e
            l0, g0 = ls_ref[idx], gs_ref[idx]
            _for_each_part(cnt_ref[idx], tm, lambda off, size: pltpu.make_async_copy(
                ys_hbm.at[pl.ds(pl.multiple_of(g0 + off, SUBLANES), size)],
                ybuf.at[to, pl.ds(pl.multiple_of(l0 + off, SUBLANES), size)], sems.at[to]).start(), RUN_CHUNK)
            return c

        lax.fori_loop(0, n_experts, per_expert, 0, unroll=RUN_UNROLL)

    @pl.when(i == 0)
    def _():
        ybuf[...] = jnp.zeros(ybuf.shape, ybuf.dtype)
        fetch(0, 0)

    @pl.when(i + 1 < pl.num_programs(0))
    def _():
        fetch(i + 1, 1 - slot)

    h = jnp.where(i < n_prompt_tiles, hp_ref[...], hs_ref[...])
    hb = h.astype(BF16)
    hid = _silu(jnp.dot(hb, wsg_ref[...], preferred_element_type=F32)) * jnp.dot(
        hb, wsu_ref[...], preferred_element_type=F32)
    shared = jnp.dot(hid.astype(BF16), wsd_ref[...], preferred_element_type=F32)

    pltpu.make_async_copy(ys_hbm.at[pl.ds(0, n)], ybuf.at[slot, pl.ds(0, n)], sems.at[slot]).wait()
    _for_each_part(tot_ref[i] - n, _top_bit(n_experts * (SUBLANES - 1)), lambda off, size: pltpu.make_async_copy(
        ys_hbm.at[pl.ds(0, size)], ybuf.at[slot, pl.ds(0, size)], sems.at[slot]).wait())

    lpt = lpt_ref[...]
    gate = gate_ref[...]
    moe_s[...] = shared

    def add_chunk(r):
        cols = r * SORT_CHUNK + lax.broadcasted_iota(jnp.int32, (tm, SORT_CHUNK), 1)
        g = jnp.zeros((tm, SORT_CHUNK), F32)
        for k in range(TOP_K):
            g = jnp.where(cols == lpt[:, k:k + 1], gate[:, k:k + 1], g)
        yb = _unpack_rows(ybuf[slot, pl.ds(r * SORT_CHUNK, SORT_CHUNK), :]).astype(BF16)
        g_hi = g.astype(BF16)
        g_lo = (g - g_hi.astype(F32)).astype(BF16)
        moe_s[...] += (jnp.dot(g_hi, yb, preferred_element_type=F32)
                       + jnp.dot(g_lo, yb, preferred_element_type=F32))

    for r in range(nl // SORT_CHUNK):
        if (r + 1) * SORT_CHUNK <= n:
            add_chunk(r)
        else:
            pl.when(tot_ref[i] > r * SORT_CHUNK)(functools.partial(add_chunk, r))
    y = _layer_norm(alpha * h + moe_s[...], lng_ref[...], lnb_ref[...])

    @pl.when(i < n_prompt_tiles)
    def _():
        yp_ref[...] = y

    @pl.when(i >= n_prompt_tiles)
    def _():
        ysm_ref[...] = y


def _combine(tables, ys, lp_t, gates_t, h_p, h_s, wsg_b, wsu_b, wsd_b, ln_g, ln_b, *, alpha):
    t_prompt, d = h_p.shape
    t_all = t_prompt + h_s.shape[0]
    tm = MOE_TILE
    ff = wsg_b.shape[-1]
    npt = t_prompt // tm
    n_experts = tables[0].shape[0] // (t_all // tm)
    const2 = lambda i, *_: (0, 0)
    return pl.pallas_call(
        functools.partial(_combine_kernel, alpha=alpha, n_experts=n_experts, n_prompt_tiles=npt),
        grid_spec=pltpu.PrefetchScalarGridSpec(
            num_scalar_prefetch=4,
            grid=(t_all // tm,),
            in_specs=[
                pl.BlockSpec(memory_space=pl.ANY),
                pl.BlockSpec((tm, SUBLANES), lambda i, *_: (i, 0)),
                pl.BlockSpec((tm, SUBLANES), lambda i, *_: (i, 0)),
                *_two_group_specs(tm, d, npt),
                pl.BlockSpec((d, ff), const2),
                pl.BlockSpec((d, ff), const2),
                pl.BlockSpec((ff, d), const2),
                pl.BlockSpec((1, d), const2),
                pl.BlockSpec((1, d), const2),
            ],
            out_specs=[
                pl.BlockSpec((tm, d), lambda i, *_: (jnp.minimum(i, npt - 1), 0)),
                pl.BlockSpec((tm, d), lambda i, *_: (jnp.maximum(i - npt, 0), 0)),
            ],
            scratch_shapes=[
                pltpu.VMEM((2, _local_rows(tm, n_experts), d // 2), jnp.uint32),
                pltpu.VMEM((tm, d), F32),
                pltpu.SemaphoreType.DMA((2,)),
            ],
        ),
        out_shape=[
            jax.ShapeDtypeStruct((t_prompt, d), F32),
            jax.ShapeDtypeStruct((t_all - t_prompt, d), F32),
        ],
        compiler_params=pltpu.CompilerParams(dimension_semantics=("arbitrary",),
                                             vmem_limit_bytes=VMEM_LIMIT_BYTES),
        name="combine",
    )(*tables, ys, lp_t, gates_t, h_p, h_s, wsg_b, wsu_b, wsd_b, ln_g, ln_b)


def _moe(h_p, h_s, router_w, router_bias, w_gate, w_up, w_down, ws_gate, ws_up, ws_down, ln_g, ln_b, *, alpha):
    t_all = h_p.shape[0] + h_s.shape[0]
    n_experts = router_w.shape[-1]
    blk = EXPERT_BLOCK
    tm = MOE_TILE
    nt = t_all // tm
    eidx8, lrank8, gate8, cnt8 = _route(h_p, h_s, router_w.T.astype(BF16), router_bias.reshape(n_experts, 1))

    cnt = cnt8.reshape(nt, SUBLANES, n_experts)[:, 0, :].astype(jnp.int32)
    cnt = (cnt + SUBLANES - 1) // SUBLANES * SUBLANES
    counts = jnp.sum(cnt, axis=0)
    blocks_per_e = (counts + blk - 1) // blk
    block_end = jnp.cumsum(blocks_per_e)
    pad_start = (block_end - blocks_per_e) * blk
    n_blocks = -(-(t_all * TOP_K + nt * n_experts * (SUBLANES - 1)) // blk) + n_experts
    block_expert = jnp.minimum(jnp.sum(block_end[None, :] <= jnp.arange(n_blocks)[:, None], axis=1),
                               n_experts - 1).astype(jnp.int32)
    nact = block_end[-1:].astype(jnp.int32)
    zrow = (jnp.maximum(block_end - 1, 0) * blk).astype(jnp.int32)
    gstart = pad_start[None, :] + jnp.cumsum(cnt, axis=0) - cnt
    lstart = jnp.cumsum(cnt, axis=1) - cnt
    tables = tuple(a.reshape(-1).astype(jnp.int32) for a in (cnt, lstart, gstart, jnp.sum(cnt, axis=1)))
    lstart_tok = jnp.repeat(lstart, tm, axis=0)
    lp8 = jnp.sum(jnp.where(eidx8[..., None] == jnp.arange(n_experts), lstart_tok[None], 0), axis=-1) + lrank8
    lp8 = lp8.astype(jnp.int32)

    xs = _dispatch(tables, zrow, lp8, h_p, h_s, n_rows=n_blocks * blk, blk=blk)
    ys = _experts(block_expert, nact, xs, w_gate, w_up, w_down, blk=blk)
    return _combine(tables, ys, lp8.T, gate8.T, h_p, h_s, ws_gate.astype(BF16), ws_up.astype(BF16), ws_down.astype(BF16),
                    ln_g, ln_b, alpha=alpha)


def kernel(x_prompt, x_sample, cache_k, cache_v, state_conv, w_in, attn_sinks, conv_w, g_attn_out, g_conv_out, w_out, ln1_g, ln1_b, router_w, router_bias, w_gate, w_up, w_down, ws_gate, ws_up, ws_down, ln2_g, ln2_b):
    depth = w_in.shape[0]
    bsz, seq, d = x_prompt.shape
    dec_b, dec_seq, _ = x_sample.shape
    win = cache_k.shape[2]
    kv_w = N_KV_HEADS * HEAD_DIM
    t_prompt = bsz * seq
    t_all = t_prompt + dec_b * dec_seq
    alpha = (2.0 * depth) ** 0.25
    assert win == WINDOW and seq % PROMPT_TILE == 0 and dec_b % SAMPLE_SEQS == 0
    assert t_prompt % MOE_TILE == 0 and (t_all - t_prompt) % MOE_TILE == 0
    assert MOE_TILE & (MOE_TILE - 1) == 0 and (TOP_K * MOE_TILE) % SORT_CHUNK == 0

    tabs_p = _rope_tables(jnp.arange(seq))
    tabs_s = tuple(jnp.tile(t, (SAMPLE_SEQS, 1)) for t in _rope_tables(PAST_LEN + jnp.arange(dec_seq)))
    row = lambda a: a.reshape(1, -1)

    xp, xs = x_prompt, x_sample
    outs = [[] for _ in range(6)]
    for l in range(depth):
        win_b, wout_b = w_in[l].astype(BF16), w_out[l].astype(BF16)
        shared = (attn_sinks[l], conv_w[l], row(g_attn_out[l]), row(g_conv_out[l]), wout_b, row(ln1_g[l]),
                  row(ln1_b[l]))
        h_p, kp, vp, cp = _prompt_mixer(xp, win_b, tabs_p, *shared, alpha=alpha)
        h_s, kn, vn, cn = _sample_mixer(xs, cache_k[l].reshape(dec_b, win, kv_w),
                                        cache_v[l].reshape(dec_b, win, kv_w), state_conv[l], win_b, tabs_s,
                                        *shared, alpha=alpha)
        yp, ys = _moe(h_p, h_s, router_w[l], router_bias[l], w_gate[l], w_up[l], w_down[l], ws_gate[l], ws_up[l],
                      ws_down[l], row(ln2_g[l]), row(ln2_b[l]), alpha=alpha)
        xp, xs = yp.reshape(bsz, seq, d), ys.reshape(dec_b, dec_seq, d)
        heads = lambda a: a.reshape(a.shape[0], win, N_KV_HEADS, HEAD_DIM)
        for o, a in zip(outs, (heads(kp), heads(vp), cp, heads(kn), heads(vn), cn)):
            o.append(a)
    return (xp, xs) + tuple(jnp.stack(o, axis=0) for o in outs)
```

```python
import functools

import jax
import jax.numpy as jnp
from jax import lax
from jax.experimental import pallas as pl
from jax.experimental.pallas import tpu as pltpu

PAST_LEN = 16384
WINDOW = 128
HEAD_DIM = 64
N_KV_HEADS = 2
ROT_DIM = HEAD_DIM // 4
ROPE_THETA = 500000.0
CONV_K = 3
TOP_K = 6
N_EXPERT_GROUPS = 8
TOPK_GROUPS = 4
ROUTED_SCALE = 2.5
LN_EPS = 1e-5
RMS_EPS = 1e-6

LANES = 128
SUBLANES = 8
VMEM_LIMIT_BYTES = 56 * 1024 * 1024

PROMPT_TILE = 512
SAMPLE_SEQS = 16
MOE_TILE = 256
SORT_CHUNK = 256
EXPERT_BLOCK = 1024
RUN_CHUNK = 64
RUN_UNROLL = 4

F32 = jnp.float32
BF16 = jnp.bfloat16
NEG_INF = float("-inf")


def _rope_tables(positions):
    half = ROT_DIM // 2
    inv_freq = ROPE_THETA ** (-jnp.arange(0, ROT_DIM, 2, dtype=F32) / ROT_DIM)
    ang = positions.astype(F32)[:, None] * inv_freq[None, :]
    cos, sin = jnp.cos(ang), jnp.sin(ang)
    n = positions.shape[0]
    rest = HEAD_DIM - ROT_DIM
    c = jnp.concatenate([cos, cos, jnp.ones((n, rest), F32)], axis=-1)
    sa = jnp.concatenate([-sin, jnp.zeros((n, half + rest), F32)], axis=-1)
    sb = jnp.concatenate([jnp.zeros((n, half), F32), sin, jnp.zeros((n, rest), F32)], axis=-1)
    reps = LANES // HEAD_DIM
    return jnp.tile(c, (1, reps)), jnp.tile(sa, (1, reps)), jnp.tile(sb, (1, reps))


def _rope(x, c, sa, sb):
    half = ROT_DIM // 2
    return x * c + pltpu.roll(x, LANES - half, 1) * sa + pltpu.roll(x, half, 1) * sb


def _rms_norm(x, g):
    return x * lax.rsqrt(jnp.mean(jnp.square(x), axis=-1, keepdims=True) + RMS_EPS) * g


def _layer_norm(x, g, b):
    mu = jnp.mean(x, axis=-1, keepdims=True)
    var = jnp.mean(jnp.square(x - mu), axis=-1, keepdims=True)
    return (x - mu) * lax.rsqrt(var + LN_EPS) * g + b


def _short_conv(gated, prev2, prev1, row, conv_w, b_gate):
    g1 = pltpu.roll(gated, 1, 0)
    g2 = pltpu.roll(gated, 2, 0)
    g1 = jnp.where(row == 0, prev1, g1)
    g2 = jnp.where(row == 0, prev2, jnp.where(row == 1, prev1, g2))
    y = conv_w[0:1, :] * g2 + conv_w[1:2, :] * g1 + conv_w[2:3, :] * gated
    return b_gate * y


def _merge_norm(x, attn_o, conv_o, gattn, gconv, wout_ref, ln_g, ln_b, alpha):
    cat = jnp.concatenate([_rms_norm(attn_o, gattn), _rms_norm(conv_o, gconv)], axis=-1)
    mix = jnp.dot(cat.astype(BF16), wout_ref[...], preferred_element_type=F32)
    return _layer_norm(alpha * x + mix, ln_g, ln_b)


def _sink_softmax(parts, sink):
    m = sink
    for s in parts:
        m = jnp.maximum(m, jnp.max(s, axis=-1, keepdims=True))
    es = [jnp.exp(s - m) for s in parts]
    den = jnp.exp(sink - m)
    for e in es:
        den = den + jnp.sum(e, axis=-1, keepdims=True)
    return [(e / den).astype(BF16) for e in es]


def _prompt_mixer_kernel(x_ref, win_ref, c_ref, sa_ref, sb_ref, sinks_ref, convw_ref, gattn_ref, gconv_ref,
                         wout_ref, lng_ref, lnb_ref,
                         h_ref, ko_ref, vo_ref, co_ref,
                         q_s, k_s, v_s, o_s, gc_s, *, alpha, n_heads, attn_w, kv_w, conv_ch):
    s = pl.program_id(1)
    last = pl.num_programs(1) - 1
    tq = x_ref.shape[0]
    q_per_kv = n_heads // N_KV_HEADS

    @pl.when(s == 0)
    def _():
        k_s[0:WINDOW, :] = jnp.zeros((WINDOW, kv_w), BF16)
        v_s[0:WINDOW, :] = jnp.zeros((WINDOW, kv_w), BF16)
        gc_s[...] = jnp.zeros(gc_s.shape, F32)

    @pl.when(s > 0)
    def _():
        k_s[0:WINDOW, :] = k_s[tq:tq + WINDOW, :]
        v_s[0:WINDOW, :] = v_s[tq:tq + WINDOW, :]

    x = x_ref[...]
    xb = x.astype(BF16)
    c, sa, sb = c_ref[...], sa_ref[...], sb_ref[...]

    def proj(lo, width):
        return jnp.dot(xb, win_ref[:, lo:lo + width], preferred_element_type=F32)

    scale = HEAD_DIM ** -0.5
    for j in range(attn_w // LANES):
        qj = _rope(proj(j * LANES, LANES), c, sa, sb)
        q_s[:, j * LANES:(j + 1) * LANES] = (qj * scale).astype(BF16)
    k = _rope(proj(attn_w, kv_w), c, sa, sb)
    v = proj(attn_w + kv_w, kv_w)
    k_s[WINDOW:WINDOW + tq, :] = k.astype(BF16)
    v_s[WINDOW:WINDOW + tq, :] = v.astype(BF16)

    @pl.when(s == last)
    def _():
        ko_ref[...] = k[tq - WINDOW:, :]
        vo_ref[...] = v[tq - WINDOW:, :]

    qi = lax.broadcasted_iota(jnp.int32, (WINDOW, 2 * WINDOW), 0)
    ci = lax.broadcasted_iota(jnp.int32, (WINDOW, 2 * WINDOW), 1)
    band = (ci > qi) & (ci <= qi + WINDOW)

    def sub_block(j, carry):
        r0 = pl.multiple_of(j * WINDOW, WINDOW)
        has_prev = (s * tq + r0) > 0
        mask = band & ((ci >= WINDOW) | has_prev)
        kk = k_s[pl.ds(r0, 2 * WINDOW), :]
        vv = v_s[pl.ds(r0, 2 * WINDOW), :]
        for hd in range(n_heads):
            kvh = hd // q_per_kv
            qh = q_s[pl.ds(r0, WINDOW), hd * HEAD_DIM:(hd + 1) * HEAD_DIM]
            kh = kk[:, kvh * HEAD_DIM:(kvh + 1) * HEAD_DIM]
            vh = vv[:, kvh * HEAD_DIM:(kvh + 1) * HEAD_DIM]
            sc = lax.dot_general(qh, kh, (((1,), (1,)), ((), ())), preferred_element_type=F32)
            sc = jnp.where(mask, sc, NEG_INF)
            (p,) = _sink_softmax([sc], sinks_ref[hd])
            o_s[pl.ds(r0, WINDOW), hd * HEAD_DIM:(hd + 1) * HEAD_DIM] = jnp.dot(
                p, vh, preferred_element_type=F32)
        return carry

    lax.fori_loop(0, tq // WINDOW, sub_block, 0)

    o3 = attn_w + 2 * kv_w
    gated = proj(o3 + 2 * conv_ch, conv_ch) * proj(o3, conv_ch)
    row = lax.broadcasted_iota(jnp.int32, (tq, 1), 0)
    conv_o = _short_conv(gated, gc_s[0:1, :], gc_s[1:2, :], row, convw_ref[...], proj(o3 + conv_ch, conv_ch))
    gc_s[0:CONV_K - 1, :] = gated[tq - (CONV_K - 1):, :]

    @pl.when(s == last)
    def _():
        co_ref[...] = gated[tq - (CONV_K - 1):, :]

    h_ref[...] = _merge_norm(x, o_s[...], conv_o, gattn_ref[...], gconv_ref[...], wout_ref,
                             lng_ref[...], lnb_ref[...], alpha)


def _prompt_mixer(x, win_b, tabs, sinks, conv_w, g_attn, g_conv, wout_b, ln_g, ln_b, *, alpha):
    bsz, seq, d = x.shape
    tq = PROMPT_TILE
    ns = seq // tq
    attn_w = g_attn.shape[-1]
    conv_ch = g_conv.shape[-1]
    n_heads = attn_w // HEAD_DIM
    kv_w = N_KV_HEADS * HEAD_DIM
    in_cols = win_b.shape[-1]
    const2 = lambda b, s: (0, 0)
    kern = functools.partial(_prompt_mixer_kernel, alpha=alpha, n_heads=n_heads, attn_w=attn_w, kv_w=kv_w,
                             conv_ch=conv_ch)
    return pl.pallas_call(
        kern,
        grid=(bsz, ns),
        in_specs=[
            pl.BlockSpec((None, tq, d), lambda b, s: (b, s, 0)),
            pl.BlockSpec((d, in_cols), const2),
            pl.BlockSpec((tq, LANES), lambda b, s: (s, 0)),
            pl.BlockSpec((tq, LANES), lambda b, s: (s, 0)),
            pl.BlockSpec((tq, LANES), lambda b, s: (s, 0)),
            pl.BlockSpec(memory_space=pltpu.SMEM),
            pl.BlockSpec((CONV_K, conv_ch), const2),
            pl.BlockSpec((1, attn_w), const2),
            pl.BlockSpec((1, conv_ch), const2),
            pl.BlockSpec((attn_w + conv_ch, d), const2),
            pl.BlockSpec((1, d), const2),
            pl.BlockSpec((1, d), const2),
        ],
        out_specs=[
            pl.BlockSpec((tq, d), lambda b, s: (b * ns + s, 0)),
            pl.BlockSpec((None, WINDOW, kv_w), lambda b, s: (b, 0, 0)),
            pl.BlockSpec((None, WINDOW, kv_w), lambda b, s: (b, 0, 0)),
            pl.BlockSpec((None, CONV_K - 1, conv_ch), lambda b, s: (b, 0, 0)),
        ],
        out_shape=[
            jax.ShapeDtypeStruct((bsz * seq, d), F32),
            jax.ShapeDtypeStruct((bsz, WINDOW, kv_w), F32),
            jax.ShapeDtypeStruct((bsz, WINDOW, kv_w), F32),
            jax.ShapeDtypeStruct((bsz, CONV_K - 1, conv_ch), F32),
        ],
        scratch_shapes=[
            pltpu.VMEM((tq, attn_w), BF16),
            pltpu.VMEM((tq + WINDOW, kv_w), BF16),
            pltpu.VMEM((tq + WINDOW, kv_w), BF16),
            pltpu.VMEM((tq, attn_w), F32),
            pltpu.VMEM((SUBLANES, conv_ch), F32),
        ],
        compiler_params=pltpu.CompilerParams(dimension_semantics=("arbitrary", "arbitrary"),
                                             vmem_limit_bytes=VMEM_LIMIT_BYTES),
        name="prompt_mixer",
    )(x, win_b, *tabs, sinks, conv_w, g_attn, g_conv, wout_b, ln_g, ln_b)


def _sample_mixer_kernel(x_ref, ck_ref, cv_ref, st_ref, win_ref, c_ref, sa_ref, sb_ref, sinks_ref,
                         convw_ref, gattn_ref, gconv_ref, wout_ref, lng_ref, lnb_ref,
                         h_ref, ko_ref, vo_ref, co_ref, *, alpha, n_heads, attn_w, kv_w, conv_ch, dec_seq):
    nb, win = ck_ref.shape[0], ck_ref.shape[1]
    rows = nb * dec_seq
    q_per_kv = n_heads // N_KV_HEADS
    x = x_ref[...]
    xb = x.astype(BF16)
    c, sa, sb = c_ref[...], sa_ref[...], sb_ref[...]

    def proj(lo, width):
        return jnp.dot(xb, win_ref[:, lo:lo + width], preferred_element_type=F32)

    scale = HEAD_DIM ** -0.5
    k = _rope(proj(attn_w, kv_w), c, sa, sb)
    v = proj(attn_w + kv_w, kv_w)
    k3 = k.reshape(nb, dec_seq, kv_w)
    v3 = v.reshape(nb, dec_seq, kv_w)
    ck = ck_ref[...]
    cv = cv_ref[...]
    ko_ref[:, 0:win - dec_seq, :] = ck[:, dec_seq:, :]
    ko_ref[:, win - dec_seq:, :] = k3
    vo_ref[:, 0:win - dec_seq, :] = cv[:, dec_seq:, :]
    vo_ref[:, win - dec_seq:, :] = v3
    ckb, cvb, k3b, v3b = ck.astype(BF16), cv.astype(BF16), k3.astype(BF16), v3.astype(BF16)

    qrows = q_per_kv * dec_seq
    qi = lax.broadcasted_iota(jnp.int32, (nb, qrows, win), 1) % dec_seq
    mask_c = lax.broadcasted_iota(jnp.int32, (nb, qrows, win), 2) > qi + (win - WINDOW)
    qn = lax.broadcasted_iota(jnp.int32, (nb, qrows, dec_seq), 1) % dec_seq
    mask_n = lax.broadcasted_iota(jnp.int32, (nb, qrows, dec_seq), 2) <= qn
    sink_row = lax.broadcasted_iota(jnp.int32, (nb, qrows, 1), 1) // dec_seq

    q_chunks = [_rope(proj(j * LANES, LANES), c, sa, sb) * scale for j in range(attn_w // LANES)]
    heads_out = []
    for kvh in range(N_KV_HEADS):
        qs = []
        for g in range(q_per_kv):
            lo = (kvh * q_per_kv + g) * HEAD_DIM
            qh = q_chunks[lo // LANES][:, lo % LANES:lo % LANES + HEAD_DIM]
            qs.append(qh.reshape(nb, dec_seq, HEAD_DIM))
        qg = jnp.concatenate(qs, axis=1).astype(BF16)
        sl = slice(kvh * HEAD_DIM, (kvh + 1) * HEAD_DIM)
        sc_c = jnp.einsum("bqd,bkd->bqk", qg, ckb[:, :, sl], preferred_element_type=F32)
        sc_n = jnp.einsum("bqd,bkd->bqk", qg, k3b[:, :, sl], preferred_element_type=F32)
        sc_c = jnp.where(mask_c, sc_c, NEG_INF)
        sc_n = jnp.where(mask_n, sc_n, NEG_INF)
        sink = jnp.zeros((nb, qrows, 1), F32)
        for g in range(q_per_kv):
            sink = jnp.where(sink_row == g, sinks_ref[kvh * q_per_kv + g], sink)
        p_c, p_n = _sink_softmax([sc_c, sc_n], sink)
        og = (jnp.einsum("bqk,bkd->bqd", p_c, cvb[:, :, sl], preferred_element_type=F32)
              + jnp.einsum("bqk,bkd->bqd", p_n, v3b[:, :, sl], preferred_element_type=F32))
        for g in range(q_per_kv):
            heads_out.append(og[:, g * dec_seq:(g + 1) * dec_seq, :].reshape(rows, HEAD_DIM))
    attn_o = jnp.concatenate(heads_out, axis=-1)

    o3 = attn_w + 2 * kv_w
    gated = proj(o3 + 2 * conv_ch, conv_ch) * proj(o3, conv_ch)
    st = st_ref[...]
    prev2 = jnp.broadcast_to(st[:, 0:1, :], (nb, dec_seq, conv_ch)).reshape(rows, conv_ch)
    prev1 = jnp.broadcast_to(st[:, 1:2, :], (nb, dec_seq, conv_ch)).reshape(rows, conv_ch)
    row = lax.broadcasted_iota(jnp.int32, (rows, 1), 0) % dec_seq
    conv_o = _short_conv(gated, prev2, prev1, row, convw_ref[...], proj(o3 + conv_ch, conv_ch))
    co_ref[...] = gated.reshape(nb, dec_seq, conv_ch)[:, dec_seq - (CONV_K - 1):, :]

    h_ref[...] = _merge_norm(x, attn_o, conv_o, gattn_ref[...], gconv_ref[...], wout_ref,
                             lng_ref[...], lnb_ref[...], alpha)


def _sample_mixer(x, ck, cv, st, win_b, tabs, sinks, conv_w, g_attn, g_conv, wout_b, ln_g, ln_b, *, alpha):
    dec_b, dec_seq, d = x.shape
    assert dec_seq >= CONV_K - 1 and dec_seq % SUBLANES == 0
    nb = SAMPLE_SEQS
    rows = nb * dec_seq
    win = ck.shape[1]
    attn_w = g_attn.shape[-1]
    conv_ch = g_conv.shape[-1]
    n_heads = attn_w // HEAD_DIM
    kv_w = N_KV_HEADS * HEAD_DIM
    in_cols = win_b.shape[-1]
    const2 = lambda i: (0, 0)
    kern = functools.partial(_sample_mixer_kernel, alpha=alpha, n_heads=n_heads, attn_w=attn_w, kv_w=kv_w,
                             conv_ch=conv_ch, dec_seq=dec_seq)
    return pl.pallas_call(
        kern,
        grid=(dec_b // nb,),
        in_specs=[
            pl.BlockSpec((rows, d), lambda i: (i, 0)),
            pl.BlockSpec((nb, win, kv_w), lambda i: (i, 0, 0)),
            pl.BlockSpec((nb, win, kv_w), lambda i: (i, 0, 0)),
            pl.BlockSpec((nb, CONV_K - 1, conv_ch), lambda i: (i, 0, 0)),
            pl.BlockSpec((d, in_cols), const2),
            pl.BlockSpec((rows, LANES), const2),
            pl.BlockSpec((rows, LANES), const2),
            pl.BlockSpec((rows, LANES), const2),
            pl.BlockSpec(memory_space=pltpu.SMEM),
            pl.BlockSpec((CONV_K, conv_ch), const2),
            pl.BlockSpec((1, attn_w), const2),
            pl.BlockSpec((1, conv_ch), const2),
            pl.BlockSpec((attn_w + conv_ch, d), const2),
            pl.BlockSpec((1, d), const2),
            pl.BlockSpec((1, d), const2),
        ],
        out_specs=[
            pl.BlockSpec((rows, d), lambda i: (i, 0)),
            pl.BlockSpec((nb, win, kv_w), lambda i: (i, 0, 0)),
            pl.BlockSpec((nb, win, kv_w), lambda i: (i, 0, 0)),
            pl.BlockSpec((nb, CONV_K - 1, conv_ch), lambda i: (i, 0, 0)),
        ],
        out_shape=[
            jax.ShapeDtypeStruct((dec_b * dec_seq, d), F32),
            jax.ShapeDtypeStruct((dec_b, win, kv_w), F32),
            jax.ShapeDtypeStruct((dec_b, win, kv_w), F32),
            jax.ShapeDtypeStruct((dec_b, CONV_K - 1, conv_ch), F32),
        ],
        compiler_params=pltpu.CompilerParams(dimension_semantics=("arbitrary",),
                                             vmem_limit_bytes=VMEM_LIMIT_BYTES),
        name="sample_mixer",
    )(x.reshape(dec_b * dec_seq, d), ck, cv, st, win_b, *tabs, sinks, conv_w, g_attn, g_conv, wout_b,
      ln_g, ln_b)


def _over_experts(fn, x):
    return fn(fn(x, axis=0, keepdims=True), axis=1, keepdims=True)


def _two_group_specs(tm, d, n_prompt_tiles):
    return [pl.BlockSpec((tm, d), lambda i, *_: (jnp.minimum(i, n_prompt_tiles - 1), 0)),
            pl.BlockSpec((tm, d), lambda i, *_: (jnp.maximum(i - n_prompt_tiles, 0), 0))]


def _route_kernel(hp_ref, hs_ref, rwt_ref, bias_ref, eidx_ref, rank_ref, gate_ref, cnt_ref, *, n_experts,
                  n_prompt_tiles):
    i = pl.program_id(0)
    tm = hp_ref.shape[0]
    per_group = n_experts // N_EXPERT_GROUPS
    shape3 = (N_EXPERT_GROUPS, per_group, tm)

    h = jnp.where(i < n_prompt_tiles, hp_ref[...], hs_ref[...])
    logits = lax.dot_general(rwt_ref[...], h.astype(BF16), (((1,), (1,)), ((), ())),
                             preferred_element_type=F32)
    scores = jax.nn.sigmoid(logits)
    sel = scores + bias_ref[...]
    scores3 = scores.reshape(shape3)
    grp = sel.reshape(shape3)
    member = lax.broadcasted_iota(jnp.int32, shape3, 1).astype(F32)
    group = lax.broadcasted_iota(jnp.int32, shape3, 0).astype(F32)
    expert = group * per_group + member

    m1 = jnp.max(grp, axis=1, keepdims=True)
    f1 = jnp.min(jnp.where(grp == m1, member, float(per_group)), axis=1, keepdims=True)
    m2 = jnp.max(jnp.where(member == f1, NEG_INF, grp), axis=1, keepdims=True)
    gscore = m1 + m2

    gid = lax.broadcasted_iota(jnp.int32, gscore.shape, 0).astype(F32)
    gmask = jnp.zeros(gscore.shape, F32)
    cur = gscore
    for _ in range(TOPK_GROUPS):
        mx = jnp.max(cur, axis=0, keepdims=True)
        pick = gid == jnp.min(jnp.where(cur == mx, gid, float(N_EXPERT_GROUPS)), axis=0, keepdims=True)
        gmask = jnp.where(pick, 1.0, gmask)
        cur = jnp.where(pick, NEG_INF, cur)

    cand = jnp.where(gmask > 0.0, grp, NEG_INF)
    chosen = jnp.zeros(shape3, F32)
    picks, firsts, weights = [], [], []
    for _ in range(TOP_K):
        mx = _over_experts(jnp.max, cand)
        first = _over_experts(jnp.min, jnp.where(cand == mx, expert, float(n_experts)))
        pick = expert == first
        picks.append(pick)
        firsts.append(first)
        weights.append(_over_experts(jnp.sum, jnp.where(pick, scores3, 0.0)))
        chosen = jnp.where(pick, 1.0, chosen)
        cand = jnp.where(pick, NEG_INF, cand)
    wsum = weights[0]
    for w in weights[1:]:
        wsum = wsum + w

    chosen_b = chosen.reshape(n_experts, tm).astype(BF16)
    earlier = (lax.broadcasted_iota(jnp.int32, (tm, tm), 0)
               < lax.broadcasted_iota(jnp.int32, (tm, tm), 1)).astype(BF16)
    before3 = jnp.dot(chosen_b, earlier, preferred_element_type=F32).reshape(shape3)

    pad = SUBLANES - TOP_K
    eidx = [f.reshape(1, tm).astype(jnp.int32) for f in firsts]
    rank = [_over_experts(jnp.sum, jnp.where(p, before3, 0.0)).reshape(1, tm).astype(jnp.int32) for p in picks]
    gate = [(w / wsum * ROUTED_SCALE).reshape(1, tm) for w in weights]
    eidx_ref[...] = jnp.concatenate(eidx + [jnp.zeros((pad, tm), jnp.int32)], axis=0)
    rank_ref[...] = jnp.concatenate(rank + [jnp.zeros((pad, tm), jnp.int32)], axis=0)
    gate_ref[...] = jnp.concatenate(gate + [jnp.zeros((pad, tm), F32)], axis=0)

    cnt_ref[...] = lax.dot_general(jnp.ones((SUBLANES, tm), BF16), chosen_b, (((1,), (1,)), ((), ())),
                                   preferred_element_type=F32)


def _route(h_p, h_s, rwt_b, bias_col):
    d = h_p.shape[1]
    t_all = h_p.shape[0] + h_s.shape[0]
    n_experts = rwt_b.shape[0]
    tm = MOE_TILE
    nt = t_all // tm
    npt = h_p.shape[0] // tm
    row_spec = pl.BlockSpec((SUBLANES, tm), lambda i: (0, i))
    return pl.pallas_call(
        functools.partial(_route_kernel, n_experts=n_experts, n_prompt_tiles=npt),
        grid=(nt,),
        in_specs=_two_group_specs(tm, d, npt) + [
            pl.BlockSpec((n_experts, d), lambda i: (0, 0)),
            pl.BlockSpec((n_experts, 1), lambda i: (0, 0)),
        ],
        out_specs=[row_spec, row_spec, row_spec, pl.BlockSpec((SUBLANES, n_experts), lambda i: (i, 0))],
        out_shape=[
            jax.ShapeDtypeStruct((SUBLANES, t_all), jnp.int32),
            jax.ShapeDtypeStruct((SUBLANES, t_all), jnp.int32),
            jax.ShapeDtypeStruct((SUBLANES, t_all), F32),
            jax.ShapeDtypeStruct((nt * SUBLANES, n_experts), F32),
        ],
        compiler_params=pltpu.CompilerParams(dimension_semantics=("arbitrary",),
                                             vmem_limit_bytes=VMEM_LIMIT_BYTES),
        name="route",
    )(h_p, h_s, rwt_b, bias_col)


def _for_each_part(rows, max_rows, fn, chunk_rows=None):
    top = max_rows if chunk_rows is None else chunk_rows // 2
    for size in [SUBLANES << b for b in range((top // SUBLANES).bit_length())]:
        @pl.when((rows & size) != 0)
        def _(size=size):
            fn(pl.multiple_of(rows & (size - 1), SUBLANES), size)
    if chunk_rows is not None:
        base = rows & (chunk_rows - 1)

        def chunk(j, c):
            fn(pl.multiple_of(base + j * chunk_rows, SUBLANES), chunk_rows)
            return c

        lax.fori_loop(0, lax.shift_right_logical(rows, chunk_rows.bit_length() - 1), chunk, 0)


def _top_bit(n):
    return 1 << (n.bit_length() - 1)


def _pack_rows(x):
    rows, two_w = x.shape
    w = two_w // 2
    x3 = x.reshape(rows // SUBLANES, SUBLANES, two_w)
    halves = jnp.concatenate([x3[:, :, :w], x3[:, :, w:]], axis=1).astype(BF16)
    return pltpu.bitcast(halves, jnp.uint32).reshape(rows, w)


def _unpack_rows(p):
    rows, w = p.shape
    halves = pltpu.bitcast(p.reshape(rows // SUBLANES, SUBLANES, w), BF16).astype(F32)
    return jnp.concatenate([halves[:, :SUBLANES, :], halves[:, SUBLANES:, :]], axis=-1).reshape(rows, 2 * w)


def _local_rows(tm, n_experts):
    n = TOP_K * tm + n_experts * (SUBLANES - 1)
    return -(-n // SORT_CHUNK) * SORT_CHUNK


def _dispatch_kernel(cnt_ref, ls_ref, gs_ref, tot_ref, zrow_ref, lp_ref, hp_ref, hs_ref, xs_hbm,
                     lbuf, zero_s, sem, *, n_experts, blk, n_prompt_tiles):
    i = pl.program_id(0)
    tm, d = hp_ref.shape
    n = TOP_K * tm
    nl = lbuf.shape[1]
    slot = i % 2

    def tile_wait(tile):
        pltpu.make_async_copy(lbuf.at[0, pl.ds(0, n)], xs_hbm.at[pl.ds(0, n)], sem).wait()
        _for_each_part(tot_ref[tile] - n, _top_bit(n_experts * (SUBLANES - 1)), lambda off, size:
                       pltpu.make_async_copy(lbuf.at[0, pl.ds(0, size)], xs_hbm.at[pl.ds(0, size)], sem).wait())

    @pl.when(i == 0)
    def _():
        zero_s[...] = jnp.zeros(zero_s.shape, zero_s.dtype)

        def zcopy(e):
            return pltpu.make_async_copy(zero_s, xs_hbm.at[pl.ds(pl.multiple_of(zrow_ref[e], blk), blk)], sem)

        def start(e, c):
            zcopy(e).start()
            return c

        def wait(e, c):
            zcopy(e).wait()
            return c

        lax.fori_loop(0, n_experts, start, 0)
        lax.fori_loop(0, n_experts, wait, 0)

    hb = jnp.where(i < n_prompt_tiles, hp_ref[...], hs_ref[...]).astype(BF16)
    lp = lp_ref[...]
    def sort_chunk(r):
        rows = r * SORT_CHUNK + lax.broadcasted_iota(jnp.int32, (SORT_CHUNK, tm), 0)
        hit = jnp.zeros((SORT_CHUNK, tm), F32)
        for k in range(TOP_K):
            hit = jnp.where(rows == lp[k:k + 1, :], 1.0, hit)
        lbuf[slot, pl.ds(r * SORT_CHUNK, SORT_CHUNK), :] = _pack_rows(
            jnp.dot(hit.astype(BF16), hb, preferred_element_type=F32))

    for r in range(nl // SORT_CHUNK):
        if (r + 1) * SORT_CHUNK <= n:
            sort_chunk(r)
        else:
            pl.when(tot_ref[i] > r * SORT_CHUNK)(functools.partial(sort_chunk, r))

    @pl.when(i > 0)
    def _():
        tile_wait(i - 1)

    def per_expert(e, c):
        idx = i * n_experts + e
        l0, g0 = ls_ref[idx], gs_ref[idx]
        _for_each_part(cnt_ref[idx], tm, lambda off, size: pltpu.make_async_copy(
            lbuf.at[slot, pl.ds(pl.multiple_of(l0 + off, SUBLANES), size)],
            xs_hbm.at[pl.ds(pl.multiple_of(g0 + off, SUBLANES), size)], sem).start(), RUN_CHUNK)
        return c

    lax.fori_loop(0, n_experts, per_expert, 0, unroll=RUN_UNROLL)

    @pl.when(i == pl.num_programs(0) - 1)
    def _():
        tile_wait(i)


def _dispatch(tables, zrow, lp, h_p, h_s, *, n_rows, blk):
    d = h_p.shape[1]
    t_all = h_p.shape[0] + h_s.shape[0]
    tm = MOE_TILE
    npt = h_p.shape[0] // tm
    n_experts = zrow.shape[0]
    return pl.pallas_call(
        functools.partial(_dispatch_kernel, n_experts=n_experts, blk=blk, n_prompt_tiles=npt),
        grid_spec=pltpu.PrefetchScalarGridSpec(
            num_scalar_prefetch=5,
            grid=(t_all // tm,),
            in_specs=[pl.BlockSpec((SUBLANES, tm), lambda i, *_: (0, i))] + _two_group_specs(tm, d, npt),
            out_specs=pl.BlockSpec(memory_space=pl.ANY),
            scratch_shapes=[
                pltpu.VMEM((2, _local_rows(tm, n_experts), d // 2), jnp.uint32),
                pltpu.VMEM((blk, d // 2), jnp.uint32),
                pltpu.SemaphoreType.DMA,
            ],
        ),
        out_shape=jax.ShapeDtypeStruct((n_rows, d // 2), jnp.uint32),
        compiler_params=pltpu.CompilerParams(dimension_semantics=("arbitrary",),
                                             vmem_limit_bytes=VMEM_LIMIT_BYTES),
        name="dispatch",
    )(*tables, zrow, lp, h_p, h_s)


def _silu(x):
    return x * jax.nn.sigmoid(x)


def _expert_kernel(be_ref, nact_ref, xs_ref, wg_ref, wu_ref, wd_ref, ys_ref, wg_s, wu_s, wd_s):
    b = pl.program_id(0)

    @pl.when(b < nact_ref[0])
    def _():
        @pl.when((b == 0) | (be_ref[b] != be_ref[jnp.maximum(b - 1, 0)]))
        def _():
            wg_s[...] = wg_ref[...].astype(BF16)
            wu_s[...] = wu_ref[...].astype(BF16)
            wd_s[...] = wd_ref[...].astype(BF16)

        xb = _unpack_rows(xs_ref[...]).astype(BF16)
        hid = _silu(jnp.dot(xb, wg_s[...], preferred_element_type=F32)) * jnp.dot(
            xb, wu_s[...], preferred_element_type=F32)
        ys_ref[...] = _pack_rows(jnp.dot(hid.astype(BF16), wd_s[...], preferred_element_type=F32))


def _experts(block_expert, nact, xs, w_gate, w_up, w_down, *, blk):
    n_rows, dw = xs.shape
    d = w_gate.shape[-2]
    ff = w_gate.shape[-1]
    n_blocks = n_rows // blk

    def active(b, be, na):
        return jnp.minimum(b, na[0] - 1)

    return pl.pallas_call(
        _expert_kernel,
        grid_spec=pltpu.PrefetchScalarGridSpec(
            num_scalar_prefetch=2,
            grid=(n_blocks,),
            in_specs=[
                pl.BlockSpec((blk, dw), lambda b, be, na: (active(b, be, na), 0)),
                pl.BlockSpec((None, d, ff), lambda b, be, na: (be[active(b, be, na)], 0, 0)),
                pl.BlockSpec((None, d, ff), lambda b, be, na: (be[active(b, be, na)], 0, 0)),
                pl.BlockSpec((None, ff, d), lambda b, be, na: (be[active(b, be, na)], 0, 0)),
            ],
            out_specs=pl.BlockSpec((blk, dw), lambda b, be, na: (active(b, be, na), 0)),
            scratch_shapes=[
                pltpu.VMEM((d, ff), BF16),
                pltpu.VMEM((d, ff), BF16),
                pltpu.VMEM((ff, d), BF16),
            ],
        ),
        out_shape=jax.ShapeDtypeStruct((n_rows, dw), jnp.uint32),
        compiler_params=pltpu.CompilerParams(dimension_semantics=("arbitrary",),
                                             vmem_limit_bytes=VMEM_LIMIT_BYTES),
        name="experts",
    )(block_expert, nact, xs, w_gate, w_up, w_down)


def _combine_kernel(cnt_ref, ls_ref, gs_ref, tot_ref, ys_hbm, lpt_ref, gate_ref, hp_ref, hs_ref, wsg_ref, wsu_ref,
                    wsd_ref, lng_ref, lnb_ref, yp_ref, ysm_ref, ybuf, moe_s, sems, *, alpha, n_experts,
                    n_prompt_tiles):
    i = pl.program_id(0)
    tm, d = hp_ref.shape
    n = TOP_K * tm
    nl = ybuf.shape[1]
    slot = i % 2

    def fetch(tile, to):
        def per_expert(e, c):
            idx = tile * n_experts + e
            l0, g0 = ls_ref[idx], gs_ref[idx]
            _for_each_part(cnt_ref[idx], tm, lambda off, size: pltpu.make_async_copy(
                ys_hbm.at[pl.ds(pl.multiple_of(g0 + off, SUBLANES), size)],
                ybuf.at[to, pl.ds(pl.multiple_of(l0 + off, SUBLANES), size)], sems.at[to]).start(), RUN_CHUNK)
            return c

        lax.fori_loop(0, n_experts, per_expert, 0, unroll=RUN_UNROLL)

    @pl.when(i == 0)
    def _():
        ybuf[...] = jnp.zeros(ybuf.shape, ybuf.dtype)
        fetch(0, 0)

    @pl.when(i + 1 < pl.num_programs(0))
    def _():
        fetch(i + 1, 1 - slot)

    h = jnp.where(i < n_prompt_tiles, hp_ref[...], hs_ref[...])
    hb = h.astype(BF16)
    hid = _silu(jnp.dot(hb, wsg_ref[...], preferred_element_type=F32)) * jnp.dot(
        hb, wsu_ref[...], preferred_element_type=F32)
    shared = jnp.dot(hid.astype(BF16), wsd_ref[...], preferred_element_type=F32)

    pltpu.make_async_copy(ys_hbm.at[pl.ds(0, n)], ybuf.at[slot, pl.ds(0, n)], sems.at[slot]).wait()
    _for_each_part(tot_ref[i] - n, _top_bit(n_experts * (SUBLANES - 1)), lambda off, size: pltpu.make_async_copy(
        ys_hbm.at[pl.ds(0, size)], ybuf.at[slot, pl.ds(0, size)], sems.at[slot]).wait())

    lpt = lpt_ref[...]
    gate = gate_ref[...]
    moe_s[...] = shared

    def add_chunk(r):
        cols = r * SORT_CHUNK + lax.broadcasted_iota(jnp.int32, (tm, SORT_CHUNK), 1)
        g = jnp.zeros((tm, SORT_CHUNK), F32)
        for k in range(TOP_K):
            g = jnp.where(cols == lpt[:, k:k + 1], gate[:, k:k + 1], g)
        yb = _unpack_rows(ybuf[slot, pl.ds(r * SORT_CHUNK, SORT_CHUNK), :]).astype(BF16)
        moe_s[...] += jnp.dot(g.astype(BF16), yb, preferred_element_type=F32)

    for r in range(nl // SORT_CHUNK):
        if (r + 1) * SORT_CHUNK <= n:
            add_chunk(r)
        else:
            pl.when(tot_ref[i] > r * SORT_CHUNK)(functools.partial(add_chunk, r))
    y = _layer_norm(alpha * h + moe_s[...], lng_ref[...], lnb_ref[...])

    @pl.when(i < n_prompt_tiles)
    def _():
        yp_ref[...] = y

    @pl.when(i >= n_prompt_tiles)
    def _():
        ysm_ref[...] = y


def _combine(tables, ys, lp_t, gates_t, h_p, h_s, wsg_b, wsu_b, wsd_b, ln_g, ln_b, *, alpha):
    t_prompt, d = h_p.shape
    t_all = t_prompt + h_s.shape[0]
    tm = MOE_TILE
    ff = wsg_b.shape[-1]
    npt = t_prompt // tm
    n_experts = tables[0].shape[0] // (t_all // tm)
    const2 = lambda i, *_: (0, 0)
    return pl.pallas_call(
        functools.partial(_combine_kernel, alpha=alpha, n_experts=n_experts, n_prompt_tiles=npt),
        grid_spec=pltpu.PrefetchScalarGridSpec(
            num_scalar_prefetch=4,
            grid=(t_all // tm,),
            in_specs=[
                pl.BlockSpec(memory_space=pl.ANY),
                pl.BlockSpec((tm, SUBLANES), lambda i, *_: (i, 0)),
                pl.BlockSpec((tm, SUBLANES), lambda i, *_: (i, 0)),
                *_two_group_specs(tm, d, npt),
                pl.BlockSpec((d, ff), const2),
                pl.BlockSpec((d, ff), const2),
                pl.BlockSpec((ff, d), const2),
                pl.BlockSpec((1, d), const2),
                pl.BlockSpec((1, d), const2),
            ],
            out_specs=[
                pl.BlockSpec((tm, d), lambda i, *_: (jnp.minimum(i, npt - 1), 0)),
                pl.BlockSpec((tm, d), lambda i, *_: (jnp.maximum(i - npt, 0), 0)),
            ],
            scratch_shapes=[
                pltpu.VMEM((2, _local_rows(tm, n_experts), d // 2), jnp.uint32),
                pltpu.VMEM((tm, d), F32),
                pltpu.SemaphoreType.DMA((2,)),
            ],
        ),
        out_shape=[
            jax.ShapeDtypeStruct((t_prompt, d), F32),
            jax.ShapeDtypeStruct((t_all - t_prompt, d), F32),
        ],
        compiler_params=pltpu.CompilerParams(dimension_semantics=("arbitrary",),
                                             vmem_limit_bytes=VMEM_LIMIT_BYTES),
        name="combine",
    )(*tables, ys, lp_t, gates_t, h_p, h_s, wsg_b, wsu_b, wsd_b, ln_g, ln_b)


def _moe(h_p, h_s, router_w, router_bias, w_gate, w_up, w_down, ws_gate, ws_up, ws_down, ln_g, ln_b, *, alpha):
    t_all = h_p.shape[0] + h_s.shape[0]
    n_experts = router_w.shape[-1]
    blk = EXPERT_BLOCK
    tm = MOE_TILE
    nt = t_all // tm
    eidx8, lrank8, gate8, cnt8 = _route(h_p, h_s, router_w.T.astype(BF16), router_bias.reshape(n_experts, 1))

    cnt = cnt8.reshape(nt, SUBLANES, n_experts)[:, 0, :].astype(jnp.int32)
    cnt = (cnt + SUBLANES - 1) // SUBLANES * SUBLANES
    counts = jnp.sum(cnt, axis=0)
    blocks_per_e = (counts + blk - 1) // blk
    block_end = jnp.cumsum(blocks_per_e)
    pad_start = (block_end - blocks_per_e) * blk
    n_blocks = -(-(t_all * TOP_K + nt * n_experts * (SUBLANES - 1)) // blk) + n_experts
    block_expert = jnp.minimum(jnp.sum(block_end[None, :] <= jnp.arange(n_blocks)[:, None], axis=1),
                               n_experts - 1).astype(jnp.int32)
    nact = block_end[-1:].astype(jnp.int32)
    zrow = (jnp.maximum(block_end - 1, 0) * blk).astype(jnp.int32)
    gstart = pad_start[None, :] + jnp.cumsum(cnt, axis=0) - cnt
    lstart = jnp.cumsum(cnt, axis=1) - cnt
    tables = tuple(a.reshape(-1).astype(jnp.int32) for a in (cnt, lstart, gstart, jnp.sum(cnt, axis=1)))
    lstart_tok = jnp.repeat(lstart, tm, axis=0)
    lp8 = jnp.sum(jnp.where(eidx8[..., None] == jnp.arange(n_experts), lstart_tok[None], 0), axis=-1) + lrank8
    lp8 = lp8.astype(jnp.int32)

    xs = _dispatch(tables, zrow, lp8, h_p, h_s, n_rows=n_blocks * blk, blk=blk)
    ys = _experts(block_expert, nact, xs, w_gate, w_up, w_down, blk=blk)
    return _combine(tables, ys, lp8.T, gate8.T, h_p, h_s, ws_gate.astype(BF16), ws_up.astype(BF16), ws_down.astype(BF16),
                    ln_g, ln_b, alpha=alpha)


def kernel(x_prompt, x_sample, cache_k, cache_v, state_conv, w_in, attn_sinks, conv_w, g_attn_out, g_conv_out, w_out, ln1_g, ln1_b, router_w, router_bias, w_gate, w_up, w_down, ws_gate, ws_up, ws_down, ln2_g, ln2_b):
    depth = w_in.shape[0]
    bsz, seq, d = x_prompt.shape
    dec_b, dec_seq, _ = x_sample.shape
    win = cache_k.shape[2]
    kv_w = N_KV_HEADS * HEAD_DIM
    t_prompt = bsz * seq
    t_all = t_prompt + dec_b * dec_seq
    alpha = (2.0 * depth) ** 0.25
    assert win == WINDOW and seq % PROMPT_TILE == 0 and dec_b % SAMPLE_SEQS == 0
    assert t_prompt % MOE_TILE == 0 and (t_all - t_prompt) % MOE_TILE == 0
    assert MOE_TILE & (MOE_TILE - 1) == 0 and (TOP_K * MOE_TILE) % SORT_CHUNK == 0

    tabs_p = _rope_tables(jnp.arange(seq))
    tabs_s = tuple(jnp.tile(t, (SAMPLE_SEQS, 1)) for t in _rope_tables(PAST_LEN + jnp.arange(dec_seq)))
    row = lambda a: a.reshape(1, -1)

    xp, xs = x_prompt, x_sample
    outs = [[] for _ in range(6)]
    for l in range(depth):
        win_b, wout_b = w_in[l].astype(BF16), w_out[l].astype(BF16)
        shared = (attn_sinks[l], conv_w[l], row(g_attn_out[l]), row(g_conv_out[l]), wout_b, row(ln1_g[l]),
                  row(ln1_b[l]))
        h_p, kp, vp, cp = _prompt_mixer(xp, win_b, tabs_p, *shared, alpha=alpha)
        h_s, kn, vn, cn = _sample_mixer(xs, cache_k[l].reshape(dec_b, win, kv_w),
                                        cache_v[l].reshape(dec_b, win, kv_w), state_conv[l], win_b, tabs_s,
                                        *shared, alpha=alpha)
        yp, ys = _moe(h_p, h_s, router_w[l], router_bias[l], w_gate[l], w_up[l], w_down[l], ws_gate[l], ws_up[l],
                      ws_down[l], row(ln2_g[l]), row(ln2_b[l]), alpha=alpha)
        xp, xs = yp.reshape(bsz, seq, d), ys.reshape(dec_b, dec_seq, d)
        heads = lambda a: a.reshape(a.shape[0], win, N_KV_HEADS, HEAD_DIM)
        for o, a in zip(outs, (heads(kp), heads(vp), cp, heads(kn), heads(vn), cn)):
            o.append(a)
    return (xp, xs) + tuple(jnp.stack(o, axis=0) for o in outs)
```

```python
import functools

import jax
import jax.numpy as jnp
from jax import lax
from jax.experimental import pallas as pl
from jax.experimental.pallas import tpu as pltpu

PAST_LEN = 16384
WINDOW = 128
HEAD_DIM = 64
N_KV_HEADS = 2
ROT_DIM = HEAD_DIM // 4
ROPE_THETA = 500000.0
CONV_K = 3
TOP_K = 6
N_EXPERT_GROUPS = 8
TOPK_GROUPS = 4
ROUTED_SCALE = 2.5
LN_EPS = 1e-5
RMS_EPS = 1e-6

LANES = 128
SUBLANES = 8
VMEM_LIMIT_BYTES = 56 * 1024 * 1024

PROMPT_TILE = 512
SAMPLE_SEQS = 16
MOE_TILE = 256
SORT_CHUNK = 256
EXPERT_BLOCK = 1024
RUN_CHUNK = 64
RUN_UNROLL = 4

F32 = jnp.float32
BF16 = jnp.bfloat16
NEG_INF = float("-inf")


def _rope_tables(positions):
    half = ROT_DIM // 2
    inv_freq = ROPE_THETA ** (-jnp.arange(0, ROT_DIM, 2, dtype=F32) / ROT_DIM)
    ang = positions.astype(F32)[:, None] * inv_freq[None, :]
    cos, sin = jnp.cos(ang), jnp.sin(ang)
    n = positions.shape[0]
    rest = HEAD_DIM - ROT_DIM
    c = jnp.concatenate([cos, cos, jnp.ones((n, rest), F32)], axis=-1)
    sa = jnp.concatenate([-sin, jnp.zeros((n, half + rest), F32)], axis=-1)
    sb = jnp.concatenate([jnp.zeros((n, half), F32), sin, jnp.zeros((n, rest), F32)], axis=-1)
    reps = LANES // HEAD_DIM
    return jnp.tile(c, (1, reps)), jnp.tile(sa, (1, reps)), jnp.tile(sb, (1, reps))


def _rope(x, c, sa, sb):
    half = ROT_DIM // 2
    return x * c + pltpu.roll(x, LANES - half, 1) * sa + pltpu.roll(x, half, 1) * sb


def _rms_norm(x, g):
    return x * lax.rsqrt(jnp.mean(jnp.square(x), axis=-1, keepdims=True) + RMS_EPS) * g


def _layer_norm(x, g, b):
    mu = jnp.mean(x, axis=-1, keepdims=True)
    var = jnp.mean(jnp.square(x - mu), axis=-1, keepdims=True)
    return (x - mu) * lax.rsqrt(var + LN_EPS) * g + b


def _short_conv(gated, prev2, prev1, row, conv_w, b_gate):
    g1 = pltpu.roll(gated, 1, 0)
    g2 = pltpu.roll(gated, 2, 0)
    g1 = jnp.where(row == 0, prev1, g1)
    g2 = jnp.where(row == 0, prev2, jnp.where(row == 1, prev1, g2))
    y = conv_w[0:1, :] * g2 + conv_w[1:2, :] * g1 + conv_w[2:3, :] * gated
    return b_gate * y


def _merge_norm(x, attn_o, conv_o, gattn, gconv, wout_ref, ln_g, ln_b, alpha):
    cat = jnp.concatenate([_rms_norm(attn_o, gattn), _rms_norm(conv_o, gconv)], axis=-1)
    mix = jnp.dot(cat.astype(BF16), wout_ref[...], preferred_element_type=F32)
    return _layer_norm(alpha * x + mix, ln_g, ln_b)


def _sink_softmax(parts, sink):
    m = sink
    for s in parts:
        m = jnp.maximum(m, jnp.max(s, axis=-1, keepdims=True))
    es = [jnp.exp(s - m) for s in parts]
    den = jnp.exp(sink - m)
    for e in es:
        den = den + jnp.sum(e, axis=-1, keepdims=True)
    return [(e / den).astype(BF16) for e in es]


def _prompt_mixer_kernel(x_ref, win_ref, c_ref, sa_ref, sb_ref, sinks_ref, convw_ref, gattn_ref, gconv_ref,
                         wout_ref, lng_ref, lnb_ref,
                         h_ref, ko_ref, vo_ref, co_ref,
                         q_s, kt_s, vm_s, o_s, gc_s, *, alpha, n_heads, attn_w, kv_w, conv_ch):
    s = pl.program_id(1)
    last = pl.num_programs(1) - 1
    tq = x_ref.shape[0]
    q_per_kv = n_heads // N_KV_HEADS
    heads_per_group = LANES // HEAD_DIM

    @pl.when(s == 0)
    def _():
        kt_s[:, :, 0:WINDOW] = jnp.zeros((kt_s.shape[0], LANES, WINDOW), BF16)
        vm_s[:, 0:WINDOW, :] = jnp.zeros((vm_s.shape[0], WINDOW, LANES), BF16)
        gc_s[...] = jnp.zeros(gc_s.shape, F32)

    @pl.when(s > 0)
    def _():
        kt_s[:, :, 0:WINDOW] = kt_s[:, :, tq:tq + WINDOW]
        vm_s[:, 0:WINDOW, :] = vm_s[:, tq:tq + WINDOW, :]

    x = x_ref[...]
    xb = x.astype(BF16)
    c, sa, sb = c_ref[...], sa_ref[...], sb_ref[...]

    def proj(lo, width):
        return jnp.dot(xb, win_ref[:, lo:lo + width], preferred_element_type=F32)

    scale = HEAD_DIM ** -0.5
    for j in range(attn_w // LANES):
        qj = _rope(proj(j * LANES, LANES), c, sa, sb)
        q_s[:, j * LANES:(j + 1) * LANES] = (qj * scale).astype(BF16)
    k = _rope(proj(attn_w, kv_w), c, sa, sb)
    v = proj(attn_w + kv_w, kv_w)

    @pl.when(s == last)
    def _():
        ko_ref[...] = k[tq - WINDOW:, :]
        vo_ref[...] = v[tq - WINDOW:, :]

    kt = k.T.astype(BF16)
    zeros_k = jnp.zeros((HEAD_DIM, tq), BF16)
    v_swapped = pltpu.roll(v, HEAD_DIM, 1)
    low_lanes = lax.broadcasted_iota(jnp.int32, (tq, LANES), 1) < HEAD_DIM
    for kvh in range(N_KV_HEADS):
        kt_h = kt[kvh * HEAD_DIM:(kvh + 1) * HEAD_DIM, :]
        v_lo = v if kvh == 0 else v_swapped
        v_hi = v_swapped if kvh == 0 else v
        kt_s[2 * kvh, :, WINDOW:] = jnp.concatenate([kt_h, zeros_k], axis=0)
        kt_s[2 * kvh + 1, :, WINDOW:] = jnp.concatenate([zeros_k, kt_h], axis=0)
        vm_s[2 * kvh, WINDOW:, :] = jnp.where(low_lanes, v_lo, 0.0).astype(BF16)
        vm_s[2 * kvh + 1, WINDOW:, :] = jnp.where(low_lanes, 0.0, v_hi).astype(BF16)

    qi = lax.broadcasted_iota(jnp.int32, (WINDOW, 2 * WINDOW), 0)
    ci = lax.broadcasted_iota(jnp.int32, (WINDOW, 2 * WINDOW), 1)
    band = (ci > qi) & (ci <= qi + WINDOW)
    for j in range(tq // WINDOW):
        r0 = j * WINDOW
        mask = band if j > 0 else band & ((ci >= WINDOW) | (s > 0))
        for grp in range(attn_w // LANES):
            q_grp = q_s[r0:r0 + WINDOW, grp * LANES:(grp + 1) * LANES]
            out = None
            for r in range(heads_per_group):
                hd = grp * heads_per_group + r
                src = 2 * (hd // q_per_kv) + r
                sink = sinks_ref[hd]
                sc = jnp.dot(q_grp, kt_s[src, :, r0:r0 + 2 * WINDOW], preferred_element_type=F32)
                sc = jnp.where(mask, sc, NEG_INF)
                m = jnp.maximum(jnp.max(sc, axis=-1, keepdims=True), sink)
                e = jnp.exp(sc - m)
                den = jnp.sum(e, axis=-1, keepdims=True) + jnp.exp(sink - m)
                o_h = jnp.dot(e.astype(BF16), vm_s[src, r0:r0 + 2 * WINDOW, :],
                              preferred_element_type=F32) * (1.0 / den)
                out = o_h if out is None else out + o_h
            o_s[r0:r0 + WINDOW, grp * LANES:(grp + 1) * LANES] = out

    o3 = attn_w + 2 * kv_w
    gated = proj(o3 + 2 * conv_ch, conv_ch) * proj(o3, conv_ch)
    row = lax.broadcasted_iota(jnp.int32, (tq, 1), 0)
    conv_o = _short_conv(gated, gc_s[0:1, :], gc_s[1:2, :], row, convw_ref[...], proj(o3 + conv_ch, conv_ch))
    gc_s[0:CONV_K - 1, :] = gated[tq - (CONV_K - 1):, :]

    @pl.when(s == last)
    def _():
        co_ref[...] = gated[tq - (CONV_K - 1):, :]

    h_ref[...] = _merge_norm(x, o_s[...], conv_o, gattn_ref[...], gconv_ref[...], wout_ref,
                             lng_ref[...], lnb_ref[...], alpha)


def _prompt_mixer(x, win_b, tabs, sinks, conv_w, g_attn, g_conv, wout_b, ln_g, ln_b, *, alpha):
    bsz, seq, d = x.shape
    tq = PROMPT_TILE
    ns = seq // tq
    attn_w = g_attn.shape[-1]
    conv_ch = g_conv.shape[-1]
    n_heads = attn_w // HEAD_DIM
    kv_w = N_KV_HEADS * HEAD_DIM
    assert kv_w == LANES and 2 * HEAD_DIM == LANES and (n_heads // N_KV_HEADS) % 2 == 0 and WINDOW == LANES
    in_cols = win_b.shape[-1]
    const2 = lambda b, s: (0, 0)
    kern = functools.partial(_prompt_mixer_kernel, alpha=alpha, n_heads=n_heads, attn_w=attn_w, kv_w=kv_w,
                             conv_ch=conv_ch)
    return pl.pallas_call(
        kern,
        grid=(bsz, ns),
        in_specs=[
            pl.BlockSpec((None, tq, d), lambda b, s: (b, s, 0)),
            pl.BlockSpec((d, in_cols), const2),
            pl.BlockSpec((tq, LANES), lambda b, s: (s, 0)),
            pl.BlockSpec((tq, LANES), lambda b, s: (s, 0)),
            pl.BlockSpec((tq, LANES), lambda b, s: (s, 0)),
            pl.BlockSpec(memory_space=pltpu.SMEM),
            pl.BlockSpec((CONV_K, conv_ch), const2),
            pl.BlockSpec((1, attn_w), const2),
            pl.BlockSpec((1, conv_ch), const2),
            pl.BlockSpec((attn_w + conv_ch, d), const2),
            pl.BlockSpec((1, d), const2),
            pl.BlockSpec((1, d), const2),
        ],
        out_specs=[
            pl.BlockSpec((tq, d), lambda b, s: (b * ns + s, 0)),
            pl.BlockSpec((None, WINDOW, kv_w), lambda b, s: (b, 0, 0)),
            pl.BlockSpec((None, WINDOW, kv_w), lambda b, s: (b, 0, 0)),
            pl.BlockSpec((None, CONV_K - 1, conv_ch), lambda b, s: (b, 0, 0)),
        ],
        out_shape=[
            jax.ShapeDtypeStruct((bsz * seq, d), F32),
            jax.ShapeDtypeStruct((bsz, WINDOW, kv_w), F32),
            jax.ShapeDtypeStruct((bsz, WINDOW, kv_w), F32),
            jax.ShapeDtypeStruct((bsz, CONV_K - 1, conv_ch), F32),
        ],
        scratch_shapes=[
            pltpu.VMEM((tq, attn_w), BF16),
            pltpu.VMEM((2 * N_KV_HEADS, LANES, WINDOW + tq), BF16),
            pltpu.VMEM((2 * N_KV_HEADS, WINDOW + tq, LANES), BF16),
            pltpu.VMEM((tq, attn_w), F32),
            pltpu.VMEM((SUBLANES, conv_ch), F32),
        ],
        compiler_params=pltpu.CompilerParams(dimension_semantics=("arbitrary", "arbitrary"),
                                             vmem_limit_bytes=VMEM_LIMIT_BYTES),
        name="prompt_mixer",
    )(x, win_b, *tabs, sinks, conv_w, g_attn, g_conv, wout_b, ln_g, ln_b)


def _sample_mixer_kernel(x_ref, ck_ref, cv_ref, st_ref, win_ref, c_ref, sa_ref, sb_ref, sinks_ref,
                         convw_ref, gattn_ref, gconv_ref, wout_ref, lng_ref, lnb_ref,
                         h_ref, ko_ref, vo_ref, co_ref, *, alpha, n_heads, attn_w, kv_w, conv_ch, dec_seq):
    nb, win = ck_ref.shape[0], ck_ref.shape[1]
    rows = nb * dec_seq
    q_per_kv = n_heads // N_KV_HEADS
    x = x_ref[...]
    xb = x.astype(BF16)
    c, sa, sb = c_ref[...], sa_ref[...], sb_ref[...]

    def proj(lo, width):
        return jnp.dot(xb, win_ref[:, lo:lo + width], preferred_element_type=F32)

    scale = HEAD_DIM ** -0.5
    k = _rope(proj(attn_w, kv_w), c, sa, sb)
    v = proj(attn_w + kv_w, kv_w)
    k3 = k.reshape(nb, dec_seq, kv_w)
    v3 = v.reshape(nb, dec_seq, kv_w)
    ck = ck_ref[...]
    cv = cv_ref[...]
    ko_ref[:, 0:win - dec_seq, :] = ck[:, dec_seq:, :]
    ko_ref[:, win - dec_seq:, :] = k3
    vo_ref[:, 0:win - dec_seq, :] = cv[:, dec_seq:, :]
    vo_ref[:, win - dec_seq:, :] = v3
    ckb, cvb, k3b, v3b = ck.astype(BF16), cv.astype(BF16), k3.astype(BF16), v3.astype(BF16)

    qrows = q_per_kv * dec_seq
    qi = lax.broadcasted_iota(jnp.int32, (nb, qrows, win), 1) % dec_seq
    mask_c = lax.broadcasted_iota(jnp.int32, (nb, qrows, win), 2) > qi + (win - WINDOW)
    qn = lax.broadcasted_iota(jnp.int32, (nb, qrows, dec_seq), 1) % dec_seq
    mask_n = lax.broadcasted_iota(jnp.int32, (nb, qrows, dec_seq), 2) <= qn
    sink_row = lax.broadcasted_iota(jnp.int32, (nb, qrows, 1), 1) // dec_seq

    q_chunks = [_rope(proj(j * LANES, LANES), c, sa, sb) * scale for j in range(attn_w // LANES)]
    heads_out = []
    for kvh in range(N_KV_HEADS):
        qs = []
        for g in range(q_per_kv):
            lo = (kvh * q_per_kv + g) * HEAD_DIM
            qh = q_chunks[lo // LANES][:, lo % LANES:lo % LANES + HEAD_DIM]
            qs.append(qh.reshape(nb, dec_seq, HEAD_DIM))
        qg = jnp.concatenate(qs, axis=1).astype(BF16)
        sl = slice(kvh * HEAD_DIM, (kvh + 1) * HEAD_DIM)
        sc_c = jnp.einsum("bqd,bkd->bqk", qg, ckb[:, :, sl], preferred_element_type=F32)
        sc_n = jnp.einsum("bqd,bkd->bqk", qg, k3b[:, :, sl], preferred_element_type=F32)
        sc_c = jnp.where(mask_c, sc_c, NEG_INF)
        sc_n = jnp.where(mask_n, sc_n, NEG_INF)
        sink = jnp.zeros((nb, qrows, 1), F32)
        for g in range(q_per_kv):
            sink = jnp.where(sink_row == g, sinks_ref[kvh * q_per_kv + g], sink)
        p_c, p_n = _sink_softmax([sc_c, sc_n], sink)
        og = (jnp.einsum("bqk,bkd->bqd", p_c, cvb[:, :, sl], preferred_element_type=F32)
              + jnp.einsum("bqk,bkd->bqd", p_n, v3b[:, :, sl], preferred_element_type=F32))
        for g in range(q_per_kv):
            heads_out.append(og[:, g * dec_seq:(g + 1) * dec_seq, :].reshape(rows, HEAD_DIM))
    attn_o = jnp.concatenate(heads_out, axis=-1)

    o3 = attn_w + 2 * kv_w
    gated = proj(o3 + 2 * conv_ch, conv_ch) * proj(o3, conv_ch)
    st = st_ref[...]
    prev2 = jnp.broadcast_to(st[:, 0:1, :], (nb, dec_seq, conv_ch)).reshape(rows, conv_ch)
    prev1 = jnp.broadcast_to(st[:, 1:2, :], (nb, dec_seq, conv_ch)).reshape(rows, conv_ch)
    row = lax.broadcasted_iota(jnp.int32, (rows, 1), 0) % dec_seq
    conv_o = _short_conv(gated, prev2, prev1, row, convw_ref[...], proj(o3 + conv_ch, conv_ch))
    co_ref[...] = gated.reshape(nb, dec_seq, conv_ch)[:, dec_seq - (CONV_K - 1):, :]

    h_ref[...] = _merge_norm(x, attn_o, conv_o, gattn_ref[...], gconv_ref[...], wout_ref,
                             lng_ref[...], lnb_ref[...], alpha)


def _sample_mixer(x, ck, cv, st, win_b, tabs, sinks, conv_w, g_attn, g_conv, wout_b, ln_g, ln_b, *, alpha):
    dec_b, dec_seq, d = x.shape
    assert dec_seq >= CONV_K - 1 and dec_seq % SUBLANES == 0
    nb = SAMPLE_SEQS
    rows = nb * dec_seq
    win = ck.shape[1]
    attn_w = g_attn.shape[-1]
    conv_ch = g_conv.shape[-1]
    n_heads = attn_w // HEAD_DIM
    kv_w = N_KV_HEADS * HEAD_DIM
    in_cols = win_b.shape[-1]
    const2 = lambda i: (0, 0)
    kern = functools.partial(_sample_mixer_kernel, alpha=alpha, n_heads=n_heads, attn_w=attn_w, kv_w=kv_w,
                             conv_ch=conv_ch, dec_seq=dec_seq)
    return pl.pallas_call(
        kern,
        grid=(dec_b // nb,),
        in_specs=[
            pl.BlockSpec((rows, d), lambda i: (i, 0)),
            pl.BlockSpec((nb, win, kv_w), lambda i: (i, 0, 0)),
            pl.BlockSpec((nb, win, kv_w), lambda i: (i, 0, 0)),
            pl.BlockSpec((nb, CONV_K - 1, conv_ch), lambda i: (i, 0, 0)),
            pl.BlockSpec((d, in_cols), const2),
            pl.BlockSpec((rows, LANES), const2),
            pl.BlockSpec((rows, LANES), const2),
            pl.BlockSpec((rows, LANES), const2),
            pl.BlockSpec(memory_space=pltpu.SMEM),
            pl.BlockSpec((CONV_K, conv_ch), const2),
            pl.BlockSpec((1, attn_w), const2),
            pl.BlockSpec((1, conv_ch), const2),
            pl.BlockSpec((attn_w + conv_ch, d), const2),
            pl.BlockSpec((1, d), const2),
            pl.BlockSpec((1, d), const2),
        ],
        out_specs=[
            pl.BlockSpec((rows, d), lambda i: (i, 0)),
            pl.BlockSpec((nb, win, kv_w), lambda i: (i, 0, 0)),
            pl.BlockSpec((nb, win, kv_w), lambda i: (i, 0, 0)),
            pl.BlockSpec((nb, CONV_K - 1, conv_ch), lambda i: (i, 0, 0)),
        ],
        out_shape=[
            jax.ShapeDtypeStruct((dec_b * dec_seq, d), F32),
            jax.ShapeDtypeStruct((dec_b, win, kv_w), F32),
            jax.ShapeDtypeStruct((dec_b, win, kv_w), F32),
            jax.ShapeDtypeStruct((dec_b, CONV_K - 1, conv_ch), F32),
        ],
        compiler_params=pltpu.CompilerParams(dimension_semantics=("arbitrary",),
                                             vmem_limit_bytes=VMEM_LIMIT_BYTES),
        name="sample_mixer",
    )(x.reshape(dec_b * dec_seq, d), ck, cv, st, win_b, *tabs, sinks, conv_w, g_attn, g_conv, wout_b,
      ln_g, ln_b)


def _over_experts(fn, x):
    return fn(fn(x, axis=0, keepdims=True), axis=1, keepdims=True)


def _two_group_specs(tm, d, n_prompt_tiles):
    return [pl.BlockSpec((tm, d), lambda i, *_: (jnp.minimum(i, n_prompt_tiles - 1), 0)),
            pl.BlockSpec((tm, d), lambda i, *_: (jnp.maximum(i - n_prompt_tiles, 0), 0))]


def _route_kernel(hp_ref, hs_ref, rwt_ref, bias_ref, eidx_ref, rank_ref, gate_ref, cnt_ref, *, n_experts,
                  n_prompt_tiles):
    i = pl.program_id(0)
    tm = hp_ref.shape[0]
    per_group = n_experts // N_EXPERT_GROUPS
    shape3 = (N_EXPERT_GROUPS, per_group, tm)

    h = jnp.where(i < n_prompt_tiles, hp_ref[...], hs_ref[...])
    logits = lax.dot_general(rwt_ref[...], h.astype(BF16), (((1,), (1,)), ((), ())),
                             preferred_element_type=F32)
    scores = jax.nn.sigmoid(logits)
    sel = scores + bias_ref[...]
    scores3 = scores.reshape(shape3)
    grp = sel.reshape(shape3)
    member = lax.broadcasted_iota(jnp.int32, shape3, 1).astype(F32)
    group = lax.broadcasted_iota(jnp.int32, shape3, 0).astype(F32)
    expert = group * per_group + member

    m1 = jnp.max(grp, axis=1, keepdims=True)
    f1 = jnp.min(jnp.where(grp == m1, member, float(per_group)), axis=1, keepdims=True)
    m2 = jnp.max(jnp.where(member == f1, NEG_INF, grp), axis=1, keepdims=True)
    gscore = m1 + m2

    gid = lax.broadcasted_iota(jnp.int32, gscore.shape, 0).astype(F32)
    gmask = jnp.zeros(gscore.shape, F32)
    cur = gscore
    for _ in range(TOPK_GROUPS):
        mx = jnp.max(cur, axis=0, keepdims=True)
        pick = gid == jnp.min(jnp.where(cur == mx, gid, float(N_EXPERT_GROUPS)), axis=0, keepdims=True)
        gmask = jnp.where(pick, 1.0, gmask)
        cur = jnp.where(pick, NEG_INF, cur)

    cand = jnp.where(gmask > 0.0, grp, NEG_INF)
    chosen = jnp.zeros(shape3, F32)
    picks, firsts, weights = [], [], []
    for _ in range(TOP_K):
        mx = _over_experts(jnp.max, cand)
        first = _over_experts(jnp.min, jnp.where(cand == mx, expert, float(n_experts)))
        pick = expert == first
        picks.append(pick)
        firsts.append(first)
        weights.append(_over_experts(jnp.sum, jnp.where(pick, scores3, 0.0)))
        chosen = jnp.where(pick, 1.0, chosen)
        cand = jnp.where(pick, NEG_INF, cand)
    wsum = weights[0]
    for w in weights[1:]:
        wsum = wsum + w

    chosen_b = chosen.reshape(n_experts, tm).astype(BF16)
    earlier = (lax.broadcasted_iota(jnp.int32, (tm, tm), 0)
               < lax.broadcasted_iota(jnp.int32, (tm, tm), 1)).astype(BF16)
    before3 = jnp.dot(chosen_b, earlier, preferred_element_type=F32).reshape(shape3)

    pad = SUBLANES - TOP_K
    eidx = [f.reshape(1, tm).astype(jnp.int32) for f in firsts]
    rank = [_over_experts(jnp.sum, jnp.where(p, before3, 0.0)).reshape(1, tm).astype(jnp.int32) for p in picks]
    gate = [(w / wsum * ROUTED_SCALE).reshape(1, tm) for w in weights]
    eidx_ref[...] = jnp.concatenate(eidx + [jnp.zeros((pad, tm), jnp.int32)], axis=0)
    rank_ref[...] = jnp.concatenate(rank + [jnp.zeros((pad, tm), jnp.int32)], axis=0)
    gate_ref[...] = jnp.concatenate(gate + [jnp.zeros((pad, tm), F32)], axis=0)

    cnt_ref[...] = lax.dot_general(jnp.ones((SUBLANES, tm), BF16), chosen_b, (((1,), (1,)), ((), ())),
                                   preferred_element_type=F32)


def _route(h_p, h_s, rwt_b, bias_col):
    d = h_p.shape[1]
    t_all = h_p.shape[0] + h_s.shape[0]
    n_experts = rwt_b.shape[0]
    tm = MOE_TILE
    nt = t_all // tm
    npt = h_p.shape[0] // tm
    row_spec = pl.BlockSpec((SUBLANES, tm), lambda i: (0, i))
    return pl.pallas_call(
        functools.partial(_route_kernel, n_experts=n_experts, n_prompt_tiles=npt),
        grid=(nt,),
        in_specs=_two_group_specs(tm, d, npt) + [
            pl.BlockSpec((n_experts, d), lambda i: (0, 0)),
            pl.BlockSpec((n_experts, 1), lambda i: (0, 0)),
        ],
        out_specs=[row_spec, row_spec, row_spec, pl.BlockSpec((SUBLANES, n_experts), lambda i: (i, 0))],
        out_shape=[
            jax.ShapeDtypeStruct((SUBLANES, t_all), jnp.int32),
            jax.ShapeDtypeStruct((SUBLANES, t_all), jnp.int32),
            jax.ShapeDtypeStruct((SUBLANES, t_all), F32),
            jax.ShapeDtypeStruct((nt * SUBLANES, n_experts), F32),
        ],
        compiler_params=pltpu.CompilerParams(dimension_semantics=("arbitrary",),
                                             vmem_limit_bytes=VMEM_LIMIT_BYTES),
        name="route",
    )(h_p, h_s, rwt_b, bias_col)


def _for_each_part(rows, max_rows, fn, chunk_rows=None):
    top = max_rows if chunk_rows is None else chunk_rows // 2
    for size in [SUBLANES << b for b in range((top // SUBLANES).bit_length())]:
        @pl.when((rows & size) != 0)
        def _(size=size):
            fn(pl.multiple_of(rows & (size - 1), SUBLANES), size)
    if chunk_rows is not None:
        base = rows & (chunk_rows - 1)

        def chunk(j, c):
            fn(pl.multiple_of(base + j * chunk_rows, SUBLANES), chunk_rows)
            return c

        lax.fori_loop(0, lax.shift_right_logical(rows, chunk_rows.bit_length() - 1), chunk, 0)


def _top_bit(n):
    return 1 << (n.bit_length() - 1)


def _pack_rows(x):
    rows, two_w = x.shape
    w = two_w // 2
    x3 = x.reshape(rows // SUBLANES, SUBLANES, two_w)
    halves = jnp.concatenate([x3[:, :, :w], x3[:, :, w:]], axis=1).astype(BF16)
    return pltpu.bitcast(halves, jnp.uint32).reshape(rows, w)


def _unpack_rows(p):
    rows, w = p.shape
    halves = pltpu.bitcast(p.reshape(rows // SUBLANES, SUBLANES, w), BF16).astype(F32)
    return jnp.concatenate([halves[:, :SUBLANES, :], halves[:, SUBLANES:, :]], axis=-1).reshape(rows, 2 * w)


def _local_rows(tm, n_experts):
    n = TOP_K * tm + n_experts * (SUBLANES - 1)
    return -(-n // SORT_CHUNK) * SORT_CHUNK


def _dispatch_kernel(cnt_ref, ls_ref, gs_ref, tot_ref, zrow_ref, lp_ref, hp_ref, hs_ref, xs_hbm,
                     lbuf, zero_s, sem, *, n_experts, blk, n_prompt_tiles):
    i = pl.program_id(0)
    tm, d = hp_ref.shape
    n = TOP_K * tm
    nl = lbuf.shape[1]
    slot = i % 2

    def tile_wait(tile):
        pltpu.make_async_copy(lbuf.at[0, pl.ds(0, n)], xs_hbm.at[pl.ds(0, n)], sem).wait()
        _for_each_part(tot_ref[tile] - n, _top_bit(n_experts * (SUBLANES - 1)), lambda off, size:
                       pltpu.make_async_copy(lbuf.at[0, pl.ds(0, size)], xs_hbm.at[pl.ds(0, size)], sem).wait())

    @pl.when(i == 0)
    def _():
        zero_s[...] = jnp.zeros(zero_s.shape, zero_s.dtype)

        def zcopy(e):
            return pltpu.make_async_copy(zero_s, xs_hbm.at[pl.ds(pl.multiple_of(zrow_ref[e], blk), blk)], sem)

        def start(e, c):
            zcopy(e).start()
            return c

        def wait(e, c):
            zcopy(e).wait()
            return c

        lax.fori_loop(0, n_experts, start, 0)
        lax.fori_loop(0, n_experts, wait, 0)

    hb = jnp.where(i < n_prompt_tiles, hp_ref[...], hs_ref[...]).astype(BF16)
    lp = lp_ref[...]
    def sort_chunk(r):
        rows = r * SORT_CHUNK + lax.broadcasted_iota(jnp.int32, (SORT_CHUNK, tm), 0)
        hit = jnp.zeros((SORT_CHUNK, tm), F32)
        for k in range(TOP_K):
            hit = jnp.where(rows == lp[k:k + 1, :], 1.0, hit)
        lbuf[slot, pl.ds(r * SORT_CHUNK, SORT_CHUNK), :] = _pack_rows(
            jnp.dot(hit.astype(BF16), hb, preferred_element_type=F32))

    for r in range(nl // SORT_CHUNK):
        if (r + 1) * SORT_CHUNK <= n:
            sort_chunk(r)
        else:
            pl.when(tot_ref[i] > r * SORT_CHUNK)(functools.partial(sort_chunk, r))

    @pl.when(i > 0)
    def _():
        tile_wait(i - 1)

    def per_expert(e, c):
        idx = i * n_experts + e
        l0, g0 = ls_ref[idx], gs_ref[idx]
        _for_each_part(cnt_ref[idx], tm, lambda off, size: pltpu.make_async_copy(
            lbuf.at[slot, pl.ds(pl.multiple_of(l0 + off, SUBLANES), size)],
            xs_hbm.at[pl.ds(pl.multiple_of(g0 + off, SUBLANES), size)], sem).start(), RUN_CHUNK)
        return c

    lax.fori_loop(0, n_experts, per_expert, 0, unroll=RUN_UNROLL)

    @pl.when(i == pl.num_programs(0) - 1)
    def _():
        tile_wait(i)


def _dispatch(tables, zrow, lp, h_p, h_s, *, n_rows, blk):
    d = h_p.shape[1]
    t_all = h_p.shape[0] + h_s.shape[0]
    tm = MOE_TILE
    npt = h_p.shape[0] // tm
    n_experts = zrow.shape[0]
    return pl.pallas_call(
        functools.partial(_dispatch_kernel, n_experts=n_experts, blk=blk, n_prompt_tiles=npt),
        grid_spec=pltpu.PrefetchScalarGridSpec(
            num_scalar_prefetch=5,
            grid=(t_all // tm,),
            in_specs=[pl.BlockSpec((SUBLANES, tm), lambda i, *_: (0, i))] + _two_group_specs(tm, d, npt),
            out_specs=pl.BlockSpec(memory_space=pl.ANY),
            scratch_shapes=[
                pltpu.VMEM((2, _local_rows(tm, n_experts), d // 2), jnp.uint32),
                pltpu.VMEM((blk, d // 2), jnp.uint32),
                pltpu.SemaphoreType.DMA,
            ],
        ),
        out_shape=jax.ShapeDtypeStruct((n_rows, d // 2), jnp.uint32),
        compiler_params=pltpu.CompilerParams(dimension_semantics=("arbitrary",),
                                             vmem_limit_bytes=VMEM_LIMIT_BYTES),
        name="dispatch",
    )(*tables, zrow, lp, h_p, h_s)


def _silu(x):
    return x * jax.nn.sigmoid(x)


def _expert_kernel(be_ref, nact_ref, xs_ref, wg_ref, wu_ref, wd_ref, ys_ref, wg_s, wu_s, wd_s):
    b = pl.program_id(0)

    @pl.when(b < nact_ref[0])
    def _():
        @pl.when((b == 0) | (be_ref[b] != be_ref[jnp.maximum(b - 1, 0)]))
        def _():
            wg_s[...] = wg_ref[...].astype(BF16)
            wu_s[...] = wu_ref[...].astype(BF16)
            wd_s[...] = wd_ref[...].astype(BF16)

        xb = _unpack_rows(xs_ref[...]).astype(BF16)
        hid = _silu(jnp.dot(xb, wg_s[...], preferred_element_type=F32)) * jnp.dot(
            xb, wu_s[...], preferred_element_type=F32)
        ys_ref[...] = _pack_rows(jnp.dot(hid.astype(BF16), wd_s[...], preferred_element_type=F32))


def _experts(block_expert, nact, xs, w_gate, w_up, w_down, *, blk):
    n_rows, dw = xs.shape
    d = w_gate.shape[-2]
    ff = w_gate.shape[-1]
    n_blocks = n_rows // blk

    def active(b, be, na):
        return jnp.minimum(b, na[0] - 1)

    return pl.pallas_call(
        _expert_kernel,
        grid_spec=pltpu.PrefetchScalarGridSpec(
            num_scalar_prefetch=2,
            grid=(n_blocks,),
            in_specs=[
                pl.BlockSpec((blk, dw), lambda b, be, na: (active(b, be, na), 0)),
                pl.BlockSpec((None, d, ff), lambda b, be, na: (be[active(b, be, na)], 0, 0)),
                pl.BlockSpec((None, d, ff), lambda b, be, na: (be[active(b, be, na)], 0, 0)),
                pl.BlockSpec((None, ff, d), lambda b, be, na: (be[active(b, be, na)], 0, 0)),
            ],
            out_specs=pl.BlockSpec((blk, dw), lambda b, be, na: (active(b, be, na), 0)),
            scratch_shapes=[
                pltpu.VMEM((d, ff), BF16),
                pltpu.VMEM((d, ff), BF16),
                pltpu.VMEM((ff, d), BF16),
            ],
        ),
        out_shape=jax.ShapeDtypeStruct((n_rows, dw), jnp.uint32),
        compiler_params=pltpu.CompilerParams(dimension_semantics=("arbitrary",),
                                             vmem_limit_bytes=VMEM_LIMIT_BYTES),
        name="experts",
    )(block_expert, nact, xs, w_gate, w_up, w_down)


def _combine_kernel(cnt_ref, ls_ref, gs_ref, tot_ref, ys_hbm, lpt_ref, gate_ref, hp_ref, hs_ref, wsg_ref, wsu_ref,
                    wsd_ref, lng_ref, lnb_ref, yp_ref, ysm_ref, ybuf, moe_s, sems, *, alpha, n_experts,
                    n_prompt_tiles):
    i = pl.program_id(0)
    tm, d = hp_ref.shape
    n = TOP_K * tm
    nl = ybuf.shape[1]
    slot = i % 2

    def fetch(tile, to):
        def per_expert(e, c):
            idx = tile * n_experts + e
            l0, g0 = ls_ref[idx], gs_ref[idx]
            _for_each_part(cnt_ref[idx], tm, lambda off, size: pltpu.make_async_copy(
                ys_hbm.at[pl.ds(pl.multiple_of(g0 + off, SUBLANES), size)],
                ybuf.at[to, pl.ds(pl.multiple_of(l0 + off, SUBLANES), size)], sems.at[to]).start(), RUN_CHUNK)
            return c

        lax.fori_loop(0, n_experts, per_expert, 0, unroll=RUN_UNROLL)

    @pl.when(i == 0)
    def _():
        ybuf[...] = jnp.zeros(ybuf.shape, ybuf.dtype)
        fetch(0, 0)

    @pl.when(i + 1 < pl.num_programs(0))
    def _():
        fetch(i + 1, 1 - slot)

    h = jnp.where(i < n_prompt_tiles, hp_ref[...], hs_ref[...])
    hb = h.astype(BF16)
    hid = _silu(jnp.dot(hb, wsg_ref[...], preferred_element_type=F32)) * jnp.dot(
        hb, wsu_ref[...], preferred_element_type=F32)
    shared = jnp.dot(hid.astype(BF16), wsd_ref[...], preferred_element_type=F32)

    pltpu.make_async_copy(ys_hbm.at[pl.ds(0, n)], ybuf.at[slot, pl.ds(0, n)], sems.at[slot]).wait()
    _for_each_part(tot_ref[i] - n, _top_bit(n_experts * (SUBLANES - 1)), lambda off, size: pltpu.make_async_copy(
        ys_hbm.at[pl.ds(0, size)], ybuf.at[slot, pl.ds(0, size)], sems.at[slot]).wait())

    lpt = lpt_ref[...]
    gate = gate_ref[...]
    moe_s[...] = shared

    def add_chunk(r):
        cols = r * SORT_CHUNK + lax.broadcasted_iota(jnp.int32, (tm, SORT_CHUNK), 1)
        g = jnp.zeros((tm, SORT_CHUNK), F32)
        for k in range(TOP_K):
            g = jnp.where(cols == lpt[:, k:k + 1], gate[:, k:k + 1], g)
        yb = _unpack_rows(ybuf[slot, pl.ds(r * SORT_CHUNK, SORT_CHUNK), :]).astype(BF16)
        moe_s[...] += jnp.dot(g.astype(BF16), yb, preferred_element_type=F32)

    for r in range(nl // SORT_CHUNK):
        if (r + 1) * SORT_CHUNK <= n:
            add_chunk(r)
        else:
            pl.when(tot_ref[i] > r * SORT_CHUNK)(functools.partial(add_chunk, r))
    y = _layer_norm(alpha * h + moe_s[...], lng_ref[...], lnb_ref[...])

    @pl.when(i < n_prompt_tiles)
    def _():
        yp_ref[...] = y

    @pl.when(i >= n_prompt_tiles)
    def _():
        ysm_ref[...] = y


def _combine(tables, ys, lp_t, gates_t, h_p, h_s, wsg_b, wsu_b, wsd_b, ln_g, ln_b, *, alpha):
    t_prompt, d = h_p.shape
    t_all = t_prompt + h_s.shape[0]
    tm = MOE_TILE
    ff = wsg_b.shape[-1]
    npt = t_prompt // tm
    n_experts = tables[0].shape[0] // (t_all // tm)
    const2 = lambda i, *_: (0, 0)
    return pl.pallas_call(
        functools.partial(_combine_kernel, alpha=alpha, n_experts=n_experts, n_prompt_tiles=npt),
        grid_spec=pltpu.PrefetchScalarGridSpec(
            num_scalar_prefetch=4,
            grid=(t_all // tm,),
            in_specs=[
                pl.BlockSpec(memory_space=pl.ANY),
                pl.BlockSpec((tm, SUBLANES), lambda i, *_: (i, 0)),
                pl.BlockSpec((tm, SUBLANES), lambda i, *_: (i, 0)),
                *_two_group_specs(tm, d, npt),
                pl.BlockSpec((d, ff), const2),
                pl.BlockSpec((d, ff), const2),
                pl.BlockSpec((ff, d), const2),
                pl.BlockSpec((1, d), const2),
                pl.BlockSpec((1, d), const2),
            ],
            out_specs=[
                pl.BlockSpec((tm, d), lambda i, *_: (jnp.minimum(i, npt - 1), 0)),
                pl.BlockSpec((tm, d), lambda i, *_: (jnp.maximum(i - npt, 0), 0)),
            ],
            scratch_shapes=[
                pltpu.VMEM((2, _local_rows(tm, n_experts), d // 2), jnp.uint32),
                pltpu.VMEM((tm, d), F32),
                pltpu.SemaphoreType.DMA((2,)),
            ],
        ),
        out_shape=[
            jax.ShapeDtypeStruct((t_prompt, d), F32),
            jax.ShapeDtypeStruct((t_all - t_prompt, d), F32),
        ],
        compiler_params=pltpu.CompilerParams(dimension_semantics=("arbitrary",),
                                             vmem_limit_bytes=VMEM_LIMIT_BYTES),
        name="combine",
    )(*tables, ys, lp_t, gates_t, h_p, h_s, wsg_b, wsu_b, wsd_b, ln_g, ln_b)


def _moe(h_p, h_s, router_w, router_bias, w_gate, w_up, w_down, ws_gate, ws_up, ws_down, ln_g, ln_b, *, alpha):
    t_all = h_p.shape[0] + h_s.shape[0]
    n_experts = router_w.shape[-1]
    blk = EXPERT_BLOCK
    tm = MOE_TILE
    nt = t_all // tm
    eidx8, lrank8, gate8, cnt8 = _route(h_p, h_s, router_w.T.astype(BF16), router_bias.reshape(n_experts, 1))

    cnt = cnt8.reshape(nt, SUBLANES, n_experts)[:, 0, :].astype(jnp.int32)
    cnt = (cnt + SUBLANES - 1) // SUBLANES * SUBLANES
    counts = jnp.sum(cnt, axis=0)
    blocks_per_e = (counts + blk - 1) // blk
    block_end = jnp.cumsum(blocks_per_e)
    pad_start = (block_end - blocks_per_e) * blk
    n_blocks = -(-(t_all * TOP_K + nt * n_experts * (SUBLANES - 1)) // blk) + n_experts
    block_expert = jnp.minimum(jnp.sum(block_end[None, :] <= jnp.arange(n_blocks)[:, None], axis=1),
                               n_experts - 1).astype(jnp.int32)
    nact = block_end[-1:].astype(jnp.int32)
    zrow = (jnp.maximum(block_end - 1, 0) * blk).astype(jnp.int32)
    gstart = pad_start[None, :] + jnp.cumsum(cnt, axis=0) - cnt
    lstart = jnp.cumsum(cnt, axis=1) - cnt
    tables = tuple(a.reshape(-1).astype(jnp.int32) for a in (cnt, lstart, gstart, jnp.sum(cnt, axis=1)))
    lstart_tok = jnp.repeat(lstart, tm, axis=0)
    lp8 = jnp.sum(jnp.where(eidx8[..., None] == jnp.arange(n_experts), lstart_tok[None], 0), axis=-1) + lrank8
    lp8 = lp8.astype(jnp.int32)

    xs = _dispatch(tables, zrow, lp8, h_p, h_s, n_rows=n_blocks * blk, blk=blk)
    ys = _experts(block_expert, nact, xs, w_gate, w_up, w_down, blk=blk)
    return _combine(tables, ys, lp8.T, gate8.T, h_p, h_s, ws_gate.astype(BF16), ws_up.astype(BF16), ws_down.astype(BF16),
                    ln_g, ln_b, alpha=alpha)


def kernel(x_prompt, x_sample, cache_k, cache_v, state_conv, w_in, attn_sinks, conv_w, g_attn_out, g_conv_out, w_out, ln1_g, ln1_b, router_w, router_bias, w_gate, w_up, w_down, ws_gate, ws_up, ws_down, ln2_g, ln2_b):
    depth = w_in.shape[0]
    bsz, seq, d = x_prompt.shape
    dec_b, dec_seq, _ = x_sample.shape
    win = cache_k.shape[2]
    kv_w = N_KV_HEADS * HEAD_DIM
    t_prompt = bsz * seq
    t_all = t_prompt + dec_b * dec_seq
    alpha = (2.0 * depth) ** 0.25
    assert win == WINDOW and seq % PROMPT_TILE == 0 and dec_b % SAMPLE_SEQS == 0
    assert t_prompt % MOE_TILE == 0 and (t_all - t_prompt) % MOE_TILE == 0
    assert MOE_TILE & (MOE_TILE - 1) == 0 and (TOP_K * MOE_TILE) % SORT_CHUNK == 0

    tabs_p = _rope_tables(jnp.arange(seq))
    tabs_s = tuple(jnp.tile(t, (SAMPLE_SEQS, 1)) for t in _rope_tables(PAST_LEN + jnp.arange(dec_seq)))
    row = lambda a: a.reshape(1, -1)

    xp, xs = x_prompt, x_sample
    outs = [[] for _ in range(6)]
    for l in range(depth):
        win_b, wout_b = w_in[l].astype(BF16), w_out[l].astype(BF16)
        shared = (attn_sinks[l], conv_w[l], row(g_attn_out[l]), row(g_conv_out[l]), wout_b, row(ln1_g[l]),
                  row(ln1_b[l]))
        h_p, kp, vp, cp = _prompt_mixer(xp, win_b, tabs_p, *shared, alpha=alpha)
        h_s, kn, vn, cn = _sample_mixer(xs, cache_k[l].reshape(dec_b, win, kv_w),
                                        cache_v[l].reshape(dec_b, win, kv_w), state_conv[l], win_b, tabs_s,
                                        *shared, alpha=alpha)
        yp, ys = _moe(h_p, h_s, router_w[l], router_bias[l], w_gate[l], w_up[l], w_down[l], ws_gate[l], ws_up[l],
                      ws_down[l], row(ln2_g[l]), row(ln2_b[l]), alpha=alpha)
        xp, xs = yp.reshape(bsz, seq, d), ys.reshape(dec_b, dec_seq, d)
        heads = lambda a: a.reshape(a.shape[0], win, N_KV_HEADS, HEAD_DIM)
        for o, a in zip(outs, (heads(kp), heads(vp), cp, heads(kn), heads(vn), cn)):
            o.append(a)
    return (xp, xs) + tuple(jnp.stack(o, axis=0) for o in outs)
```

```python
import functools

import jax
import jax.numpy as jnp
from jax import lax
from jax.experimental import pallas as pl
from jax.experimental.pallas import tpu as pltpu

PAST_LEN = 16384
WINDOW = 128
HEAD_DIM = 64
N_KV_HEADS = 2
ROT_DIM = HEAD_DIM // 4
ROPE_THETA = 500000.0
CONV_K = 3
TOP_K = 6
N_EXPERT_GROUPS = 8
TOPK_GROUPS = 4
ROUTED_SCALE = 2.5
LN_EPS = 1e-5
RMS_EPS = 1e-6

LANES = 128
SUBLANES = 8
VMEM_LIMIT_BYTES = 56 * 1024 * 1024

PROMPT_TILE = 512
SAMPLE_SEQS = 16
MOE_TILE = 256
SORT_CHUNK = 256
EXPERT_BLOCK = 1024
RUN_CHUNK = 64

F32 = jnp.float32
BF16 = jnp.bfloat16
NEG_INF = float("-inf")


def _rope_tables(positions):
    half = ROT_DIM // 2
    inv_freq = ROPE_THETA ** (-jnp.arange(0, ROT_DIM, 2, dtype=F32) / ROT_DIM)
    ang = positions.astype(F32)[:, None] * inv_freq[None, :]
    cos, sin = jnp.cos(ang), jnp.sin(ang)
    n = positions.shape[0]
    rest = HEAD_DIM - ROT_DIM
    c = jnp.concatenate([cos, cos, jnp.ones((n, rest), F32)], axis=-1)
    sa = jnp.concatenate([-sin, jnp.zeros((n, half + rest), F32)], axis=-1)
    sb = jnp.concatenate([jnp.zeros((n, half), F32), sin, jnp.zeros((n, rest), F32)], axis=-1)
    reps = LANES // HEAD_DIM
    return jnp.tile(c, (1, reps)), jnp.tile(sa, (1, reps)), jnp.tile(sb, (1, reps))


def _rope(x, c, sa, sb):
    half = ROT_DIM // 2
    return x * c + pltpu.roll(x, LANES - half, 1) * sa + pltpu.roll(x, half, 1) * sb


def _rms_norm(x, g):
    return x * lax.rsqrt(jnp.mean(jnp.square(x), axis=-1, keepdims=True) + RMS_EPS) * g


def _layer_norm(x, g, b):
    mu = jnp.mean(x, axis=-1, keepdims=True)
    var = jnp.mean(jnp.square(x - mu), axis=-1, keepdims=True)
    return (x - mu) * lax.rsqrt(var + LN_EPS) * g + b


def _short_conv(gated, prev2, prev1, row, conv_w, b_gate):
    g1 = pltpu.roll(gated, 1, 0)
    g2 = pltpu.roll(gated, 2, 0)
    g1 = jnp.where(row == 0, prev1, g1)
    g2 = jnp.where(row == 0, prev2, jnp.where(row == 1, prev1, g2))
    y = conv_w[0:1, :] * g2 + conv_w[1:2, :] * g1 + conv_w[2:3, :] * gated
    return b_gate * y


def _merge_norm(x, attn_o, conv_o, gattn, gconv, wout_ref, ln_g, ln_b, alpha):
    cat = jnp.concatenate([_rms_norm(attn_o, gattn), _rms_norm(conv_o, gconv)], axis=-1)
    mix = jnp.dot(cat.astype(BF16), wout_ref[...], preferred_element_type=F32)
    return _layer_norm(alpha * x + mix, ln_g, ln_b)


def _sink_softmax(parts, sink):
    m = sink
    for s in parts:
        m = jnp.maximum(m, jnp.max(s, axis=-1, keepdims=True))
    es = [jnp.exp(s - m) for s in parts]
    den = jnp.exp(sink - m)
    for e in es:
        den = den + jnp.sum(e, axis=-1, keepdims=True)
    return [(e / den).astype(BF16) for e in es]


def _prompt_mixer_kernel(x_ref, win_ref, c_ref, sa_ref, sb_ref, sinks_ref, convw_ref, gattn_ref, gconv_ref,
                         wout_ref, lng_ref, lnb_ref,
                         h_ref, ko_ref, vo_ref, co_ref,
                         q_s, kt_s, vm_s, o_s, gc_s, *, alpha, n_heads, attn_w, kv_w, conv_ch):
    s = pl.program_id(1)
    last = pl.num_programs(1) - 1
    tq = x_ref.shape[0]
    q_per_kv = n_heads // N_KV_HEADS
    heads_per_group = LANES // HEAD_DIM

    @pl.when(s == 0)
    def _():
        kt_s[:, :, 0:WINDOW] = jnp.zeros((kt_s.shape[0], LANES, WINDOW), BF16)
        vm_s[:, 0:WINDOW, :] = jnp.zeros((vm_s.shape[0], WINDOW, LANES), BF16)
        gc_s[...] = jnp.zeros(gc_s.shape, F32)

    @pl.when(s > 0)
    def _():
        kt_s[:, :, 0:WINDOW] = kt_s[:, :, tq:tq + WINDOW]
        vm_s[:, 0:WINDOW, :] = vm_s[:, tq:tq + WINDOW, :]

    x = x_ref[...]
    xb = x.astype(BF16)
    c, sa, sb = c_ref[...], sa_ref[...], sb_ref[...]

    def proj(lo, width):
        return jnp.dot(xb, win_ref[:, lo:lo + width], preferred_element_type=F32)

    scale = HEAD_DIM ** -0.5
    for j in range(attn_w // LANES):
        qj = _rope(proj(j * LANES, LANES), c, sa, sb)
        q_s[:, j * LANES:(j + 1) * LANES] = (qj * scale).astype(BF16)
    k = _rope(proj(attn_w, kv_w), c, sa, sb)
    v = proj(attn_w + kv_w, kv_w)

    @pl.when(s == last)
    def _():
        ko_ref[...] = k[tq - WINDOW:, :]
        vo_ref[...] = v[tq - WINDOW:, :]

    kt = k.T.astype(BF16)
    zeros_k = jnp.zeros((HEAD_DIM, tq), BF16)
    v_swapped = pltpu.roll(v, HEAD_DIM, 1)
    low_lanes = lax.broadcasted_iota(jnp.int32, (tq, LANES), 1) < HEAD_DIM
    for kvh in range(N_KV_HEADS):
        kt_h = kt[kvh * HEAD_DIM:(kvh + 1) * HEAD_DIM, :]
        v_lo = v if kvh == 0 else v_swapped
        v_hi = v_swapped if kvh == 0 else v
        kt_s[2 * kvh, :, WINDOW:] = jnp.concatenate([kt_h, zeros_k], axis=0)
        kt_s[2 * kvh + 1, :, WINDOW:] = jnp.concatenate([zeros_k, kt_h], axis=0)
        vm_s[2 * kvh, WINDOW:, :] = jnp.where(low_lanes, v_lo, 0.0).astype(BF16)
        vm_s[2 * kvh + 1, WINDOW:, :] = jnp.where(low_lanes, 0.0, v_hi).astype(BF16)

    qi = lax.broadcasted_iota(jnp.int32, (WINDOW, 2 * WINDOW), 0)
    ci = lax.broadcasted_iota(jnp.int32, (WINDOW, 2 * WINDOW), 1)
    band = (ci > qi) & (ci <= qi + WINDOW)
    for j in range(tq // WINDOW):
        r0 = j * WINDOW
        mask = band if j > 0 else band & ((ci >= WINDOW) | (s > 0))
        for grp in range(attn_w // LANES):
            q_grp = q_s[r0:r0 + WINDOW, grp * LANES:(grp + 1) * LANES]
            out = None
            for r in range(heads_per_group):
                hd = grp * heads_per_group + r
                src = 2 * (hd // q_per_kv) + r
                sink = sinks_ref[hd]
                sc = jnp.dot(q_grp, kt_s[src, :, r0:r0 + 2 * WINDOW], preferred_element_type=F32)
                sc = jnp.where(mask, sc, NEG_INF)
                m = jnp.maximum(jnp.max(sc, axis=-1, keepdims=True), sink)
                e = jnp.exp(sc - m)
                den = jnp.sum(e, axis=-1, keepdims=True) + jnp.exp(sink - m)
                o_h = jnp.dot(e.astype(BF16), vm_s[src, r0:r0 + 2 * WINDOW, :],
                              preferred_element_type=F32) * (1.0 / den)
                out = o_h if out is None else out + o_h
            o_s[r0:r0 + WINDOW, grp * LANES:(grp + 1) * LANES] = out

    o3 = attn_w + 2 * kv_w
    gated = proj(o3 + 2 * conv_ch, conv_ch) * proj(o3, conv_ch)
    row = lax.broadcasted_iota(jnp.int32, (tq, 1), 0)
    conv_o = _short_conv(gated, gc_s[0:1, :], gc_s[1:2, :], row, convw_ref[...], proj(o3 + conv_ch, conv_ch))
    gc_s[0:CONV_K - 1, :] = gated[tq - (CONV_K - 1):, :]

    @pl.when(s == last)
    def _():
        co_ref[...] = gated[tq - (CONV_K - 1):, :]

    h_ref[...] = _merge_norm(x, o_s[...], conv_o, gattn_ref[...], gconv_ref[...], wout_ref,
                             lng_ref[...], lnb_ref[...], alpha)


def _prompt_mixer(x, win_b, tabs, sinks, conv_w, g_attn, g_conv, wout_b, ln_g, ln_b, *, alpha):
    bsz, seq, d = x.shape
    tq = PROMPT_TILE
    ns = seq // tq
    attn_w = g_attn.shape[-1]
    conv_ch = g_conv.shape[-1]
    n_heads = attn_w // HEAD_DIM
    kv_w = N_KV_HEADS * HEAD_DIM
    assert kv_w == LANES and 2 * HEAD_DIM == LANES and (n_heads // N_KV_HEADS) % 2 == 0 and WINDOW == LANES
    in_cols = win_b.shape[-1]
    const2 = lambda b, s: (0, 0)
    kern = functools.partial(_prompt_mixer_kernel, alpha=alpha, n_heads=n_heads, attn_w=attn_w, kv_w=kv_w,
                             conv_ch=conv_ch)
    return pl.pallas_call(
        kern,
        grid=(bsz, ns),
        in_specs=[
            pl.BlockSpec((None, tq, d), lambda b, s: (b, s, 0)),
            pl.BlockSpec((d, in_cols), const2),
            pl.BlockSpec((tq, LANES), lambda b, s: (s, 0)),
            pl.BlockSpec((tq, LANES), lambda b, s: (s, 0)),
            pl.BlockSpec((tq, LANES), lambda b, s: (s, 0)),
            pl.BlockSpec(memory_space=pltpu.SMEM),
            pl.BlockSpec((CONV_K, conv_ch), const2),
            pl.BlockSpec((1, attn_w), const2),
            pl.BlockSpec((1, conv_ch), const2),
            pl.BlockSpec((attn_w + conv_ch, d), const2),
            pl.BlockSpec((1, d), const2),
            pl.BlockSpec((1, d), const2),
        ],
        out_specs=[
            pl.BlockSpec((tq, d), lambda b, s: (b * ns + s, 0)),
            pl.BlockSpec((None, WINDOW, kv_w), lambda b, s: (b, 0, 0)),
            pl.BlockSpec((None, WINDOW, kv_w), lambda b, s: (b, 0, 0)),
            pl.BlockSpec((None, CONV_K - 1, conv_ch), lambda b, s: (b, 0, 0)),
        ],
        out_shape=[
            jax.ShapeDtypeStruct((bsz * seq, d), F32),
            jax.ShapeDtypeStruct((bsz, WINDOW, kv_w), F32),
            jax.ShapeDtypeStruct((bsz, WINDOW, kv_w), F32),
            jax.ShapeDtypeStruct((bsz, CONV_K - 1, conv_ch), F32),
        ],
        scratch_shapes=[
            pltpu.VMEM((tq, attn_w), BF16),
            pltpu.VMEM((2 * N_KV_HEADS, LANES, WINDOW + tq), BF16),
            pltpu.VMEM((2 * N_KV_HEADS, WINDOW + tq, LANES), BF16),
            pltpu.VMEM((tq, attn_w), F32),
            pltpu.VMEM((SUBLANES, conv_ch), F32),
        ],
        compiler_params=pltpu.CompilerParams(dimension_semantics=("arbitrary", "arbitrary"),
                                             vmem_limit_bytes=VMEM_LIMIT_BYTES),
        name="prompt_mixer",
    )(x, win_b, *tabs, sinks, conv_w, g_attn, g_conv, wout_b, ln_g, ln_b)


def _sample_mixer_kernel(x_ref, ck_ref, cv_ref, st_ref, win_ref, c_ref, sa_ref, sb_ref, sinks_ref,
                         convw_ref, gattn_ref, gconv_ref, wout_ref, lng_ref, lnb_ref,
                         h_ref, ko_ref, vo_ref, co_ref, *, alpha, n_heads, attn_w, kv_w, conv_ch, dec_seq):
    nb, win = ck_ref.shape[0], ck_ref.shape[1]
    rows = nb * dec_seq
    q_per_kv = n_heads // N_KV_HEADS
    x = x_ref[...]
    xb = x.astype(BF16)
    c, sa, sb = c_ref[...], sa_ref[...], sb_ref[...]

    def proj(lo, width):
        return jnp.dot(xb, win_ref[:, lo:lo + width], preferred_element_type=F32)

    scale = HEAD_DIM ** -0.5
    k = _rope(proj(attn_w, kv_w), c, sa, sb)
    v = proj(attn_w + kv_w, kv_w)
    k3 = k.reshape(nb, dec_seq, kv_w)
    v3 = v.reshape(nb, dec_seq, kv_w)
    ck = ck_ref[...]
    cv = cv_ref[...]
    ko_ref[:, 0:win - dec_seq, :] = ck[:, dec_seq:, :]
    ko_ref[:, win - dec_seq:, :] = k3
    vo_ref[:, 0:win - dec_seq, :] = cv[:, dec_seq:, :]
    vo_ref[:, win - dec_seq:, :] = v3
    ckb, cvb, k3b, v3b = ck.astype(BF16), cv.astype(BF16), k3.astype(BF16), v3.astype(BF16)

    qrows = q_per_kv * dec_seq
    qi = lax.broadcasted_iota(jnp.int32, (nb, qrows, win), 1) % dec_seq
    mask_c = lax.broadcasted_iota(jnp.int32, (nb, qrows, win), 2) > qi + (win - WINDOW)
    qn = lax.broadcasted_iota(jnp.int32, (nb, qrows, dec_seq), 1) % dec_seq
    mask_n = lax.broadcasted_iota(jnp.int32, (nb, qrows, dec_seq), 2) <= qn
    sink_row = lax.broadcasted_iota(jnp.int32, (nb, qrows, 1), 1) // dec_seq

    q_chunks = [_rope(proj(j * LANES, LANES), c, sa, sb) * scale for j in range(attn_w // LANES)]
    heads_out = []
    for kvh in range(N_KV_HEADS):
        qs = []
        for g in range(q_per_kv):
            lo = (kvh * q_per_kv + g) * HEAD_DIM
            qh = q_chunks[lo // LANES][:, lo % LANES:lo % LANES + HEAD_DIM]
            qs.append(qh.reshape(nb, dec_seq, HEAD_DIM))
        qg = jnp.concatenate(qs, axis=1).astype(BF16)
        sl = slice(kvh * HEAD_DIM, (kvh + 1) * HEAD_DIM)
        sc_c = jnp.einsum("bqd,bkd->bqk", qg, ckb[:, :, sl], preferred_element_type=F32)
        sc_n = jnp.einsum("bqd,bkd->bqk", qg, k3b[:, :, sl], preferred_element_type=F32)
        sc_c = jnp.where(mask_c, sc_c, NEG_INF)
        sc_n = jnp.where(mask_n, sc_n, NEG_INF)
        sink = jnp.zeros((nb, qrows, 1), F32)
        for g in range(q_per_kv):
            sink = jnp.where(sink_row == g, sinks_ref[kvh * q_per_kv + g], sink)
        p_c, p_n = _sink_softmax([sc_c, sc_n], sink)
        og = (jnp.einsum("bqk,bkd->bqd", p_c, cvb[:, :, sl], preferred_element_type=F32)
              + jnp.einsum("bqk,bkd->bqd", p_n, v3b[:, :, sl], preferred_element_type=F32))
        for g in range(q_per_kv):
            heads_out.append(og[:, g * dec_seq:(g + 1) * dec_seq, :].reshape(rows, HEAD_DIM))
    attn_o = jnp.concatenate(heads_out, axis=-1)

    o3 = attn_w + 2 * kv_w
    gated = proj(o3 + 2 * conv_ch, conv_ch) * proj(o3, conv_ch)
    st = st_ref[...]
    prev2 = jnp.broadcast_to(st[:, 0:1, :], (nb, dec_seq, conv_ch)).reshape(rows, conv_ch)
    prev1 = jnp.broadcast_to(st[:, 1:2, :], (nb, dec_seq, conv_ch)).reshape(rows, conv_ch)
    row = lax.broadcasted_iota(jnp.int32, (rows, 1), 0) % dec_seq
    conv_o = _short_conv(gated, prev2, prev1, row, convw_ref[...], proj(o3 + conv_ch, conv_ch))
    co_ref[...] = gated.reshape(nb, dec_seq, conv_ch)[:, dec_seq - (CONV_K - 1):, :]

    h_ref[...] = _merge_norm(x, attn_o, conv_o, gattn_ref[...], gconv_ref[...], wout_ref,
                             lng_ref[...], lnb_ref[...], alpha)


def _sample_mixer(x, ck, cv, st, win_b, tabs, sinks, conv_w, g_attn, g_conv, wout_b, ln_g, ln_b, *, alpha):
    dec_b, dec_seq, d = x.shape
    assert dec_seq >= CONV_K - 1 and dec_seq % SUBLANES == 0
    nb = SAMPLE_SEQS
    rows = nb * dec_seq
    win = ck.shape[1]
    attn_w = g_attn.shape[-1]
    conv_ch = g_conv.shape[-1]
    n_heads = attn_w // HEAD_DIM
    kv_w = N_KV_HEADS * HEAD_DIM
    in_cols = win_b.shape[-1]
    const2 = lambda i: (0, 0)
    kern = functools.partial(_sample_mixer_kernel, alpha=alpha, n_heads=n_heads, attn_w=attn_w, kv_w=kv_w,
                             conv_ch=conv_ch, dec_seq=dec_seq)
    return pl.pallas_call(
        kern,
        grid=(dec_b // nb,),
        in_specs=[
            pl.BlockSpec((rows, d), lambda i: (i, 0)),
            pl.BlockSpec((nb, win, kv_w), lambda i: (i, 0, 0)),
            pl.BlockSpec((nb, win, kv_w), lambda i: (i, 0, 0)),
            pl.BlockSpec((nb, CONV_K - 1, conv_ch), lambda i: (i, 0, 0)),
            pl.BlockSpec((d, in_cols), const2),
            pl.BlockSpec((rows, LANES), const2),
            pl.BlockSpec((rows, LANES), const2),
            pl.BlockSpec((rows, LANES), const2),
            pl.BlockSpec(memory_space=pltpu.SMEM),
            pl.BlockSpec((CONV_K, conv_ch), const2),
            pl.BlockSpec((1, attn_w), const2),
            pl.BlockSpec((1, conv_ch), const2),
            pl.BlockSpec((attn_w + conv_ch, d), const2),
            pl.BlockSpec((1, d), const2),
            pl.BlockSpec((1, d), const2),
        ],
        out_specs=[
            pl.BlockSpec((rows, d), lambda i: (i, 0)),
            pl.BlockSpec((nb, win, kv_w), lambda i: (i, 0, 0)),
            pl.BlockSpec((nb, win, kv_w), lambda i: (i, 0, 0)),
            pl.BlockSpec((nb, CONV_K - 1, conv_ch), lambda i: (i, 0, 0)),
        ],
        out_shape=[
            jax.ShapeDtypeStruct((dec_b * dec_seq, d), F32),
            jax.ShapeDtypeStruct((dec_b, win, kv_w), F32),
            jax.ShapeDtypeStruct((dec_b, win, kv_w), F32),
            jax.ShapeDtypeStruct((dec_b, CONV_K - 1, conv_ch), F32),
        ],
        compiler_params=pltpu.CompilerParams(dimension_semantics=("arbitrary",),
                                             vmem_limit_bytes=VMEM_LIMIT_BYTES),
        name="sample_mixer",
    )(x.reshape(dec_b * dec_seq, d), ck, cv, st, win_b, *tabs, sinks, conv_w, g_attn, g_conv, wout_b,
      ln_g, ln_b)


def _over_experts(fn, x):
    return fn(fn(x, axis=0, keepdims=True), axis=1, keepdims=True)


def _two_group_specs(tm, d, n_prompt_tiles):
    return [pl.BlockSpec((tm, d), lambda i, *_: (jnp.minimum(i, n_prompt_tiles - 1), 0)),
            pl.BlockSpec((tm, d), lambda i, *_: (jnp.maximum(i - n_prompt_tiles, 0), 0))]


def _route_kernel(hp_ref, hs_ref, rwt_ref, bias_ref, eidx_ref, rank_ref, gate_ref, cnt_ref, *, n_experts,
                  n_prompt_tiles):
    i = pl.program_id(0)
    tm = hp_ref.shape[0]
    per_group = n_experts // N_EXPERT_GROUPS
    shape3 = (N_EXPERT_GROUPS, per_group, tm)

    h = jnp.where(i < n_prompt_tiles, hp_ref[...], hs_ref[...])
    logits = lax.dot_general(rwt_ref[...], h.astype(BF16), (((1,), (1,)), ((), ())),
                             preferred_element_type=F32)
    scores = jax.nn.sigmoid(logits)
    sel = scores + bias_ref[...]
    scores3 = scores.reshape(shape3)
    grp = sel.reshape(shape3)
    member = lax.broadcasted_iota(jnp.int32, shape3, 1).astype(F32)
    group = lax.broadcasted_iota(jnp.int32, shape3, 0).astype(F32)
    expert = group * per_group + member

    m1 = jnp.max(grp, axis=1, keepdims=True)
    f1 = jnp.min(jnp.where(grp == m1, member, float(per_group)), axis=1, keepdims=True)
    m2 = jnp.max(jnp.where(member == f1, NEG_INF, grp), axis=1, keepdims=True)
    gscore = m1 + m2

    gid = lax.broadcasted_iota(jnp.int32, gscore.shape, 0).astype(F32)
    gmask = jnp.zeros(gscore.shape, F32)
    cur = gscore
    for _ in range(TOPK_GROUPS):
        mx = jnp.max(cur, axis=0, keepdims=True)
        pick = gid == jnp.min(jnp.where(cur == mx, gid, float(N_EXPERT_GROUPS)), axis=0, keepdims=True)
        gmask = jnp.where(pick, 1.0, gmask)
        cur = jnp.where(pick, NEG_INF, cur)

    cand = jnp.where(gmask > 0.0, grp, NEG_INF)
    chosen = jnp.zeros(shape3, F32)
    picks, firsts, weights = [], [], []
    for _ in range(TOP_K):
        mx = _over_experts(jnp.max, cand)
        first = _over_experts(jnp.min, jnp.where(cand == mx, expert, float(n_experts)))
        pick = expert == first
        picks.append(pick)
        firsts.append(first)
        weights.append(_over_experts(jnp.sum, jnp.where(pick, scores3, 0.0)))
        chosen = jnp.where(pick, 1.0, chosen)
        cand = jnp.where(pick, NEG_INF, cand)
    wsum = weights[0]
    for w in weights[1:]:
        wsum = wsum + w

    chosen_b = chosen.reshape(n_experts, tm).astype(BF16)
    earlier = (lax.broadcasted_iota(jnp.int32, (tm, tm), 0)
               < lax.broadcasted_iota(jnp.int32, (tm, tm), 1)).astype(BF16)
    before3 = jnp.dot(chosen_b, earlier, preferred_element_type=F32).reshape(shape3)

    pad = SUBLANES - TOP_K
    eidx = [f.reshape(1, tm).astype(jnp.int32) for f in firsts]
    rank = [_over_experts(jnp.sum, jnp.where(p, before3, 0.0)).reshape(1, tm).astype(jnp.int32) for p in picks]
    gate = [(w / wsum * ROUTED_SCALE).reshape(1, tm) for w in weights]
    eidx_ref[...] = jnp.concatenate(eidx + [jnp.zeros((pad, tm), jnp.int32)], axis=0)
    rank_ref[...] = jnp.concatenate(rank + [jnp.zeros((pad, tm), jnp.int32)], axis=0)
    gate_ref[...] = jnp.concatenate(gate + [jnp.zeros((pad, tm), F32)], axis=0)

    cnt_ref[...] = lax.dot_general(jnp.ones((SUBLANES, tm), BF16), chosen_b, (((1,), (1,)), ((), ())),
                                   preferred_element_type=F32)


def _route(h_p, h_s, rwt_b, bias_col):
    d = h_p.shape[1]
    t_all = h_p.shape[0] + h_s.shape[0]
    n_experts = rwt_b.shape[0]
    tm = MOE_TILE
    nt = t_all // tm
    npt = h_p.shape[0] // tm
    row_spec = pl.BlockSpec((SUBLANES, tm), lambda i: (0, i))
    return pl.pallas_call(
        functools.partial(_route_kernel, n_experts=n_experts, n_prompt_tiles=npt),
        grid=(nt,),
        in_specs=_two_group_specs(tm, d, npt) + [
            pl.BlockSpec((n_experts, d), lambda i: (0, 0)),
            pl.BlockSpec((n_experts, 1), lambda i: (0, 0)),
        ],
        out_specs=[row_spec, row_spec, row_spec, pl.BlockSpec((SUBLANES, n_experts), lambda i: (i, 0))],
        out_shape=[
            jax.ShapeDtypeStruct((SUBLANES, t_all), jnp.int32),
            jax.ShapeDtypeStruct((SUBLANES, t_all), jnp.int32),
            jax.ShapeDtypeStruct((SUBLANES, t_all), F32),
            jax.ShapeDtypeStruct((nt * SUBLANES, n_experts), F32),
        ],
        compiler_params=pltpu.CompilerParams(dimension_semantics=("arbitrary",),
                                             vmem_limit_bytes=VMEM_LIMIT_BYTES),
        name="route",
    )(h_p, h_s, rwt_b, bias_col)


def _for_each_part(rows, max_rows, fn):
    for size in [SUBLANES << b for b in range((max_rows // SUBLANES).bit_length())]:
        @pl.when((rows & size) != 0)
        def _(size=size):
            fn(pl.multiple_of(rows & (size - 1), SUBLANES), size)


def _start_run_copies(tile, cnt_ref, ls_ref, gs_ref, big_ref, n_experts, start_copy):
    def run(e):
        idx = tile * n_experts + e
        return cnt_ref[idx], ls_ref[idx], gs_ref[idx]

    for e in range(n_experts):
        rows, l0, g0 = run(e)
        _for_each_part(rows, RUN_CHUNK // 2, lambda off, size: start_copy(
            pl.multiple_of(l0 + off, SUBLANES), pl.multiple_of(g0 + off, SUBLANES), size))

    @pl.when(big_ref[tile] != 0)
    def _():
        def per_expert(e, c):
            rows, l0, g0 = run(e)
            base = rows & (RUN_CHUNK - 1)

            def chunk(j, c2):
                off = base + j * RUN_CHUNK
                start_copy(pl.multiple_of(l0 + off, SUBLANES), pl.multiple_of(g0 + off, SUBLANES), RUN_CHUNK)
                return c2

            lax.fori_loop(0, lax.shift_right_logical(rows, RUN_CHUNK.bit_length() - 1), chunk, 0)
            return c

        lax.fori_loop(0, n_experts, per_expert, 0)


def _top_bit(n):
    return 1 << (n.bit_length() - 1)


def _pack_rows(x):
    rows, two_w = x.shape
    w = two_w // 2
    x3 = x.reshape(rows // SUBLANES, SUBLANES, two_w)
    halves = jnp.concatenate([x3[:, :, :w], x3[:, :, w:]], axis=1).astype(BF16)
    return pltpu.bitcast(halves, jnp.uint32).reshape(rows, w)


def _unpack_rows(p):
    rows, w = p.shape
    halves = pltpu.bitcast(p.reshape(rows // SUBLANES, SUBLANES, w), BF16).astype(F32)
    return jnp.concatenate([halves[:, :SUBLANES, :], halves[:, SUBLANES:, :]], axis=-1).reshape(rows, 2 * w)


def _local_rows(tm, n_experts):
    n = TOP_K * tm + n_experts * (SUBLANES - 1)
    return -(-n // SORT_CHUNK) * SORT_CHUNK


def _dispatch_kernel(cnt_ref, ls_ref, gs_ref, tot_ref, big_ref, zrow_ref, lp_ref, hp_ref, hs_ref, xs_hbm,
                     lbuf, zero_s, sem, *, n_experts, blk, n_prompt_tiles):
    i = pl.program_id(0)
    tm, d = hp_ref.shape
    n = TOP_K * tm
    nl = lbuf.shape[1]
    slot = i % 2

    def tile_wait(tile):
        pltpu.make_async_copy(lbuf.at[0, pl.ds(0, n)], xs_hbm.at[pl.ds(0, n)], sem).wait()
        _for_each_part(tot_ref[tile] - n, _top_bit(n_experts * (SUBLANES - 1)), lambda off, size:
                       pltpu.make_async_copy(lbuf.at[0, pl.ds(0, size)], xs_hbm.at[pl.ds(0, size)], sem).wait())

    @pl.when(i == 0)
    def _():
        zero_s[...] = jnp.zeros(zero_s.shape, zero_s.dtype)

        def zcopy(e):
            return pltpu.make_async_copy(zero_s, xs_hbm.at[pl.ds(pl.multiple_of(zrow_ref[e], blk), blk)], sem)

        def start(e, c):
            @pl.when(zrow_ref[e] >= 0)
            def _():
                zcopy(e).start()
            return c

        def wait(e, c):
            @pl.when(zrow_ref[e] >= 0)
            def _():
                zcopy(e).wait()
            return c

        lax.fori_loop(0, n_experts, start, 0)
        lax.fori_loop(0, n_experts, wait, 0)

    hb = jnp.where(i < n_prompt_tiles, hp_ref[...], hs_ref[...]).astype(BF16)
    lp = lp_ref[...]
    def sort_chunk(r):
        rows = r * SORT_CHUNK + lax.broadcasted_iota(jnp.int32, (SORT_CHUNK, tm), 0)
        hit = jnp.zeros((SORT_CHUNK, tm), F32)
        for k in range(TOP_K):
            hit = jnp.where(rows == lp[k:k + 1, :], 1.0, hit)
        lbuf[slot, pl.ds(r * SORT_CHUNK, SORT_CHUNK), :] = _pack_rows(
            jnp.dot(hit.astype(BF16), hb, preferred_element_type=F32))

    for r in range(nl // SORT_CHUNK):
        if (r + 1) * SORT_CHUNK <= n:
            sort_chunk(r)
        else:
            pl.when(tot_ref[i] > r * SORT_CHUNK)(functools.partial(sort_chunk, r))

    @pl.when(i > 0)
    def _():
        tile_wait(i - 1)

    _start_run_copies(i, cnt_ref, ls_ref, gs_ref, big_ref, n_experts, lambda lrow, grow, size: pltpu.make_async_copy(
        lbuf.at[slot, pl.ds(lrow, size)], xs_hbm.at[pl.ds(grow, size)], sem).start())

    @pl.when(i == pl.num_programs(0) - 1)
    def _():
        tile_wait(i)


def _dispatch(tables, zrow, lp, h_p, h_s, *, n_rows, blk):
    d = h_p.shape[1]
    t_all = h_p.shape[0] + h_s.shape[0]
    tm = MOE_TILE
    npt = h_p.shape[0] // tm
    n_experts = zrow.shape[0]
    return pl.pallas_call(
        functools.partial(_dispatch_kernel, n_experts=n_experts, blk=blk, n_prompt_tiles=npt),
        grid_spec=pltpu.PrefetchScalarGridSpec(
            num_scalar_prefetch=6,
            grid=(t_all // tm,),
            in_specs=[pl.BlockSpec((SUBLANES, tm), lambda i, *_: (0, i))] + _two_group_specs(tm, d, npt),
            out_specs=pl.BlockSpec(memory_space=pl.ANY),
            scratch_shapes=[
                pltpu.VMEM((2, _local_rows(tm, n_experts), d // 2), jnp.uint32),
                pltpu.VMEM((blk, d // 2), jnp.uint32),
                pltpu.SemaphoreType.DMA,
            ],
        ),
        out_shape=jax.ShapeDtypeStruct((n_rows, d // 2), jnp.uint32),
        compiler_params=pltpu.CompilerParams(dimension_semantics=("arbitrary",),
                                             vmem_limit_bytes=VMEM_LIMIT_BYTES),
        name="dispatch",
    )(*tables, zrow, lp, h_p, h_s)


def _silu(x):
    return x * jax.nn.sigmoid(x)


def _expert_kernel(be_ref, nact_ref, xs_ref, wg_ref, wu_ref, wd_ref, ys_ref, wg_s, wu_s, wd_s):
    b = pl.program_id(0)

    @pl.when(b < nact_ref[0])
    def _():
        @pl.when((b == 0) | (be_ref[b] != be_ref[jnp.maximum(b - 1, 0)]))
        def _():
            wg_s[...] = wg_ref[...].astype(BF16)
            wu_s[...] = wu_ref[...].astype(BF16)
            wd_s[...] = wd_ref[...].astype(BF16)

        xb = _unpack_rows(xs_ref[...]).astype(BF16)
        hid = _silu(jnp.dot(xb, wg_s[...], preferred_element_type=F32)) * jnp.dot(
            xb, wu_s[...], preferred_element_type=F32)
        ys_ref[...] = _pack_rows(jnp.dot(hid.astype(BF16), wd_s[...], preferred_element_type=F32))


def _experts(block_expert, nact, xs, w_gate, w_up, w_down, *, blk):
    n_rows, dw = xs.shape
    d = w_gate.shape[-2]
    ff = w_gate.shape[-1]
    n_blocks = n_rows // blk

    def active(b, be, na):
        return jnp.minimum(b, na[0] - 1)

    return pl.pallas_call(
        _expert_kernel,
        grid_spec=pltpu.PrefetchScalarGridSpec(
            num_scalar_prefetch=2,
            grid=(n_blocks,),
            in_specs=[
                pl.BlockSpec((blk, dw), lambda b, be, na: (active(b, be, na), 0)),
                pl.BlockSpec((None, d, ff), lambda b, be, na: (be[active(b, be, na)], 0, 0)),
                pl.BlockSpec((None, d, ff), lambda b, be, na: (be[active(b, be, na)], 0, 0)),
                pl.BlockSpec((None, ff, d), lambda b, be, na: (be[active(b, be, na)], 0, 0)),
            ],
            out_specs=pl.BlockSpec((blk, dw), lambda b, be, na: (active(b, be, na), 0)),
            scratch_shapes=[
                pltpu.VMEM((d, ff), BF16),
                pltpu.VMEM((d, ff), BF16),
                pltpu.VMEM((ff, d), BF16),
            ],
        ),
        out_shape=jax.ShapeDtypeStruct((n_rows, dw), jnp.uint32),
        compiler_params=pltpu.CompilerParams(dimension_semantics=("arbitrary",),
                                             vmem_limit_bytes=VMEM_LIMIT_BYTES),
        name="experts",
    )(block_expert, nact, xs, w_gate, w_up, w_down)


def _combine_kernel(cnt_ref, ls_ref, gs_ref, tot_ref, big_ref, ys_hbm, lpt_ref, gate_ref, hp_ref, hs_ref, wsg_ref, wsu_ref,
                    wsd_ref, lng_ref, lnb_ref, yp_ref, ysm_ref, ybuf, moe_s, sems, *, alpha, n_experts,
                    n_prompt_tiles):
    i = pl.program_id(0)
    tm, d = hp_ref.shape
    n = TOP_K * tm
    nl = ybuf.shape[1]
    slot = i % 2

    def fetch(tile, to):
        _start_run_copies(tile, cnt_ref, ls_ref, gs_ref, big_ref, n_experts, lambda lrow, grow, size:
                          pltpu.make_async_copy(ys_hbm.at[pl.ds(grow, size)], ybuf.at[to, pl.ds(lrow, size)],
                                                sems.at[to]).start())

    @pl.when(i == 0)
    def _():
        ybuf[...] = jnp.zeros(ybuf.shape, ybuf.dtype)
        fetch(0, 0)

    @pl.when(i + 1 < pl.num_programs(0))
    def _():
        fetch(i + 1, 1 - slot)

    h = jnp.where(i < n_prompt_tiles, hp_ref[...], hs_ref[...])
    hb = h.astype(BF16)
    hid = _silu(jnp.dot(hb, wsg_ref[...], preferred_element_type=F32)) * jnp.dot(
        hb, wsu_ref[...], preferred_element_type=F32)
    shared = jnp.dot(hid.astype(BF16), wsd_ref[...], preferred_element_type=F32)

    pltpu.make_async_copy(ys_hbm.at[pl.ds(0, n)], ybuf.at[slot, pl.ds(0, n)], sems.at[slot]).wait()
    _for_each_part(tot_ref[i] - n, _top_bit(n_experts * (SUBLANES - 1)), lambda off, size: pltpu.make_async_copy(
        ys_hbm.at[pl.ds(0, size)], ybuf.at[slot, pl.ds(0, size)], sems.at[slot]).wait())

    lpt = lpt_ref[...]
    gate = gate_ref[...]
    moe_s[...] = shared

    def add_chunk(r):
        cols = r * SORT_CHUNK + lax.broadcasted_iota(jnp.int32, (tm, SORT_CHUNK), 1)
        g = jnp.zeros((tm, SORT_CHUNK), F32)
        for k in range(TOP_K):
            g = jnp.where(cols == lpt[:, k:k + 1], gate[:, k:k + 1], g)
        yb = _unpack_rows(ybuf[slot, pl.ds(r * SORT_CHUNK, SORT_CHUNK), :]).astype(BF16)
        moe_s[...] += jnp.dot(g.astype(BF16), yb, preferred_element_type=F32)

    for r in range(nl // SORT_CHUNK):
        if (r + 1) * SORT_CHUNK <= n:
            add_chunk(r)
        else:
            pl.when(tot_ref[i] > r * SORT_CHUNK)(functools.partial(add_chunk, r))
    y = _layer_norm(alpha * h + moe_s[...], lng_ref[...], lnb_ref[...])

    @pl.when(i < n_prompt_tiles)
    def _():
        yp_ref[...] = y

    @pl.when(i >= n_prompt_tiles)
    def _():
        ysm_ref[...] = y


def _combine(tables, ys, lp_t, gates_t, h_p, h_s, wsg_b, wsu_b, wsd_b, ln_g, ln_b, *, alpha):
    t_prompt, d = h_p.shape
    t_all = t_prompt + h_s.shape[0]
    tm = MOE_TILE
    ff = wsg_b.shape[-1]
    npt = t_prompt // tm
    n_experts = tables[0].shape[0] // (t_all // tm)
    const2 = lambda i, *_: (0, 0)
    return pl.pallas_call(
        functools.partial(_combine_kernel, alpha=alpha, n_experts=n_experts, n_prompt_tiles=npt),
        grid_spec=pltpu.PrefetchScalarGridSpec(
            num_scalar_prefetch=5,
            grid=(t_all // tm,),
            in_specs=[
                pl.BlockSpec(memory_space=pl.ANY),
                pl.BlockSpec((tm, SUBLANES), lambda i, *_: (i, 0)),
                pl.BlockSpec((tm, SUBLANES), lambda i, *_: (i, 0)),
                *_two_group_specs(tm, d, npt),
                pl.BlockSpec((d, ff), const2),
                pl.BlockSpec((d, ff), const2),
                pl.BlockSpec((ff, d), const2),
                pl.BlockSpec((1, d), const2),
                pl.BlockSpec((1, d), const2),
            ],
            out_specs=[
                pl.BlockSpec((tm, d), lambda i, *_: (jnp.minimum(i, npt - 1), 0)),
                pl.BlockSpec((tm, d), lambda i, *_: (jnp.maximum(i - npt, 0), 0)),
            ],
            scratch_shapes=[
                pltpu.VMEM((2, _local_rows(tm, n_experts), d // 2), jnp.uint32),
                pltpu.VMEM((tm, d), F32),
                pltpu.SemaphoreType.DMA((2,)),
            ],
        ),
        out_shape=[
            jax.ShapeDtypeStruct((t_prompt, d), F32),
            jax.ShapeDtypeStruct((t_all - t_prompt, d), F32),
        ],
        compiler_params=pltpu.CompilerParams(dimension_semantics=("arbitrary",),
                                             vmem_limit_bytes=VMEM_LIMIT_BYTES),
        name="combine",
    )(*tables, ys, lp_t, gates_t, h_p, h_s, wsg_b, wsu_b, wsd_b, ln_g, ln_b)


def _moe(h_p, h_s, router_w, router_bias, w_gate, w_up, w_down, ws_gate, ws_up, ws_down, ln_g, ln_b, *, alpha):
    t_all = h_p.shape[0] + h_s.shape[0]
    n_experts = router_w.shape[-1]
    blk = EXPERT_BLOCK
    tm = MOE_TILE
    nt = t_all // tm
    eidx8, lrank8, gate8, cnt8 = _route(h_p, h_s, router_w.T.astype(BF16), router_bias.reshape(n_experts, 1))

    cnt = cnt8.reshape(nt, SUBLANES, n_experts)[:, 0, :].astype(jnp.int32)
    cnt = (cnt + SUBLANES - 1) // SUBLANES * SUBLANES
    counts = jnp.sum(cnt, axis=0)
    blocks_per_e = (counts + blk - 1) // blk
    block_end = jnp.cumsum(blocks_per_e)
    pad_start = (block_end - blocks_per_e) * blk
    n_blocks = -(-(t_all * TOP_K + nt * n_experts * (SUBLANES - 1)) // blk) + n_experts
    block_expert = jnp.minimum(jnp.sum(block_end[None, :] <= jnp.arange(n_blocks)[:, None], axis=1),
                               n_experts - 1).astype(jnp.int32)
    nact = block_end[-1:].astype(jnp.int32)
    zrow = jnp.where(blocks_per_e > 0, (block_end - 1) * blk, -1).astype(jnp.int32)
    gstart = pad_start[None, :] + jnp.cumsum(cnt, axis=0) - cnt
    lstart = jnp.cumsum(cnt, axis=1) - cnt
    tables = tuple(a.reshape(-1).astype(jnp.int32) for a in (
        cnt, lstart, gstart, jnp.sum(cnt, axis=1), jnp.max(cnt, axis=1) >= RUN_CHUNK))
    lstart_tok = jnp.repeat(lstart, tm, axis=0)
    lp8 = jnp.sum(jnp.where(eidx8[..., None] == jnp.arange(n_experts), lstart_tok[None], 0), axis=-1) + lrank8
    lp8 = lp8.astype(jnp.int32)

    xs = _dispatch(tables, zrow, lp8, h_p, h_s, n_rows=n_blocks * blk, blk=blk)
    ys = _experts(block_expert, nact, xs, w_gate, w_up, w_down, blk=blk)
    return _combine(tables, ys, lp8.T, gate8.T, h_p, h_s, ws_gate.astype(BF16), ws_up.astype(BF16), ws_down.astype(BF16),
                    ln_g, ln_b, alpha=alpha)


def kernel(x_prompt, x_sample, cache_k, cache_v, state_conv, w_in, attn_sinks, conv_w, g_attn_out, g_conv_out, w_out, ln1_g, ln1_b, router_w, router_bias, w_gate, w_up, w_down, ws_gate, ws_up, ws_down, ln2_g, ln2_b):
    depth = w_in.shape[0]
    bsz, seq, d = x_prompt.shape
    dec_b, dec_seq, _ = x_sample.shape
    win = cache_k.shape[2]
    kv_w = N_KV_HEADS * HEAD_DIM
    t_prompt = bsz * seq
    t_all = t_prompt + dec_b * dec_seq
    alpha = (2.0 * depth) ** 0.25
    assert win == WINDOW and seq % PROMPT_TILE == 0 and dec_b % SAMPLE_SEQS == 0
    assert t_prompt % MOE_TILE == 0 and (t_all - t_prompt) % MOE_TILE == 0
    assert MOE_TILE & (MOE_TILE - 1) == 0 and (TOP_K * MOE_TILE) % SORT_CHUNK == 0

    tabs_p = _rope_tables(jnp.arange(seq))
    tabs_s = tuple(jnp.tile(t, (SAMPLE_SEQS, 1)) for t in _rope_tables(PAST_LEN + jnp.arange(dec_seq)))
    row = lambda a: a.reshape(1, -1)

    xp, xs = x_prompt, x_sample
    outs = [[] for _ in range(6)]
    for l in range(depth):
        win_b, wout_b = w_in[l].astype(BF16), w_out[l].astype(BF16)
        shared = (attn_sinks[l], conv_w[l], row(g_attn_out[l]), row(g_conv_out[l]), wout_b, row(ln1_g[l]),
                  row(ln1_b[l]))
        h_p, kp, vp, cp = _prompt_mixer(xp, win_b, tabs_p, *shared, alpha=alpha)
        h_s, kn, vn, cn = _sample_mixer(xs, cache_k[l].reshape(dec_b, win, kv_w),
                                        cache_v[l].reshape(dec_b, win, kv_w), state_conv[l], win_b, tabs_s,
                                        *shared, alpha=alpha)
        yp, ys = _moe(h_p, h_s, router_w[l], router_bias[l], w_gate[l], w_up[l], w_down[l], ws_gate[l], ws_up[l],
                      ws_down[l], row(ln2_g[l]), row(ln2_b[l]), alpha=alpha)
        xp, xs = yp.reshape(bsz, seq, d), ys.reshape(dec_b, dec_seq, d)
        heads = lambda a: a.reshape(a.shape[0], win, N_KV_HEADS, HEAD_DIM)
        for o, a in zip(outs, (heads(kp), heads(vp), cp, heads(kn), heads(vn), cn)):
            o.append(a)
    return (xp, xs) + tuple(jnp.stack(o, axis=0) for o in outs)
```

```python
import functools

import jax
import jax.numpy as jnp
from jax import lax
from jax.experimental import pallas as pl
from jax.experimental.pallas import tpu as pltpu

PAST_LEN = 16384
WINDOW = 128
HEAD_DIM = 64
N_KV_HEADS = 2
ROT_DIM = HEAD_DIM // 4
ROPE_THETA = 500000.0
CONV_K = 3
TOP_K = 6
N_EXPERT_GROUPS = 8
TOPK_GROUPS = 4
ROUTED_SCALE = 2.5
LN_EPS = 1e-5
RMS_EPS = 1e-6

LANES = 128
SUBLANES = 8
VMEM_LIMIT_BYTES = 56 * 1024 * 1024

PROMPT_TILE = 512
SAMPLE_SEQS = 16
MOE_TILE = 256
SORT_CHUNK = 256
EXPERT_BLOCK = 1024
ROUTE_TILE = 512
RUN_CHUNK = 64

F32 = jnp.float32
BF16 = jnp.bfloat16
NEG_INF = float("-inf")


def _rope_tables(positions):
    half = ROT_DIM // 2
    inv_freq = ROPE_THETA ** (-jnp.arange(0, ROT_DIM, 2, dtype=F32) / ROT_DIM)
    ang = positions.astype(F32)[:, None] * inv_freq[None, :]
    cos, sin = jnp.cos(ang), jnp.sin(ang)
    n = positions.shape[0]
    rest = HEAD_DIM - ROT_DIM
    c = jnp.concatenate([cos, cos, jnp.ones((n, rest), F32)], axis=-1)
    sa = jnp.concatenate([-sin, jnp.zeros((n, half + rest), F32)], axis=-1)
    sb = jnp.concatenate([jnp.zeros((n, half), F32), sin, jnp.zeros((n, rest), F32)], axis=-1)
    reps = LANES // HEAD_DIM
    return jnp.tile(c, (1, reps)), jnp.tile(sa, (1, reps)), jnp.tile(sb, (1, reps))


def _rope(x, c, sa, sb):
    half = ROT_DIM // 2
    return x * c + pltpu.roll(x, LANES - half, 1) * sa + pltpu.roll(x, half, 1) * sb


def _rms_norm(x, g):
    return x * lax.rsqrt(jnp.mean(jnp.square(x), axis=-1, keepdims=True) + RMS_EPS) * g


def _layer_norm(x, g, b):
    mu = jnp.mean(x, axis=-1, keepdims=True)
    var = jnp.mean(jnp.square(x - mu), axis=-1, keepdims=True)
    return (x - mu) * lax.rsqrt(var + LN_EPS) * g + b


def _short_conv(gated, prev2, prev1, row, conv_w, b_gate):
    g1 = pltpu.roll(gated, 1, 0)
    g2 = pltpu.roll(gated, 2, 0)
    g1 = jnp.where(row == 0, prev1, g1)
    g2 = jnp.where(row == 0, prev2, jnp.where(row == 1, prev1, g2))
    y = conv_w[0:1, :] * g2 + conv_w[1:2, :] * g1 + conv_w[2:3, :] * gated
    return b_gate * y


def _merge_norm(x, attn_o, conv_o, gattn, gconv, wout_ref, ln_g, ln_b, alpha):
    cat = jnp.concatenate([_rms_norm(attn_o, gattn), _rms_norm(conv_o, gconv)], axis=-1)
    mix = jnp.dot(cat.astype(BF16), wout_ref[...], preferred_element_type=F32)
    return _layer_norm(alpha * x + mix, ln_g, ln_b)


def _sink_softmax(parts, sink):
    m = sink
    for s in parts:
        m = jnp.maximum(m, jnp.max(s, axis=-1, keepdims=True))
    es = [jnp.exp(s - m) for s in parts]
    den = jnp.exp(sink - m)
    for e in es:
        den = den + jnp.sum(e, axis=-1, keepdims=True)
    return [(e / den).astype(BF16) for e in es]


def _prompt_mixer_kernel(x_ref, win_ref, c_ref, sa_ref, sb_ref, sinks_ref, convw_ref, gattn_ref, gconv_ref,
                         wout_ref, lng_ref, lnb_ref,
                         h_ref, ko_ref, vo_ref, co_ref,
                         q_s, kt_s, vm_s, o_s, gc_s, *, alpha, n_heads, attn_w, kv_w, conv_ch):
    s = pl.program_id(1)
    last = pl.num_programs(1) - 1
    tq = x_ref.shape[0]
    q_per_kv = n_heads // N_KV_HEADS
    heads_per_group = LANES // HEAD_DIM

    @pl.when(s == 0)
    def _():
        kt_s[:, :, 0:WINDOW] = jnp.zeros((kt_s.shape[0], LANES, WINDOW), BF16)
        vm_s[:, 0:WINDOW, :] = jnp.zeros((vm_s.shape[0], WINDOW, LANES), BF16)
        gc_s[...] = jnp.zeros(gc_s.shape, F32)

    @pl.when(s > 0)
    def _():
        kt_s[:, :, 0:WINDOW] = kt_s[:, :, tq:tq + WINDOW]
        vm_s[:, 0:WINDOW, :] = vm_s[:, tq:tq + WINDOW, :]

    x = x_ref[...]
    xb = x.astype(BF16)
    c, sa, sb = c_ref[...], sa_ref[...], sb_ref[...]

    def proj(lo, width):
        return jnp.dot(xb, win_ref[:, lo:lo + width], preferred_element_type=F32)

    scale = HEAD_DIM ** -0.5
    for j in range(attn_w // LANES):
        qj = _rope(proj(j * LANES, LANES), c, sa, sb)
        q_s[:, j * LANES:(j + 1) * LANES] = (qj * scale).astype(BF16)
    k = _rope(proj(attn_w, kv_w), c, sa, sb)
    v = proj(attn_w + kv_w, kv_w)

    @pl.when(s == last)
    def _():
        ko_ref[...] = k[tq - WINDOW:, :]
        vo_ref[...] = v[tq - WINDOW:, :]

    kt = k.T.astype(BF16)
    zeros_k = jnp.zeros((HEAD_DIM, tq), BF16)
    v_swapped = pltpu.roll(v, HEAD_DIM, 1)
    low_lanes = lax.broadcasted_iota(jnp.int32, (tq, LANES), 1) < HEAD_DIM
    for kvh in range(N_KV_HEADS):
        kt_h = kt[kvh * HEAD_DIM:(kvh + 1) * HEAD_DIM, :]
        v_lo = v if kvh == 0 else v_swapped
        v_hi = v_swapped if kvh == 0 else v
        kt_s[2 * kvh, :, WINDOW:] = jnp.concatenate([kt_h, zeros_k], axis=0)
        kt_s[2 * kvh + 1, :, WINDOW:] = jnp.concatenate([zeros_k, kt_h], axis=0)
        vm_s[2 * kvh, WINDOW:, :] = jnp.where(low_lanes, v_lo, 0.0).astype(BF16)
        vm_s[2 * kvh + 1, WINDOW:, :] = jnp.where(low_lanes, 0.0, v_hi).astype(BF16)

    qi = lax.broadcasted_iota(jnp.int32, (WINDOW, 2 * WINDOW), 0)
    ci = lax.broadcasted_iota(jnp.int32, (WINDOW, 2 * WINDOW), 1)
    band = (ci > qi) & (ci <= qi + WINDOW)
    for j in range(tq // WINDOW):
        r0 = j * WINDOW
        mask = band if j > 0 else band & ((ci >= WINDOW) | (s > 0))
        for grp in range(attn_w // LANES):
            q_grp = q_s[r0:r0 + WINDOW, grp * LANES:(grp + 1) * LANES]
            out = None
            for r in range(heads_per_group):
                hd = grp * heads_per_group + r
                src = 2 * (hd // q_per_kv) + r
                sink = sinks_ref[hd]
                sc = jnp.dot(q_grp, kt_s[src, :, r0:r0 + 2 * WINDOW], preferred_element_type=F32)
                sc = jnp.where(mask, sc, NEG_INF)
                m = jnp.maximum(jnp.max(sc, axis=-1, keepdims=True), sink)
                e = jnp.exp(sc - m)
                den = jnp.sum(e, axis=-1, keepdims=True) + jnp.exp(sink - m)
                o_h = jnp.dot(e.astype(BF16), vm_s[src, r0:r0 + 2 * WINDOW, :],
                              preferred_element_type=F32) * (1.0 / den)
                out = o_h if out is None else out + o_h
            o_s[r0:r0 + WINDOW, grp * LANES:(grp + 1) * LANES] = out

    o3 = attn_w + 2 * kv_w
    gated = proj(o3 + 2 * conv_ch, conv_ch) * proj(o3, conv_ch)
    row = lax.broadcasted_iota(jnp.int32, (tq, 1), 0)
    conv_o = _short_conv(gated, gc_s[0:1, :], gc_s[1:2, :], row, convw_ref[...], proj(o3 + conv_ch, conv_ch))
    gc_s[0:CONV_K - 1, :] = gated[tq - (CONV_K - 1):, :]

    @pl.when(s == last)
    def _():
        co_ref[...] = gated[tq - (CONV_K - 1):, :]

    h_ref[...] = _merge_norm(x, o_s[...], conv_o, gattn_ref[...], gconv_ref[...], wout_ref,
                             lng_ref[...], lnb_ref[...], alpha)


def _prompt_mixer(x, win_b, tabs, sinks, conv_w, g_attn, g_conv, wout_b, ln_g, ln_b, *, alpha):
    bsz, seq, d = x.shape
    tq = PROMPT_TILE
    ns = seq // tq
    attn_w = g_attn.shape[-1]
    conv_ch = g_conv.shape[-1]
    n_heads = attn_w // HEAD_DIM
    kv_w = N_KV_HEADS * HEAD_DIM
    assert kv_w == LANES and 2 * HEAD_DIM == LANES and (n_heads // N_KV_HEADS) % 2 == 0 and WINDOW == LANES
    in_cols = win_b.shape[-1]
    const2 = lambda b, s: (0, 0)
    kern = functools.partial(_prompt_mixer_kernel, alpha=alpha, n_heads=n_heads, attn_w=attn_w, kv_w=kv_w,
                             conv_ch=conv_ch)
    return pl.pallas_call(
        kern,
        grid=(bsz, ns),
        in_specs=[
            pl.BlockSpec((None, tq, d), lambda b, s: (b, s, 0)),
            pl.BlockSpec((d, in_cols), const2),
            pl.BlockSpec((tq, LANES), lambda b, s: (s, 0)),
            pl.BlockSpec((tq, LANES), lambda b, s: (s, 0)),
            pl.BlockSpec((tq, LANES), lambda b, s: (s, 0)),
            pl.BlockSpec(memory_space=pltpu.SMEM),
            pl.BlockSpec((CONV_K, conv_ch), const2),
            pl.BlockSpec((1, attn_w), const2),
            pl.BlockSpec((1, conv_ch), const2),
            pl.BlockSpec((attn_w + conv_ch, d), const2),
            pl.BlockSpec((1, d), const2),
            pl.BlockSpec((1, d), const2),
        ],
        out_specs=[
            pl.BlockSpec((tq, d), lambda b, s: (b * ns + s, 0)),
            pl.BlockSpec((None, WINDOW, kv_w), lambda b, s: (b, 0, 0)),
            pl.BlockSpec((None, WINDOW, kv_w), lambda b, s: (b, 0, 0)),
            pl.BlockSpec((None, CONV_K - 1, conv_ch), lambda b, s: (b, 0, 0)),
        ],
        out_shape=[
            jax.ShapeDtypeStruct((bsz * seq, d), F32),
            jax.ShapeDtypeStruct((bsz, WINDOW, kv_w), F32),
            jax.ShapeDtypeStruct((bsz, WINDOW, kv_w), F32),
            jax.ShapeDtypeStruct((bsz, CONV_K - 1, conv_ch), F32),
        ],
        scratch_shapes=[
            pltpu.VMEM((tq, attn_w), BF16),
            pltpu.VMEM((2 * N_KV_HEADS, LANES, WINDOW + tq), BF16),
            pltpu.VMEM((2 * N_KV_HEADS, WINDOW + tq, LANES), BF16),
            pltpu.VMEM((tq, attn_w), F32),
            pltpu.VMEM((SUBLANES, conv_ch), F32),
        ],
        compiler_params=pltpu.CompilerParams(dimension_semantics=("arbitrary", "arbitrary"),
                                             vmem_limit_bytes=VMEM_LIMIT_BYTES),
        name="prompt_mixer",
    )(x, win_b, *tabs, sinks, conv_w, g_attn, g_conv, wout_b, ln_g, ln_b)


def _sample_mixer_kernel(x_ref, ck_ref, cv_ref, st_ref, win_ref, c_ref, sa_ref, sb_ref, sinks_ref,
                         convw_ref, gattn_ref, gconv_ref, wout_ref, lng_ref, lnb_ref,
                         h_ref, ko_ref, vo_ref, co_ref, *, alpha, n_heads, attn_w, kv_w, conv_ch, dec_seq):
    nb, win = ck_ref.shape[0], ck_ref.shape[1]
    rows = nb * dec_seq
    q_per_kv = n_heads // N_KV_HEADS
    x = x_ref[...]
    xb = x.astype(BF16)
    c, sa, sb = c_ref[...], sa_ref[...], sb_ref[...]

    def proj(lo, width):
        return jnp.dot(xb, win_ref[:, lo:lo + width], preferred_element_type=F32)

    scale = HEAD_DIM ** -0.5
    k = _rope(proj(attn_w, kv_w), c, sa, sb)
    v = proj(attn_w + kv_w, kv_w)
    k3 = k.reshape(nb, dec_seq, kv_w)
    v3 = v.reshape(nb, dec_seq, kv_w)
    ck = ck_ref[...]
    cv = cv_ref[...]
    ko_ref[:, 0:win - dec_seq, :] = ck[:, dec_seq:, :]
    ko_ref[:, win - dec_seq:, :] = k3
    vo_ref[:, 0:win - dec_seq, :] = cv[:, dec_seq:, :]
    vo_ref[:, win - dec_seq:, :] = v3
    ckb, cvb, k3b, v3b = ck.astype(BF16), cv.astype(BF16), k3.astype(BF16), v3.astype(BF16)

    qrows = q_per_kv * dec_seq
    qi = lax.broadcasted_iota(jnp.int32, (nb, qrows, win), 1) % dec_seq
    mask_c = lax.broadcasted_iota(jnp.int32, (nb, qrows, win), 2) > qi + (win - WINDOW)
    qn = lax.broadcasted_iota(jnp.int32, (nb, qrows, dec_seq), 1) % dec_seq
    mask_n = lax.broadcasted_iota(jnp.int32, (nb, qrows, dec_seq), 2) <= qn
    sink_row = lax.broadcasted_iota(jnp.int32, (nb, qrows, 1), 1) // dec_seq

    q_chunks = [_rope(proj(j * LANES, LANES), c, sa, sb) * scale for j in range(attn_w // LANES)]
    heads_out = []
    for kvh in range(N_KV_HEADS):
        qs = []
        for g in range(q_per_kv):
            lo = (kvh * q_per_kv + g) * HEAD_DIM
            qh = q_chunks[lo // LANES][:, lo % LANES:lo % LANES + HEAD_DIM]
            qs.append(qh.reshape(nb, dec_seq, HEAD_DIM))
        qg = jnp.concatenate(qs, axis=1).astype(BF16)
        sl = slice(kvh * HEAD_DIM, (kvh + 1) * HEAD_DIM)
        sc_c = jnp.einsum("bqd,bkd->bqk", qg, ckb[:, :, sl], preferred_element_type=F32)
        sc_n = jnp.einsum("bqd,bkd->bqk", qg, k3b[:, :, sl], preferred_element_type=F32)
        sc_c = jnp.where(mask_c, sc_c, NEG_INF)
        sc_n = jnp.where(mask_n, sc_n, NEG_INF)
        sink = jnp.zeros((nb, qrows, 1), F32)
        for g in range(q_per_kv):
            sink = jnp.where(sink_row == g, sinks_ref[kvh * q_per_kv + g], sink)
        p_c, p_n = _sink_softmax([sc_c, sc_n], sink)
        og = (jnp.einsum("bqk,bkd->bqd", p_c, cvb[:, :, sl], preferred_element_type=F32)
              + jnp.einsum("bqk,bkd->bqd", p_n, v3b[:, :, sl], preferred_element_type=F32))
        for g in range(q_per_kv):
            heads_out.append(og[:, g * dec_seq:(g + 1) * dec_seq, :].reshape(rows, HEAD_DIM))
    attn_o = jnp.concatenate(heads_out, axis=-1)

    o3 = attn_w + 2 * kv_w
    gated = proj(o3 + 2 * conv_ch, conv_ch) * proj(o3, conv_ch)
    st = st_ref[...]
    prev2 = jnp.broadcast_to(st[:, 0:1, :], (nb, dec_seq, conv_ch)).reshape(rows, conv_ch)
    prev1 = jnp.broadcast_to(st[:, 1:2, :], (nb, dec_seq, conv_ch)).reshape(rows, conv_ch)
    row = lax.broadcasted_iota(jnp.int32, (rows, 1), 0) % dec_seq
    conv_o = _short_conv(gated, prev2, prev1, row, convw_ref[...], proj(o3 + conv_ch, conv_ch))
    co_ref[...] = gated.reshape(nb, dec_seq, conv_ch)[:, dec_seq - (CONV_K - 1):, :]

    h_ref[...] = _merge_norm(x, attn_o, conv_o, gattn_ref[...], gconv_ref[...], wout_ref,
                             lng_ref[...], lnb_ref[...], alpha)


def _sample_mixer(x, ck, cv, st, win_b, tabs, sinks, conv_w, g_attn, g_conv, wout_b, ln_g, ln_b, *, alpha):
    dec_b, dec_seq, d = x.shape
    assert dec_seq >= CONV_K - 1 and dec_seq % SUBLANES == 0
    nb = SAMPLE_SEQS
    rows = nb * dec_seq
    win = ck.shape[1]
    attn_w = g_attn.shape[-1]
    conv_ch = g_conv.shape[-1]
    n_heads = attn_w // HEAD_DIM
    kv_w = N_KV_HEADS * HEAD_DIM
    in_cols = win_b.shape[-1]
    const2 = lambda i: (0, 0)
    kern = functools.partial(_sample_mixer_kernel, alpha=alpha, n_heads=n_heads, attn_w=attn_w, kv_w=kv_w,
                             conv_ch=conv_ch, dec_seq=dec_seq)
    return pl.pallas_call(
        kern,
        grid=(dec_b // nb,),
        in_specs=[
            pl.BlockSpec((rows, d), lambda i: (i, 0)),
            pl.BlockSpec((nb, win, kv_w), lambda i: (i, 0, 0)),
            pl.BlockSpec((nb, win, kv_w), lambda i: (i, 0, 0)),
            pl.BlockSpec((nb, CONV_K - 1, conv_ch), lambda i: (i, 0, 0)),
            pl.BlockSpec((d, in_cols), const2),
            pl.BlockSpec((rows, LANES), const2),
            pl.BlockSpec((rows, LANES), const2),
            pl.BlockSpec((rows, LANES), const2),
            pl.BlockSpec(memory_space=pltpu.SMEM),
            pl.BlockSpec((CONV_K, conv_ch), const2),
            pl.BlockSpec((1, attn_w), const2),
            pl.BlockSpec((1, conv_ch), const2),
            pl.BlockSpec((attn_w + conv_ch, d), const2),
            pl.BlockSpec((1, d), const2),
            pl.BlockSpec((1, d), const2),
        ],
        out_specs=[
            pl.BlockSpec((rows, d), lambda i: (i, 0)),
            pl.BlockSpec((nb, win, kv_w), lambda i: (i, 0, 0)),
            pl.BlockSpec((nb, win, kv_w), lambda i: (i, 0, 0)),
            pl.BlockSpec((nb, CONV_K - 1, conv_ch), lambda i: (i, 0, 0)),
        ],
        out_shape=[
            jax.ShapeDtypeStruct((dec_b * dec_seq, d), F32),
            jax.ShapeDtypeStruct((dec_b, win, kv_w), F32),
            jax.ShapeDtypeStruct((dec_b, win, kv_w), F32),
            jax.ShapeDtypeStruct((dec_b, CONV_K - 1, conv_ch), F32),
        ],
        compiler_params=pltpu.CompilerParams(dimension_semantics=("arbitrary",),
                                             vmem_limit_bytes=VMEM_LIMIT_BYTES),
        name="sample_mixer",
    )(x.reshape(dec_b * dec_seq, d), ck, cv, st, win_b, *tabs, sinks, conv_w, g_attn, g_conv, wout_b,
      ln_g, ln_b)


def _over_experts(fn, x):
    return fn(fn(x, axis=0, keepdims=True), axis=1, keepdims=True)


def _two_group_specs(tm, d, n_prompt_tiles):
    return [pl.BlockSpec((tm, d), lambda i, *_: (jnp.minimum(i, n_prompt_tiles - 1), 0)),
            pl.BlockSpec((tm, d), lambda i, *_: (jnp.maximum(i - n_prompt_tiles, 0), 0))]


def _route_kernel(hp_ref, hs_ref, rwt_ref, bias_ref, eidx_ref, rank_ref, gate_ref, cnt_ref, *, n_experts,
                  n_prompt_tiles):
    i = pl.program_id(0)
    tm = hp_ref.shape[0]
    per_group = n_experts // N_EXPERT_GROUPS
    shape3 = (N_EXPERT_GROUPS, per_group, tm)

    h = jnp.where(i < n_prompt_tiles, hp_ref[...], hs_ref[...])
    logits = lax.dot_general(rwt_ref[...], h.astype(BF16), (((1,), (1,)), ((), ())),
                             preferred_element_type=F32)
    scores = jax.nn.sigmoid(logits)
    sel = scores + bias_ref[...]
    scores3 = scores.reshape(shape3)
    grp = sel.reshape(shape3)
    member = lax.broadcasted_iota(jnp.int32, shape3, 1).astype(F32)
    group = lax.broadcasted_iota(jnp.int32, shape3, 0).astype(F32)
    expert = group * per_group + member

    m1 = jnp.max(grp, axis=1, keepdims=True)
    f1 = jnp.min(jnp.where(grp == m1, member, float(per_group)), axis=1, keepdims=True)
    m2 = jnp.max(jnp.where(member == f1, NEG_INF, grp), axis=1, keepdims=True)
    gscore = m1 + m2

    gid = lax.broadcasted_iota(jnp.int32, gscore.shape, 0).astype(F32)
    gmask = jnp.zeros(gscore.shape, F32)
    cur = gscore
    for _ in range(TOPK_GROUPS):
        mx = jnp.max(cur, axis=0, keepdims=True)
        pick = gid == jnp.min(jnp.where(cur == mx, gid, float(N_EXPERT_GROUPS)), axis=0, keepdims=True)
        gmask = jnp.where(pick, 1.0, gmask)
        cur = jnp.where(pick, NEG_INF, cur)

    cand = jnp.where(gmask > 0.0, grp, NEG_INF)
    chosen = jnp.zeros(shape3, F32)
    picks, firsts, weights = [], [], []
    for _ in range(TOP_K):
        mx = _over_experts(jnp.max, cand)
        first = _over_experts(jnp.min, jnp.where(cand == mx, expert, float(n_experts)))
        pick = expert == first
        picks.append(pick)
        firsts.append(first)
        weights.append(_over_experts(jnp.sum, jnp.where(pick, scores3, 0.0)))
        chosen = jnp.where(pick, 1.0, chosen)
        cand = jnp.where(pick, NEG_INF, cand)
    wsum = weights[0]
    for w in weights[1:]:
        wsum = wsum + w

    chosen_b = chosen.reshape(n_experts, tm).astype(BF16)
    t_from = lax.broadcasted_iota(jnp.int32, (tm, tm), 0)
    t_to = lax.broadcasted_iota(jnp.int32, (tm, tm), 1)
    earlier = ((t_from < t_to) & (t_from // MOE_TILE == t_to // MOE_TILE)).astype(BF16)
    before3 = jnp.dot(chosen_b, earlier, preferred_element_type=F32).reshape(shape3)

    pad = SUBLANES - TOP_K
    eidx = [f.reshape(1, tm).astype(jnp.int32) for f in firsts]
    rank = [_over_experts(jnp.sum, jnp.where(p, before3, 0.0)).reshape(1, tm).astype(jnp.int32) for p in picks]
    gate = [(w / wsum * ROUTED_SCALE).reshape(1, tm) for w in weights]
    eidx_ref[...] = jnp.concatenate(eidx + [jnp.zeros((pad, tm), jnp.int32)], axis=0)
    rank_ref[...] = jnp.concatenate(rank + [jnp.zeros((pad, tm), jnp.int32)], axis=0)
    gate_ref[...] = jnp.concatenate(gate + [jnp.zeros((pad, tm), F32)], axis=0)

    for sub in range(tm // MOE_TILE):
        cnt_ref[sub * SUBLANES:(sub + 1) * SUBLANES, :] = lax.dot_general(
            jnp.ones((SUBLANES, MOE_TILE), BF16), chosen_b[:, sub * MOE_TILE:(sub + 1) * MOE_TILE],
            (((1,), (1,)), ((), ())), preferred_element_type=F32)


def _route(h_p, h_s, rwt_b, bias_col):
    d = h_p.shape[1]
    t_all = h_p.shape[0] + h_s.shape[0]
    n_experts = rwt_b.shape[0]
    tm = ROUTE_TILE
    npt = h_p.shape[0] // tm
    moe_tiles = tm // MOE_TILE
    row_spec = pl.BlockSpec((SUBLANES, tm), lambda i: (0, i))
    return pl.pallas_call(
        functools.partial(_route_kernel, n_experts=n_experts, n_prompt_tiles=npt),
        grid=(t_all // tm,),
        in_specs=_two_group_specs(tm, d, npt) + [
            pl.BlockSpec((n_experts, d), lambda i: (0, 0)),
            pl.BlockSpec((n_experts, 1), lambda i: (0, 0)),
        ],
        out_specs=[row_spec, row_spec, row_spec,
                   pl.BlockSpec((moe_tiles * SUBLANES, n_experts), lambda i: (i, 0))],
        out_shape=[
            jax.ShapeDtypeStruct((SUBLANES, t_all), jnp.int32),
            jax.ShapeDtypeStruct((SUBLANES, t_all), jnp.int32),
            jax.ShapeDtypeStruct((SUBLANES, t_all), F32),
            jax.ShapeDtypeStruct((t_all // MOE_TILE * SUBLANES, n_experts), F32),
        ],
        compiler_params=pltpu.CompilerParams(dimension_semantics=("arbitrary",),
                                             vmem_limit_bytes=VMEM_LIMIT_BYTES),
        name="route",
    )(h_p, h_s, rwt_b, bias_col)


def _for_each_part(rows, max_rows, fn):
    for size in [SUBLANES << b for b in range((max_rows // SUBLANES).bit_length())]:
        @pl.when((rows & size) != 0)
        def _(size=size):
            fn(pl.multiple_of(rows & (size - 1), SUBLANES), size)


def _start_run_copies(tile, cnt_ref, ls_ref, gs_ref, big_ref, n_experts, start_copy):
    def run(e):
        idx = tile * n_experts + e
        return cnt_ref[idx], ls_ref[idx], gs_ref[idx]

    for e in range(n_experts):
        rows, l0, g0 = run(e)
        _for_each_part(rows, RUN_CHUNK // 2, lambda off, size: start_copy(
            pl.multiple_of(l0 + off, SUBLANES), pl.multiple_of(g0 + off, SUBLANES), size))

    @pl.when(big_ref[tile] != 0)
    def _():
        def per_expert(e, c):
            rows, l0, g0 = run(e)
            base = rows & (RUN_CHUNK - 1)

            def chunk(j, c2):
                off = base + j * RUN_CHUNK
                start_copy(pl.multiple_of(l0 + off, SUBLANES), pl.multiple_of(g0 + off, SUBLANES), RUN_CHUNK)
                return c2

            lax.fori_loop(0, lax.shift_right_logical(rows, RUN_CHUNK.bit_length() - 1), chunk, 0)
            return c

        lax.fori_loop(0, n_experts, per_expert, 0)


def _top_bit(n):
    return 1 << (n.bit_length() - 1)


def _pack_rows(x):
    rows, two_w = x.shape
    w = two_w // 2
    x3 = x.reshape(rows // SUBLANES, SUBLANES, two_w)
    halves = jnp.concatenate([x3[:, :, :w], x3[:, :, w:]], axis=1).astype(BF16)
    return pltpu.bitcast(halves, jnp.uint32).reshape(rows, w)


def _unpack_rows(p):
    rows, w = p.shape
    halves = pltpu.bitcast(p.reshape(rows // SUBLANES, SUBLANES, w), BF16).astype(F32)
    return jnp.concatenate([halves[:, :SUBLANES, :], halves[:, SUBLANES:, :]], axis=-1).reshape(rows, 2 * w)


def _local_rows(tm, n_experts):
    n = TOP_K * tm + n_experts * (SUBLANES - 1)
    return -(-n // SORT_CHUNK) * SORT_CHUNK


def _dispatch_kernel(cnt_ref, ls_ref, gs_ref, tot_ref, big_ref, zrow_ref, lp_ref, hp_ref, hs_ref, xs_hbm,
                     lbuf, zero_s, sem, *, n_experts, blk, n_prompt_tiles):
    i = pl.program_id(0)
    tm, d = hp_ref.shape
    n = TOP_K * tm
    nl = lbuf.shape[1]
    slot = i % 2

    def tile_wait(tile):
        pltpu.make_async_copy(lbuf.at[0, pl.ds(0, n)], xs_hbm.at[pl.ds(0, n)], sem).wait()
        _for_each_part(tot_ref[tile] - n, _top_bit(n_experts * (SUBLANES - 1)), lambda off, size:
                       pltpu.make_async_copy(lbuf.at[0, pl.ds(0, size)], xs_hbm.at[pl.ds(0, size)], sem).wait())

    @pl.when(i == 0)
    def _():
        zero_s[...] = jnp.zeros(zero_s.shape, zero_s.dtype)

        def zcopy(e):
            return pltpu.make_async_copy(zero_s, xs_hbm.at[pl.ds(pl.multiple_of(zrow_ref[e], blk), blk)], sem)

        def start(e, c):
            @pl.when(zrow_ref[e] >= 0)
            def _():
                zcopy(e).start()
            return c

        def wait(e, c):
            @pl.when(zrow_ref[e] >= 0)
            def _():
                zcopy(e).wait()
            return c

        lax.fori_loop(0, n_experts, start, 0)
        lax.fori_loop(0, n_experts, wait, 0)

    hb = jnp.where(i < n_prompt_tiles, hp_ref[...], hs_ref[...]).astype(BF16)
    lp = lp_ref[...]
    def sort_chunk(r):
        rows = r * SORT_CHUNK + lax.broadcasted_iota(jnp.int32, (SORT_CHUNK, tm), 0)
        hit = jnp.zeros((SORT_CHUNK, tm), F32)
        for k in range(TOP_K):
            hit = jnp.where(rows == lp[k:k + 1, :], 1.0, hit)
        lbuf[slot, pl.ds(r * SORT_CHUNK, SORT_CHUNK), :] = _pack_rows(
            jnp.dot(hit.astype(BF16), hb, preferred_element_type=F32))

    last_chunk = nl // SORT_CHUNK - 1
    for r in range(last_chunk):
        sort_chunk(r)
    pl.when(tot_ref[i] > last_chunk * SORT_CHUNK)(functools.partial(sort_chunk, last_chunk))

    @pl.when(i > 0)
    def _():
        tile_wait(i - 1)

    _start_run_copies(i, cnt_ref, ls_ref, gs_ref, big_ref, n_experts, lambda lrow, grow, size: pltpu.make_async_copy(
        lbuf.at[slot, pl.ds(lrow, size)], xs_hbm.at[pl.ds(grow, size)], sem).start())

    @pl.when(i == pl.num_programs(0) - 1)
    def _():
        tile_wait(i)


def _dispatch(tables, zrow, lp, h_p, h_s, *, n_rows, blk):
    d = h_p.shape[1]
    t_all = h_p.shape[0] + h_s.shape[0]
    tm = MOE_TILE
    npt = h_p.shape[0] // tm
    n_experts = zrow.shape[0]
    return pl.pallas_call(
        functools.partial(_dispatch_kernel, n_experts=n_experts, blk=blk, n_prompt_tiles=npt),
        grid_spec=pltpu.PrefetchScalarGridSpec(
            num_scalar_prefetch=6,
            grid=(t_all // tm,),
            in_specs=[pl.BlockSpec((SUBLANES, tm), lambda i, *_: (0, i))] + _two_group_specs(tm, d, npt),
            out_specs=pl.BlockSpec(memory_space=pl.ANY),
            scratch_shapes=[
                pltpu.VMEM((2, _local_rows(tm, n_experts), d // 2), jnp.uint32),
                pltpu.VMEM((blk, d // 2), jnp.uint32),
                pltpu.SemaphoreType.DMA,
            ],
        ),
        out_shape=jax.ShapeDtypeStruct((n_rows, d // 2), jnp.uint32),
        compiler_params=pltpu.CompilerParams(dimension_semantics=("arbitrary",),
                                             vmem_limit_bytes=VMEM_LIMIT_BYTES),
        name="dispatch",
    )(*tables, zrow, lp, h_p, h_s)


def _silu(x):
    return x * jax.nn.sigmoid(x)


def _expert_kernel(be_ref, nact_ref, xs_ref, wg_ref, wu_ref, wd_ref, ys_ref, wg_s, wu_s, wd_s):
    b = pl.program_id(0)

    @pl.when(b < nact_ref[0])
    def _():
        @pl.when((b == 0) | (be_ref[b] != be_ref[jnp.maximum(b - 1, 0)]))
        def _():
            wg_s[...] = wg_ref[...].astype(BF16)
            wu_s[...] = wu_ref[...].astype(BF16)
            wd_s[...] = wd_ref[...].astype(BF16)

        xb = _unpack_rows(xs_ref[...]).astype(BF16)
        hid = _silu(jnp.dot(xb, wg_s[...], preferred_element_type=F32)) * jnp.dot(
            xb, wu_s[...], preferred_element_type=F32)
        ys_ref[...] = _pack_rows(jnp.dot(hid.astype(BF16), wd_s[...], preferred_element_type=F32))


def _experts(block_expert, nact, xs, w_gate, w_up, w_down, *, blk):
    n_rows, dw = xs.shape
    d = w_gate.shape[-2]
    ff = w_gate.shape[-1]
    n_blocks = n_rows // blk

    def active(b, be, na):
        return jnp.minimum(b, na[0] - 1)

    return pl.pallas_call(
        _expert_kernel,
        grid_spec=pltpu.PrefetchScalarGridSpec(
            num_scalar_prefetch=2,
            grid=(n_blocks,),
            in_specs=[
                pl.BlockSpec((blk, dw), lambda b, be, na: (active(b, be, na), 0)),
                pl.BlockSpec((None, d, ff), lambda b, be, na: (be[active(b, be, na)], 0, 0)),
                pl.BlockSpec((None, d, ff), lambda b, be, na: (be[active(b, be, na)], 0, 0)),
                pl.BlockSpec((None, ff, d), lambda b, be, na: (be[active(b, be, na)], 0, 0)),
            ],
            out_specs=pl.BlockSpec((blk, dw), lambda b, be, na: (active(b, be, na), 0)),
            scratch_shapes=[
                pltpu.VMEM((d, ff), BF16),
                pltpu.VMEM((d, ff), BF16),
                pltpu.VMEM((ff, d), BF16),
            ],
        ),
        out_shape=jax.ShapeDtypeStruct((n_rows, dw), jnp.uint32),
        compiler_params=pltpu.CompilerParams(dimension_semantics=("arbitrary",),
                                             vmem_limit_bytes=VMEM_LIMIT_BYTES),
        name="experts",
    )(block_expert, nact, xs, w_gate, w_up, w_down)


def _combine_kernel(cnt_ref, ls_ref, gs_ref, tot_ref, big_ref, ys_hbm, lpt_ref, gate_ref, hp_ref, hs_ref, wsg_ref, wsu_ref,
                    wsd_ref, lng_ref, lnb_ref, yp_ref, ysm_ref, ybuf, moe_s, sems, *, alpha, n_experts,
                    n_prompt_tiles):
    i = pl.program_id(0)
    tm, d = hp_ref.shape
    n = TOP_K * tm
    nl = ybuf.shape[1]
    slot = i % 2

    def fetch(tile, to):
        _start_run_copies(tile, cnt_ref, ls_ref, gs_ref, big_ref, n_experts, lambda lrow, grow, size:
                          pltpu.make_async_copy(ys_hbm.at[pl.ds(grow, size)], ybuf.at[to, pl.ds(lrow, size)],
                                                sems.at[to]).start())

    @pl.when(i == 0)
    def _():
        ybuf[...] = jnp.zeros(ybuf.shape, ybuf.dtype)
        fetch(0, 0)

    @pl.when(i + 1 < pl.num_programs(0))
    def _():
        fetch(i + 1, 1 - slot)

    h = jnp.where(i < n_prompt_tiles, hp_ref[...], hs_ref[...])
    hb = h.astype(BF16)
    hid = _silu(jnp.dot(hb, wsg_ref[...], preferred_element_type=F32)) * jnp.dot(
        hb, wsu_ref[...], preferred_element_type=F32)
    shared = jnp.dot(hid.astype(BF16), wsd_ref[...], preferred_element_type=F32)

    pltpu.make_async_copy(ys_hbm.at[pl.ds(0, n)], ybuf.at[slot, pl.ds(0, n)], sems.at[slot]).wait()
    _for_each_part(tot_ref[i] - n, _top_bit(n_experts * (SUBLANES - 1)), lambda off, size: pltpu.make_async_copy(
        ys_hbm.at[pl.ds(0, size)], ybuf.at[slot, pl.ds(0, size)], sems.at[slot]).wait())

    lpt = lpt_ref[...]
    gate = gate_ref[...]

    def chunk_sum(r):
        cols = r * SORT_CHUNK + lax.broadcasted_iota(jnp.int32, (tm, SORT_CHUNK), 1)
        g = jnp.zeros((tm, SORT_CHUNK), F32)
        for k in range(TOP_K):
            g = jnp.where(cols == lpt[:, k:k + 1], gate[:, k:k + 1], g)
        yb = _unpack_rows(ybuf[slot, pl.ds(r * SORT_CHUNK, SORT_CHUNK), :]).astype(BF16)
        return jnp.dot(g.astype(BF16), yb, preferred_element_type=F32)

    last_chunk = nl // SORT_CHUNK - 1
    moe = shared
    for r in range(last_chunk):
        moe = moe + chunk_sum(r)
    moe_s[...] = moe

    @pl.when(tot_ref[i] > last_chunk * SORT_CHUNK)
    def _():
        moe_s[...] += chunk_sum(last_chunk)

    y = _layer_norm(alpha * h + moe_s[...], lng_ref[...], lnb_ref[...])

    @pl.when(i < n_prompt_tiles)
    def _():
        yp_ref[...] = y

    @pl.when(i >= n_prompt_tiles)
    def _():
        ysm_ref[...] = y


def _combine(tables, ys, lp_t, gates_t, h_p, h_s, wsg_b, wsu_b, wsd_b, ln_g, ln_b, *, alpha):
    t_prompt, d = h_p.shape
    t_all = t_prompt + h_s.shape[0]
    tm = MOE_TILE
    ff = wsg_b.shape[-1]
    npt = t_prompt // tm
    n_experts = tables[0].shape[0] // (t_all // tm)
    const2 = lambda i, *_: (0, 0)
    return pl.pallas_call(
        functools.partial(_combine_kernel, alpha=alpha, n_experts=n_experts, n_prompt_tiles=npt),
        grid_spec=pltpu.PrefetchScalarGridSpec(
            num_scalar_prefetch=5,
            grid=(t_all // tm,),
            in_specs=[
                pl.BlockSpec(memory_space=pl.ANY),
                pl.BlockSpec((tm, SUBLANES), lambda i, *_: (i, 0)),
                pl.BlockSpec((tm, SUBLANES), lambda i, *_: (i, 0)),
                *_two_group_specs(tm, d, npt),
                pl.BlockSpec((d, ff), const2),
                pl.BlockSpec((d, ff), const2),
                pl.BlockSpec((ff, d), const2),
                pl.BlockSpec((1, d), const2),
                pl.BlockSpec((1, d), const2),
            ],
            out_specs=[
                pl.BlockSpec((tm, d), lambda i, *_: (jnp.minimum(i, npt - 1), 0)),
                pl.BlockSpec((tm, d), lambda i, *_: (jnp.maximum(i - npt, 0), 0)),
            ],
            scratch_shapes=[
                pltpu.VMEM((2, _local_rows(tm, n_experts), d // 2), jnp.uint32),
                pltpu.VMEM((tm, d), F32),
                pltpu.SemaphoreType.DMA((2,)),
            ],
        ),
        out_shape=[
            jax.ShapeDtypeStruct((t_prompt, d), F32),
            jax.ShapeDtypeStruct((t_all - t_prompt, d), F32),
        ],
        compiler_params=pltpu.CompilerParams(dimension_semantics=("arbitrary",),
                                             vmem_limit_bytes=VMEM_LIMIT_BYTES),
        name="combine",
    )(*tables, ys, lp_t, gates_t, h_p, h_s, wsg_b, wsu_b, wsd_b, ln_g, ln_b)


def _moe(h_p, h_s, router_w, router_bias, w_gate, w_up, w_down, ws_gate, ws_up, ws_down, ln_g, ln_b, *, alpha):
    t_all = h_p.shape[0] + h_s.shape[0]
    n_experts = router_w.shape[-1]
    blk = EXPERT_BLOCK
    tm = MOE_TILE
    nt = t_all // tm
    eidx8, lrank8, gate8, cnt8 = _route(h_p, h_s, router_w.T.astype(BF16), router_bias.reshape(n_experts, 1))

    cnt = cnt8.reshape(nt, SUBLANES, n_experts)[:, 0, :].astype(jnp.int32)
    cnt = (cnt + SUBLANES - 1) // SUBLANES * SUBLANES
    counts = jnp.sum(cnt, axis=0)
    blocks_per_e = (counts + blk - 1) // blk
    block_end = jnp.cumsum(blocks_per_e)
    pad_start = (block_end - blocks_per_e) * blk
    n_blocks = -(-(t_all * TOP_K + nt * n_experts * (SUBLANES - 1)) // blk) + n_experts
    block_expert = jnp.minimum(jnp.sum(block_end[None, :] <= jnp.arange(n_blocks)[:, None], axis=1),
                               n_experts - 1).astype(jnp.int32)
    nact = block_end[-1:].astype(jnp.int32)
    zrow = jnp.where(blocks_per_e > 0, (block_end - 1) * blk, -1).astype(jnp.int32)
    gstart = pad_start[None, :] + jnp.cumsum(cnt, axis=0) - cnt
    lstart = jnp.cumsum(cnt, axis=1) - cnt
    tables = tuple(a.reshape(-1).astype(jnp.int32) for a in (
        cnt, lstart, gstart, jnp.sum(cnt, axis=1), jnp.max(cnt, axis=1) >= RUN_CHUNK))
    lstart_tok = jnp.repeat(lstart, tm, axis=0)
    lp8 = jnp.sum(jnp.where(eidx8[..., None] == jnp.arange(n_experts), lstart_tok[None], 0), axis=-1) + lrank8
    lp8 = lp8.astype(jnp.int32)

    xs = _dispatch(tables, zrow, lp8, h_p, h_s, n_rows=n_blocks * blk, blk=blk)
    ys = _experts(block_expert, nact, xs, w_gate, w_up, w_down, blk=blk)
    return _combine(tables, ys, lp8.T, gate8.T, h_p, h_s, ws_gate.astype(BF16), ws_up.astype(BF16), ws_down.astype(BF16),
                    ln_g, ln_b, alpha=alpha)


def kernel(x_prompt, x_sample, cache_k, cache_v, state_conv, w_in, attn_sinks, conv_w, g_attn_out, g_conv_out, w_out, ln1_g, ln1_b, router_w, router_bias, w_gate, w_up, w_down, ws_gate, ws_up, ws_down, ln2_g, ln2_b):
    depth = w_in.shape[0]
    bsz, seq, d = x_prompt.shape
    dec_b, dec_seq, _ = x_sample.shape
    win = cache_k.shape[2]
    kv_w = N_KV_HEADS * HEAD_DIM
    t_prompt = bsz * seq
    t_all = t_prompt + dec_b * dec_seq
    alpha = (2.0 * depth) ** 0.25
    assert win == WINDOW and seq % PROMPT_TILE == 0 and dec_b % SAMPLE_SEQS == 0
    assert t_prompt % ROUTE_TILE == 0 and (t_all - t_prompt) % ROUTE_TILE == 0 and ROUTE_TILE % MOE_TILE == 0
    assert MOE_TILE & (MOE_TILE - 1) == 0 and (TOP_K * MOE_TILE) % SORT_CHUNK == 0

    tabs_p = _rope_tables(jnp.arange(seq))
    tabs_s = tuple(jnp.tile(t, (SAMPLE_SEQS, 1)) for t in _rope_tables(PAST_LEN + jnp.arange(dec_seq)))
    row = lambda a: a.reshape(1, -1)

    xp, xs = x_prompt, x_sample
    outs = [[] for _ in range(6)]
    for l in range(depth):
        win_b, wout_b = w_in[l].astype(BF16), w_out[l].astype(BF16)
        shared = (attn_sinks[l], conv_w[l], row(g_attn_out[l]), row(g_conv_out[l]), wout_b, row(ln1_g[l]),
                  row(ln1_b[l]))
        h_p, kp, vp, cp = _prompt_mixer(xp, win_b, tabs_p, *shared, alpha=alpha)
        h_s, kn, vn, cn = _sample_mixer(xs, cache_k[l].reshape(dec_b, win, kv_w),
                                        cache_v[l].reshape(dec_b, win, kv_w), state_conv[l], win_b, tabs_s,
                                        *shared, alpha=alpha)
        yp, ys = _moe(h_p, h_s, router_w[l], router_bias[l], w_gate[l], w_up[l], w_down[l], ws_gate[l], ws_up[l],
                      ws_down[l], row(ln2_g[l]), row(ln2_b[l]), alpha=alpha)
        xp, xs = yp.reshape(bsz, seq, d), ys.reshape(dec_b, dec_seq, d)
        heads = lambda a: a.reshape(a.shape[0], win, N_KV_HEADS, HEAD_DIM)
        for o, a in zip(outs, (heads(kp), heads(vp), cp, heads(kn), heads(vn), cn)):
            o.append(a)
    return (xp, xs) + tuple(jnp.stack(o, axis=0) for o in outs)
```

```python
import functools

import jax
import jax.numpy as jnp
from jax import lax
from jax.experimental import pallas as pl
from jax.experimental.pallas import tpu as pltpu

PAST_LEN = 16384
WINDOW = 128
HEAD_DIM = 64
N_KV_HEADS = 2
ROT_DIM = HEAD_DIM // 4
ROPE_THETA = 500000.0
CONV_K = 3
TOP_K = 6
N_EXPERT_GROUPS = 8
TOPK_GROUPS = 4
ROUTED_SCALE = 2.5
LN_EPS = 1e-5
RMS_EPS = 1e-6

LANES = 128
SUBLANES = 8
VMEM_LIMIT_BYTES = 56 * 1024 * 1024

PROMPT_TILE = 512
SAMPLE_SEQS = 32
MOE_TILE = 256
SORT_CHUNK = 256
EXPERT_BLOCK = 2048
ROUTE_TILE = 512
RUN_CHUNK = 64

F32 = jnp.float32
BF16 = jnp.bfloat16
NEG_INF = float("-inf")


def _rope_tables(positions):
    half = ROT_DIM // 2
    inv_freq = ROPE_THETA ** (-jnp.arange(0, ROT_DIM, 2, dtype=F32) / ROT_DIM)
    ang = positions.astype(F32)[:, None] * inv_freq[None, :]
    cos, sin = jnp.cos(ang), jnp.sin(ang)
    n = positions.shape[0]
    rest = HEAD_DIM - ROT_DIM
    c = jnp.concatenate([cos, cos, jnp.ones((n, rest), F32)], axis=-1)
    sa = jnp.concatenate([-sin, jnp.zeros((n, half + rest), F32)], axis=-1)
    sb = jnp.concatenate([jnp.zeros((n, half), F32), sin, jnp.zeros((n, rest), F32)], axis=-1)
    reps = LANES // HEAD_DIM
    return jnp.tile(c, (1, reps)), jnp.tile(sa, (1, reps)), jnp.tile(sb, (1, reps))


def _rope(x, c, sa, sb):
    half = ROT_DIM // 2
    return x * c + pltpu.roll(x, LANES - half, 1) * sa + pltpu.roll(x, half, 1) * sb


def _rms_norm(x, g):
    return x * lax.rsqrt(jnp.mean(jnp.square(x), axis=-1, keepdims=True) + RMS_EPS) * g


def _layer_norm(x, g, b):
    mu = jnp.mean(x, axis=-1, keepdims=True)
    var = jnp.mean(jnp.square(x - mu), axis=-1, keepdims=True)
    return (x - mu) * lax.rsqrt(var + LN_EPS) * g + b


def _short_conv(gated, prev2, prev1, row, conv_w, b_gate):
    g1 = pltpu.roll(gated, 1, 0)
    g2 = pltpu.roll(gated, 2, 0)
    g1 = jnp.where(row == 0, prev1, g1)
    g2 = jnp.where(row == 0, prev2, jnp.where(row == 1, prev1, g2))
    y = conv_w[0:1, :] * g2 + conv_w[1:2, :] * g1 + conv_w[2:3, :] * gated
    return b_gate * y


def _merge_norm(x, attn_o, conv_o, gattn, gconv, wout_ref, ln_g, ln_b, alpha):
    cat = jnp.concatenate([_rms_norm(attn_o, gattn), _rms_norm(conv_o, gconv)], axis=-1)
    mix = jnp.dot(cat.astype(BF16), wout_ref[...], preferred_element_type=F32)
    return _layer_norm(alpha * x + mix, ln_g, ln_b)


def _sink_softmax(parts, sink):
    m = sink
    for s in parts:
        m = jnp.maximum(m, jnp.max(s, axis=-1, keepdims=True))
    es = [jnp.exp(s - m) for s in parts]
    den = jnp.exp(sink - m)
    for e in es:
        den = den + jnp.sum(e, axis=-1, keepdims=True)
    return [(e / den).astype(BF16) for e in es]


def _prompt_mixer_kernel(x_ref, win_ref, c_ref, sa_ref, sb_ref, sinks_ref, convw_ref, gattn_ref, gconv_ref,
                         wout_ref, lng_ref, lnb_ref,
                         h_ref, ko_ref, vo_ref, co_ref,
                         q_s, kt_s, vm_s, o_s, gc_s, *, alpha, n_heads, attn_w, kv_w, conv_ch):
    s = pl.program_id(1)
    last = pl.num_programs(1) - 1
    tq = x_ref.shape[0]
    q_per_kv = n_heads // N_KV_HEADS
    heads_per_group = LANES // HEAD_DIM

    @pl.when(s == 0)
    def _():
        kt_s[:, :, 0:WINDOW] = jnp.zeros((kt_s.shape[0], LANES, WINDOW), BF16)
        vm_s[:, 0:WINDOW, :] = jnp.zeros((vm_s.shape[0], WINDOW, LANES), BF16)
        gc_s[...] = jnp.zeros(gc_s.shape, F32)

    @pl.when(s > 0)
    def _():
        kt_s[:, :, 0:WINDOW] = kt_s[:, :, tq:tq + WINDOW]
        vm_s[:, 0:WINDOW, :] = vm_s[:, tq:tq + WINDOW, :]

    x = x_ref[...]
    xb = x.astype(BF16)
    c, sa, sb = c_ref[...], sa_ref[...], sb_ref[...]

    def proj(lo, width):
        return jnp.dot(xb, win_ref[:, lo:lo + width], preferred_element_type=F32)

    scale = HEAD_DIM ** -0.5
    for j in range(attn_w // LANES):
        qj = _rope(proj(j * LANES, LANES), c, sa, sb)
        q_s[:, j * LANES:(j + 1) * LANES] = (qj * scale).astype(BF16)
    k = _rope(proj(attn_w, kv_w), c, sa, sb)
    v = proj(attn_w + kv_w, kv_w)

    @pl.when(s == last)
    def _():
        ko_ref[...] = k[tq - WINDOW:, :]
        vo_ref[...] = v[tq - WINDOW:, :]

    kt = k.T.astype(BF16)
    zeros_k = jnp.zeros((HEAD_DIM, tq), BF16)
    v_swapped = pltpu.roll(v, HEAD_DIM, 1)
    low_lanes = lax.broadcasted_iota(jnp.int32, (tq, LANES), 1) < HEAD_DIM
    for kvh in range(N_KV_HEADS):
        kt_h = kt[kvh * HEAD_DIM:(kvh + 1) * HEAD_DIM, :]
        v_lo = v if kvh == 0 else v_swapped
        v_hi = v_swapped if kvh == 0 else v
        kt_s[2 * kvh, :, WINDOW:] = jnp.concatenate([kt_h, zeros_k], axis=0)
        kt_s[2 * kvh + 1, :, WINDOW:] = jnp.concatenate([zeros_k, kt_h], axis=0)
        vm_s[2 * kvh, WINDOW:, :] = jnp.where(low_lanes, v_lo, 0.0).astype(BF16)
        vm_s[2 * kvh + 1, WINDOW:, :] = jnp.where(low_lanes, 0.0, v_hi).astype(BF16)

    qi = lax.broadcasted_iota(jnp.int32, (WINDOW, 2 * WINDOW), 0)
    ci = lax.broadcasted_iota(jnp.int32, (WINDOW, 2 * WINDOW), 1)
    band = (ci > qi) & (ci <= qi + WINDOW)
    for j in range(tq // WINDOW):
        r0 = j * WINDOW
        mask = band if j > 0 else band & ((ci >= WINDOW) | (s > 0))
        for grp in range(attn_w // LANES):
            q_grp = q_s[r0:r0 + WINDOW, grp * LANES:(grp + 1) * LANES]
            out = None
            for r in range(heads_per_group):
                hd = grp * heads_per_group + r
                src = 2 * (hd // q_per_kv) + r
                sink = sinks_ref[hd]
                sc = jnp.dot(q_grp, kt_s[src, :, r0:r0 + 2 * WINDOW], preferred_element_type=F32)
                sc = jnp.where(mask, sc, NEG_INF)
                m = jnp.maximum(jnp.max(sc, axis=-1, keepdims=True), sink)
                e = jnp.exp(sc - m)
                den = jnp.sum(e, axis=-1, keepdims=True) + jnp.exp(sink - m)
                o_h = jnp.dot(e.astype(BF16), vm_s[src, r0:r0 + 2 * WINDOW, :],
                              preferred_element_type=F32) * (1.0 / den)
                out = o_h if out is None else out + o_h
            o_s[r0:r0 + WINDOW, grp * LANES:(grp + 1) * LANES] = out

    o3 = attn_w + 2 * kv_w
    gated = proj(o3 + 2 * conv_ch, conv_ch) * proj(o3, conv_ch)
    row = lax.broadcasted_iota(jnp.int32, (tq, 1), 0)
    conv_o = _short_conv(gated, gc_s[0:1, :], gc_s[1:2, :], row, convw_ref[...], proj(o3 + conv_ch, conv_ch))
    gc_s[0:CONV_K - 1, :] = gated[tq - (CONV_K - 1):, :]

    @pl.when(s == last)
    def _():
        co_ref[...] = gated[tq - (CONV_K - 1):, :]

    h_ref[...] = _merge_norm(x, o_s[...], conv_o, gattn_ref[...], gconv_ref[...], wout_ref,
                             lng_ref[...], lnb_ref[...], alpha)


def _prompt_mixer(x, win_b, tabs, sinks, conv_w, g_attn, g_conv, wout_b, ln_g, ln_b, *, alpha):
    bsz, seq, d = x.shape
    tq = PROMPT_TILE
    ns = seq // tq
    attn_w = g_attn.shape[-1]
    conv_ch = g_conv.shape[-1]
    n_heads = attn_w // HEAD_DIM
    kv_w = N_KV_HEADS * HEAD_DIM
    assert kv_w == LANES and 2 * HEAD_DIM == LANES and (n_heads // N_KV_HEADS) % 2 == 0 and WINDOW == LANES
    in_cols = win_b.shape[-1]
    const2 = lambda b, s: (0, 0)
    kern = functools.partial(_prompt_mixer_kernel, alpha=alpha, n_heads=n_heads, attn_w=attn_w, kv_w=kv_w,
                             conv_ch=conv_ch)
    return pl.pallas_call(
        kern,
        grid=(bsz, ns),
        in_specs=[
            pl.BlockSpec((None, tq, d), lambda b, s: (b, s, 0)),
            pl.BlockSpec((d, in_cols), const2),
            pl.BlockSpec((tq, LANES), lambda b, s: (s, 0)),
            pl.BlockSpec((tq, LANES), lambda b, s: (s, 0)),
            pl.BlockSpec((tq, LANES), lambda b, s: (s, 0)),
            pl.BlockSpec(memory_space=pltpu.SMEM),
            pl.BlockSpec((CONV_K, conv_ch), const2),
            pl.BlockSpec((1, attn_w), const2),
            pl.BlockSpec((1, conv_ch), const2),
            pl.BlockSpec((attn_w + conv_ch, d), const2),
            pl.BlockSpec((1, d), const2),
            pl.BlockSpec((1, d), const2),
        ],
        out_specs=[
            pl.BlockSpec((tq, d), lambda b, s: (b * ns + s, 0)),
            pl.BlockSpec((None, WINDOW, kv_w), lambda b, s: (b, 0, 0)),
            pl.BlockSpec((None, WINDOW, kv_w), lambda b, s: (b, 0, 0)),
            pl.BlockSpec((None, CONV_K - 1, conv_ch), lambda b, s: (b, 0, 0)),
        ],
        out_shape=[
            jax.ShapeDtypeStruct((bsz * seq, d), F32),
            jax.ShapeDtypeStruct((bsz, WINDOW, kv_w), F32),
            jax.ShapeDtypeStruct((bsz, WINDOW, kv_w), F32),
            jax.ShapeDtypeStruct((bsz, CONV_K - 1, conv_ch), F32),
        ],
        scratch_shapes=[
            pltpu.VMEM((tq, attn_w), BF16),
            pltpu.VMEM((2 * N_KV_HEADS, LANES, WINDOW + tq), BF16),
            pltpu.VMEM((2 * N_KV_HEADS, WINDOW + tq, LANES), BF16),
            pltpu.VMEM((tq, attn_w), F32),
            pltpu.VMEM((SUBLANES, conv_ch), F32),
        ],
        compiler_params=pltpu.CompilerParams(dimension_semantics=("arbitrary", "arbitrary"),
                                             vmem_limit_bytes=VMEM_LIMIT_BYTES),
        name="prompt_mixer",
    )(x, win_b, *tabs, sinks, conv_w, g_attn, g_conv, wout_b, ln_g, ln_b)


def _sample_mixer_kernel(x_ref, ck_ref, cv_ref, st_ref, win_ref, c_ref, sa_ref, sb_ref, sinks_ref,
                         convw_ref, gattn_ref, gconv_ref, wout_ref, lng_ref, lnb_ref,
                         h_ref, ko_ref, vo_ref, co_ref, *, alpha, n_heads, attn_w, kv_w, conv_ch, dec_seq):
    nb, win = ck_ref.shape[0], ck_ref.shape[1]
    rows = nb * dec_seq
    q_per_kv = n_heads // N_KV_HEADS
    x = x_ref[...]
    xb = x.astype(BF16)
    c, sa, sb = c_ref[...], sa_ref[...], sb_ref[...]

    def proj(lo, width):
        return jnp.dot(xb, win_ref[:, lo:lo + width], preferred_element_type=F32)

    scale = HEAD_DIM ** -0.5
    k = _rope(proj(attn_w, kv_w), c, sa, sb)
    v = proj(attn_w + kv_w, kv_w)
    k3 = k.reshape(nb, dec_seq, kv_w)
    v3 = v.reshape(nb, dec_seq, kv_w)
    ck = ck_ref[...]
    cv = cv_ref[...]
    ko_ref[:, 0:win - dec_seq, :] = ck[:, dec_seq:, :]
    ko_ref[:, win - dec_seq:, :] = k3
    vo_ref[:, 0:win - dec_seq, :] = cv[:, dec_seq:, :]
    vo_ref[:, win - dec_seq:, :] = v3
    ckb, cvb, k3b, v3b = ck.astype(BF16), cv.astype(BF16), k3.astype(BF16), v3.astype(BF16)

    qrows = q_per_kv * dec_seq
    qi = lax.broadcasted_iota(jnp.int32, (nb, qrows, win), 1) % dec_seq
    mask_c = lax.broadcasted_iota(jnp.int32, (nb, qrows, win), 2) > qi + (win - WINDOW)
    qn = lax.broadcasted_iota(jnp.int32, (nb, qrows, dec_seq), 1) % dec_seq
    mask_n = lax.broadcasted_iota(jnp.int32, (nb, qrows, dec_seq), 2) <= qn
    sink_row = lax.broadcasted_iota(jnp.int32, (nb, qrows, 1), 1) // dec_seq

    q_chunks = [_rope(proj(j * LANES, LANES), c, sa, sb) * scale for j in range(attn_w // LANES)]
    heads_out = []
    for kvh in range(N_KV_HEADS):
        qs = []
        for g in range(q_per_kv):
            lo = (kvh * q_per_kv + g) * HEAD_DIM
            qh = q_chunks[lo // LANES][:, lo % LANES:lo % LANES + HEAD_DIM]
            qs.append(qh.reshape(nb, dec_seq, HEAD_DIM))
        qg = jnp.concatenate(qs, axis=1).astype(BF16)
        sl = slice(kvh * HEAD_DIM, (kvh + 1) * HEAD_DIM)
        sc_c = jnp.einsum("bqd,bkd->bqk", qg, ckb[:, :, sl], preferred_element_type=F32)
        sc_n = jnp.einsum("bqd,bkd->bqk", qg, k3b[:, :, sl], preferred_element_type=F32)
        sc_c = jnp.where(mask_c, sc_c, NEG_INF)
        sc_n = jnp.where(mask_n, sc_n, NEG_INF)
        sink = jnp.zeros((nb, qrows, 1), F32)
        for g in range(q_per_kv):
            sink = jnp.where(sink_row == g, sinks_ref[kvh * q_per_kv + g], sink)
        p_c, p_n = _sink_softmax([sc_c, sc_n], sink)
        og = (jnp.einsum("bqk,bkd->bqd", p_c, cvb[:, :, sl], preferred_element_type=F32)
              + jnp.einsum("bqk,bkd->bqd", p_n, v3b[:, :, sl], preferred_element_type=F32))
        for g in range(q_per_kv):
            heads_out.append(og[:, g * dec_seq:(g + 1) * dec_seq, :].reshape(rows, HEAD_DIM))
    attn_o = jnp.concatenate(heads_out, axis=-1)

    o3 = attn_w + 2 * kv_w
    gated = proj(o3 + 2 * conv_ch, conv_ch) * proj(o3, conv_ch)
    st = st_ref[...]
    prev2 = jnp.broadcast_to(st[:, 0:1, :], (nb, dec_seq, conv_ch)).reshape(rows, conv_ch)
    prev1 = jnp.broadcast_to(st[:, 1:2, :], (nb, dec_seq, conv_ch)).reshape(rows, conv_ch)
    row = lax.broadcasted_iota(jnp.int32, (rows, 1), 0) % dec_seq
    conv_o = _short_conv(gated, prev2, prev1, row, convw_ref[...], proj(o3 + conv_ch, conv_ch))
    co_ref[...] = gated.reshape(nb, dec_seq, conv_ch)[:, dec_seq - (CONV_K - 1):, :]

    h_ref[...] = _merge_norm(x, attn_o, conv_o, gattn_ref[...], gconv_ref[...], wout_ref,
                             lng_ref[...], lnb_ref[...], alpha)


def _sample_mixer(x, ck, cv, st, win_b, tabs, sinks, conv_w, g_attn, g_conv, wout_b, ln_g, ln_b, *, alpha):
    dec_b, dec_seq, d = x.shape
    assert dec_seq >= CONV_K - 1 and dec_seq % SUBLANES == 0
    nb = SAMPLE_SEQS
    rows = nb * dec_seq
    win = ck.shape[1]
    attn_w = g_attn.shape[-1]
    conv_ch = g_conv.shape[-1]
    n_heads = attn_w // HEAD_DIM
    kv_w = N_KV_HEADS * HEAD_DIM
    in_cols = win_b.shape[-1]
    const2 = lambda i: (0, 0)
    kern = functools.partial(_sample_mixer_kernel, alpha=alpha, n_heads=n_heads, attn_w=attn_w, kv_w=kv_w,
                             conv_ch=conv_ch, dec_seq=dec_seq)
    return pl.pallas_call(
        kern,
        grid=(dec_b // nb,),
        in_specs=[
            pl.BlockSpec((rows, d), lambda i: (i, 0)),
            pl.BlockSpec((nb, win, kv_w), lambda i: (i, 0, 0)),
            pl.BlockSpec((nb, win, kv_w), lambda i: (i, 0, 0)),
            pl.BlockSpec((nb, CONV_K - 1, conv_ch), lambda i: (i, 0, 0)),
            pl.BlockSpec((d, in_cols), const2),
            pl.BlockSpec((rows, LANES), const2),
            pl.BlockSpec((rows, LANES), const2),
            pl.BlockSpec((rows, LANES), const2),
            pl.BlockSpec(memory_space=pltpu.SMEM),
            pl.BlockSpec((CONV_K, conv_ch), const2),
            pl.BlockSpec((1, attn_w), const2),
            pl.BlockSpec((1, conv_ch), const2),
            pl.BlockSpec((attn_w + conv_ch, d), const2),
            pl.BlockSpec((1, d), const2),
            pl.BlockSpec((1, d), const2),
        ],
        out_specs=[
            pl.BlockSpec((rows, d), lambda i: (i, 0)),
            pl.BlockSpec((nb, win, kv_w), lambda i: (i, 0, 0)),
            pl.BlockSpec((nb, win, kv_w), lambda i: (i, 0, 0)),
            pl.BlockSpec((nb, CONV_K - 1, conv_ch), lambda i: (i, 0, 0)),
        ],
        out_shape=[
            jax.ShapeDtypeStruct((dec_b * dec_seq, d), F32),
            jax.ShapeDtypeStruct((dec_b, win, kv_w), F32),
            jax.ShapeDtypeStruct((dec_b, win, kv_w), F32),
            jax.ShapeDtypeStruct((dec_b, CONV_K - 1, conv_ch), F32),
        ],
        compiler_params=pltpu.CompilerParams(dimension_semantics=("arbitrary",),
                                             vmem_limit_bytes=VMEM_LIMIT_BYTES),
        name="sample_mixer",
    )(x.reshape(dec_b * dec_seq, d), ck, cv, st, win_b, *tabs, sinks, conv_w, g_attn, g_conv, wout_b,
      ln_g, ln_b)


def _over_experts(fn, x):
    return fn(fn(x, axis=0, keepdims=True), axis=1, keepdims=True)


def _two_group_specs(tm, d, n_prompt_tiles):
    return [pl.BlockSpec((tm, d), lambda i, *_: (jnp.minimum(i, n_prompt_tiles - 1), 0)),
            pl.BlockSpec((tm, d), lambda i, *_: (jnp.maximum(i - n_prompt_tiles, 0), 0))]


def _route_kernel(hp_ref, hs_ref, rwt_ref, bias_ref, eidx_ref, rank_ref, gate_ref, cnt_ref, *, n_experts,
                  n_prompt_tiles):
    i = pl.program_id(0)
    tm = hp_ref.shape[0]
    per_group = n_experts // N_EXPERT_GROUPS
    shape3 = (N_EXPERT_GROUPS, per_group, tm)

    h = jnp.where(i < n_prompt_tiles, hp_ref[...], hs_ref[...])
    logits = lax.dot_general(rwt_ref[...], h.astype(BF16), (((1,), (1,)), ((), ())),
                             preferred_element_type=F32)
    scores = jax.nn.sigmoid(logits)
    sel = scores + bias_ref[...]
    scores3 = scores.reshape(shape3)
    grp = sel.reshape(shape3)
    member = lax.broadcasted_iota(jnp.int32, shape3, 1).astype(F32)
    group = lax.broadcasted_iota(jnp.int32, shape3, 0).astype(F32)
    expert = group * per_group + member

    m1 = jnp.max(grp, axis=1, keepdims=True)
    f1 = jnp.min(jnp.where(grp == m1, member, float(per_group)), axis=1, keepdims=True)
    m2 = jnp.max(jnp.where(member == f1, NEG_INF, grp), axis=1, keepdims=True)
    gscore = m1 + m2

    gid = lax.broadcasted_iota(jnp.int32, gscore.shape, 0).astype(F32)
    gmask = jnp.zeros(gscore.shape, F32)
    cur = gscore
    for _ in range(TOPK_GROUPS):
        mx = jnp.max(cur, axis=0, keepdims=True)
        pick = gid == jnp.min(jnp.where(cur == mx, gid, float(N_EXPERT_GROUPS)), axis=0, keepdims=True)
        gmask = jnp.where(pick, 1.0, gmask)
        cur = jnp.where(pick, NEG_INF, cur)

    cand = jnp.where(gmask > 0.0, grp, NEG_INF)
    chosen = jnp.zeros(shape3, F32)
    picks, firsts, weights = [], [], []
    for _ in range(TOP_K):
        mx = _over_experts(jnp.max, cand)
        first = _over_experts(jnp.min, jnp.where(cand == mx, expert, float(n_experts)))
        pick = expert == first
        picks.append(pick)
        firsts.append(first)
        weights.append(_over_experts(jnp.sum, jnp.where(pick, scores3, 0.0)))
        chosen = jnp.where(pick, 1.0, chosen)
        cand = jnp.where(pick, NEG_INF, cand)
    wsum = weights[0]
    for w in weights[1:]:
        wsum = wsum + w

    chosen_b = chosen.reshape(n_experts, tm).astype(BF16)
    t_from = lax.broadcasted_iota(jnp.int32, (tm, tm), 0)
    t_to = lax.broadcasted_iota(jnp.int32, (tm, tm), 1)
    earlier = ((t_from < t_to) & (t_from // MOE_TILE == t_to // MOE_TILE)).astype(BF16)
    before3 = jnp.dot(chosen_b, earlier, preferred_element_type=F32).reshape(shape3)

    pad = SUBLANES - TOP_K
    eidx = [f.reshape(1, tm).astype(jnp.int32) for f in firsts]
    rank = [_over_experts(jnp.sum, jnp.where(p, before3, 0.0)).reshape(1, tm).astype(jnp.int32) for p in picks]
    gate = [(w / wsum * ROUTED_SCALE).reshape(1, tm) for w in weights]
    eidx_ref[...] = jnp.concatenate(eidx + [jnp.zeros((pad, tm), jnp.int32)], axis=0)
    rank_ref[...] = jnp.concatenate(rank + [jnp.zeros((pad, tm), jnp.int32)], axis=0)
    gate_ref[...] = jnp.concatenate(gate + [jnp.zeros((pad, tm), F32)], axis=0)

    for sub in range(tm // MOE_TILE):
        cnt_ref[sub * SUBLANES:(sub + 1) * SUBLANES, :] = lax.dot_general(
            jnp.ones((SUBLANES, MOE_TILE), BF16), chosen_b[:, sub * MOE_TILE:(sub + 1) * MOE_TILE],
            (((1,), (1,)), ((), ())), preferred_element_type=F32)


def _route(h_p, h_s, rwt_b, bias_col):
    d = h_p.shape[1]
    t_all = h_p.shape[0] + h_s.shape[0]
    n_experts = rwt_b.shape[0]
    tm = ROUTE_TILE
    npt = h_p.shape[0] // tm
    moe_tiles = tm // MOE_TILE
    row_spec = pl.BlockSpec((SUBLANES, tm), lambda i: (0, i))
    return pl.pallas_call(
        functools.partial(_route_kernel, n_experts=n_experts, n_prompt_tiles=npt),
        grid=(t_all // tm,),
        in_specs=_two_group_specs(tm, d, npt) + [
            pl.BlockSpec((n_experts, d), lambda i: (0, 0)),
            pl.BlockSpec((n_experts, 1), lambda i: (0, 0)),
        ],
        out_specs=[row_spec, row_spec, row_spec,
                   pl.BlockSpec((moe_tiles * SUBLANES, n_experts), lambda i: (i, 0))],
        out_shape=[
            jax.ShapeDtypeStruct((SUBLANES, t_all), jnp.int32),
            jax.ShapeDtypeStruct((SUBLANES, t_all), jnp.int32),
            jax.ShapeDtypeStruct((SUBLANES, t_all), F32),
            jax.ShapeDtypeStruct((t_all // MOE_TILE * SUBLANES, n_experts), F32),
        ],
        compiler_params=pltpu.CompilerParams(dimension_semantics=("arbitrary",),
                                             vmem_limit_bytes=VMEM_LIMIT_BYTES),
        name="route",
    )(h_p, h_s, rwt_b, bias_col)


def _for_each_part(rows, max_rows, fn):
    for size in [SUBLANES << b for b in range((max_rows // SUBLANES).bit_length())]:
        @pl.when((rows & size) != 0)
        def _(size=size):
            fn(pl.multiple_of(rows & (size - 1), SUBLANES), size)


def _start_run_copies(tile, cnt_ref, ls_ref, gs_ref, big_ref, n_experts, start_copy):
    def run(e):
        idx = tile * n_experts + e
        return cnt_ref[idx], ls_ref[idx], gs_ref[idx]

    for e in range(n_experts):
        rows, l0, g0 = run(e)
        _for_each_part(rows, RUN_CHUNK // 2, lambda off, size: start_copy(
            pl.multiple_of(l0 + off, SUBLANES), pl.multiple_of(g0 + off, SUBLANES), size))

    @pl.when(big_ref[tile] != 0)
    def _():
        def per_expert(e, c):
            rows, l0, g0 = run(e)
            base = rows & (RUN_CHUNK - 1)

            def chunk(j, c2):
                off = base + j * RUN_CHUNK
                start_copy(pl.multiple_of(l0 + off, SUBLANES), pl.multiple_of(g0 + off, SUBLANES), RUN_CHUNK)
                return c2

            lax.fori_loop(0, lax.shift_right_logical(rows, RUN_CHUNK.bit_length() - 1), chunk, 0)
            return c

        lax.fori_loop(0, n_experts, per_expert, 0)


def _top_bit(n):
    return 1 << (n.bit_length() - 1)


def _pack_rows(x):
    rows, two_w = x.shape
    w = two_w // 2
    x3 = x.reshape(rows // SUBLANES, SUBLANES, two_w)
    halves = jnp.concatenate([x3[:, :, :w], x3[:, :, w:]], axis=1).astype(BF16)
    return pltpu.bitcast(halves, jnp.uint32).reshape(rows, w)


def _unpack_rows(p):
    rows, w = p.shape
    halves = pltpu.bitcast(p.reshape(rows // SUBLANES, SUBLANES, w), BF16).astype(F32)
    return jnp.concatenate([halves[:, :SUBLANES, :], halves[:, SUBLANES:, :]], axis=-1).reshape(rows, 2 * w)


def _local_rows(tm, n_experts):
    n = TOP_K * tm + n_experts * (SUBLANES - 1)
    return -(-n // SORT_CHUNK) * SORT_CHUNK


def _dispatch_kernel(cnt_ref, ls_ref, gs_ref, tot_ref, big_ref, zrow_ref, lp_ref, hp_ref, hs_ref, xs_hbm,
                     lbuf, zero_s, sem, *, n_experts, blk, n_prompt_tiles):
    i = pl.program_id(0)
    tm, d = hp_ref.shape
    n = TOP_K * tm
    nl = lbuf.shape[1]
    slot = i % 2

    def tile_wait(tile):
        pltpu.make_async_copy(lbuf.at[0, pl.ds(0, n)], xs_hbm.at[pl.ds(0, n)], sem).wait()
        _for_each_part(tot_ref[tile] - n, _top_bit(n_experts * (SUBLANES - 1)), lambda off, size:
                       pltpu.make_async_copy(lbuf.at[0, pl.ds(0, size)], xs_hbm.at[pl.ds(0, size)], sem).wait())

    @pl.when(i == 0)
    def _():
        zero_s[...] = jnp.zeros(zero_s.shape, zero_s.dtype)

        def zcopy(e):
            return pltpu.make_async_copy(zero_s, xs_hbm.at[pl.ds(pl.multiple_of(zrow_ref[e], blk), blk)], sem)

        def start(e, c):
            @pl.when(zrow_ref[e] >= 0)
            def _():
                zcopy(e).start()
            return c

        def wait(e, c):
            @pl.when(zrow_ref[e] >= 0)
            def _():
                zcopy(e).wait()
            return c

        lax.fori_loop(0, n_experts, start, 0)
        lax.fori_loop(0, n_experts, wait, 0)

    hb = jnp.where(i < n_prompt_tiles, hp_ref[...], hs_ref[...]).astype(BF16)
    lp = lp_ref[...]
    def sort_chunk(r):
        rows = r * SORT_CHUNK + lax.broadcasted_iota(jnp.int32, (SORT_CHUNK, tm), 0)
        hit = jnp.zeros((SORT_CHUNK, tm), F32)
        for k in range(TOP_K):
            hit = jnp.where(rows == lp[k:k + 1, :], 1.0, hit)
        lbuf[slot, pl.ds(r * SORT_CHUNK, SORT_CHUNK), :] = _pack_rows(
            jnp.dot(hit.astype(BF16), hb, preferred_element_type=F32))

    last_chunk = nl // SORT_CHUNK - 1
    for r in range(last_chunk):
        sort_chunk(r)
    pl.when(tot_ref[i] > last_chunk * SORT_CHUNK)(functools.partial(sort_chunk, last_chunk))

    @pl.when(i > 0)
    def _():
        tile_wait(i - 1)

    _start_run_copies(i, cnt_ref, ls_ref, gs_ref, big_ref, n_experts, lambda lrow, grow, size: pltpu.make_async_copy(
        lbuf.at[slot, pl.ds(lrow, size)], xs_hbm.at[pl.ds(grow, size)], sem).start())

    @pl.when(i == pl.num_programs(0) - 1)
    def _():
        tile_wait(i)


def _dispatch(tables, zrow, lp, h_p, h_s, *, n_rows, blk):
    d = h_p.shape[1]
    t_all = h_p.shape[0] + h_s.shape[0]
    tm = MOE_TILE
    npt = h_p.shape[0] // tm
    n_experts = zrow.shape[0]
    return pl.pallas_call(
        functools.partial(_dispatch_kernel, n_experts=n_experts, blk=blk, n_prompt_tiles=npt),
        grid_spec=pltpu.PrefetchScalarGridSpec(
            num_scalar_prefetch=6,
            grid=(t_all // tm,),
            in_specs=[pl.BlockSpec((SUBLANES, tm), lambda i, *_: (0, i))] + _two_group_specs(tm, d, npt),
            out_specs=pl.BlockSpec(memory_space=pl.ANY),
            scratch_shapes=[
                pltpu.VMEM((2, _local_rows(tm, n_experts), d // 2), jnp.uint32),
                pltpu.VMEM((blk, d // 2), jnp.uint32),
                pltpu.SemaphoreType.DMA,
            ],
        ),
        out_shape=jax.ShapeDtypeStruct((n_rows, d // 2), jnp.uint32),
        compiler_params=pltpu.CompilerParams(dimension_semantics=("arbitrary",),
                                             vmem_limit_bytes=VMEM_LIMIT_BYTES),
        name="dispatch",
    )(*tables, zrow, lp, h_p, h_s)


def _silu(x):
    return x * jax.nn.sigmoid(x)


def _expert_kernel(be_ref, nact_ref, xs_ref, wg_ref, wu_ref, wd_ref, ys_ref, wg_s, wu_s, wd_s):
    b = pl.program_id(0)

    @pl.when(b < nact_ref[0])
    def _():
        @pl.when((b == 0) | (be_ref[b] != be_ref[jnp.maximum(b - 1, 0)]))
        def _():
            wg_s[...] = wg_ref[...].astype(BF16)
            wu_s[...] = wu_ref[...].astype(BF16)
            wd_s[...] = wd_ref[...].astype(BF16)

        xb = _unpack_rows(xs_ref[...]).astype(BF16)
        hid = _silu(jnp.dot(xb, wg_s[...], preferred_element_type=F32)) * jnp.dot(
            xb, wu_s[...], preferred_element_type=F32)
        ys_ref[...] = _pack_rows(jnp.dot(hid.astype(BF16), wd_s[...], preferred_element_type=F32))


def _experts(block_expert, nact, xs, w_gate, w_up, w_down, *, blk):
    n_rows, dw = xs.shape
    d = w_gate.shape[-2]
    ff = w_gate.shape[-1]
    n_blocks = n_rows // blk

    def active(b, be, na):
        return jnp.minimum(b, na[0] - 1)

    return pl.pallas_call(
        _expert_kernel,
        grid_spec=pltpu.PrefetchScalarGridSpec(
            num_scalar_prefetch=2,
            grid=(n_blocks,),
            in_specs=[
                pl.BlockSpec((blk, dw), lambda b, be, na: (active(b, be, na), 0)),
                pl.BlockSpec((None, d, ff), lambda b, be, na: (be[active(b, be, na)], 0, 0)),
                pl.BlockSpec((None, d, ff), lambda b, be, na: (be[active(b, be, na)], 0, 0)),
                pl.BlockSpec((None, ff, d), lambda b, be, na: (be[active(b, be, na)], 0, 0)),
            ],
            out_specs=pl.BlockSpec((blk, dw), lambda b, be, na: (active(b, be, na), 0)),
            scratch_shapes=[
                pltpu.VMEM((d, ff), BF16),
                pltpu.VMEM((d, ff), BF16),
                pltpu.VMEM((ff, d), BF16),
            ],
        ),
        out_shape=jax.ShapeDtypeStruct((n_rows, dw), jnp.uint32),
        compiler_params=pltpu.CompilerParams(dimension_semantics=("arbitrary",),
                                             vmem_limit_bytes=VMEM_LIMIT_BYTES),
        name="experts",
    )(block_expert, nact, xs, w_gate, w_up, w_down)


def _combine_kernel(cnt_ref, ls_ref, gs_ref, tot_ref, big_ref, ys_hbm, lpt_ref, gate_ref, hp_ref, hs_ref, wsg_ref, wsu_ref,
                    wsd_ref, lng_ref, lnb_ref, yp_ref, ysm_ref, ybuf, moe_s, sems, *, alpha, n_experts,
                    n_prompt_tiles):
    i = pl.program_id(0)
    tm, d = hp_ref.shape
    n = TOP_K * tm
    nl = ybuf.shape[1]
    slot = i % 2

    def fetch(tile, to):
        _start_run_copies(tile, cnt_ref, ls_ref, gs_ref, big_ref, n_experts, lambda lrow, grow, size:
                          pltpu.make_async_copy(ys_hbm.at[pl.ds(grow, size)], ybuf.at[to, pl.ds(lrow, size)],
                                                sems.at[to]).start())

    @pl.when(i == 0)
    def _():
        ybuf[...] = jnp.zeros(ybuf.shape, ybuf.dtype)
        fetch(0, 0)

    @pl.when(i + 1 < pl.num_programs(0))
    def _():
        fetch(i + 1, 1 - slot)

    h = jnp.where(i < n_prompt_tiles, hp_ref[...], hs_ref[...])
    hb = h.astype(BF16)
    hid = _silu(jnp.dot(hb, wsg_ref[...], preferred_element_type=F32)) * jnp.dot(
        hb, wsu_ref[...], preferred_element_type=F32)
    shared = jnp.dot(hid.astype(BF16), wsd_ref[...], preferred_element_type=F32)

    pltpu.make_async_copy(ys_hbm.at[pl.ds(0, n)], ybuf.at[slot, pl.ds(0, n)], sems.at[slot]).wait()
    _for_each_part(tot_ref[i] - n, _top_bit(n_experts * (SUBLANES - 1)), lambda off, size: pltpu.make_async_copy(
        ys_hbm.at[pl.ds(0, size)], ybuf.at[slot, pl.ds(0, size)], sems.at[slot]).wait())

    lpt = lpt_ref[...]
    gate = gate_ref[...]

    def chunk_sum(r):
        cols = r * SORT_CHUNK + lax.broadcasted_iota(jnp.int32, (tm, SORT_CHUNK), 1)
        g = jnp.zeros((tm, SORT_CHUNK), F32)
        for k in range(TOP_K):
            g = jnp.where(cols == lpt[:, k:k + 1], gate[:, k:k + 1], g)
        yb = _unpack_rows(ybuf[slot, pl.ds(r * SORT_CHUNK, SORT_CHUNK), :]).astype(BF16)
        return jnp.dot(g.astype(BF16), yb, preferred_element_type=F32)

    last_chunk = nl // SORT_CHUNK - 1
    moe = shared
    for r in range(last_chunk):
        moe = moe + chunk_sum(r)
    moe_s[...] = moe

    @pl.when(tot_ref[i] > last_chunk * SORT_CHUNK)
    def _():
        moe_s[...] += chunk_sum(last_chunk)

    y = _layer_norm(alpha * h + moe_s[...], lng_ref[...], lnb_ref[...])

    @pl.when(i < n_prompt_tiles)
    def _():
        yp_ref[...] = y

    @pl.when(i >= n_prompt_tiles)
    def _():
        ysm_ref[...] = y


def _combine(tables, ys, lp_t, gates_t, h_p, h_s, wsg_b, wsu_b, wsd_b, ln_g, ln_b, *, alpha):
    t_prompt, d = h_p.shape
    t_all = t_prompt + h_s.shape[0]
    tm = MOE_TILE
    ff = wsg_b.shape[-1]
    npt = t_prompt // tm
    n_experts = tables[0].shape[0] // (t_all // tm)
    const2 = lambda i, *_: (0, 0)
    return pl.pallas_call(
        functools.partial(_combine_kernel, alpha=alpha, n_experts=n_experts, n_prompt_tiles=npt),
        grid_spec=pltpu.PrefetchScalarGridSpec(
            num_scalar_prefetch=5,
            grid=(t_all // tm,),
            in_specs=[
                pl.BlockSpec(memory_space=pl.ANY),
                pl.BlockSpec((tm, SUBLANES), lambda i, *_: (i, 0)),
                pl.BlockSpec((tm, SUBLANES), lambda i, *_: (i, 0)),
                *_two_group_specs(tm, d, npt),
                pl.BlockSpec((d, ff), const2),
                pl.BlockSpec((d, ff), const2),
                pl.BlockSpec((ff, d), const2),
                pl.BlockSpec((1, d), const2),
                pl.BlockSpec((1, d), const2),
            ],
            out_specs=[
                pl.BlockSpec((tm, d), lambda i, *_: (jnp.minimum(i, npt - 1), 0)),
                pl.BlockSpec((tm, d), lambda i, *_: (jnp.maximum(i - npt, 0), 0)),
            ],
            scratch_shapes=[
                pltpu.VMEM((2, _local_rows(tm, n_experts), d // 2), jnp.uint32),
                pltpu.VMEM((tm, d), F32),
                pltpu.SemaphoreType.DMA((2,)),
            ],
        ),
        out_shape=[
            jax.ShapeDtypeStruct((t_prompt, d), F32),
            jax.ShapeDtypeStruct((t_all - t_prompt, d), F32),
        ],
        compiler_params=pltpu.CompilerParams(dimension_semantics=("arbitrary",),
                                             vmem_limit_bytes=VMEM_LIMIT_BYTES),
        name="combine",
    )(*tables, ys, lp_t, gates_t, h_p, h_s, wsg_b, wsu_b, wsd_b, ln_g, ln_b)


def _moe(h_p, h_s, router_w, router_bias, w_gate, w_up, w_down, ws_gate, ws_up, ws_down, ln_g, ln_b, *, alpha):
    t_all = h_p.shape[0] + h_s.shape[0]
    n_experts = router_w.shape[-1]
    blk = EXPERT_BLOCK
    tm = MOE_TILE
    nt = t_all // tm
    eidx8, lrank8, gate8, cnt8 = _route(h_p, h_s, router_w.T.astype(BF16), router_bias.reshape(n_experts, 1))

    cnt = cnt8.reshape(nt, SUBLANES, n_experts)[:, 0, :].astype(jnp.int32)
    cnt = (cnt + SUBLANES - 1) // SUBLANES * SUBLANES
    counts = jnp.sum(cnt, axis=0)
    blocks_per_e = (counts + blk - 1) // blk
    block_end = jnp.cumsum(blocks_per_e)
    pad_start = (block_end - blocks_per_e) * blk
    n_blocks = -(-(t_all * TOP_K + nt * n_experts * (SUBLANES - 1)) // blk) + n_experts
    block_expert = jnp.minimum(jnp.sum(block_end[None, :] <= jnp.arange(n_blocks)[:, None], axis=1),
                               n_experts - 1).astype(jnp.int32)
    nact = block_end[-1:].astype(jnp.int32)
    zrow = jnp.where(blocks_per_e > 0, (block_end - 1) * blk, -1).astype(jnp.int32)
    gstart = pad_start[None, :] + jnp.cumsum(cnt, axis=0) - cnt
    lstart = jnp.cumsum(cnt, axis=1) - cnt
    tables = tuple(a.reshape(-1).astype(jnp.int32) for a in (
        cnt, lstart, gstart, jnp.sum(cnt, axis=1), jnp.max(cnt, axis=1) >= RUN_CHUNK))
    lstart_tok = jnp.repeat(lstart, tm, axis=0)
    lp8 = jnp.sum(jnp.where(eidx8[..., None] == jnp.arange(n_experts), lstart_tok[None], 0), axis=-1) + lrank8
    lp8 = lp8.astype(jnp.int32)

    xs = _dispatch(tables, zrow, lp8, h_p, h_s, n_rows=n_blocks * blk, blk=blk)
    ys = _experts(block_expert, nact, xs, w_gate, w_up, w_down, blk=blk)
    return _combine(tables, ys, lp8.T, gate8.T, h_p, h_s, ws_gate.astype(BF16), ws_up.astype(BF16), ws_down.astype(BF16),
                    ln_g, ln_b, alpha=alpha)


def kernel(x_prompt, x_sample, cache_k, cache_v, state_conv, w_in, attn_sinks, conv_w, g_attn_out, g_conv_out, w_out, ln1_g, ln1_b, router_w, router_bias, w_gate, w_up, w_down, ws_gate, ws_up, ws_down, ln2_g, ln2_b):
    depth = w_in.shape[0]
    bsz, seq, d = x_prompt.shape
    dec_b, dec_seq, _ = x_sample.shape
    win = cache_k.shape[2]
    kv_w = N_KV_HEADS * HEAD_DIM
    t_prompt = bsz * seq
    t_all = t_prompt + dec_b * dec_seq
    alpha = (2.0 * depth) ** 0.25
    assert win == WINDOW and seq % PROMPT_TILE == 0 and dec_b % SAMPLE_SEQS == 0
    assert t_prompt % ROUTE_TILE == 0 and (t_all - t_prompt) % ROUTE_TILE == 0 and ROUTE_TILE % MOE_TILE == 0
    assert MOE_TILE & (MOE_TILE - 1) == 0 and (TOP_K * MOE_TILE) % SORT_CHUNK == 0

    tabs_p = _rope_tables(jnp.arange(seq))
    tabs_s = tuple(jnp.tile(t, (SAMPLE_SEQS, 1)) for t in _rope_tables(PAST_LEN + jnp.arange(dec_seq)))
    row = lambda a: a.reshape(1, -1)

    xp, xs = x_prompt, x_sample
    outs = [[] for _ in range(6)]
    for l in range(depth):
        win_b, wout_b = w_in[l].astype(BF16), w_out[l].astype(BF16)
        shared = (attn_sinks[l], conv_w[l], row(g_attn_out[l]), row(g_conv_out[l]), wout_b, row(ln1_g[l]),
                  row(ln1_b[l]))
        h_p, kp, vp, cp = _prompt_mixer(xp, win_b, tabs_p, *shared, alpha=alpha)
        h_s, kn, vn, cn = _sample_mixer(xs, cache_k[l].reshape(dec_b, win, kv_w),
                                        cache_v[l].reshape(dec_b, win, kv_w), state_conv[l], win_b, tabs_s,
                                        *shared, alpha=alpha)
        yp, ys = _moe(h_p, h_s, router_w[l], router_bias[l], w_gate[l], w_up[l], w_down[l], ws_gate[l], ws_up[l],
                      ws_down[l], row(ln2_g[l]), row(ln2_b[l]), alpha=alpha)
        xp, xs = yp.reshape(bsz, seq, d), ys.reshape(dec_b, dec_seq, d)
        heads = lambda a: a.reshape(a.shape[0], win, N_KV_HEADS, HEAD_DIM)
        for o, a in zip(outs, (heads(kp), heads(vp), cp, heads(kn), heads(vn), cn)):
            o.append(a)
    return (xp, xs) + tuple(jnp.stack(o, axis=0) for o in outs)
```

```python
import functools

import jax
import jax.numpy as jnp
from jax import lax
from jax.experimental import pallas as pl
from jax.experimental.pallas import tpu as pltpu

PAST_LEN = 16384
WINDOW = 128
HEAD_DIM = 64
N_KV_HEADS = 2
ROT_DIM = HEAD_DIM // 4
ROPE_THETA = 500000.0
CONV_K = 3
TOP_K = 6
N_EXPERT_GROUPS = 8
TOPK_GROUPS = 4
ROUTED_SCALE = 2.5
LN_EPS = 1e-5
RMS_EPS = 1e-6

LANES = 128
SUBLANES = 8
VMEM_LIMIT_BYTES = 56 * 1024 * 1024

PROMPT_TILE = 512
SAMPLE_SEQS = 32
MOE_TILE = 256
SORT_CHUNK = 256
EXPERT_BLOCK = 2048
ROUTE_TILE = 512
RUN_CHUNK = 64

F32 = jnp.float32
BF16 = jnp.bfloat16
NEG_INF = float("-inf")


def _rope_tables(positions):
    half = ROT_DIM // 2
    inv_freq = ROPE_THETA ** (-jnp.arange(0, ROT_DIM, 2, dtype=F32) / ROT_DIM)
    ang = positions.astype(F32)[:, None] * inv_freq[None, :]
    cos, sin = jnp.cos(ang), jnp.sin(ang)
    n = positions.shape[0]
    rest = HEAD_DIM - ROT_DIM
    c = jnp.concatenate([cos, cos, jnp.ones((n, rest), F32)], axis=-1)
    sa = jnp.concatenate([-sin, jnp.zeros((n, half + rest), F32)], axis=-1)
    sb = jnp.concatenate([jnp.zeros((n, half), F32), sin, jnp.zeros((n, rest), F32)], axis=-1)
    reps = LANES // HEAD_DIM
    return jnp.tile(c, (1, reps)), jnp.tile(sa, (1, reps)), jnp.tile(sb, (1, reps))


def _rope(x, c, sa, sb):
    half = ROT_DIM // 2
    return x * c + pltpu.roll(x, LANES - half, 1) * sa + pltpu.roll(x, half, 1) * sb


def _rms_norm(x, g):
    return x * lax.rsqrt(jnp.mean(jnp.square(x), axis=-1, keepdims=True) + RMS_EPS) * g


def _layer_norm(x, g, b):
    mu = jnp.mean(x, axis=-1, keepdims=True)
    var = jnp.mean(jnp.square(x - mu), axis=-1, keepdims=True)
    return (x - mu) * lax.rsqrt(var + LN_EPS) * g + b


def _short_conv(gated, prev2, prev1, row, conv_w, b_gate):
    g1 = pltpu.roll(gated, 1, 0)
    g2 = pltpu.roll(gated, 2, 0)
    g1 = jnp.where(row == 0, prev1, g1)
    g2 = jnp.where(row == 0, prev2, jnp.where(row == 1, prev1, g2))
    y = conv_w[0:1, :] * g2 + conv_w[1:2, :] * g1 + conv_w[2:3, :] * gated
    return b_gate * y


def _merge_norm(x, attn_o, conv_o, gattn, gconv, wout_ref, ln_g, ln_b, alpha):
    cat = jnp.concatenate([_rms_norm(attn_o, gattn), _rms_norm(conv_o, gconv)], axis=-1)
    mix = jnp.dot(cat.astype(BF16), wout_ref[...], preferred_element_type=F32)
    return _layer_norm(alpha * x + mix, ln_g, ln_b)


def _sink_softmax(parts, sink):
    m = sink
    for s in parts:
        m = jnp.maximum(m, jnp.max(s, axis=-1, keepdims=True))
    es = [jnp.exp(s - m) for s in parts]
    den = jnp.exp(sink - m)
    for e in es:
        den = den + jnp.sum(e, axis=-1, keepdims=True)
    return [(e / den).astype(BF16) for e in es]


def _prompt_mixer_kernel(x_ref, win_ref, c_ref, sa_ref, sb_ref, sinks_ref, convw_ref, gattn_ref, gconv_ref,
                         wout_ref, lng_ref, lnb_ref,
                         h_ref, ko_ref, vo_ref, co_ref,
                         q_s, kt_s, vm_s, o_s, gc_s, *, alpha, n_heads, attn_w, kv_w, conv_ch):
    s = pl.program_id(1)
    last = pl.num_programs(1) - 1
    tq = x_ref.shape[0]
    q_per_kv = n_heads // N_KV_HEADS
    heads_per_group = LANES // HEAD_DIM

    @pl.when(s == 0)
    def _():
        kt_s[:, :, 0:WINDOW] = jnp.zeros((kt_s.shape[0], LANES, WINDOW), BF16)
        vm_s[:, 0:WINDOW, :] = jnp.zeros((vm_s.shape[0], WINDOW, LANES), BF16)
        gc_s[...] = jnp.zeros(gc_s.shape, F32)

    @pl.when(s > 0)
    def _():
        kt_s[:, :, 0:WINDOW] = kt_s[:, :, tq:tq + WINDOW]
        vm_s[:, 0:WINDOW, :] = vm_s[:, tq:tq + WINDOW, :]

    x = x_ref[...]
    xb = x.astype(BF16)
    c, sa, sb = c_ref[...], sa_ref[...], sb_ref[...]

    def proj(lo, width):
        return jnp.dot(xb, win_ref[:, lo:lo + width], preferred_element_type=F32)

    scale = HEAD_DIM ** -0.5
    for j in range(attn_w // LANES):
        qj = _rope(proj(j * LANES, LANES), c, sa, sb)
        q_s[:, j * LANES:(j + 1) * LANES] = (qj * scale).astype(BF16)
    k = _rope(proj(attn_w, kv_w), c, sa, sb)
    v = proj(attn_w + kv_w, kv_w)

    @pl.when(s == last)
    def _():
        ko_ref[...] = k[tq - WINDOW:, :]
        vo_ref[...] = v[tq - WINDOW:, :]

    kt = k.T.astype(BF16)
    zeros_k = jnp.zeros((HEAD_DIM, tq), BF16)
    v_swapped = pltpu.roll(v, HEAD_DIM, 1)
    low_lanes = lax.broadcasted_iota(jnp.int32, (tq, LANES), 1) < HEAD_DIM
    for kvh in range(N_KV_HEADS):
        kt_h = kt[kvh * HEAD_DIM:(kvh + 1) * HEAD_DIM, :]
        v_lo = v if kvh == 0 else v_swapped
        v_hi = v_swapped if kvh == 0 else v
        kt_s[2 * kvh, :, WINDOW:] = jnp.concatenate([kt_h, zeros_k], axis=0)
        kt_s[2 * kvh + 1, :, WINDOW:] = jnp.concatenate([zeros_k, kt_h], axis=0)
        vm_s[2 * kvh, WINDOW:, :] = jnp.where(low_lanes, v_lo, 0.0).astype(BF16)
        vm_s[2 * kvh + 1, WINDOW:, :] = jnp.where(low_lanes, 0.0, v_hi).astype(BF16)

    qi = lax.broadcasted_iota(jnp.int32, (WINDOW, 2 * WINDOW), 0)
    ci = lax.broadcasted_iota(jnp.int32, (WINDOW, 2 * WINDOW), 1)
    band = (ci > qi) & (ci <= qi + WINDOW)
    for j in range(tq // WINDOW):
        r0 = j * WINDOW
        mask = band if j > 0 else band & ((ci >= WINDOW) | (s > 0))
        for grp in range(attn_w // LANES):
            q_grp = q_s[r0:r0 + WINDOW, grp * LANES:(grp + 1) * LANES]
            out = None
            for r in range(heads_per_group):
                hd = grp * heads_per_group + r
                src = 2 * (hd // q_per_kv) + r
                sink = sinks_ref[hd]
                sc = jnp.dot(q_grp, kt_s[src, :, r0:r0 + 2 * WINDOW], preferred_element_type=F32)
                sc = jnp.where(mask, sc, NEG_INF)
                m = jnp.maximum(jnp.max(sc, axis=-1, keepdims=True), sink)
                e = jnp.exp(sc - m)
                den = jnp.sum(e, axis=-1, keepdims=True) + jnp.exp(sink - m)
                o_h = jnp.dot(e.astype(BF16), vm_s[src, r0:r0 + 2 * WINDOW, :],
                              preferred_element_type=F32) * (1.0 / den)
                out = o_h if out is None else out + o_h
            o_s[r0:r0 + WINDOW, grp * LANES:(grp + 1) * LANES] = out

    o3 = attn_w + 2 * kv_w
    gated = proj(o3 + 2 * conv_ch, conv_ch) * proj(o3, conv_ch)
    row = lax.broadcasted_iota(jnp.int32, (tq, 1), 0)
    conv_o = _short_conv(gated, gc_s[0:1, :], gc_s[1:2, :], row, convw_ref[...], proj(o3 + conv_ch, conv_ch))
    gc_s[0:CONV_K - 1, :] = gated[tq - (CONV_K - 1):, :]

    @pl.when(s == last)
    def _():
        co_ref[...] = gated[tq - (CONV_K - 1):, :]

    h_ref[...] = _merge_norm(x, o_s[...], conv_o, gattn_ref[...], gconv_ref[...], wout_ref,
                             lng_ref[...], lnb_ref[...], alpha)


def _prompt_mixer(x, win_b, tabs, sinks, conv_w, g_attn, g_conv, wout_b, ln_g, ln_b, *, alpha):
    bsz, seq, d = x.shape
    tq = PROMPT_TILE
    ns = seq // tq
    attn_w = g_attn.shape[-1]
    conv_ch = g_conv.shape[-1]
    n_heads = attn_w // HEAD_DIM
    kv_w = N_KV_HEADS * HEAD_DIM
    assert kv_w == LANES and 2 * HEAD_DIM == LANES and (n_heads // N_KV_HEADS) % 2 == 0 and WINDOW == LANES
    in_cols = win_b.shape[-1]
    const2 = lambda b, s: (0, 0)
    kern = functools.partial(_prompt_mixer_kernel, alpha=alpha, n_heads=n_heads, attn_w=attn_w, kv_w=kv_w,
                             conv_ch=conv_ch)
    return pl.pallas_call(
        kern,
        grid=(bsz, ns),
        in_specs=[
            pl.BlockSpec((None, tq, d), lambda b, s: (b, s, 0)),
            pl.BlockSpec((d, in_cols), const2),
            pl.BlockSpec((tq, LANES), lambda b, s: (s, 0)),
            pl.BlockSpec((tq, LANES), lambda b, s: (s, 0)),
            pl.BlockSpec((tq, LANES), lambda b, s: (s, 0)),
            pl.BlockSpec(memory_space=pltpu.SMEM),
            pl.BlockSpec((CONV_K, conv_ch), const2),
            pl.BlockSpec((1, attn_w), const2),
            pl.BlockSpec((1, conv_ch), const2),
            pl.BlockSpec((attn_w + conv_ch, d), const2),
            pl.BlockSpec((1, d), const2),
            pl.BlockSpec((1, d), const2),
        ],
        out_specs=[
            pl.BlockSpec((tq, d), lambda b, s: (b * ns + s, 0)),
            pl.BlockSpec((None, WINDOW, kv_w), lambda b, s: (b, 0, 0)),
            pl.BlockSpec((None, WINDOW, kv_w), lambda b, s: (b, 0, 0)),
            pl.BlockSpec((None, CONV_K - 1, conv_ch), lambda b, s: (b, 0, 0)),
        ],
        out_shape=[
            jax.ShapeDtypeStruct((bsz * seq, d), F32),
            jax.ShapeDtypeStruct((bsz, WINDOW, kv_w), F32),
            jax.ShapeDtypeStruct((bsz, WINDOW, kv_w), F32),
            jax.ShapeDtypeStruct((bsz, CONV_K - 1, conv_ch), F32),
        ],
        scratch_shapes=[
            pltpu.VMEM((tq, attn_w), BF16),
            pltpu.VMEM((2 * N_KV_HEADS, LANES, WINDOW + tq), BF16),
            pltpu.VMEM((2 * N_KV_HEADS, WINDOW + tq, LANES), BF16),
            pltpu.VMEM((tq, attn_w), F32),
            pltpu.VMEM((SUBLANES, conv_ch), F32),
        ],
        compiler_params=pltpu.CompilerParams(dimension_semantics=("arbitrary", "arbitrary"),
                                             vmem_limit_bytes=VMEM_LIMIT_BYTES),
        name="prompt_mixer",
    )(x, win_b, *tabs, sinks, conv_w, g_attn, g_conv, wout_b, ln_g, ln_b)


def _sample_mixer_kernel(x_ref, ck_ref, cv_ref, st_ref, win_ref, c_ref, sa_ref, sb_ref, sinks_ref,
                         convw_ref, gattn_ref, gconv_ref, wout_ref, lng_ref, lnb_ref,
                         h_ref, ko_ref, vo_ref, co_ref, *, alpha, n_heads, attn_w, kv_w, conv_ch, dec_seq):
    nb, win = ck_ref.shape[0], ck_ref.shape[1]
    rows = nb * dec_seq
    q_per_kv = n_heads // N_KV_HEADS
    x = x_ref[...]
    xb = x.astype(BF16)
    c, sa, sb = c_ref[...], sa_ref[...], sb_ref[...]

    def proj(lo, width):
        return jnp.dot(xb, win_ref[:, lo:lo + width], preferred_element_type=F32)

    scale = HEAD_DIM ** -0.5
    k = _rope(proj(attn_w, kv_w), c, sa, sb)
    v = proj(attn_w + kv_w, kv_w)
    k3 = k.reshape(nb, dec_seq, kv_w)
    v3 = v.reshape(nb, dec_seq, kv_w)
    ck = ck_ref[...]
    cv = cv_ref[...]
    ko_ref[:, 0:win - dec_seq, :] = ck[:, dec_seq:, :]
    ko_ref[:, win - dec_seq:, :] = k3
    vo_ref[:, 0:win - dec_seq, :] = cv[:, dec_seq:, :]
    vo_ref[:, win - dec_seq:, :] = v3
    ckb, cvb, k3b, v3b = ck.astype(BF16), cv.astype(BF16), k3.astype(BF16), v3.astype(BF16)

    qrows = q_per_kv * dec_seq
    qi = lax.broadcasted_iota(jnp.int32, (nb, qrows, win), 1) % dec_seq
    mask_c = lax.broadcasted_iota(jnp.int32, (nb, qrows, win), 2) > qi + (win - WINDOW)
    qn = lax.broadcasted_iota(jnp.int32, (nb, qrows, dec_seq), 1) % dec_seq
    mask_n = lax.broadcasted_iota(jnp.int32, (nb, qrows, dec_seq), 2) <= qn
    sink_row = lax.broadcasted_iota(jnp.int32, (nb, qrows, 1), 1) // dec_seq

    q_chunks = [_rope(proj(j * LANES, LANES), c, sa, sb) * scale for j in range(attn_w // LANES)]
    heads_out = []
    for kvh in range(N_KV_HEADS):
        qs = []
        for g in range(q_per_kv):
            lo = (kvh * q_per_kv + g) * HEAD_DIM
            qh = q_chunks[lo // LANES][:, lo % LANES:lo % LANES + HEAD_DIM]
            qs.append(qh.reshape(nb, dec_seq, HEAD_DIM))
        qg = jnp.concatenate(qs, axis=1).astype(BF16)
        sl = slice(kvh * HEAD_DIM, (kvh + 1) * HEAD_DIM)
        sc_c = jnp.einsum("bqd,bkd->bqk", qg, ckb[:, :, sl], preferred_element_type=F32)
        sc_n = jnp.einsum("bqd,bkd->bqk", qg, k3b[:, :, sl], preferred_element_type=F32)
        sc_c = jnp.where(mask_c, sc_c, NEG_INF)
        sc_n = jnp.where(mask_n, sc_n, NEG_INF)
        sink = jnp.zeros((nb, qrows, 1), F32)
        for g in range(q_per_kv):
            sink = jnp.where(sink_row == g, sinks_ref[kvh * q_per_kv + g], sink)
        p_c, p_n = _sink_softmax([sc_c, sc_n], sink)
        og = (jnp.einsum("bqk,bkd->bqd", p_c, cvb[:, :, sl], preferred_element_type=F32)
              + jnp.einsum("bqk,bkd->bqd", p_n, v3b[:, :, sl], preferred_element_type=F32))
        for g in range(q_per_kv):
            heads_out.append(og[:, g * dec_seq:(g + 1) * dec_seq, :].reshape(rows, HEAD_DIM))
    attn_o = jnp.concatenate(heads_out, axis=-1)

    o3 = attn_w + 2 * kv_w
    gated = proj(o3 + 2 * conv_ch, conv_ch) * proj(o3, conv_ch)
    st = st_ref[...]
    prev2 = jnp.broadcast_to(st[:, 0:1, :], (nb, dec_seq, conv_ch)).reshape(rows, conv_ch)
    prev1 = jnp.broadcast_to(st[:, 1:2, :], (nb, dec_seq, conv_ch)).reshape(rows, conv_ch)
    row = lax.broadcasted_iota(jnp.int32, (rows, 1), 0) % dec_seq
    conv_o = _short_conv(gated, prev2, prev1, row, convw_ref[...], proj(o3 + conv_ch, conv_ch))
    co_ref[...] = gated.reshape(nb, dec_seq, conv_ch)[:, dec_seq - (CONV_K - 1):, :]

    h_ref[...] = _merge_norm(x, attn_o, conv_o, gattn_ref[...], gconv_ref[...], wout_ref,
                             lng_ref[...], lnb_ref[...], alpha)


def _sample_mixer(x, ck, cv, st, win_b, tabs, sinks, conv_w, g_attn, g_conv, wout_b, ln_g, ln_b, *, alpha):
    dec_b, dec_seq, d = x.shape
    assert dec_seq >= CONV_K - 1 and dec_seq % SUBLANES == 0
    nb = SAMPLE_SEQS
    rows = nb * dec_seq
    win = ck.shape[1]
    attn_w = g_attn.shape[-1]
    conv_ch = g_conv.shape[-1]
    n_heads = attn_w // HEAD_DIM
    kv_w = N_KV_HEADS * HEAD_DIM
    in_cols = win_b.shape[-1]
    const2 = lambda i: (0, 0)
    kern = functools.partial(_sample_mixer_kernel, alpha=alpha, n_heads=n_heads, attn_w=attn_w, kv_w=kv_w,
                             conv_ch=conv_ch, dec_seq=dec_seq)
    return pl.pallas_call(
        kern,
        grid=(dec_b // nb,),
        in_specs=[
            pl.BlockSpec((rows, d), lambda i: (i, 0)),
            pl.BlockSpec((nb, win, kv_w), lambda i: (i, 0, 0)),
            pl.BlockSpec((nb, win, kv_w), lambda i: (i, 0, 0)),
            pl.BlockSpec((nb, CONV_K - 1, conv_ch), lambda i: (i, 0, 0)),
            pl.BlockSpec((d, in_cols), const2),
            pl.BlockSpec((rows, LANES), const2),
            pl.BlockSpec((rows, LANES), const2),
            pl.BlockSpec((rows, LANES), const2),
            pl.BlockSpec(memory_space=pltpu.SMEM),
            pl.BlockSpec((CONV_K, conv_ch), const2),
            pl.BlockSpec((1, attn_w), const2),
            pl.BlockSpec((1, conv_ch), const2),
            pl.BlockSpec((attn_w + conv_ch, d), const2),
            pl.BlockSpec((1, d), const2),
            pl.BlockSpec((1, d), const2),
        ],
        out_specs=[
            pl.BlockSpec((rows, d), lambda i: (i, 0)),
            pl.BlockSpec((nb, win, kv_w), lambda i: (i, 0, 0)),
            pl.BlockSpec((nb, win, kv_w), lambda i: (i, 0, 0)),
            pl.BlockSpec((nb, CONV_K - 1, conv_ch), lambda i: (i, 0, 0)),
        ],
        out_shape=[
            jax.ShapeDtypeStruct((dec_b * dec_seq, d), F32),
            jax.ShapeDtypeStruct((dec_b, win, kv_w), F32),
            jax.ShapeDtypeStruct((dec_b, win, kv_w), F32),
            jax.ShapeDtypeStruct((dec_b, CONV_K - 1, conv_ch), F32),
        ],
        compiler_params=pltpu.CompilerParams(dimension_semantics=("arbitrary",),
                                             vmem_limit_bytes=VMEM_LIMIT_BYTES),
        name="sample_mixer",
    )(x.reshape(dec_b * dec_seq, d), ck, cv, st, win_b, *tabs, sinks, conv_w, g_attn, g_conv, wout_b,
      ln_g, ln_b)


def _over_experts(fn, x):
    return fn(fn(x, axis=0, keepdims=True), axis=1, keepdims=True)


def _two_group_specs(tm, d, n_prompt_tiles):
    return [pl.BlockSpec((tm, d), lambda i, *_: (jnp.minimum(i, n_prompt_tiles - 1), 0)),
            pl.BlockSpec((tm, d), lambda i, *_: (jnp.maximum(i - n_prompt_tiles, 0), 0))]


def _route_kernel(hp_ref, hs_ref, rwt_ref, bias_ref, eidx_ref, rank_ref, gate_ref, cnt_ref, *, n_experts,
                  n_prompt_tiles):
    i = pl.program_id(0)
    tm = hp_ref.shape[0]
    per_group = n_experts // N_EXPERT_GROUPS
    shape3 = (N_EXPERT_GROUPS, per_group, tm)

    h = jnp.where(i < n_prompt_tiles, hp_ref[...], hs_ref[...])
    logits = lax.dot_general(rwt_ref[...], h.astype(BF16), (((1,), (1,)), ((), ())),
                             preferred_element_type=F32)
    scores = jax.nn.sigmoid(logits)
    sel = scores + bias_ref[...]
    scores3 = scores.reshape(shape3)
    grp = sel.reshape(shape3)
    member = lax.broadcasted_iota(jnp.int32, shape3, 1).astype(F32)
    group = lax.broadcasted_iota(jnp.int32, shape3, 0).astype(F32)
    expert = group * per_group + member

    m1 = jnp.max(grp, axis=1, keepdims=True)
    f1 = jnp.min(jnp.where(grp == m1, member, float(per_group)), axis=1, keepdims=True)
    m2 = jnp.max(jnp.where(member == f1, NEG_INF, grp), axis=1, keepdims=True)
    gscore = m1 + m2

    gid = lax.broadcasted_iota(jnp.int32, gscore.shape, 0).astype(F32)
    gmask = jnp.zeros(gscore.shape, F32)
    cur = gscore
    for _ in range(TOPK_GROUPS):
        mx = jnp.max(cur, axis=0, keepdims=True)
        pick = gid == jnp.min(jnp.where(cur == mx, gid, float(N_EXPERT_GROUPS)), axis=0, keepdims=True)
        gmask = jnp.where(pick, 1.0, gmask)
        cur = jnp.where(pick, NEG_INF, cur)

    cand = jnp.where(gmask > 0.0, grp, NEG_INF)
    chosen = jnp.zeros(shape3, F32)
    picks, firsts, weights = [], [], []
    for _ in range(TOP_K):
        mx = _over_experts(jnp.max, cand)
        first = _over_experts(jnp.min, jnp.where(cand == mx, expert, float(n_experts)))
        pick = expert == first
        picks.append(pick)
        firsts.append(first)
        weights.append(_over_experts(jnp.sum, jnp.where(pick, scores3, 0.0)))
        chosen = jnp.where(pick, 1.0, chosen)
        cand = jnp.where(pick, NEG_INF, cand)
    wsum = weights[0]
    for w in weights[1:]:
        wsum = wsum + w

    chosen_b = chosen.reshape(n_experts, tm).astype(BF16)
    t_from = lax.broadcasted_iota(jnp.int32, (tm, tm), 0)
    t_to = lax.broadcasted_iota(jnp.int32, (tm, tm), 1)
    earlier = ((t_from < t_to) & (t_from // MOE_TILE == t_to // MOE_TILE)).astype(BF16)
    before3 = jnp.dot(chosen_b, earlier, preferred_element_type=F32).reshape(shape3)

    pad = SUBLANES - TOP_K
    eidx = [f.reshape(1, tm).astype(jnp.int32) for f in firsts]
    rank = [_over_experts(jnp.sum, jnp.where(p, before3, 0.0)).reshape(1, tm).astype(jnp.int32) for p in picks]
    gate = [(w / wsum * ROUTED_SCALE).reshape(1, tm) for w in weights]
    eidx_ref[...] = jnp.concatenate(eidx + [jnp.zeros((pad, tm), jnp.int32)], axis=0)
    rank_ref[...] = jnp.concatenate(rank + [jnp.zeros((pad, tm), jnp.int32)], axis=0)
    gate_ref[...] = jnp.concatenate(gate + [jnp.zeros((pad, tm), F32)], axis=0)

    for sub in range(tm // MOE_TILE):
        cnt_ref[sub * SUBLANES:(sub + 1) * SUBLANES, :] = lax.dot_general(
            jnp.ones((SUBLANES, MOE_TILE), BF16), chosen_b[:, sub * MOE_TILE:(sub + 1) * MOE_TILE],
            (((1,), (1,)), ((), ())), preferred_element_type=F32)


def _route(h_p, h_s, rwt_b, bias_col):
    d = h_p.shape[1]
    t_all = h_p.shape[0] + h_s.shape[0]
    n_experts = rwt_b.shape[0]
    tm = ROUTE_TILE
    npt = h_p.shape[0] // tm
    moe_tiles = tm // MOE_TILE
    row_spec = pl.BlockSpec((SUBLANES, tm), lambda i: (0, i))
    return pl.pallas_call(
        functools.partial(_route_kernel, n_experts=n_experts, n_prompt_tiles=npt),
        grid=(t_all // tm,),
        in_specs=_two_group_specs(tm, d, npt) + [
            pl.BlockSpec((n_experts, d), lambda i: (0, 0)),
            pl.BlockSpec((n_experts, 1), lambda i: (0, 0)),
        ],
        out_specs=[row_spec, row_spec, row_spec,
                   pl.BlockSpec((moe_tiles * SUBLANES, n_experts), lambda i: (i, 0))],
        out_shape=[
            jax.ShapeDtypeStruct((SUBLANES, t_all), jnp.int32),
            jax.ShapeDtypeStruct((SUBLANES, t_all), jnp.int32),
            jax.ShapeDtypeStruct((SUBLANES, t_all), F32),
            jax.ShapeDtypeStruct((t_all // MOE_TILE * SUBLANES, n_experts), F32),
        ],
        compiler_params=pltpu.CompilerParams(dimension_semantics=("arbitrary",),
                                             vmem_limit_bytes=VMEM_LIMIT_BYTES),
        name="route",
    )(h_p, h_s, rwt_b, bias_col)


def _for_each_part(rows, max_rows, fn):
    for size in [SUBLANES << b for b in range((max_rows // SUBLANES).bit_length())]:
        @pl.when((rows & size) != 0)
        def _(size=size):
            fn(pl.multiple_of(rows & (size - 1), SUBLANES), size)


def _start_run_copies(tile, cnt_ref, ls_ref, gs_ref, big_ref, n_experts, start_copy):
    def run(e):
        idx = tile * n_experts + e
        return cnt_ref[idx], ls_ref[idx], gs_ref[idx]

    for e in range(n_experts):
        rows, l0, g0 = run(e)
        _for_each_part(rows, RUN_CHUNK // 2, lambda off, size: start_copy(
            pl.multiple_of(l0 + off, SUBLANES), pl.multiple_of(g0 + off, SUBLANES), size))

    @pl.when(big_ref[tile] != 0)
    def _():
        def per_expert(e, c):
            rows, l0, g0 = run(e)
            base = rows & (RUN_CHUNK - 1)

            def chunk(j, c2):
                off = base + j * RUN_CHUNK
                start_copy(pl.multiple_of(l0 + off, SUBLANES), pl.multiple_of(g0 + off, SUBLANES), RUN_CHUNK)
                return c2

            lax.fori_loop(0, lax.shift_right_logical(rows, RUN_CHUNK.bit_length() - 1), chunk, 0)
            return c

        lax.fori_loop(0, n_experts, per_expert, 0)


def _top_bit(n):
    return 1 << (n.bit_length() - 1)


def _pack_rows(x):
    rows, two_w = x.shape
    w = two_w // 2
    x3 = x.reshape(rows // SUBLANES, SUBLANES, two_w)
    halves = jnp.concatenate([x3[:, :, :w], x3[:, :, w:]], axis=1).astype(BF16)
    return pltpu.bitcast(halves, jnp.uint32).reshape(rows, w)


def _unpack_rows(p):
    rows, w = p.shape
    halves = pltpu.bitcast(p.reshape(rows // SUBLANES, SUBLANES, w), BF16).astype(F32)
    return jnp.concatenate([halves[:, :SUBLANES, :], halves[:, SUBLANES:, :]], axis=-1).reshape(rows, 2 * w)


def _local_rows(tm, n_experts):
    n = TOP_K * tm + n_experts * (SUBLANES - 1)
    return -(-n // SORT_CHUNK) * SORT_CHUNK


def _dispatch_kernel(cnt_ref, ls_ref, gs_ref, tot_ref, big_ref, zstart_ref, zlen_ref, lp_ref, hp_ref, hs_ref,
                     xs_hbm, lbuf, zero_s, sem, zero_sem, *, n_experts, n_prompt_tiles):
    i = pl.program_id(0)
    tm, d = hp_ref.shape
    n = TOP_K * tm
    nl = lbuf.shape[1]
    slot = i % 2

    def tile_wait(tile):
        pltpu.make_async_copy(lbuf.at[0, pl.ds(0, n)], xs_hbm.at[pl.ds(0, n)], sem).wait()
        _for_each_part(tot_ref[tile] - n, _top_bit(n_experts * (SUBLANES - 1)), lambda off, size:
                       pltpu.make_async_copy(lbuf.at[0, pl.ds(0, size)], xs_hbm.at[pl.ds(0, size)], sem).wait())

    def clear_padding(act):
        def per_expert(e, c):
            _for_each_part(zlen_ref[e], zero_s.shape[0], lambda off, size: act(pltpu.make_async_copy(
                zero_s.at[pl.ds(0, size)],
                xs_hbm.at[pl.ds(pl.multiple_of(zstart_ref[e] + off, SUBLANES), size)], zero_sem)))
            return c

        lax.fori_loop(0, n_experts, per_expert, 0)

    @pl.when(i == 0)
    def _():
        zero_s[...] = jnp.zeros(zero_s.shape, zero_s.dtype)
        clear_padding(lambda copy: copy.start())

    hb = jnp.where(i < n_prompt_tiles, hp_ref[...], hs_ref[...]).astype(BF16)
    lp = lp_ref[...]
    def sort_chunk(r):
        rows = r * SORT_CHUNK + lax.broadcasted_iota(jnp.int32, (SORT_CHUNK, tm), 0)
        hit = jnp.zeros((SORT_CHUNK, tm), F32)
        for k in range(TOP_K):
            hit = jnp.where(rows == lp[k:k + 1, :], 1.0, hit)
        lbuf[slot, pl.ds(r * SORT_CHUNK, SORT_CHUNK), :] = _pack_rows(
            jnp.dot(hit.astype(BF16), hb, preferred_element_type=F32))

    last_chunk = nl // SORT_CHUNK - 1
    for r in range(last_chunk):
        sort_chunk(r)
    pl.when(tot_ref[i] > last_chunk * SORT_CHUNK)(functools.partial(sort_chunk, last_chunk))

    @pl.when(i > 0)
    def _():
        tile_wait(i - 1)

    _start_run_copies(i, cnt_ref, ls_ref, gs_ref, big_ref, n_experts, lambda lrow, grow, size: pltpu.make_async_copy(
        lbuf.at[slot, pl.ds(lrow, size)], xs_hbm.at[pl.ds(grow, size)], sem).start())

    @pl.when(i == pl.num_programs(0) - 1)
    def _():
        tile_wait(i)
        clear_padding(lambda copy: copy.wait())


def _dispatch(tables, zstart, zlen, lp, h_p, h_s, *, n_rows, blk):
    d = h_p.shape[1]
    t_all = h_p.shape[0] + h_s.shape[0]
    tm = MOE_TILE
    npt = h_p.shape[0] // tm
    n_experts = zstart.shape[0]
    return pl.pallas_call(
        functools.partial(_dispatch_kernel, n_experts=n_experts, n_prompt_tiles=npt),
        grid_spec=pltpu.PrefetchScalarGridSpec(
            num_scalar_prefetch=7,
            grid=(t_all // tm,),
            in_specs=[pl.BlockSpec((SUBLANES, tm), lambda i, *_: (0, i))] + _two_group_specs(tm, d, npt),
            out_specs=pl.BlockSpec(memory_space=pl.ANY),
            scratch_shapes=[
                pltpu.VMEM((2, _local_rows(tm, n_experts), d // 2), jnp.uint32),
                pltpu.VMEM((_top_bit(blk - SUBLANES), d // 2), jnp.uint32),
                pltpu.SemaphoreType.DMA,
                pltpu.SemaphoreType.DMA,
            ],
        ),
        out_shape=jax.ShapeDtypeStruct((n_rows, d // 2), jnp.uint32),
        compiler_params=pltpu.CompilerParams(dimension_semantics=("arbitrary",),
                                             vmem_limit_bytes=VMEM_LIMIT_BYTES),
        name="dispatch",
    )(*tables, zstart, zlen, lp, h_p, h_s)


def _silu(x):
    return x * jax.nn.sigmoid(x)


def _expert_kernel(be_ref, nact_ref, xs_ref, wg_ref, wu_ref, wd_ref, ys_ref, wg_s, wu_s, wd_s):
    b = pl.program_id(0)

    @pl.when(b < nact_ref[0])
    def _():
        @pl.when((b == 0) | (be_ref[b] != be_ref[jnp.maximum(b - 1, 0)]))
        def _():
            wg_s[...] = wg_ref[...].astype(BF16)
            wu_s[...] = wu_ref[...].astype(BF16)
            wd_s[...] = wd_ref[...].astype(BF16)

        xb = _unpack_rows(xs_ref[...]).astype(BF16)
        hid = _silu(jnp.dot(xb, wg_s[...], preferred_element_type=F32)) * jnp.dot(
            xb, wu_s[...], preferred_element_type=F32)
        ys_ref[...] = _pack_rows(jnp.dot(hid.astype(BF16), wd_s[...], preferred_element_type=F32))


def _experts(block_expert, nact, xs, w_gate, w_up, w_down, *, blk):
    n_rows, dw = xs.shape
    d = w_gate.shape[-2]
    ff = w_gate.shape[-1]
    n_blocks = n_rows // blk

    def active(b, be, na):
        return jnp.minimum(b, na[0] - 1)

    return pl.pallas_call(
        _expert_kernel,
        grid_spec=pltpu.PrefetchScalarGridSpec(
            num_scalar_prefetch=2,
            grid=(n_blocks,),
            in_specs=[
                pl.BlockSpec((blk, dw), lambda b, be, na: (active(b, be, na), 0)),
                pl.BlockSpec((None, d, ff), lambda b, be, na: (be[active(b, be, na)], 0, 0)),
                pl.BlockSpec((None, d, ff), lambda b, be, na: (be[active(b, be, na)], 0, 0)),
                pl.BlockSpec((None, ff, d), lambda b, be, na: (be[active(b, be, na)], 0, 0)),
            ],
            out_specs=pl.BlockSpec((blk, dw), lambda b, be, na: (active(b, be, na), 0)),
            scratch_shapes=[
                pltpu.VMEM((d, ff), BF16),
                pltpu.VMEM((d, ff), BF16),
                pltpu.VMEM((ff, d), BF16),
            ],
        ),
        out_shape=jax.ShapeDtypeStruct((n_rows, dw), jnp.uint32),
        compiler_params=pltpu.CompilerParams(dimension_semantics=("arbitrary",),
                                             vmem_limit_bytes=VMEM_LIMIT_BYTES),
        name="experts",
    )(block_expert, nact, xs, w_gate, w_up, w_down)


def _combine_kernel(cnt_ref, ls_ref, gs_ref, tot_ref, big_ref, ys_hbm, lpt_ref, gate_ref, hp_ref, hs_ref, wsg_ref, wsu_ref,
                    wsd_ref, lng_ref, lnb_ref, yp_ref, ysm_ref, ybuf, moe_s, sems, *, alpha, n_experts,
                    n_prompt_tiles):
    i = pl.program_id(0)
    tm, d = hp_ref.shape
    n = TOP_K * tm
    nl = ybuf.shape[1]
    slot = i % 2

    def fetch(tile, to):
        _start_run_copies(tile, cnt_ref, ls_ref, gs_ref, big_ref, n_experts, lambda lrow, grow, size:
                          pltpu.make_async_copy(ys_hbm.at[pl.ds(grow, size)], ybuf.at[to, pl.ds(lrow, size)],
                                                sems.at[to]).start())

    @pl.when(i == 0)
    def _():
        ybuf[...] = jnp.zeros(ybuf.shape, ybuf.dtype)
        fetch(0, 0)

    @pl.when(i + 1 < pl.num_programs(0))
    def _():
        fetch(i + 1, 1 - slot)

    h = jnp.where(i < n_prompt_tiles, hp_ref[...], hs_ref[...])
    hb = h.astype(BF16)
    hid = _silu(jnp.dot(hb, wsg_ref[...], preferred_element_type=F32)) * jnp.dot(
        hb, wsu_ref[...], preferred_element_type=F32)
    shared = jnp.dot(hid.astype(BF16), wsd_ref[...], preferred_element_type=F32)

    pltpu.make_async_copy(ys_hbm.at[pl.ds(0, n)], ybuf.at[slot, pl.ds(0, n)], sems.at[slot]).wait()
    _for_each_part(tot_ref[i] - n, _top_bit(n_experts * (SUBLANES - 1)), lambda off, size: pltpu.make_async_copy(
        ys_hbm.at[pl.ds(0, size)], ybuf.at[slot, pl.ds(0, size)], sems.at[slot]).wait())

    lpt = lpt_ref[...]
    gate = gate_ref[...]

    def chunk_sum(r):
        cols = r * SORT_CHUNK + lax.broadcasted_iota(jnp.int32, (tm, SORT_CHUNK), 1)
        g = jnp.zeros((tm, SORT_CHUNK), F32)
        for k in range(TOP_K):
            g = jnp.where(cols == lpt[:, k:k + 1], gate[:, k:k + 1], g)
        yb = _unpack_rows(ybuf[slot, pl.ds(r * SORT_CHUNK, SORT_CHUNK), :]).astype(BF16)
        return jnp.dot(g.astype(BF16), yb, preferred_element_type=F32)

    last_chunk = nl // SORT_CHUNK - 1
    moe = shared
    for r in range(last_chunk):
        moe = moe + chunk_sum(r)
    moe_s[...] = moe

    @pl.when(tot_ref[i] > last_chunk * SORT_CHUNK)
    def _():
        moe_s[...] += chunk_sum(last_chunk)

    y = _layer_norm(alpha * h + moe_s[...], lng_ref[...], lnb_ref[...])

    @pl.when(i < n_prompt_tiles)
    def _():
        yp_ref[...] = y

    @pl.when(i >= n_prompt_tiles)
    def _():
        ysm_ref[...] = y


def _combine(tables, ys, lp_t, gates_t, h_p, h_s, wsg_b, wsu_b, wsd_b, ln_g, ln_b, *, alpha):
    t_prompt, d = h_p.shape
    t_all = t_prompt + h_s.shape[0]
    tm = MOE_TILE
    ff = wsg_b.shape[-1]
    npt = t_prompt // tm
    n_experts = tables[0].shape[0] // (t_all // tm)
    const2 = lambda i, *_: (0, 0)
    return pl.pallas_call(
        functools.partial(_combine_kernel, alpha=alpha, n_experts=n_experts, n_prompt_tiles=npt),
        grid_spec=pltpu.PrefetchScalarGridSpec(
            num_scalar_prefetch=5,
            grid=(t_all // tm,),
            in_specs=[
                pl.BlockSpec(memory_space=pl.ANY),
                pl.BlockSpec((tm, SUBLANES), lambda i, *_: (i, 0)),
                pl.BlockSpec((tm, SUBLANES), lambda i, *_: (i, 0)),
                *_two_group_specs(tm, d, npt),
                pl.BlockSpec((d, ff), const2),
                pl.BlockSpec((d, ff), const2),
                pl.BlockSpec((ff, d), const2),
                pl.BlockSpec((1, d), const2),
                pl.BlockSpec((1, d), const2),
            ],
            out_specs=[
                pl.BlockSpec((tm, d), lambda i, *_: (jnp.minimum(i, npt - 1), 0)),
                pl.BlockSpec((tm, d), lambda i, *_: (jnp.maximum(i - npt, 0), 0)),
            ],
            scratch_shapes=[
                pltpu.VMEM((2, _local_rows(tm, n_experts), d // 2), jnp.uint32),
                pltpu.VMEM((tm, d), F32),
                pltpu.SemaphoreType.DMA((2,)),
            ],
        ),
        out_shape=[
            jax.ShapeDtypeStruct((t_prompt, d), F32),
            jax.ShapeDtypeStruct((t_all - t_prompt, d), F32),
        ],
        compiler_params=pltpu.CompilerParams(dimension_semantics=("arbitrary",),
                                             vmem_limit_bytes=VMEM_LIMIT_BYTES),
        name="combine",
    )(*tables, ys, lp_t, gates_t, h_p, h_s, wsg_b, wsu_b, wsd_b, ln_g, ln_b)


def _moe(h_p, h_s, router_w, router_bias, w_gate, w_up, w_down, ws_gate, ws_up, ws_down, ln_g, ln_b, *, alpha):
    t_all = h_p.shape[0] + h_s.shape[0]
    n_experts = router_w.shape[-1]
    blk = EXPERT_BLOCK
    tm = MOE_TILE
    nt = t_all // tm
    eidx8, lrank8, gate8, cnt8 = _route(h_p, h_s, router_w.T.astype(BF16), router_bias.reshape(n_experts, 1))

    cnt = cnt8.reshape(nt, SUBLANES, n_experts)[:, 0, :].astype(jnp.int32)
    cnt = (cnt + SUBLANES - 1) // SUBLANES * SUBLANES
    counts = jnp.sum(cnt, axis=0)
    blocks_per_e = (counts + blk - 1) // blk
    block_end = jnp.cumsum(blocks_per_e)
    pad_start = (block_end - blocks_per_e) * blk
    n_blocks = -(-(t_all * TOP_K + nt * n_experts * (SUBLANES - 1)) // blk) + n_experts
    block_expert = jnp.minimum(jnp.sum(block_end[None, :] <= jnp.arange(n_blocks)[:, None], axis=1),
                               n_experts - 1).astype(jnp.int32)
    nact = block_end[-1:].astype(jnp.int32)
    zstart = (pad_start + counts).astype(jnp.int32)
    zlen = (blocks_per_e * blk - counts).astype(jnp.int32)
    gstart = pad_start[None, :] + jnp.cumsum(cnt, axis=0) - cnt
    lstart = jnp.cumsum(cnt, axis=1) - cnt
    tables = tuple(a.reshape(-1).astype(jnp.int32) for a in (
        cnt, lstart, gstart, jnp.sum(cnt, axis=1), jnp.max(cnt, axis=1) >= RUN_CHUNK))
    lstart_tok = jnp.repeat(lstart, tm, axis=0)
    lp8 = jnp.sum(jnp.where(eidx8[..., None] == jnp.arange(n_experts), lstart_tok[None], 0), axis=-1) + lrank8
    lp8 = lp8.astype(jnp.int32)

    xs = _dispatch(tables, zstart, zlen, lp8, h_p, h_s, n_rows=n_blocks * blk, blk=blk)
    ys = _experts(block_expert, nact, xs, w_gate, w_up, w_down, blk=blk)
    return _combine(tables, ys, lp8.T, gate8.T, h_p, h_s, ws_gate.astype(BF16), ws_up.astype(BF16), ws_down.astype(BF16),
                    ln_g, ln_b, alpha=alpha)


def kernel(x_prompt, x_sample, cache_k, cache_v, state_conv, w_in, attn_sinks, conv_w, g_attn_out, g_conv_out, w_out, ln1_g, ln1_b, router_w, router_bias, w_gate, w_up, w_down, ws_gate, ws_up, ws_down, ln2_g, ln2_b):
    depth = w_in.shape[0]
    bsz, seq, d = x_prompt.shape
    dec_b, dec_seq, _ = x_sample.shape
    win = cache_k.shape[2]
    kv_w = N_KV_HEADS * HEAD_DIM
    t_prompt = bsz * seq
    t_all = t_prompt + dec_b * dec_seq
    alpha = (2.0 * depth) ** 0.25
    assert win == WINDOW and seq % PROMPT_TILE == 0 and dec_b % SAMPLE_SEQS == 0
    assert t_prompt % ROUTE_TILE == 0 and (t_all - t_prompt) % ROUTE_TILE == 0 and ROUTE_TILE % MOE_TILE == 0
    assert MOE_TILE & (MOE_TILE - 1) == 0 and (TOP_K * MOE_TILE) % SORT_CHUNK == 0

    tabs_p = _rope_tables(jnp.arange(seq))
    tabs_s = tuple(jnp.tile(t, (SAMPLE_SEQS, 1)) for t in _rope_tables(PAST_LEN + jnp.arange(dec_seq)))
    row = lambda a: a.reshape(1, -1)

    xp, xs = x_prompt, x_sample
    outs = [[] for _ in range(6)]
    for l in range(depth):
        win_b, wout_b = w_in[l].astype(BF16), w_out[l].astype(BF16)
        shared = (attn_sinks[l], conv_w[l], row(g_attn_out[l]), row(g_conv_out[l]), wout_b, row(ln1_g[l]),
                  row(ln1_b[l]))
        h_p, kp, vp, cp = _prompt_mixer(xp, win_b, tabs_p, *shared, alpha=alpha)
        h_s, kn, vn, cn = _sample_mixer(xs, cache_k[l].reshape(dec_b, win, kv_w),
                                        cache_v[l].reshape(dec_b, win, kv_w), state_conv[l], win_b, tabs_s,
                                        *shared, alpha=alpha)
        yp, ys = _moe(h_p, h_s, router_w[l], router_bias[l], w_gate[l], w_up[l], w_down[l], ws_gate[l], ws_up[l],
                      ws_down[l], row(ln2_g[l]), row(ln2_b[l]), alpha=alpha)
        xp, xs = yp.reshape(bsz, seq, d), ys.reshape(dec_b, dec_seq, d)
        heads = lambda a: a.reshape(a.shape[0], win, N_KV_HEADS, HEAD_DIM)
        for o, a in zip(outs, (heads(kp), heads(vp), cp, heads(kn), heads(vn), cn)):
            o.append(a)
    return (xp, xs) + tuple(jnp.stack(o, axis=0) for o in outs)
```

```python
import functools

import jax
import jax.numpy as jnp
from jax import lax
from jax.experimental import pallas as pl
from jax.experimental.pallas import tpu as pltpu

PAST_LEN = 16384
WINDOW = 128
HEAD_DIM = 64
N_KV_HEADS = 2
ROT_DIM = HEAD_DIM // 4
ROPE_THETA = 500000.0
CONV_K = 3
TOP_K = 6
N_EXPERT_GROUPS = 8
TOPK_GROUPS = 4
ROUTED_SCALE = 2.5
LN_EPS = 1e-5
RMS_EPS = 1e-6

LANES = 128
SUBLANES = 8
VMEM_LIMIT_BYTES = 56 * 1024 * 1024

PROMPT_TILE = 1024
SAMPLE_SEQS = 32
MOE_TILE = 256
SORT_CHUNK = 256
EXPERT_BLOCK = 2048
ROUTE_TILE = 512
RUN_CHUNK = 64

F32 = jnp.float32
BF16 = jnp.bfloat16
NEG_INF = float("-inf")


def _rope_tables(positions):
    half = ROT_DIM // 2
    inv_freq = ROPE_THETA ** (-jnp.arange(0, ROT_DIM, 2, dtype=F32) / ROT_DIM)
    ang = positions.astype(F32)[:, None] * inv_freq[None, :]
    cos, sin = jnp.cos(ang), jnp.sin(ang)
    n = positions.shape[0]
    rest = HEAD_DIM - ROT_DIM
    c = jnp.concatenate([cos, cos, jnp.ones((n, rest), F32)], axis=-1)
    sa = jnp.concatenate([-sin, jnp.zeros((n, half + rest), F32)], axis=-1)
    sb = jnp.concatenate([jnp.zeros((n, half), F32), sin, jnp.zeros((n, rest), F32)], axis=-1)
    reps = LANES // HEAD_DIM
    return jnp.tile(c, (1, reps)), jnp.tile(sa, (1, reps)), jnp.tile(sb, (1, reps))


def _rope(x, c, sa, sb):
    half = ROT_DIM // 2
    return x * c + pltpu.roll(x, LANES - half, 1) * sa + pltpu.roll(x, half, 1) * sb


def _rms_norm(x, g):
    return x * lax.rsqrt(jnp.mean(jnp.square(x), axis=-1, keepdims=True) + RMS_EPS) * g


def _layer_norm(x, g, b):
    mu = jnp.mean(x, axis=-1, keepdims=True)
    var = jnp.mean(jnp.square(x - mu), axis=-1, keepdims=True)
    return (x - mu) * lax.rsqrt(var + LN_EPS) * g + b


def _short_conv(gated, prev2, prev1, row, conv_w, b_gate):
    g1 = pltpu.roll(gated, 1, 0)
    g2 = pltpu.roll(gated, 2, 0)
    g1 = jnp.where(row == 0, prev1, g1)
    g2 = jnp.where(row == 0, prev2, jnp.where(row == 1, prev1, g2))
    y = conv_w[0:1, :] * g2 + conv_w[1:2, :] * g1 + conv_w[2:3, :] * gated
    return b_gate * y


def _merge_norm(x, attn_o, conv_o, gattn, gconv, wout_ref, ln_g, ln_b, alpha):
    cat = jnp.concatenate([_rms_norm(attn_o, gattn), _rms_norm(conv_o, gconv)], axis=-1)
    mix = jnp.dot(cat.astype(BF16), wout_ref[...], preferred_element_type=F32)
    return _layer_norm(alpha * x + mix, ln_g, ln_b)


def _sink_softmax(parts, sink):
    m = sink
    for s in parts:
        m = jnp.maximum(m, jnp.max(s, axis=-1, keepdims=True))
    es = [jnp.exp(s - m) for s in parts]
    den = jnp.exp(sink - m)
    for e in es:
        den = den + jnp.sum(e, axis=-1, keepdims=True)
    return [(e / den).astype(BF16) for e in es]


def _prompt_mixer_kernel(x_ref, win_ref, c_ref, sa_ref, sb_ref, sinks_ref, convw_ref, gattn_ref, gconv_ref,
                         wout_ref, lng_ref, lnb_ref,
                         h_ref, ko_ref, vo_ref, co_ref,
                         q_s, kt_s, vm_s, o_s, gc_s, *, alpha, n_heads, attn_w, kv_w, conv_ch):
    s = pl.program_id(1)
    last = pl.num_programs(1) - 1
    tq = x_ref.shape[0]
    q_per_kv = n_heads // N_KV_HEADS
    heads_per_group = LANES // HEAD_DIM

    @pl.when(s == 0)
    def _():
        kt_s[:, :, 0:WINDOW] = jnp.zeros((kt_s.shape[0], LANES, WINDOW), BF16)
        vm_s[:, 0:WINDOW, :] = jnp.zeros((vm_s.shape[0], WINDOW, LANES), BF16)
        gc_s[...] = jnp.zeros(gc_s.shape, F32)

    @pl.when(s > 0)
    def _():
        kt_s[:, :, 0:WINDOW] = kt_s[:, :, tq:tq + WINDOW]
        vm_s[:, 0:WINDOW, :] = vm_s[:, tq:tq + WINDOW, :]

    x = x_ref[...]
    xb = x.astype(BF16)
    c, sa, sb = c_ref[...], sa_ref[...], sb_ref[...]

    def proj(lo, width):
        return jnp.dot(xb, win_ref[:, lo:lo + width], preferred_element_type=F32)

    scale = HEAD_DIM ** -0.5
    for j in range(attn_w // LANES):
        qj = _rope(proj(j * LANES, LANES), c, sa, sb)
        q_s[:, j * LANES:(j + 1) * LANES] = (qj * scale).astype(BF16)
    k = _rope(proj(attn_w, kv_w), c, sa, sb)
    v = proj(attn_w + kv_w, kv_w)

    @pl.when(s == last)
    def _():
        ko_ref[...] = k[tq - WINDOW:, :]
        vo_ref[...] = v[tq - WINDOW:, :]

    kt = k.T.astype(BF16)
    zeros_k = jnp.zeros((HEAD_DIM, tq), BF16)
    v_swapped = pltpu.roll(v, HEAD_DIM, 1)
    low_lanes = lax.broadcasted_iota(jnp.int32, (tq, LANES), 1) < HEAD_DIM
    for kvh in range(N_KV_HEADS):
        kt_h = kt[kvh * HEAD_DIM:(kvh + 1) * HEAD_DIM, :]
        v_lo = v if kvh == 0 else v_swapped
        v_hi = v_swapped if kvh == 0 else v
        kt_s[2 * kvh, :, WINDOW:] = jnp.concatenate([kt_h, zeros_k], axis=0)
        kt_s[2 * kvh + 1, :, WINDOW:] = jnp.concatenate([zeros_k, kt_h], axis=0)
        vm_s[2 * kvh, WINDOW:, :] = jnp.where(low_lanes, v_lo, 0.0).astype(BF16)
        vm_s[2 * kvh + 1, WINDOW:, :] = jnp.where(low_lanes, 0.0, v_hi).astype(BF16)

    qi = lax.broadcasted_iota(jnp.int32, (WINDOW, 2 * WINDOW), 0)
    ci = lax.broadcasted_iota(jnp.int32, (WINDOW, 2 * WINDOW), 1)
    band = (ci > qi) & (ci <= qi + WINDOW)
    for j in range(tq // WINDOW):
        r0 = j * WINDOW
        mask = band if j > 0 else band & ((ci >= WINDOW) | (s > 0))
        for grp in range(attn_w // LANES):
            q_grp = q_s[r0:r0 + WINDOW, grp * LANES:(grp + 1) * LANES]
            out = None
            for r in range(heads_per_group):
                hd = grp * heads_per_group + r
                src = 2 * (hd // q_per_kv) + r
                sink = sinks_ref[hd]
                sc = jnp.dot(q_grp, kt_s[src, :, r0:r0 + 2 * WINDOW], preferred_element_type=F32)
                sc = jnp.where(mask, sc, NEG_INF)
                m = jnp.maximum(jnp.max(sc, axis=-1, keepdims=True), sink)
                e = jnp.exp(sc - m)
                den = jnp.sum(e, axis=-1, keepdims=True) + jnp.exp(sink - m)
                o_h = jnp.dot(e.astype(BF16), vm_s[src, r0:r0 + 2 * WINDOW, :],
                              preferred_element_type=F32) * (1.0 / den)
                out = o_h if out is None else out + o_h
            o_s[r0:r0 + WINDOW, grp * LANES:(grp + 1) * LANES] = out

    o3 = attn_w + 2 * kv_w
    gated = proj(o3 + 2 * conv_ch, conv_ch) * proj(o3, conv_ch)
    row = lax.broadcasted_iota(jnp.int32, (tq, 1), 0)
    conv_o = _short_conv(gated, gc_s[0:1, :], gc_s[1:2, :], row, convw_ref[...], proj(o3 + conv_ch, conv_ch))
    gc_s[0:CONV_K - 1, :] = gated[tq - (CONV_K - 1):, :]

    @pl.when(s == last)
    def _():
        co_ref[...] = gated[tq - (CONV_K - 1):, :]

    h_ref[...] = _merge_norm(x, o_s[...], conv_o, gattn_ref[...], gconv_ref[...], wout_ref,
                             lng_ref[...], lnb_ref[...], alpha)


def _prompt_mixer(x, win_b, tabs, sinks, conv_w, g_attn, g_conv, wout_b, ln_g, ln_b, *, alpha):
    bsz, seq, d = x.shape
    tq = PROMPT_TILE
    ns = seq // tq
    attn_w = g_attn.shape[-1]
    conv_ch = g_conv.shape[-1]
    n_heads = attn_w // HEAD_DIM
    kv_w = N_KV_HEADS * HEAD_DIM
    assert kv_w == LANES and 2 * HEAD_DIM == LANES and (n_heads // N_KV_HEADS) % 2 == 0 and WINDOW == LANES
    in_cols = win_b.shape[-1]
    const2 = lambda b, s: (0, 0)
    kern = functools.partial(_prompt_mixer_kernel, alpha=alpha, n_heads=n_heads, attn_w=attn_w, kv_w=kv_w,
                             conv_ch=conv_ch)
    return pl.pallas_call(
        kern,
        grid=(bsz, ns),
        in_specs=[
            pl.BlockSpec((None, tq, d), lambda b, s: (b, s, 0)),
            pl.BlockSpec((d, in_cols), const2),
            pl.BlockSpec((tq, LANES), lambda b, s: (s, 0)),
            pl.BlockSpec((tq, LANES), lambda b, s: (s, 0)),
            pl.BlockSpec((tq, LANES), lambda b, s: (s, 0)),
            pl.BlockSpec(memory_space=pltpu.SMEM),
            pl.BlockSpec((CONV_K, conv_ch), const2),
            pl.BlockSpec((1, attn_w), const2),
            pl.BlockSpec((1, conv_ch), const2),
            pl.BlockSpec((attn_w + conv_ch, d), const2),
            pl.BlockSpec((1, d), const2),
            pl.BlockSpec((1, d), const2),
        ],
        out_specs=[
            pl.BlockSpec((tq, d), lambda b, s: (b * ns + s, 0)),
            pl.BlockSpec((None, WINDOW, kv_w), lambda b, s: (b, 0, 0)),
            pl.BlockSpec((None, WINDOW, kv_w), lambda b, s: (b, 0, 0)),
            pl.BlockSpec((None, CONV_K - 1, conv_ch), lambda b, s: (b, 0, 0)),
        ],
        out_shape=[
            jax.ShapeDtypeStruct((bsz * seq, d), F32),
            jax.ShapeDtypeStruct((bsz, WINDOW, kv_w), F32),
            jax.ShapeDtypeStruct((bsz, WINDOW, kv_w), F32),
            jax.ShapeDtypeStruct((bsz, CONV_K - 1, conv_ch), F32),
        ],
        scratch_shapes=[
            pltpu.VMEM((tq, attn_w), BF16),
            pltpu.VMEM((2 * N_KV_HEADS, LANES, WINDOW + tq), BF16),
            pltpu.VMEM((2 * N_KV_HEADS, WINDOW + tq, LANES), BF16),
            pltpu.VMEM((tq, attn_w), F32),
            pltpu.VMEM((SUBLANES, conv_ch), F32),
        ],
        compiler_params=pltpu.CompilerParams(dimension_semantics=("arbitrary", "arbitrary"),
                                             vmem_limit_bytes=VMEM_LIMIT_BYTES),
        name="prompt_mixer",
    )(x, win_b, *tabs, sinks, conv_w, g_attn, g_conv, wout_b, ln_g, ln_b)


def _sample_mixer_kernel(x_ref, ck_ref, cv_ref, st_ref, win_ref, c_ref, sa_ref, sb_ref, sinks_ref,
                         convw_ref, gattn_ref, gconv_ref, wout_ref, lng_ref, lnb_ref,
                         h_ref, ko_ref, vo_ref, co_ref, *, alpha, n_heads, attn_w, kv_w, conv_ch, dec_seq):
    nb, win = ck_ref.shape[0], ck_ref.shape[1]
    rows = nb * dec_seq
    q_per_kv = n_heads // N_KV_HEADS
    x = x_ref[...]
    xb = x.astype(BF16)
    c, sa, sb = c_ref[...], sa_ref[...], sb_ref[...]

    def proj(lo, width):
        return jnp.dot(xb, win_ref[:, lo:lo + width], preferred_element_type=F32)

    scale = HEAD_DIM ** -0.5
    k = _rope(proj(attn_w, kv_w), c, sa, sb)
    v = proj(attn_w + kv_w, kv_w)
    k3 = k.reshape(nb, dec_seq, kv_w)
    v3 = v.reshape(nb, dec_seq, kv_w)
    ck = ck_ref[...]
    cv = cv_ref[...]
    ko_ref[:, 0:win - dec_seq, :] = ck[:, dec_seq:, :]
    ko_ref[:, win - dec_seq:, :] = k3
    vo_ref[:, 0:win - dec_seq, :] = cv[:, dec_seq:, :]
    vo_ref[:, win - dec_seq:, :] = v3
    ckb, cvb, k3b, v3b = ck.astype(BF16), cv.astype(BF16), k3.astype(BF16), v3.astype(BF16)

    qrows = q_per_kv * dec_seq
    qi = lax.broadcasted_iota(jnp.int32, (nb, qrows, win), 1) % dec_seq
    mask_c = lax.broadcasted_iota(jnp.int32, (nb, qrows, win), 2) > qi + (win - WINDOW)
    qn = lax.broadcasted_iota(jnp.int32, (nb, qrows, dec_seq), 1) % dec_seq
    mask_n = lax.broadcasted_iota(jnp.int32, (nb, qrows, dec_seq), 2) <= qn
    sink_row = lax.broadcasted_iota(jnp.int32, (nb, qrows, 1), 1) // dec_seq

    q_chunks = [_rope(proj(j * LANES, LANES), c, sa, sb) * scale for j in range(attn_w // LANES)]
    heads_out = []
    for kvh in range(N_KV_HEADS):
        qs = []
        for g in range(q_per_kv):
            lo = (kvh * q_per_kv + g) * HEAD_DIM
            qh = q_chunks[lo // LANES][:, lo % LANES:lo % LANES + HEAD_DIM]
            qs.append(qh.reshape(nb, dec_seq, HEAD_DIM))
        qg = jnp.concatenate(qs, axis=1).astype(BF16)
        sl = slice(kvh * HEAD_DIM, (kvh + 1) * HEAD_DIM)
        sc_c = jnp.einsum("bqd,bkd->bqk", qg, ckb[:, :, sl], preferred_element_type=F32)
        sc_n = jnp.einsum("bqd,bkd->bqk", qg, k3b[:, :, sl], preferred_element_type=F32)
        sc_c = jnp.where(mask_c, sc_c, NEG_INF)
        sc_n = jnp.where(mask_n, sc_n, NEG_INF)
        sink = jnp.zeros((nb, qrows, 1), F32)
        for g in range(q_per_kv):
            sink = jnp.where(sink_row == g, sinks_ref[kvh * q_per_kv + g], sink)
        p_c, p_n = _sink_softmax([sc_c, sc_n], sink)
        og = (jnp.einsum("bqk,bkd->bqd", p_c, cvb[:, :, sl], preferred_element_type=F32)
              + jnp.einsum("bqk,bkd->bqd", p_n, v3b[:, :, sl], preferred_element_type=F32))
        for g in range(q_per_kv):
            heads_out.append(og[:, g * dec_seq:(g + 1) * dec_seq, :].reshape(rows, HEAD_DIM))
    attn_o = jnp.concatenate(heads_out, axis=-1)

    o3 = attn_w + 2 * kv_w
    gated = proj(o3 + 2 * conv_ch, conv_ch) * proj(o3, conv_ch)
    st = st_ref[...]
    prev2 = jnp.broadcast_to(st[:, 0:1, :], (nb, dec_seq, conv_ch)).reshape(rows, conv_ch)
    prev1 = jnp.broadcast_to(st[:, 1:2, :], (nb, dec_seq, conv_ch)).reshape(rows, conv_ch)
    row = lax.broadcasted_iota(jnp.int32, (rows, 1), 0) % dec_seq
    conv_o = _short_conv(gated, prev2, prev1, row, convw_ref[...], proj(o3 + conv_ch, conv_ch))
    co_ref[...] = gated.reshape(nb, dec_seq, conv_ch)[:, dec_seq - (CONV_K - 1):, :]

    h_ref[...] = _merge_norm(x, attn_o, conv_o, gattn_ref[...], gconv_ref[...], wout_ref,
                             lng_ref[...], lnb_ref[...], alpha)


def _sample_mixer(x, ck, cv, st, win_b, tabs, sinks, conv_w, g_attn, g_conv, wout_b, ln_g, ln_b, *, alpha):
    dec_b, dec_seq, d = x.shape
    assert dec_seq >= CONV_K - 1 and dec_seq % SUBLANES == 0
    nb = SAMPLE_SEQS
    rows = nb * dec_seq
    win = ck.shape[1]
    attn_w = g_attn.shape[-1]
    conv_ch = g_conv.shape[-1]
    n_heads = attn_w // HEAD_DIM
    kv_w = N_KV_HEADS * HEAD_DIM
    in_cols = win_b.shape[-1]
    const2 = lambda i: (0, 0)
    kern = functools.partial(_sample_mixer_kernel, alpha=alpha, n_heads=n_heads, attn_w=attn_w, kv_w=kv_w,
                             conv_ch=conv_ch, dec_seq=dec_seq)
    return pl.pallas_call(
        kern,
        grid=(dec_b // nb,),
        in_specs=[
            pl.BlockSpec((rows, d), lambda i: (i, 0)),
            pl.BlockSpec((nb, win, kv_w), lambda i: (i, 0, 0)),
            pl.BlockSpec((nb, win, kv_w), lambda i: (i, 0, 0)),
            pl.BlockSpec((nb, CONV_K - 1, conv_ch), lambda i: (i, 0, 0)),
            pl.BlockSpec((d, in_cols), const2),
            pl.BlockSpec((rows, LANES), const2),
            pl.BlockSpec((rows, LANES), const2),
            pl.BlockSpec((rows, LANES), const2),
            pl.BlockSpec(memory_space=pltpu.SMEM),
            pl.BlockSpec((CONV_K, conv_ch), const2),
            pl.BlockSpec((1, attn_w), const2),
            pl.BlockSpec((1, conv_ch), const2),
            pl.BlockSpec((attn_w + conv_ch, d), const2),
            pl.BlockSpec((1, d), const2),
            pl.BlockSpec((1, d), const2),
        ],
        out_specs=[
            pl.BlockSpec((rows, d), lambda i: (i, 0)),
            pl.BlockSpec((nb, win, kv_w), lambda i: (i, 0, 0)),
            pl.BlockSpec((nb, win, kv_w), lambda i: (i, 0, 0)),
            pl.BlockSpec((nb, CONV_K - 1, conv_ch), lambda i: (i, 0, 0)),
        ],
        out_shape=[
            jax.ShapeDtypeStruct((dec_b * dec_seq, d), F32),
            jax.ShapeDtypeStruct((dec_b, win, kv_w), F32),
            jax.ShapeDtypeStruct((dec_b, win, kv_w), F32),
            jax.ShapeDtypeStruct((dec_b, CONV_K - 1, conv_ch), F32),
        ],
        compiler_params=pltpu.CompilerParams(dimension_semantics=("arbitrary",),
                                             vmem_limit_bytes=VMEM_LIMIT_BYTES),
        name="sample_mixer",
    )(x.reshape(dec_b * dec_seq, d), ck, cv, st, win_b, *tabs, sinks, conv_w, g_attn, g_conv, wout_b,
      ln_g, ln_b)


def _over_experts(fn, x):
    return fn(fn(x, axis=0, keepdims=True), axis=1, keepdims=True)


def _two_group_specs(tm, d, n_prompt_tiles):
    return [pl.BlockSpec((tm, d), lambda i, *_: (jnp.minimum(i, n_prompt_tiles - 1), 0)),
            pl.BlockSpec((tm, d), lambda i, *_: (jnp.maximum(i - n_prompt_tiles, 0), 0))]


def _route_kernel(hp_ref, hs_ref, rwt_ref, bias_ref, eidx_ref, rank_ref, gate_ref, cnt_ref, *, n_experts,
                  n_prompt_tiles):
    i = pl.program_id(0)
    tm = hp_ref.shape[0]
    per_group = n_experts // N_EXPERT_GROUPS
    shape3 = (N_EXPERT_GROUPS, per_group, tm)

    h = jnp.where(i < n_prompt_tiles, hp_ref[...], hs_ref[...])
    logits = lax.dot_general(rwt_ref[...], h.astype(BF16), (((1,), (1,)), ((), ())),
                             preferred_element_type=F32)
    scores = jax.nn.sigmoid(logits)
    sel = scores + bias_ref[...]
    scores3 = scores.reshape(shape3)
    grp = sel.reshape(shape3)
    member = lax.broadcasted_iota(jnp.int32, shape3, 1).astype(F32)
    group = lax.broadcasted_iota(jnp.int32, shape3, 0).astype(F32)
    expert = group * per_group + member

    m1 = jnp.max(grp, axis=1, keepdims=True)
    f1 = jnp.min(jnp.where(grp == m1, member, float(per_group)), axis=1, keepdims=True)
    m2 = jnp.max(jnp.where(member == f1, NEG_INF, grp), axis=1, keepdims=True)
    gscore = m1 + m2

    gid = lax.broadcasted_iota(jnp.int32, gscore.shape, 0).astype(F32)
    gmask = jnp.zeros(gscore.shape, F32)
    cur = gscore
    for _ in range(TOPK_GROUPS):
        mx = jnp.max(cur, axis=0, keepdims=True)
        pick = gid == jnp.min(jnp.where(cur == mx, gid, float(N_EXPERT_GROUPS)), axis=0, keepdims=True)
        gmask = jnp.where(pick, 1.0, gmask)
        cur = jnp.where(pick, NEG_INF, cur)

    cand = jnp.where(gmask > 0.0, grp, NEG_INF)
    chosen = jnp.zeros(shape3, F32)
    picks, firsts, weights = [], [], []
    for _ in range(TOP_K):
        mx = _over_experts(jnp.max, cand)
        first = _over_experts(jnp.min, jnp.where(cand == mx, expert, float(n_experts)))
        pick = expert == first
        picks.append(pick)
        firsts.append(first)
        weights.append(_over_experts(jnp.sum, jnp.where(pick, scores3, 0.0)))
        chosen = jnp.where(pick, 1.0, chosen)
        cand = jnp.where(pick, NEG_INF, cand)
    wsum = weights[0]
    for w in weights[1:]:
        wsum = wsum + w

    chosen_b = chosen.reshape(n_experts, tm).astype(BF16)
    t_from = lax.broadcasted_iota(jnp.int32, (tm, tm), 0)
    t_to = lax.broadcasted_iota(jnp.int32, (tm, tm), 1)
    earlier = ((t_from < t_to) & (t_from // MOE_TILE == t_to // MOE_TILE)).astype(BF16)
    before3 = jnp.dot(chosen_b, earlier, preferred_element_type=F32).reshape(shape3)

    pad = SUBLANES - TOP_K
    eidx = [f.reshape(1, tm).astype(jnp.int32) for f in firsts]
    rank = [_over_experts(jnp.sum, jnp.where(p, before3, 0.0)).reshape(1, tm).astype(jnp.int32) for p in picks]
    gate = [(w / wsum * ROUTED_SCALE).reshape(1, tm) for w in weights]
    eidx_ref[...] = jnp.concatenate(eidx + [jnp.zeros((pad, tm), jnp.int32)], axis=0)
    rank_ref[...] = jnp.concatenate(rank + [jnp.zeros((pad, tm), jnp.int32)], axis=0)
    gate_ref[...] = jnp.concatenate(gate + [jnp.zeros((pad, tm), F32)], axis=0)

    for sub in range(tm // MOE_TILE):
        cnt_ref[sub * SUBLANES:(sub + 1) * SUBLANES, :] = lax.dot_general(
            jnp.ones((SUBLANES, MOE_TILE), BF16), chosen_b[:, sub * MOE_TILE:(sub + 1) * MOE_TILE],
            (((1,), (1,)), ((), ())), preferred_element_type=F32)


def _route(h_p, h_s, rwt_b, bias_col):
    d = h_p.shape[1]
    t_all = h_p.shape[0] + h_s.shape[0]
    n_experts = rwt_b.shape[0]
    tm = ROUTE_TILE
    npt = h_p.shape[0] // tm
    moe_tiles = tm // MOE_TILE
    row_spec = pl.BlockSpec((SUBLANES, tm), lambda i: (0, i))
    return pl.pallas_call(
        functools.partial(_route_kernel, n_experts=n_experts, n_prompt_tiles=npt),
        grid=(t_all // tm,),
        in_specs=_two_group_specs(tm, d, npt) + [
            pl.BlockSpec((n_experts, d), lambda i: (0, 0)),
            pl.BlockSpec((n_experts, 1), lambda i: (0, 0)),
        ],
        out_specs=[row_spec, row_spec, row_spec,
                   pl.BlockSpec((moe_tiles * SUBLANES, n_experts), lambda i: (i, 0))],
        out_shape=[
            jax.ShapeDtypeStruct((SUBLANES, t_all), jnp.int32),
            jax.ShapeDtypeStruct((SUBLANES, t_all), jnp.int32),
            jax.ShapeDtypeStruct((SUBLANES, t_all), F32),
            jax.ShapeDtypeStruct((t_all // MOE_TILE * SUBLANES, n_experts), F32),
        ],
        compiler_params=pltpu.CompilerParams(dimension_semantics=("arbitrary",),
                                             vmem_limit_bytes=VMEM_LIMIT_BYTES),
        name="route",
    )(h_p, h_s, rwt_b, bias_col)


def _for_each_part(rows, max_rows, fn):
    for size in [SUBLANES << b for b in range((max_rows // SUBLANES).bit_length())]:
        @pl.when((rows & size) != 0)
        def _(size=size):
            fn(pl.multiple_of(rows & (size - 1), SUBLANES), size)


def _start_run_copies(tile, cnt_ref, ls_ref, gs_ref, big_ref, n_experts, start_copy):
    def run(e):
        idx = tile * n_experts + e
        return cnt_ref[idx], ls_ref[idx], gs_ref[idx]

    for e in range(n_experts):
        rows, l0, g0 = run(e)
        _for_each_part(rows, RUN_CHUNK // 2, lambda off, size: start_copy(
            pl.multiple_of(l0 + off, SUBLANES), pl.multiple_of(g0 + off, SUBLANES), size))

    @pl.when(big_ref[tile] != 0)
    def _():
        def per_expert(e, c):
            rows, l0, g0 = run(e)
            base = rows & (RUN_CHUNK - 1)

            def chunk(j, c2):
                off = base + j * RUN_CHUNK
                start_copy(pl.multiple_of(l0 + off, SUBLANES), pl.multiple_of(g0 + off, SUBLANES), RUN_CHUNK)
                return c2

            lax.fori_loop(0, lax.shift_right_logical(rows, RUN_CHUNK.bit_length() - 1), chunk, 0)
            return c

        lax.fori_loop(0, n_experts, per_expert, 0)


def _top_bit(n):
    return 1 << (n.bit_length() - 1)


def _pack_rows(x):
    rows, two_w = x.shape
    w = two_w // 2
    x3 = x.reshape(rows // SUBLANES, SUBLANES, two_w)
    halves = jnp.concatenate([x3[:, :, :w], x3[:, :, w:]], axis=1).astype(BF16)
    return pltpu.bitcast(halves, jnp.uint32).reshape(rows, w)


def _unpack_rows(p):
    rows, w = p.shape
    halves = pltpu.bitcast(p.reshape(rows // SUBLANES, SUBLANES, w), BF16).astype(F32)
    return jnp.concatenate([halves[:, :SUBLANES, :], halves[:, SUBLANES:, :]], axis=-1).reshape(rows, 2 * w)


def _local_rows(tm, n_experts):
    n = TOP_K * tm + n_experts * (SUBLANES - 1)
    return -(-n // SORT_CHUNK) * SORT_CHUNK


def _dispatch_kernel(cnt_ref, ls_ref, gs_ref, tot_ref, big_ref, zstart_ref, zlen_ref, lp_ref, hp_ref, hs_ref,
                     xs_hbm, lbuf, zero_s, sem, zero_sem, *, n_experts, n_prompt_tiles):
    i = pl.program_id(0)
    tm, d = hp_ref.shape
    n = TOP_K * tm
    nl = lbuf.shape[1]
    slot = i % 2

    def tile_wait(tile):
        pltpu.make_async_copy(lbuf.at[0, pl.ds(0, n)], xs_hbm.at[pl.ds(0, n)], sem).wait()
        _for_each_part(tot_ref[tile] - n, _top_bit(n_experts * (SUBLANES - 1)), lambda off, size:
                       pltpu.make_async_copy(lbuf.at[0, pl.ds(0, size)], xs_hbm.at[pl.ds(0, size)], sem).wait())

    def clear_padding(act):
        def per_expert(e, c):
            _for_each_part(zlen_ref[e], zero_s.shape[0], lambda off, size: act(pltpu.make_async_copy(
                zero_s.at[pl.ds(0, size)],
                xs_hbm.at[pl.ds(pl.multiple_of(zstart_ref[e] + off, SUBLANES), size)], zero_sem)))
            return c

        lax.fori_loop(0, n_experts, per_expert, 0)

    @pl.when(i == 0)
    def _():
        zero_s[...] = jnp.zeros(zero_s.shape, zero_s.dtype)
        clear_padding(lambda copy: copy.start())

    hb = jnp.where(i < n_prompt_tiles, hp_ref[...], hs_ref[...]).astype(BF16)
    lp = lp_ref[...]
    def sort_chunk(r):
        rows = r * SORT_CHUNK + lax.broadcasted_iota(jnp.int32, (SORT_CHUNK, tm), 0)
        hit = jnp.zeros((SORT_CHUNK, tm), F32)
        for k in range(TOP_K):
            hit = jnp.where(rows == lp[k:k + 1, :], 1.0, hit)
        lbuf[slot, pl.ds(r * SORT_CHUNK, SORT_CHUNK), :] = _pack_rows(
            jnp.dot(hit.astype(BF16), hb, preferred_element_type=F32))

    last_chunk = nl // SORT_CHUNK - 1
    for r in range(last_chunk):
        sort_chunk(r)
    pl.when(tot_ref[i] > last_chunk * SORT_CHUNK)(functools.partial(sort_chunk, last_chunk))

    @pl.when(i > 0)
    def _():
        tile_wait(i - 1)

    _start_run_copies(i, cnt_ref, ls_ref, gs_ref, big_ref, n_experts, lambda lrow, grow, size: pltpu.make_async_copy(
        lbuf.at[slot, pl.ds(lrow, size)], xs_hbm.at[pl.ds(grow, size)], sem).start())

    @pl.when(i == pl.num_programs(0) - 1)
    def _():
        tile_wait(i)
        clear_padding(lambda copy: copy.wait())


def _dispatch(tables, zstart, zlen, lp, h_p, h_s, *, n_rows, blk):
    d = h_p.shape[1]
    t_all = h_p.shape[0] + h_s.shape[0]
    tm = MOE_TILE
    npt = h_p.shape[0] // tm
    n_experts = zstart.shape[0]
    return pl.pallas_call(
        functools.partial(_dispatch_kernel, n_experts=n_experts, n_prompt_tiles=npt),
        grid_spec=pltpu.PrefetchScalarGridSpec(
            num_scalar_prefetch=7,
            grid=(t_all // tm,),
            in_specs=[pl.BlockSpec((SUBLANES, tm), lambda i, *_: (0, i))] + _two_group_specs(tm, d, npt),
            out_specs=pl.BlockSpec(memory_space=pl.ANY),
            scratch_shapes=[
                pltpu.VMEM((2, _local_rows(tm, n_experts), d // 2), jnp.uint32),
                pltpu.VMEM((_top_bit(blk - SUBLANES), d // 2), jnp.uint32),
                pltpu.SemaphoreType.DMA,
                pltpu.SemaphoreType.DMA,
            ],
        ),
        out_shape=jax.ShapeDtypeStruct((n_rows, d // 2), jnp.uint32),
        compiler_params=pltpu.CompilerParams(dimension_semantics=("arbitrary",),
                                             vmem_limit_bytes=VMEM_LIMIT_BYTES),
        name="dispatch",
    )(*tables, zstart, zlen, lp, h_p, h_s)


def _silu(x):
    return x * jax.nn.sigmoid(x)


def _expert_kernel(be_ref, nact_ref, xs_ref, wg_ref, wu_ref, wd_ref, ys_ref, wg_s, wu_s, wd_s):
    b = pl.program_id(0)

    @pl.when(b < nact_ref[0])
    def _():
        @pl.when((b == 0) | (be_ref[b] != be_ref[jnp.maximum(b - 1, 0)]))
        def _():
            wg_s[...] = wg_ref[...].astype(BF16)
            wu_s[...] = wu_ref[...].astype(BF16)
            wd_s[...] = wd_ref[...].astype(BF16)

        xb = _unpack_rows(xs_ref[...]).astype(BF16)
        hid = _silu(jnp.dot(xb, wg_s[...], preferred_element_type=F32)) * jnp.dot(
            xb, wu_s[...], preferred_element_type=F32)
        ys_ref[...] = _pack_rows(jnp.dot(hid.astype(BF16), wd_s[...], preferred_element_type=F32))


def _experts(block_expert, nact, xs, w_gate, w_up, w_down, *, blk):
    n_rows, dw = xs.shape
    d = w_gate.shape[-2]
    ff = w_gate.shape[-1]
    n_blocks = n_rows // blk

    def active(b, be, na):
        return jnp.minimum(b, na[0] - 1)

    return pl.pallas_call(
        _expert_kernel,
        grid_spec=pltpu.PrefetchScalarGridSpec(
            num_scalar_prefetch=2,
            grid=(n_blocks,),
            in_specs=[
                pl.BlockSpec((blk, dw), lambda b, be, na: (active(b, be, na), 0)),
                pl.BlockSpec((None, d, ff), lambda b, be, na: (be[active(b, be, na)], 0, 0)),
                pl.BlockSpec((None, d, ff), lambda b, be, na: (be[active(b, be, na)], 0, 0)),
                pl.BlockSpec((None, ff, d), lambda b, be, na: (be[active(b, be, na)], 0, 0)),
            ],
            out_specs=pl.BlockSpec((blk, dw), lambda b, be, na: (active(b, be, na), 0)),
            scratch_shapes=[
                pltpu.VMEM((d, ff), BF16),
                pltpu.VMEM((d, ff), BF16),
                pltpu.VMEM((ff, d), BF16),
            ],
        ),
        out_shape=jax.ShapeDtypeStruct((n_rows, dw), jnp.uint32),
        compiler_params=pltpu.CompilerParams(dimension_semantics=("arbitrary",),
                                             vmem_limit_bytes=VMEM_LIMIT_BYTES),
        name="experts",
    )(block_expert, nact, xs, w_gate, w_up, w_down)


def _combine_kernel(cnt_ref, ls_ref, gs_ref, tot_ref, big_ref, ys_hbm, lpt_ref, gate_ref, hp_ref, hs_ref, wsg_ref, wsu_ref,
                    wsd_ref, lng_ref, lnb_ref, yp_ref, ysm_ref, ybuf, moe_s, sems, *, alpha, n_experts,
                    n_prompt_tiles):
    i = pl.program_id(0)
    tm, d = hp_ref.shape
    n = TOP_K * tm
    nl = ybuf.shape[1]
    slot = i % 2

    def fetch(tile, to):
        _start_run_copies(tile, cnt_ref, ls_ref, gs_ref, big_ref, n_experts, lambda lrow, grow, size:
                          pltpu.make_async_copy(ys_hbm.at[pl.ds(grow, size)], ybuf.at[to, pl.ds(lrow, size)],
                                                sems.at[to]).start())

    @pl.when(i == 0)
    def _():
        ybuf[...] = jnp.zeros(ybuf.shape, ybuf.dtype)
        fetch(0, 0)

    @pl.when(i + 1 < pl.num_programs(0))
    def _():
        fetch(i + 1, 1 - slot)

    h = jnp.where(i < n_prompt_tiles, hp_ref[...], hs_ref[...])
    hb = h.astype(BF16)
    hid = _silu(jnp.dot(hb, wsg_ref[...], preferred_element_type=F32)) * jnp.dot(
        hb, wsu_ref[...], preferred_element_type=F32)
    shared = jnp.dot(hid.astype(BF16), wsd_ref[...], preferred_element_type=F32)

    pltpu.make_async_copy(ys_hbm.at[pl.ds(0, n)], ybuf.at[slot, pl.ds(0, n)], sems.at[slot]).wait()
    _for_each_part(tot_ref[i] - n, _top_bit(n_experts * (SUBLANES - 1)), lambda off, size: pltpu.make_async_copy(
        ys_hbm.at[pl.ds(0, size)], ybuf.at[slot, pl.ds(0, size)], sems.at[slot]).wait())

    lpt = lpt_ref[...]
    gate = gate_ref[...]

    def chunk_sum(r):
        cols = r * SORT_CHUNK + lax.broadcasted_iota(jnp.int32, (tm, SORT_CHUNK), 1)
        g = jnp.zeros((tm, SORT_CHUNK), F32)
        for k in range(TOP_K):
            g = jnp.where(cols == lpt[:, k:k + 1], gate[:, k:k + 1], g)
        yb = _unpack_rows(ybuf[slot, pl.ds(r * SORT_CHUNK, SORT_CHUNK), :]).astype(BF16)
        return jnp.dot(g.astype(BF16), yb, preferred_element_type=F32)

    last_chunk = nl // SORT_CHUNK - 1
    moe = shared
    for r in range(last_chunk):
        moe = moe + chunk_sum(r)
    moe_s[...] = moe

    @pl.when(tot_ref[i] > last_chunk * SORT_CHUNK)
    def _():
        moe_s[...] += chunk_sum(last_chunk)

    y = _layer_norm(alpha * h + moe_s[...], lng_ref[...], lnb_ref[...])

    @pl.when(i < n_prompt_tiles)
    def _():
        yp_ref[...] = y

    @pl.when(i >= n_prompt_tiles)
    def _():
        ysm_ref[...] = y


def _combine(tables, ys, lp_t, gates_t, h_p, h_s, wsg_b, wsu_b, wsd_b, ln_g, ln_b, *, alpha):
    t_prompt, d = h_p.shape
    t_all = t_prompt + h_s.shape[0]
    tm = MOE_TILE
    ff = wsg_b.shape[-1]
    npt = t_prompt // tm
    n_experts = tables[0].shape[0] // (t_all // tm)
    const2 = lambda i, *_: (0, 0)
    return pl.pallas_call(
        functools.partial(_combine_kernel, alpha=alpha, n_experts=n_experts, n_prompt_tiles=npt),
        grid_spec=pltpu.PrefetchScalarGridSpec(
            num_scalar_prefetch=5,
            grid=(t_all // tm,),
            in_specs=[
                pl.BlockSpec(memory_space=pl.ANY),
                pl.BlockSpec((tm, SUBLANES), lambda i, *_: (i, 0)),
                pl.BlockSpec((tm, SUBLANES), lambda i, *_: (i, 0)),
                *_two_group_specs(tm, d, npt),
                pl.BlockSpec((d, ff), const2),
                pl.BlockSpec((d, ff), const2),
                pl.BlockSpec((ff, d), const2),
                pl.BlockSpec((1, d), const2),
                pl.BlockSpec((1, d), const2),
            ],
            out_specs=[
                pl.BlockSpec((tm, d), lambda i, *_: (jnp.minimum(i, npt - 1), 0)),
                pl.BlockSpec((tm, d), lambda i, *_: (jnp.maximum(i - npt, 0), 0)),
            ],
            scratch_shapes=[
                pltpu.VMEM((2, _local_rows(tm, n_experts), d // 2), jnp.uint32),
                pltpu.VMEM((tm, d), F32),
                pltpu.SemaphoreType.DMA((2,)),
            ],
        ),
        out_shape=[
            jax.ShapeDtypeStruct((t_prompt, d), F32),
            jax.ShapeDtypeStruct((t_all - t_prompt, d), F32),
        ],
        compiler_params=pltpu.CompilerParams(dimension_semantics=("arbitrary",),
                                             vmem_limit_bytes=VMEM_LIMIT_BYTES),
        name="combine",
    )(*tables, ys, lp_t, gates_t, h_p, h_s, wsg_b, wsu_b, wsd_b, ln_g, ln_b)


def _moe(h_p, h_s, router_w, router_bias, w_gate, w_up, w_down, ws_gate, ws_up, ws_down, ln_g, ln_b, *, alpha):
    t_all = h_p.shape[0] + h_s.shape[0]
    n_experts = router_w.shape[-1]
    blk = EXPERT_BLOCK
    tm = MOE_TILE
    nt = t_all // tm
    eidx8, lrank8, gate8, cnt8 = _route(h_p, h_s, router_w.T.astype(BF16), router_bias.reshape(n_experts, 1))

    cnt = cnt8.reshape(nt, SUBLANES, n_experts)[:, 0, :].astype(jnp.int32)
    cnt = (cnt + SUBLANES - 1) // SUBLANES * SUBLANES
    counts = jnp.sum(cnt, axis=0)
    blocks_per_e = (counts + blk - 1) // blk
    block_end = jnp.cumsum(blocks_per_e)
    pad_start = (block_end - blocks_per_e) * blk
    n_blocks = -(-(t_all * TOP_K + nt * n_experts * (SUBLANES - 1)) // blk) + n_experts
    block_expert = jnp.minimum(jnp.sum(block_end[None, :] <= jnp.arange(n_blocks)[:, None], axis=1),
                               n_experts - 1).astype(jnp.int32)
    nact = block_end[-1:].astype(jnp.int32)
    zstart = (pad_start + counts).astype(jnp.int32)
    zlen = (blocks_per_e * blk - counts).astype(jnp.int32)
    gstart = pad_start[None, :] + jnp.cumsum(cnt, axis=0) - cnt
    lstart = jnp.cumsum(cnt, axis=1) - cnt
    tables = tuple(a.reshape(-1).astype(jnp.int32) for a in (
        cnt, lstart, gstart, jnp.sum(cnt, axis=1), jnp.max(cnt, axis=1) >= RUN_CHUNK))
    lstart_tok = jnp.repeat(lstart, tm, axis=0)
    lp8 = jnp.sum(jnp.where(eidx8[..., None] == jnp.arange(n_experts), lstart_tok[None], 0), axis=-1) + lrank8
    lp8 = lp8.astype(jnp.int32)

    xs = _dispatch(tables, zstart, zlen, lp8, h_p, h_s, n_rows=n_blocks * blk, blk=blk)
    ys = _experts(block_expert, nact, xs, w_gate, w_up, w_down, blk=blk)
    return _combine(tables, ys, lp8.T, gate8.T, h_p, h_s, ws_gate.astype(BF16), ws_up.astype(BF16), ws_down.astype(BF16),
                    ln_g, ln_b, alpha=alpha)


def kernel(x_prompt, x_sample, cache_k, cache_v, state_conv, w_in, attn_sinks, conv_w, g_attn_out, g_conv_out, w_out, ln1_g, ln1_b, router_w, router_bias, w_gate, w_up, w_down, ws_gate, ws_up, ws_down, ln2_g, ln2_b):
    depth = w_in.shape[0]
    bsz, seq, d = x_prompt.shape
    dec_b, dec_seq, _ = x_sample.shape
    win = cache_k.shape[2]
    kv_w = N_KV_HEADS * HEAD_DIM
    t_prompt = bsz * seq
    t_all = t_prompt + dec_b * dec_seq
    alpha = (2.0 * depth) ** 0.25
    assert win == WINDOW and seq % PROMPT_TILE == 0 and dec_b % SAMPLE_SEQS == 0
    assert t_prompt % ROUTE_TILE == 0 and (t_all - t_prompt) % ROUTE_TILE == 0 and ROUTE_TILE % MOE_TILE == 0
    assert MOE_TILE & (MOE_TILE - 1) == 0 and (TOP_K * MOE_TILE) % SORT_CHUNK == 0

    tabs_p = _rope_tables(jnp.arange(seq))
    tabs_s = tuple(jnp.tile(t, (SAMPLE_SEQS, 1)) for t in _rope_tables(PAST_LEN + jnp.arange(dec_seq)))
    row = lambda a: a.reshape(1, -1)

    xp, xs = x_prompt, x_sample
    outs = [[] for _ in range(6)]
    for l in range(depth):
        win_b, wout_b = w_in[l].astype(BF16), w_out[l].astype(BF16)
        shared = (attn_sinks[l], conv_w[l], row(g_attn_out[l]), row(g_conv_out[l]), wout_b, row(ln1_g[l]),
                  row(ln1_b[l]))
        h_p, kp, vp, cp = _prompt_mixer(xp, win_b, tabs_p, *shared, alpha=alpha)
        h_s, kn, vn, cn = _sample_mixer(xs, cache_k[l].reshape(dec_b, win, kv_w),
                                        cache_v[l].reshape(dec_b, win, kv_w), state_conv[l], win_b, tabs_s,
                                        *shared, alpha=alpha)
        yp, ys = _moe(h_p, h_s, router_w[l], router_bias[l], w_gate[l], w_up[l], w_down[l], ws_gate[l], ws_up[l],
                      ws_down[l], row(ln2_g[l]), row(ln2_b[l]), alpha=alpha)
        xp, xs = yp.reshape(bsz, seq, d), ys.reshape(dec_b, dec_seq, d)
        heads = lambda a: a.reshape(a.shape[0], win, N_KV_HEADS, HEAD_DIM)
        for o, a in zip(outs, (heads(kp), heads(vp), cp, heads(kn), heads(vn), cn)):
            o.append(a)
    return (xp, xs) + tuple(jnp.stack(o, axis=0) for o in outs)
```

```python
import functools

import jax
import jax.numpy as jnp
from jax import lax
from jax.experimental import pallas as pl
from jax.experimental.pallas import tpu as pltpu

PAST_LEN = 16384
WINDOW = 128
HEAD_DIM = 64
N_KV_HEADS = 2
ROT_DIM = HEAD_DIM // 4
ROPE_THETA = 500000.0
CONV_K = 3
TOP_K = 6
N_EXPERT_GROUPS = 8
TOPK_GROUPS = 4
ROUTED_SCALE = 2.5
LN_EPS = 1e-5
RMS_EPS = 1e-6

LANES = 128
SUBLANES = 8
VMEM_LIMIT_BYTES = 56 * 1024 * 1024

PROMPT_TILE = 1024
SAMPLE_SEQS = 32
MOE_TILE = 256
SORT_CHUNK = 256
EXPERT_BLOCK = 2048
ROUTE_TILE = 1024
RUN_CHUNK = 64

F32 = jnp.float32
BF16 = jnp.bfloat16
NEG_INF = float("-inf")


def _rope_tables(positions):
    half = ROT_DIM // 2
    inv_freq = ROPE_THETA ** (-jnp.arange(0, ROT_DIM, 2, dtype=F32) / ROT_DIM)
    ang = positions.astype(F32)[:, None] * inv_freq[None, :]
    cos, sin = jnp.cos(ang), jnp.sin(ang)
    n = positions.shape[0]
    rest = HEAD_DIM - ROT_DIM
    c = jnp.concatenate([cos, cos, jnp.ones((n, rest), F32)], axis=-1)
    sa = jnp.concatenate([-sin, jnp.zeros((n, half + rest), F32)], axis=-1)
    sb = jnp.concatenate([jnp.zeros((n, half), F32), sin, jnp.zeros((n, rest), F32)], axis=-1)
    reps = LANES // HEAD_DIM
    return jnp.tile(c, (1, reps)), jnp.tile(sa, (1, reps)), jnp.tile(sb, (1, reps))


def _rope(x, c, sa, sb):
    half = ROT_DIM // 2
    return x * c + pltpu.roll(x, LANES - half, 1) * sa + pltpu.roll(x, half, 1) * sb


def _rms_norm(x, g):
    return x * lax.rsqrt(jnp.mean(jnp.square(x), axis=-1, keepdims=True) + RMS_EPS) * g


def _layer_norm(x, g, b):
    mu = jnp.mean(x, axis=-1, keepdims=True)
    var = jnp.mean(jnp.square(x - mu), axis=-1, keepdims=True)
    return (x - mu) * lax.rsqrt(var + LN_EPS) * g + b


def _short_conv(gated, prev2, prev1, row, conv_w, b_gate):
    g1 = pltpu.roll(gated, 1, 0)
    g2 = pltpu.roll(gated, 2, 0)
    g1 = jnp.where(row == 0, prev1, g1)
    g2 = jnp.where(row == 0, prev2, jnp.where(row == 1, prev1, g2))
    y = conv_w[0:1, :] * g2 + conv_w[1:2, :] * g1 + conv_w[2:3, :] * gated
    return b_gate * y


def _merge_norm(x, attn_o, conv_o, gattn, gconv, wout_ref, ln_g, ln_b, alpha):
    cat = jnp.concatenate([_rms_norm(attn_o, gattn), _rms_norm(conv_o, gconv)], axis=-1)
    mix = jnp.dot(cat.astype(BF16), wout_ref[...], preferred_element_type=F32)
    return _layer_norm(alpha * x + mix, ln_g, ln_b)


def _sink_softmax(parts, sink):
    m = sink
    for s in parts:
        m = jnp.maximum(m, jnp.max(s, axis=-1, keepdims=True))
    es = [jnp.exp(s - m) for s in parts]
    den = jnp.exp(sink - m)
    for e in es:
        den = den + jnp.sum(e, axis=-1, keepdims=True)
    return [(e / den).astype(BF16) for e in es]


def _prompt_mixer_kernel(x_ref, win_ref, c_ref, sa_ref, sb_ref, sinks_ref, convw_ref, gattn_ref, gconv_ref,
                         wout_ref, lng_ref, lnb_ref,
                         h_ref, ko_ref, vo_ref, co_ref,
                         q_s, kt_s, vm_s, o_s, gc_s, *, alpha, n_heads, attn_w, kv_w, conv_ch):
    s = pl.program_id(1)
    last = pl.num_programs(1) - 1
    tq = x_ref.shape[0]
    q_per_kv = n_heads // N_KV_HEADS
    heads_per_group = LANES // HEAD_DIM

    @pl.when(s == 0)
    def _():
        kt_s[:, :, 0:WINDOW] = jnp.zeros((kt_s.shape[0], LANES, WINDOW), BF16)
        vm_s[:, 0:WINDOW, :] = jnp.zeros((vm_s.shape[0], WINDOW, LANES), BF16)
        gc_s[...] = jnp.zeros(gc_s.shape, F32)

    @pl.when(s > 0)
    def _():
        kt_s[:, :, 0:WINDOW] = kt_s[:, :, tq:tq + WINDOW]
        vm_s[:, 0:WINDOW, :] = vm_s[:, tq:tq + WINDOW, :]

    x = x_ref[...]
    xb = x.astype(BF16)
    c, sa, sb = c_ref[...], sa_ref[...], sb_ref[...]

    def proj(lo, width):
        return jnp.dot(xb, win_ref[:, lo:lo + width], preferred_element_type=F32)

    scale = HEAD_DIM ** -0.5
    for j in range(attn_w // LANES):
        qj = _rope(proj(j * LANES, LANES), c, sa, sb)
        q_s[:, j * LANES:(j + 1) * LANES] = (qj * scale).astype(BF16)
    k = _rope(proj(attn_w, kv_w), c, sa, sb)
    v = proj(attn_w + kv_w, kv_w)

    @pl.when(s == last)
    def _():
        ko_ref[...] = k[tq - WINDOW:, :]
        vo_ref[...] = v[tq - WINDOW:, :]

    kt = k.T.astype(BF16)
    zeros_k = jnp.zeros((HEAD_DIM, tq), BF16)
    v_swapped = pltpu.roll(v, HEAD_DIM, 1)
    low_lanes = lax.broadcasted_iota(jnp.int32, (tq, LANES), 1) < HEAD_DIM
    for kvh in range(N_KV_HEADS):
        kt_h = kt[kvh * HEAD_DIM:(kvh + 1) * HEAD_DIM, :]
        v_lo = v if kvh == 0 else v_swapped
        v_hi = v_swapped if kvh == 0 else v
        kt_s[2 * kvh, :, WINDOW:] = jnp.concatenate([kt_h, zeros_k], axis=0)
        kt_s[2 * kvh + 1, :, WINDOW:] = jnp.concatenate([zeros_k, kt_h], axis=0)
        vm_s[2 * kvh, WINDOW:, :] = jnp.where(low_lanes, v_lo, 0.0).astype(BF16)
        vm_s[2 * kvh + 1, WINDOW:, :] = jnp.where(low_lanes, 0.0, v_hi).astype(BF16)

    qi = lax.broadcasted_iota(jnp.int32, (WINDOW, 2 * WINDOW), 0)
    ci = lax.broadcasted_iota(jnp.int32, (WINDOW, 2 * WINDOW), 1)
    band = (ci > qi) & (ci <= qi + WINDOW)
    for j in range(tq // WINDOW):
        r0 = j * WINDOW
        mask = band if j > 0 else band & ((ci >= WINDOW) | (s > 0))
        for grp in range(attn_w // LANES):
            q_grp = q_s[r0:r0 + WINDOW, grp * LANES:(grp + 1) * LANES]
            out = None
            for r in range(heads_per_group):
                hd = grp * heads_per_group + r
                src = 2 * (hd // q_per_kv) + r
                sink = sinks_ref[hd]
                sc = jnp.dot(q_grp, kt_s[src, :, r0:r0 + 2 * WINDOW], preferred_element_type=F32)
                sc = jnp.where(mask, sc, NEG_INF)
                m = jnp.maximum(jnp.max(sc, axis=-1, keepdims=True), sink)
                e = jnp.exp(sc - m)
                den = jnp.sum(e, axis=-1, keepdims=True) + jnp.exp(sink - m)
                o_h = jnp.dot(e.astype(BF16), vm_s[src, r0:r0 + 2 * WINDOW, :],
                              preferred_element_type=F32) * (1.0 / den)
                out = o_h if out is None else out + o_h
            o_s[r0:r0 + WINDOW, grp * LANES:(grp + 1) * LANES] = out

    o3 = attn_w + 2 * kv_w
    gated = proj(o3 + 2 * conv_ch, conv_ch) * proj(o3, conv_ch)
    row = lax.broadcasted_iota(jnp.int32, (tq, 1), 0)
    conv_o = _short_conv(gated, gc_s[0:1, :], gc_s[1:2, :], row, convw_ref[...], proj(o3 + conv_ch, conv_ch))
    gc_s[0:CONV_K - 1, :] = gated[tq - (CONV_K - 1):, :]

    @pl.when(s == last)
    def _():
        co_ref[...] = gated[tq - (CONV_K - 1):, :]

    h_ref[...] = _merge_norm(x, o_s[...], conv_o, gattn_ref[...], gconv_ref[...], wout_ref,
                             lng_ref[...], lnb_ref[...], alpha)


def _prompt_mixer(x, win_b, tabs, sinks, conv_w, g_attn, g_conv, wout_b, ln_g, ln_b, *, alpha):
    bsz, seq, d = x.shape
    tq = PROMPT_TILE
    ns = seq // tq
    attn_w = g_attn.shape[-1]
    conv_ch = g_conv.shape[-1]
    n_heads = attn_w // HEAD_DIM
    kv_w = N_KV_HEADS * HEAD_DIM
    assert kv_w == LANES and 2 * HEAD_DIM == LANES and (n_heads // N_KV_HEADS) % 2 == 0 and WINDOW == LANES
    in_cols = win_b.shape[-1]
    const2 = lambda b, s: (0, 0)
    kern = functools.partial(_prompt_mixer_kernel, alpha=alpha, n_heads=n_heads, attn_w=attn_w, kv_w=kv_w,
                             conv_ch=conv_ch)
    return pl.pallas_call(
        kern,
        grid=(bsz, ns),
        in_specs=[
            pl.BlockSpec((None, tq, d), lambda b, s: (b, s, 0)),
            pl.BlockSpec((d, in_cols), const2),
            pl.BlockSpec((tq, LANES), lambda b, s: (s, 0)),
            pl.BlockSpec((tq, LANES), lambda b, s: (s, 0)),
            pl.BlockSpec((tq, LANES), lambda b, s: (s, 0)),
            pl.BlockSpec(memory_space=pltpu.SMEM),
            pl.BlockSpec((CONV_K, conv_ch), const2),
            pl.BlockSpec((1, attn_w), const2),
            pl.BlockSpec((1, conv_ch), const2),
            pl.BlockSpec((attn_w + conv_ch, d), const2),
            pl.BlockSpec((1, d), const2),
            pl.BlockSpec((1, d), const2),
        ],
        out_specs=[
            pl.BlockSpec((tq, d), lambda b, s: (b * ns + s, 0)),
            pl.BlockSpec((None, WINDOW, kv_w), lambda b, s: (b, 0, 0)),
            pl.BlockSpec((None, WINDOW, kv_w), lambda b, s: (b, 0, 0)),
            pl.BlockSpec((None, CONV_K - 1, conv_ch), lambda b, s: (b, 0, 0)),
        ],
        out_shape=[
            jax.ShapeDtypeStruct((bsz * seq, d), F32),
            jax.ShapeDtypeStruct((bsz, WINDOW, kv_w), F32),
            jax.ShapeDtypeStruct((bsz, WINDOW, kv_w), F32),
            jax.ShapeDtypeStruct((bsz, CONV_K - 1, conv_ch), F32),
        ],
        scratch_shapes=[
            pltpu.VMEM((tq, attn_w), BF16),
            pltpu.VMEM((2 * N_KV_HEADS, LANES, WINDOW + tq), BF16),
            pltpu.VMEM((2 * N_KV_HEADS, WINDOW + tq, LANES), BF16),
            pltpu.VMEM((tq, attn_w), F32),
            pltpu.VMEM((SUBLANES, conv_ch), F32),
        ],
        compiler_params=pltpu.CompilerParams(dimension_semantics=("arbitrary", "arbitrary"),
                                             vmem_limit_bytes=VMEM_LIMIT_BYTES),
        name="prompt_mixer",
    )(x, win_b, *tabs, sinks, conv_w, g_attn, g_conv, wout_b, ln_g, ln_b)


def _sample_mixer_kernel(x_ref, ck_ref, cv_ref, st_ref, win_ref, c_ref, sa_ref, sb_ref, sinks_ref,
                         convw_ref, gattn_ref, gconv_ref, wout_ref, lng_ref, lnb_ref,
                         h_ref, ko_ref, vo_ref, co_ref, *, alpha, n_heads, attn_w, kv_w, conv_ch, dec_seq):
    nb, win = ck_ref.shape[0], ck_ref.shape[1]
    rows = nb * dec_seq
    q_per_kv = n_heads // N_KV_HEADS
    x = x_ref[...]
    xb = x.astype(BF16)
    c, sa, sb = c_ref[...], sa_ref[...], sb_ref[...]

    def proj(lo, width):
        return jnp.dot(xb, win_ref[:, lo:lo + width], preferred_element_type=F32)

    scale = HEAD_DIM ** -0.5
    k = _rope(proj(attn_w, kv_w), c, sa, sb)
    v = proj(attn_w + kv_w, kv_w)
    k3 = k.reshape(nb, dec_seq, kv_w)
    v3 = v.reshape(nb, dec_seq, kv_w)
    ck = ck_ref[...]
    cv = cv_ref[...]
    ko_ref[:, 0:win - dec_seq, :] = ck[:, dec_seq:, :]
    ko_ref[:, win - dec_seq:, :] = k3
    vo_ref[:, 0:win - dec_seq, :] = cv[:, dec_seq:, :]
    vo_ref[:, win - dec_seq:, :] = v3
    ckb, cvb, k3b, v3b = ck.astype(BF16), cv.astype(BF16), k3.astype(BF16), v3.astype(BF16)

    qrows = q_per_kv * dec_seq
    qi = lax.broadcasted_iota(jnp.int32, (nb, qrows, win), 1) % dec_seq
    mask_c = lax.broadcasted_iota(jnp.int32, (nb, qrows, win), 2) > qi + (win - WINDOW)
    qn = lax.broadcasted_iota(jnp.int32, (nb, qrows, dec_seq), 1) % dec_seq
    mask_n = lax.broadcasted_iota(jnp.int32, (nb, qrows, dec_seq), 2) <= qn
    sink_row = lax.broadcasted_iota(jnp.int32, (nb, qrows, 1), 1) // dec_seq

    q_chunks = [_rope(proj(j * LANES, LANES), c, sa, sb) * scale for j in range(attn_w // LANES)]
    heads_out = []
    for kvh in range(N_KV_HEADS):
        qs = []
        for g in range(q_per_kv):
            lo = (kvh * q_per_kv + g) * HEAD_DIM
            qh = q_chunks[lo // LANES][:, lo % LANES:lo % LANES + HEAD_DIM]
            qs.append(qh.reshape(nb, dec_seq, HEAD_DIM))
        qg = jnp.concatenate(qs, axis=1).astype(BF16)
        sl = slice(kvh * HEAD_DIM, (kvh + 1) * HEAD_DIM)
        sc_c = jnp.einsum("bqd,bkd->bqk", qg, ckb[:, :, sl], preferred_element_type=F32)
        sc_n = jnp.einsum("bqd,bkd->bqk", qg, k3b[:, :, sl], preferred_element_type=F32)
        sc_c = jnp.where(mask_c, sc_c, NEG_INF)
        sc_n = jnp.where(mask_n, sc_n, NEG_INF)
        sink = jnp.zeros((nb, qrows, 1), F32)
        for g in range(q_per_kv):
            sink = jnp.where(sink_row == g, sinks_ref[kvh * q_per_kv + g], sink)
        p_c, p_n = _sink_softmax([sc_c, sc_n], sink)
        og = (jnp.einsum("bqk,bkd->bqd", p_c, cvb[:, :, sl], preferred_element_type=F32)
              + jnp.einsum("bqk,bkd->bqd", p_n, v3b[:, :, sl], preferred_element_type=F32))
        for g in range(q_per_kv):
            heads_out.append(og[:, g * dec_seq:(g + 1) * dec_seq, :].reshape(rows, HEAD_DIM))
    attn_o = jnp.concatenate(heads_out, axis=-1)

    o3 = attn_w + 2 * kv_w
    gated = proj(o3 + 2 * conv_ch, conv_ch) * proj(o3, conv_ch)
    st = st_ref[...]
    prev2 = jnp.broadcast_to(st[:, 0:1, :], (nb, dec_seq, conv_ch)).reshape(rows, conv_ch)
    prev1 = jnp.broadcast_to(st[:, 1:2, :], (nb, dec_seq, conv_ch)).reshape(rows, conv_ch)
    row = lax.broadcasted_iota(jnp.int32, (rows, 1), 0) % dec_seq
    conv_o = _short_conv(gated, prev2, prev1, row, convw_ref[...], proj(o3 + conv_ch, conv_ch))
    co_ref[...] = gated.reshape(nb, dec_seq, conv_ch)[:, dec_seq - (CONV_K - 1):, :]

    h_ref[...] = _merge_norm(x, attn_o, conv_o, gattn_ref[...], gconv_ref[...], wout_ref,
                             lng_ref[...], lnb_ref[...], alpha)


def _sample_mixer(x, ck, cv, st, win_b, tabs, sinks, conv_w, g_attn, g_conv, wout_b, ln_g, ln_b, *, alpha):
    dec_b, dec_seq, d = x.shape
    assert dec_seq >= CONV_K - 1 and dec_seq % SUBLANES == 0
    nb = SAMPLE_SEQS
    rows = nb * dec_seq
    win = ck.shape[1]
    attn_w = g_attn.shape[-1]
    conv_ch = g_conv.shape[-1]
    n_heads = attn_w // HEAD_DIM
    kv_w = N_KV_HEADS * HEAD_DIM
    in_cols = win_b.shape[-1]
    const2 = lambda i: (0, 0)
    kern = functools.partial(_sample_mixer_kernel, alpha=alpha, n_heads=n_heads, attn_w=attn_w, kv_w=kv_w,
                             conv_ch=conv_ch, dec_seq=dec_seq)
    return pl.pallas_call(
        kern,
        grid=(dec_b // nb,),
        in_specs=[
            pl.BlockSpec((rows, d), lambda i: (i, 0)),
            pl.BlockSpec((nb, win, kv_w), lambda i: (i, 0, 0)),
            pl.BlockSpec((nb, win, kv_w), lambda i: (i, 0, 0)),
            pl.BlockSpec((nb, CONV_K - 1, conv_ch), lambda i: (i, 0, 0)),
            pl.BlockSpec((d, in_cols), const2),
            pl.BlockSpec((rows, LANES), const2),
            pl.BlockSpec((rows, LANES), const2),
            pl.BlockSpec((rows, LANES), const2),
            pl.BlockSpec(memory_space=pltpu.SMEM),
            pl.BlockSpec((CONV_K, conv_ch), const2),
            pl.BlockSpec((1, attn_w), const2),
            pl.BlockSpec((1, conv_ch), const2),
            pl.BlockSpec((attn_w + conv_ch, d), const2),
            pl.BlockSpec((1, d), const2),
            pl.BlockSpec((1, d), const2),
        ],
        out_specs=[
            pl.BlockSpec((rows, d), lambda i: (i, 0)),
            pl.BlockSpec((nb, win, kv_w), lambda i: (i, 0, 0)),
            pl.BlockSpec((nb, win, kv_w), lambda i: (i, 0, 0)),
            pl.BlockSpec((nb, CONV_K - 1, conv_ch), lambda i: (i, 0, 0)),
        ],
        out_shape=[
            jax.ShapeDtypeStruct((dec_b * dec_seq, d), F32),
            jax.ShapeDtypeStruct((dec_b, win, kv_w), F32),
            jax.ShapeDtypeStruct((dec_b, win, kv_w), F32),
            jax.ShapeDtypeStruct((dec_b, CONV_K - 1, conv_ch), F32),
        ],
        compiler_params=pltpu.CompilerParams(dimension_semantics=("arbitrary",),
                                             vmem_limit_bytes=VMEM_LIMIT_BYTES),
        name="sample_mixer",
    )(x.reshape(dec_b * dec_seq, d), ck, cv, st, win_b, *tabs, sinks, conv_w, g_attn, g_conv, wout_b,
      ln_g, ln_b)


def _over_experts(fn, x):
    return fn(fn(x, axis=0, keepdims=True), axis=1, keepdims=True)


def _two_group_specs(tm, d, n_prompt_tiles):
    return [pl.BlockSpec((tm, d), lambda i, *_: (jnp.minimum(i, n_prompt_tiles - 1), 0)),
            pl.BlockSpec((tm, d), lambda i, *_: (jnp.maximum(i - n_prompt_tiles, 0), 0))]


def _route_kernel(hp_ref, hs_ref, rwt_ref, bias_ref, eidx_ref, rank_ref, gate_ref, cnt_ref, earlier_s, *,
                  n_experts, n_prompt_tiles):
    i = pl.program_id(0)
    tm = hp_ref.shape[0]
    per_group = n_experts // N_EXPERT_GROUPS
    shape3 = (N_EXPERT_GROUPS, per_group, tm)

    @pl.when(i == 0)
    def _():
        t_from = lax.broadcasted_iota(jnp.int32, (tm, tm), 0)
        t_to = lax.broadcasted_iota(jnp.int32, (tm, tm), 1)
        earlier_s[...] = ((t_from < t_to) & (t_from // MOE_TILE == t_to // MOE_TILE)).astype(BF16)

    h = jnp.where(i < n_prompt_tiles, hp_ref[...], hs_ref[...])
    logits = lax.dot_general(rwt_ref[...], h.astype(BF16), (((1,), (1,)), ((), ())),
                             preferred_element_type=F32)
    scores = jax.nn.sigmoid(logits)
    sel = scores + bias_ref[...]
    scores3 = scores.reshape(shape3)
    grp = sel.reshape(shape3)
    member = lax.broadcasted_iota(jnp.int32, shape3, 1).astype(F32)
    group = lax.broadcasted_iota(jnp.int32, shape3, 0).astype(F32)
    expert = group * per_group + member

    m1 = jnp.max(grp, axis=1, keepdims=True)
    f1 = jnp.min(jnp.where(grp == m1, member, float(per_group)), axis=1, keepdims=True)
    m2 = jnp.max(jnp.where(member == f1, NEG_INF, grp), axis=1, keepdims=True)
    gscore = m1 + m2

    gid = lax.broadcasted_iota(jnp.int32, gscore.shape, 0).astype(F32)
    gmask = jnp.zeros(gscore.shape, F32)
    cur = gscore
    for _ in range(TOPK_GROUPS):
        mx = jnp.max(cur, axis=0, keepdims=True)
        pick = gid == jnp.min(jnp.where(cur == mx, gid, float(N_EXPERT_GROUPS)), axis=0, keepdims=True)
        gmask = jnp.where(pick, 1.0, gmask)
        cur = jnp.where(pick, NEG_INF, cur)

    cand = jnp.where(gmask > 0.0, grp, NEG_INF)
    chosen = jnp.zeros(shape3, F32)
    picks, firsts, weights = [], [], []
    for _ in range(TOP_K):
        mx = _over_experts(jnp.max, cand)
        first = _over_experts(jnp.min, jnp.where(cand == mx, expert, float(n_experts)))
        pick = expert == first
        picks.append(pick)
        firsts.append(first)
        weights.append(_over_experts(jnp.sum, jnp.where(pick, scores3, 0.0)))
        chosen = jnp.where(pick, 1.0, chosen)
        cand = jnp.where(pick, NEG_INF, cand)
    wsum = weights[0]
    for w in weights[1:]:
        wsum = wsum + w

    chosen_b = chosen.reshape(n_experts, tm).astype(BF16)
    before3 = jnp.dot(chosen_b, earlier_s[...], preferred_element_type=F32).reshape(shape3)

    pad = SUBLANES - TOP_K
    eidx = [f.reshape(1, tm).astype(jnp.int32) for f in firsts]
    rank = [_over_experts(jnp.sum, jnp.where(p, before3, 0.0)).reshape(1, tm).astype(jnp.int32) for p in picks]
    gate = [(w / wsum * ROUTED_SCALE).reshape(1, tm) for w in weights]
    eidx_ref[...] = jnp.concatenate(eidx + [jnp.zeros((pad, tm), jnp.int32)], axis=0)
    rank_ref[...] = jnp.concatenate(rank + [jnp.zeros((pad, tm), jnp.int32)], axis=0)
    gate_ref[...] = jnp.concatenate(gate + [jnp.zeros((pad, tm), F32)], axis=0)

    for sub in range(tm // MOE_TILE):
        cnt_ref[sub * SUBLANES:(sub + 1) * SUBLANES, :] = lax.dot_general(
            jnp.ones((SUBLANES, MOE_TILE), BF16), chosen_b[:, sub * MOE_TILE:(sub + 1) * MOE_TILE],
            (((1,), (1,)), ((), ())), preferred_element_type=F32)


def _route(h_p, h_s, rwt_b, bias_col):
    d = h_p.shape[1]
    t_all = h_p.shape[0] + h_s.shape[0]
    n_experts = rwt_b.shape[0]
    tm = ROUTE_TILE
    npt = h_p.shape[0] // tm
    moe_tiles = tm // MOE_TILE
    row_spec = pl.BlockSpec((SUBLANES, tm), lambda i: (0, i))
    return pl.pallas_call(
        functools.partial(_route_kernel, n_experts=n_experts, n_prompt_tiles=npt),
        grid=(t_all // tm,),
        in_specs=_two_group_specs(tm, d, npt) + [
            pl.BlockSpec((n_experts, d), lambda i: (0, 0)),
            pl.BlockSpec((n_experts, 1), lambda i: (0, 0)),
        ],
        out_specs=[row_spec, row_spec, row_spec,
                   pl.BlockSpec((moe_tiles * SUBLANES, n_experts), lambda i: (i, 0))],
        out_shape=[
            jax.ShapeDtypeStruct((SUBLANES, t_all), jnp.int32),
            jax.ShapeDtypeStruct((SUBLANES, t_all), jnp.int32),
            jax.ShapeDtypeStruct((SUBLANES, t_all), F32),
            jax.ShapeDtypeStruct((t_all // MOE_TILE * SUBLANES, n_experts), F32),
        ],
        scratch_shapes=[pltpu.VMEM((tm, tm), BF16)],
        compiler_params=pltpu.CompilerParams(dimension_semantics=("arbitrary",),
                                             vmem_limit_bytes=VMEM_LIMIT_BYTES),
        name="route",
    )(h_p, h_s, rwt_b, bias_col)


def _for_each_part(rows, max_rows, fn):
    for size in [SUBLANES << b for b in range((max_rows // SUBLANES).bit_length())]:
        @pl.when((rows & size) != 0)
        def _(size=size):
            fn(pl.multiple_of(rows & (size - 1), SUBLANES), size)


def _start_run_copies(tile, cnt_ref, ls_ref, gs_ref, big_ref, n_experts, start_copy):
    def run(e):
        idx = tile * n_experts + e
        return cnt_ref[idx], ls_ref[idx], gs_ref[idx]

    for e in range(n_experts):
        rows, l0, g0 = run(e)
        _for_each_part(rows, RUN_CHUNK // 2, lambda off, size: start_copy(
            pl.multiple_of(l0 + off, SUBLANES), pl.multiple_of(g0 + off, SUBLANES), size))

    @pl.when(big_ref[tile] != 0)
    def _():
        def per_expert(e, c):
            rows, l0, g0 = run(e)
            base = rows & (RUN_CHUNK - 1)

            def chunk(j, c2):
                off = base + j * RUN_CHUNK
                start_copy(pl.multiple_of(l0 + off, SUBLANES), pl.multiple_of(g0 + off, SUBLANES), RUN_CHUNK)
                return c2

            lax.fori_loop(0, lax.shift_right_logical(rows, RUN_CHUNK.bit_length() - 1), chunk, 0)
            return c

        lax.fori_loop(0, n_experts, per_expert, 0)


def _top_bit(n):
    return 1 << (n.bit_length() - 1)


def _pack_rows(x):
    rows, two_w = x.shape
    w = two_w // 2
    x3 = x.reshape(rows // SUBLANES, SUBLANES, two_w)
    halves = jnp.concatenate([x3[:, :, :w], x3[:, :, w:]], axis=1).astype(BF16)
    return pltpu.bitcast(halves, jnp.uint32).reshape(rows, w)


def _unpack_rows(p):
    rows, w = p.shape
    halves = pltpu.bitcast(p.reshape(rows // SUBLANES, SUBLANES, w), BF16).astype(F32)
    return jnp.concatenate([halves[:, :SUBLANES, :], halves[:, SUBLANES:, :]], axis=-1).reshape(rows, 2 * w)


def _local_rows(tm, n_experts):
    n = TOP_K * tm + n_experts * (SUBLANES - 1)
    return -(-n // SORT_CHUNK) * SORT_CHUNK


def _dispatch_kernel(cnt_ref, ls_ref, gs_ref, tot_ref, big_ref, zstart_ref, zlen_ref, lp_ref, hp_ref, hs_ref,
                     xs_hbm, lbuf, zero_s, sem, zero_sem, *, n_experts, n_prompt_tiles):
    i = pl.program_id(0)
    tm, d = hp_ref.shape
    n = TOP_K * tm
    nl = lbuf.shape[1]
    slot = i % 2

    def tile_wait(tile):
        pltpu.make_async_copy(lbuf.at[0, pl.ds(0, n)], xs_hbm.at[pl.ds(0, n)], sem).wait()
        _for_each_part(tot_ref[tile] - n, _top_bit(n_experts * (SUBLANES - 1)), lambda off, size:
                       pltpu.make_async_copy(lbuf.at[0, pl.ds(0, size)], xs_hbm.at[pl.ds(0, size)], sem).wait())

    def clear_padding(act):
        def per_expert(e, c):
            _for_each_part(zlen_ref[e], zero_s.shape[0], lambda off, size: act(pltpu.make_async_copy(
                zero_s.at[pl.ds(0, size)],
                xs_hbm.at[pl.ds(pl.multiple_of(zstart_ref[e] + off, SUBLANES), size)], zero_sem)))
            return c

        lax.fori_loop(0, n_experts, per_expert, 0)

    @pl.when(i == 0)
    def _():
        zero_s[...] = jnp.zeros(zero_s.shape, zero_s.dtype)
        clear_padding(lambda copy: copy.start())

    hb = jnp.where(i < n_prompt_tiles, hp_ref[...], hs_ref[...]).astype(BF16)
    lp = lp_ref[...]
    def sort_chunk(r):
        rows = r * SORT_CHUNK + lax.broadcasted_iota(jnp.int32, (SORT_CHUNK, tm), 0)
        hit = jnp.zeros((SORT_CHUNK, tm), F32)
        for k in range(TOP_K):
            hit = jnp.where(rows == lp[k:k + 1, :], 1.0, hit)
        lbuf[slot, pl.ds(r * SORT_CHUNK, SORT_CHUNK), :] = _pack_rows(
            jnp.dot(hit.astype(BF16), hb, preferred_element_type=F32))

    last_chunk = nl // SORT_CHUNK - 1
    for r in range(last_chunk):
        sort_chunk(r)
    pl.when(tot_ref[i] > last_chunk * SORT_CHUNK)(functools.partial(sort_chunk, last_chunk))

    @pl.when(i > 0)
    def _():
        tile_wait(i - 1)

    _start_run_copies(i, cnt_ref, ls_ref, gs_ref, big_ref, n_experts, lambda lrow, grow, size: pltpu.make_async_copy(
        lbuf.at[slot, pl.ds(lrow, size)], xs_hbm.at[pl.ds(grow, size)], sem).start())

    @pl.when(i == pl.num_programs(0) - 1)
    def _():
        tile_wait(i)
        clear_padding(lambda copy: copy.wait())


def _dispatch(tables, zstart, zlen, lp, h_p, h_s, *, n_rows, blk):
    d = h_p.shape[1]
    t_all = h_p.shape[0] + h_s.shape[0]
    tm = MOE_TILE
    npt = h_p.shape[0] // tm
    n_experts = zstart.shape[0]
    return pl.pallas_call(
        functools.partial(_dispatch_kernel, n_experts=n_experts, n_prompt_tiles=npt),
        grid_spec=pltpu.PrefetchScalarGridSpec(
            num_scalar_prefetch=7,
            grid=(t_all // tm,),
            in_specs=[pl.BlockSpec((SUBLANES, tm), lambda i, *_: (0, i))] + _two_group_specs(tm, d, npt),
            out_specs=pl.BlockSpec(memory_space=pl.ANY),
            scratch_shapes=[
                pltpu.VMEM((2, _local_rows(tm, n_experts), d // 2), jnp.uint32),
                pltpu.VMEM((_top_bit(blk - SUBLANES), d // 2), jnp.uint32),
                pltpu.SemaphoreType.DMA,
                pltpu.SemaphoreType.DMA,
            ],
        ),
        out_shape=jax.ShapeDtypeStruct((n_rows, d // 2), jnp.uint32),
        compiler_params=pltpu.CompilerParams(dimension_semantics=("arbitrary",),
                                             vmem_limit_bytes=VMEM_LIMIT_BYTES),
        name="dispatch",
    )(*tables, zstart, zlen, lp, h_p, h_s)


def _silu(x):
    return x * jax.nn.sigmoid(x)


def _expert_kernel(be_ref, nact_ref, xs_ref, wg_ref, wu_ref, wd_ref, ys_ref, wg_s, wu_s, wd_s):
    b = pl.program_id(0)

    @pl.when(b < nact_ref[0])
    def _():
        @pl.when((b == 0) | (be_ref[b] != be_ref[jnp.maximum(b - 1, 0)]))
        def _():
            wg_s[...] = wg_ref[...].astype(BF16)
            wu_s[...] = wu_ref[...].astype(BF16)
            wd_s[...] = wd_ref[...].astype(BF16)

        xb = _unpack_rows(xs_ref[...]).astype(BF16)
        hid = _silu(jnp.dot(xb, wg_s[...], preferred_element_type=F32)) * jnp.dot(
            xb, wu_s[...], preferred_element_type=F32)
        ys_ref[...] = _pack_rows(jnp.dot(hid.astype(BF16), wd_s[...], preferred_element_type=F32))


def _experts(block_expert, nact, xs, w_gate, w_up, w_down, *, blk):
    n_rows, dw = xs.shape
    d = w_gate.shape[-2]
    ff = w_gate.shape[-1]
    n_blocks = n_rows // blk

    def active(b, be, na):
        return jnp.minimum(b, na[0] - 1)

    return pl.pallas_call(
        _expert_kernel,
        grid_spec=pltpu.PrefetchScalarGridSpec(
            num_scalar_prefetch=2,
            grid=(n_blocks,),
            in_specs=[
                pl.BlockSpec((blk, dw), lambda b, be, na: (active(b, be, na), 0)),
                pl.BlockSpec((None, d, ff), lambda b, be, na: (be[active(b, be, na)], 0, 0)),
                pl.BlockSpec((None, d, ff), lambda b, be, na: (be[active(b, be, na)], 0, 0)),
                pl.BlockSpec((None, ff, d), lambda b, be, na: (be[active(b, be, na)], 0, 0)),
            ],
            out_specs=pl.BlockSpec((blk, dw), lambda b, be, na: (active(b, be, na), 0)),
            scratch_shapes=[
                pltpu.VMEM((d, ff), BF16),
                pltpu.VMEM((d, ff), BF16),
                pltpu.VMEM((ff, d), BF16),
            ],
        ),
        out_shape=jax.ShapeDtypeStruct((n_rows, dw), jnp.uint32),
        compiler_params=pltpu.CompilerParams(dimension_semantics=("arbitrary",),
                                             vmem_limit_bytes=VMEM_LIMIT_BYTES),
        name="experts",
    )(block_expert, nact, xs, w_gate, w_up, w_down)


def _combine_kernel(cnt_ref, ls_ref, gs_ref, tot_ref, big_ref, ys_hbm, lpt_ref, gate_ref, hp_ref, hs_ref, wsg_ref, wsu_ref,
                    wsd_ref, lng_ref, lnb_ref, yp_ref, ysm_ref, ybuf, moe_s, sems, *, alpha, n_experts,
                    n_prompt_tiles):
    i = pl.program_id(0)
    tm, d = hp_ref.shape
    n = TOP_K * tm
    nl = ybuf.shape[1]
    slot = i % 2

    def fetch(tile, to):
        _start_run_copies(tile, cnt_ref, ls_ref, gs_ref, big_ref, n_experts, lambda lrow, grow, size:
                          pltpu.make_async_copy(ys_hbm.at[pl.ds(grow, size)], ybuf.at[to, pl.ds(lrow, size)],
                                                sems.at[to]).start())

    @pl.when(i == 0)
    def _():
        ybuf[...] = jnp.zeros(ybuf.shape, ybuf.dtype)
        fetch(0, 0)

    @pl.when(i + 1 < pl.num_programs(0))
    def _():
        fetch(i + 1, 1 - slot)

    h = jnp.where(i < n_prompt_tiles, hp_ref[...], hs_ref[...])
    hb = h.astype(BF16)
    hid = _silu(jnp.dot(hb, wsg_ref[...], preferred_element_type=F32)) * jnp.dot(
        hb, wsu_ref[...], preferred_element_type=F32)
    shared = jnp.dot(hid.astype(BF16), wsd_ref[...], preferred_element_type=F32)

    pltpu.make_async_copy(ys_hbm.at[pl.ds(0, n)], ybuf.at[slot, pl.ds(0, n)], sems.at[slot]).wait()
    _for_each_part(tot_ref[i] - n, _top_bit(n_experts * (SUBLANES - 1)), lambda off, size: pltpu.make_async_copy(
        ys_hbm.at[pl.ds(0, size)], ybuf.at[slot, pl.ds(0, size)], sems.at[slot]).wait())

    lpt = lpt_ref[...]
    gate = gate_ref[...]

    def chunk_sum(r):
        cols = r * SORT_CHUNK + lax.broadcasted_iota(jnp.int32, (tm, SORT_CHUNK), 1)
        g = jnp.zeros((tm, SORT_CHUNK), F32)
        for k in range(TOP_K):
            g = jnp.where(cols == lpt[:, k:k + 1], gate[:, k:k + 1], g)
        yb = _unpack_rows(ybuf[slot, pl.ds(r * SORT_CHUNK, SORT_CHUNK), :]).astype(BF16)
        return jnp.dot(g.astype(BF16), yb, preferred_element_type=F32)

    last_chunk = nl // SORT_CHUNK - 1
    moe = shared
    for r in range(last_chunk):
        moe = moe + chunk_sum(r)
    moe_s[...] = moe

    @pl.when(tot_ref[i] > last_chunk * SORT_CHUNK)
    def _():
        moe_s[...] += chunk_sum(last_chunk)

    y = _layer_norm(alpha * h + moe_s[...], lng_ref[...], lnb_ref[...])

    @pl.when(i < n_prompt_tiles)
    def _():
        yp_ref[...] = y

    @pl.when(i >= n_prompt_tiles)
    def _():
        ysm_ref[...] = y


def _combine(tables, ys, lp_t, gates_t, h_p, h_s, wsg_b, wsu_b, wsd_b, ln_g, ln_b, *, alpha):
    t_prompt, d = h_p.shape
    t_all = t_prompt + h_s.shape[0]
    tm = MOE_TILE
    ff = wsg_b.shape[-1]
    npt = t_prompt // tm
    n_experts = tables[0].shape[0] // (t_all // tm)
    const2 = lambda i, *_: (0, 0)
    return pl.pallas_call(
        functools.partial(_combine_kernel, alpha=alpha, n_experts=n_experts, n_prompt_tiles=npt),
        grid_spec=pltpu.PrefetchScalarGridSpec(
            num_scalar_prefetch=5,
            grid=(t_all // tm,),
            in_specs=[
                pl.BlockSpec(memory_space=pl.ANY),
                pl.BlockSpec((tm, SUBLANES), lambda i, *_: (i, 0)),
                pl.BlockSpec((tm, SUBLANES), lambda i, *_: (i, 0)),
                *_two_group_specs(tm, d, npt),
                pl.BlockSpec((d, ff), const2),
                pl.BlockSpec((d, ff), const2),
                pl.BlockSpec((ff, d), const2),
                pl.BlockSpec((1, d), const2),
                pl.BlockSpec((1, d), const2),
            ],
            out_specs=[
                pl.BlockSpec((tm, d), lambda i, *_: (jnp.minimum(i, npt - 1), 0)),
                pl.BlockSpec((tm, d), lambda i, *_: (jnp.maximum(i - npt, 0), 0)),
            ],
            scratch_shapes=[
                pltpu.VMEM((2, _local_rows(tm, n_experts), d // 2), jnp.uint32),
                pltpu.VMEM((tm, d), F32),
                pltpu.SemaphoreType.DMA((2,)),
            ],
        ),
        out_shape=[
            jax.ShapeDtypeStruct((t_prompt, d), F32),
            jax.ShapeDtypeStruct((t_all - t_prompt, d), F32),
        ],
        compiler_params=pltpu.CompilerParams(dimension_semantics=("arbitrary",),
                                             vmem_limit_bytes=VMEM_LIMIT_BYTES),
        name="combine",
    )(*tables, ys, lp_t, gates_t, h_p, h_s, wsg_b, wsu_b, wsd_b, ln_g, ln_b)


def _moe(h_p, h_s, router_w, router_bias, w_gate, w_up, w_down, ws_gate, ws_up, ws_down, ln_g, ln_b, *, alpha):
    t_all = h_p.shape[0] + h_s.shape[0]
    n_experts = router_w.shape[-1]
    blk = EXPERT_BLOCK
    tm = MOE_TILE
    nt = t_all // tm
    eidx8, lrank8, gate8, cnt8 = _route(h_p, h_s, router_w.T.astype(BF16), router_bias.reshape(n_experts, 1))

    cnt = cnt8.reshape(nt, SUBLANES, n_experts)[:, 0, :].astype(jnp.int32)
    cnt = (cnt + SUBLANES - 1) // SUBLANES * SUBLANES
    counts = jnp.sum(cnt, axis=0)
    blocks_per_e = (counts + blk - 1) // blk
    block_end = jnp.cumsum(blocks_per_e)
    pad_start = (block_end - blocks_per_e) * blk
    n_blocks = -(-(t_all * TOP_K + nt * n_experts * (SUBLANES - 1)) // blk) + n_experts
    block_expert = jnp.minimum(jnp.sum(block_end[None, :] <= jnp.arange(n_blocks)[:, None], axis=1),
                               n_experts - 1).astype(jnp.int32)
    nact = block_end[-1:].astype(jnp.int32)
    zstart = (pad_start + counts).astype(jnp.int32)
    zlen = (blocks_per_e * blk - counts).astype(jnp.int32)
    gstart = pad_start[None, :] + jnp.cumsum(cnt, axis=0) - cnt
    lstart = jnp.cumsum(cnt, axis=1) - cnt
    tables = tuple(a.reshape(-1).astype(jnp.int32) for a in (
        cnt, lstart, gstart, jnp.sum(cnt, axis=1), jnp.max(cnt, axis=1) >= RUN_CHUNK))
    lstart_tok = jnp.repeat(lstart, tm, axis=0)
    lp8 = jnp.sum(jnp.where(eidx8[..., None] == jnp.arange(n_experts), lstart_tok[None], 0), axis=-1) + lrank8
    lp8 = lp8.astype(jnp.int32)

    xs = _dispatch(tables, zstart, zlen, lp8, h_p, h_s, n_rows=n_blocks * blk, blk=blk)
    ys = _experts(block_expert, nact, xs, w_gate, w_up, w_down, blk=blk)
    return _combine(tables, ys, lp8.T, gate8.T, h_p, h_s, ws_gate.astype(BF16), ws_up.astype(BF16), ws_down.astype(BF16),
                    ln_g, ln_b, alpha=alpha)


def kernel(x_prompt, x_sample, cache_k, cache_v, state_conv, w_in, attn_sinks, conv_w, g_attn_out, g_conv_out, w_out, ln1_g, ln1_b, router_w, router_bias, w_gate, w_up, w_down, ws_gate, ws_up, ws_down, ln2_g, ln2_b):
    depth = w_in.shape[0]
    bsz, seq, d = x_prompt.shape
    dec_b, dec_seq, _ = x_sample.shape
    win = cache_k.shape[2]
    kv_w = N_KV_HEADS * HEAD_DIM
    t_prompt = bsz * seq
    t_all = t_prompt + dec_b * dec_seq
    alpha = (2.0 * depth) ** 0.25
    assert win == WINDOW and seq % PROMPT_TILE == 0 and dec_b % SAMPLE_SEQS == 0
    assert t_prompt % ROUTE_TILE == 0 and (t_all - t_prompt) % ROUTE_TILE == 0 and ROUTE_TILE % MOE_TILE == 0
    assert MOE_TILE & (MOE_TILE - 1) == 0 and (TOP_K * MOE_TILE) % SORT_CHUNK == 0

    tabs_p = _rope_tables(jnp.arange(seq))
    tabs_s = tuple(jnp.tile(t, (SAMPLE_SEQS, 1)) for t in _rope_tables(PAST_LEN + jnp.arange(dec_seq)))
    row = lambda a: a.reshape(1, -1)

    xp, xs = x_prompt, x_sample
    outs = [[] for _ in range(6)]
    for l in range(depth):
        win_b, wout_b = w_in[l].astype(BF16), w_out[l].astype(BF16)
        shared = (attn_sinks[l], conv_w[l], row(g_attn_out[l]), row(g_conv_out[l]), wout_b, row(ln1_g[l]),
                  row(ln1_b[l]))
        h_p, kp, vp, cp = _prompt_mixer(xp, win_b, tabs_p, *shared, alpha=alpha)
        h_s, kn, vn, cn = _sample_mixer(xs, cache_k[l].reshape(dec_b, win, kv_w),
                                        cache_v[l].reshape(dec_b, win, kv_w), state_conv[l], win_b, tabs_s,
                                        *shared, alpha=alpha)
        yp, ys = _moe(h_p, h_s, router_w[l], router_bias[l], w_gate[l], w_up[l], w_down[l], ws_gate[l], ws_up[l],
                      ws_down[l], row(ln2_g[l]), row(ln2_b[l]), alpha=alpha)
        xp, xs = yp.reshape(bsz, seq, d), ys.reshape(dec_b, dec_seq, d)
        heads = lambda a: a.reshape(a.shape[0], win, N_KV_HEADS, HEAD_DIM)
        for o, a in zip(outs, (heads(kp), heads(vp), cp, heads(kn), heads(vn), cn)):
            o.append(a)
    return (xp, xs) + tuple(jnp.stack(o, axis=0) for o in outs)
```

```python
import functools

import jax
import jax.numpy as jnp
from jax import lax
from jax.experimental import pallas as pl
from jax.experimental.pallas import tpu as pltpu

PAST_LEN = 16384
WINDOW = 128
HEAD_DIM = 64
N_KV_HEADS = 2
ROT_DIM = HEAD_DIM // 4
ROPE_THETA = 500000.0
CONV_K = 3
TOP_K = 6
N_EXPERT_GROUPS = 8
TOPK_GROUPS = 4
ROUTED_SCALE = 2.5
LN_EPS = 1e-5
RMS_EPS = 1e-6

LANES = 128
SUBLANES = 8
VMEM_LIMIT_BYTES = 56 * 1024 * 1024

PROMPT_TILE = 1024
SAMPLE_SEQS = 32
MOE_TILE = 256
SORT_CHUNK = 256
EXPERT_BLOCK = 2048
ROUTE_TILE = 1024
RUN_CHUNK = 8

F32 = jnp.float32
BF16 = jnp.bfloat16
NEG_INF = float("-inf")


def _rope_tables(positions):
    half = ROT_DIM // 2
    inv_freq = ROPE_THETA ** (-jnp.arange(0, ROT_DIM, 2, dtype=F32) / ROT_DIM)
    ang = positions.astype(F32)[:, None] * inv_freq[None, :]
    cos, sin = jnp.cos(ang), jnp.sin(ang)
    n = positions.shape[0]
    rest = HEAD_DIM - ROT_DIM
    c = jnp.concatenate([cos, cos, jnp.ones((n, rest), F32)], axis=-1)
    sa = jnp.concatenate([-sin, jnp.zeros((n, half + rest), F32)], axis=-1)
    sb = jnp.concatenate([jnp.zeros((n, half), F32), sin, jnp.zeros((n, rest), F32)], axis=-1)
    reps = LANES // HEAD_DIM
    return jnp.tile(c, (1, reps)), jnp.tile(sa, (1, reps)), jnp.tile(sb, (1, reps))


def _rope(x, c, sa, sb):
    half = ROT_DIM // 2
    return x * c + pltpu.roll(x, LANES - half, 1) * sa + pltpu.roll(x, half, 1) * sb


def _rms_norm(x, g):
    return x * lax.rsqrt(jnp.mean(jnp.square(x), axis=-1, keepdims=True) + RMS_EPS) * g


def _layer_norm(x, g, b):
    mu = jnp.mean(x, axis=-1, keepdims=True)
    var = jnp.mean(jnp.square(x - mu), axis=-1, keepdims=True)
    return (x - mu) * lax.rsqrt(var + LN_EPS) * g + b


def _short_conv(gated, prev2, prev1, row, conv_w, b_gate):
    g1 = pltpu.roll(gated, 1, 0)
    g2 = pltpu.roll(gated, 2, 0)
    g1 = jnp.where(row == 0, prev1, g1)
    g2 = jnp.where(row == 0, prev2, jnp.where(row == 1, prev1, g2))
    y = conv_w[0:1, :] * g2 + conv_w[1:2, :] * g1 + conv_w[2:3, :] * gated
    return b_gate * y


def _merge_norm(x, attn_o, conv_o, gattn, gconv, wout_ref, ln_g, ln_b, alpha):
    cat = jnp.concatenate([_rms_norm(attn_o, gattn), _rms_norm(conv_o, gconv)], axis=-1)
    mix = jnp.dot(cat.astype(BF16), wout_ref[...], preferred_element_type=F32)
    return _layer_norm(alpha * x + mix, ln_g, ln_b)


def _sink_softmax(parts, sink):
    m = sink
    for s in parts:
        m = jnp.maximum(m, jnp.max(s, axis=-1, keepdims=True))
    es = [jnp.exp(s - m) for s in parts]
    den = jnp.exp(sink - m)
    for e in es:
        den = den + jnp.sum(e, axis=-1, keepdims=True)
    return [(e / den).astype(BF16) for e in es]


def _prompt_mixer_kernel(x_ref, win_ref, c_ref, sa_ref, sb_ref, sinks_ref, convw_ref, gattn_ref, gconv_ref,
                         wout_ref, lng_ref, lnb_ref,
                         h_ref, ko_ref, vo_ref, co_ref,
                         q_s, kt_s, vm_s, o_s, gc_s, *, alpha, n_heads, attn_w, kv_w, conv_ch):
    s = pl.program_id(1)
    last = pl.num_programs(1) - 1
    tq = x_ref.shape[0]
    q_per_kv = n_heads // N_KV_HEADS
    heads_per_group = LANES // HEAD_DIM

    @pl.when(s == 0)
    def _():
        kt_s[:, :, 0:WINDOW] = jnp.zeros((kt_s.shape[0], LANES, WINDOW), BF16)
        vm_s[:, 0:WINDOW, :] = jnp.zeros((vm_s.shape[0], WINDOW, LANES), BF16)
        gc_s[...] = jnp.zeros(gc_s.shape, F32)

    @pl.when(s > 0)
    def _():
        kt_s[:, :, 0:WINDOW] = kt_s[:, :, tq:tq + WINDOW]
        vm_s[:, 0:WINDOW, :] = vm_s[:, tq:tq + WINDOW, :]

    x = x_ref[...]
    xb = x.astype(BF16)
    c, sa, sb = c_ref[...], sa_ref[...], sb_ref[...]

    def proj(lo, width):
        return jnp.dot(xb, win_ref[:, lo:lo + width], preferred_element_type=F32)

    scale = HEAD_DIM ** -0.5
    for j in range(attn_w // LANES):
        qj = _rope(proj(j * LANES, LANES), c, sa, sb)
        q_s[:, j * LANES:(j + 1) * LANES] = (qj * scale).astype(BF16)
    k = _rope(proj(attn_w, kv_w), c, sa, sb)
    v = proj(attn_w + kv_w, kv_w)

    @pl.when(s == last)
    def _():
        ko_ref[...] = k[tq - WINDOW:, :]
        vo_ref[...] = v[tq - WINDOW:, :]

    kt = k.T.astype(BF16)
    zeros_k = jnp.zeros((HEAD_DIM, tq), BF16)
    v_swapped = pltpu.roll(v, HEAD_DIM, 1)
    low_lanes = lax.broadcasted_iota(jnp.int32, (tq, LANES), 1) < HEAD_DIM
    for kvh in range(N_KV_HEADS):
        kt_h = kt[kvh * HEAD_DIM:(kvh + 1) * HEAD_DIM, :]
        v_lo = v if kvh == 0 else v_swapped
        v_hi = v_swapped if kvh == 0 else v
        kt_s[2 * kvh, :, WINDOW:] = jnp.concatenate([kt_h, zeros_k], axis=0)
        kt_s[2 * kvh + 1, :, WINDOW:] = jnp.concatenate([zeros_k, kt_h], axis=0)
        vm_s[2 * kvh, WINDOW:, :] = jnp.where(low_lanes, v_lo, 0.0).astype(BF16)
        vm_s[2 * kvh + 1, WINDOW:, :] = jnp.where(low_lanes, 0.0, v_hi).astype(BF16)

    qi = lax.broadcasted_iota(jnp.int32, (WINDOW, 2 * WINDOW), 0)
    ci = lax.broadcasted_iota(jnp.int32, (WINDOW, 2 * WINDOW), 1)
    band = (ci > qi) & (ci <= qi + WINDOW)
    for j in range(tq // WINDOW):
        r0 = j * WINDOW
        mask = band if j > 0 else band & ((ci >= WINDOW) | (s > 0))
        for grp in range(attn_w // LANES):
            q_grp = q_s[r0:r0 + WINDOW, grp * LANES:(grp + 1) * LANES]
            out = None
            for r in range(heads_per_group):
                hd = grp * heads_per_group + r
                src = 2 * (hd // q_per_kv) + r
                sink = sinks_ref[hd]
                sc = jnp.dot(q_grp, kt_s[src, :, r0:r0 + 2 * WINDOW], preferred_element_type=F32)
                sc = jnp.where(mask, sc, NEG_INF)
                m = jnp.maximum(jnp.max(sc, axis=-1, keepdims=True), sink)
                e = jnp.exp(sc - m)
                den = jnp.sum(e, axis=-1, keepdims=True) + jnp.exp(sink - m)
                o_h = jnp.dot(e.astype(BF16), vm_s[src, r0:r0 + 2 * WINDOW, :],
                              preferred_element_type=F32) * (1.0 / den)
                out = o_h if out is None else out + o_h
            o_s[r0:r0 + WINDOW, grp * LANES:(grp + 1) * LANES] = out

    o3 = attn_w + 2 * kv_w
    gated = proj(o3 + 2 * conv_ch, conv_ch) * proj(o3, conv_ch)
    row = lax.broadcasted_iota(jnp.int32, (tq, 1), 0)
    conv_o = _short_conv(gated, gc_s[0:1, :], gc_s[1:2, :], row, convw_ref[...], proj(o3 + conv_ch, conv_ch))
    gc_s[0:CONV_K - 1, :] = gated[tq - (CONV_K - 1):, :]

    @pl.when(s == last)
    def _():
        co_ref[...] = gated[tq - (CONV_K - 1):, :]

    h_ref[...] = _merge_norm(x, o_s[...], conv_o, gattn_ref[...], gconv_ref[...], wout_ref,
                             lng_ref[...], lnb_ref[...], alpha)


def _prompt_mixer(x, win_b, tabs, sinks, conv_w, g_attn, g_conv, wout_b, ln_g, ln_b, *, alpha):
    bsz, seq, d = x.shape
    tq = PROMPT_TILE
    ns = seq // tq
    attn_w = g_attn.shape[-1]
    conv_ch = g_conv.shape[-1]
    n_heads = attn_w // HEAD_DIM
    kv_w = N_KV_HEADS * HEAD_DIM
    assert kv_w == LANES and 2 * HEAD_DIM == LANES and (n_heads // N_KV_HEADS) % 2 == 0 and WINDOW == LANES
    in_cols = win_b.shape[-1]
    const2 = lambda b, s: (0, 0)
    kern = functools.partial(_prompt_mixer_kernel, alpha=alpha, n_heads=n_heads, attn_w=attn_w, kv_w=kv_w,
                             conv_ch=conv_ch)
    return pl.pallas_call(
        kern,
        grid=(bsz, ns),
        in_specs=[
            pl.BlockSpec((None, tq, d), lambda b, s: (b, s, 0)),
            pl.BlockSpec((d, in_cols), const2),
            pl.BlockSpec((tq, LANES), lambda b, s: (s, 0)),
            pl.BlockSpec((tq, LANES), lambda b, s: (s, 0)),
            pl.BlockSpec((tq, LANES), lambda b, s: (s, 0)),
            pl.BlockSpec(memory_space=pltpu.SMEM),
            pl.BlockSpec((CONV_K, conv_ch), const2),
            pl.BlockSpec((1, attn_w), const2),
            pl.BlockSpec((1, conv_ch), const2),
            pl.BlockSpec((attn_w + conv_ch, d), const2),
            pl.BlockSpec((1, d), const2),
            pl.BlockSpec((1, d), const2),
        ],
        out_specs=[
            pl.BlockSpec((tq, d), lambda b, s: (b * ns + s, 0)),
            pl.BlockSpec((None, WINDOW, kv_w), lambda b, s: (b, 0, 0)),
            pl.BlockSpec((None, WINDOW, kv_w), lambda b, s: (b, 0, 0)),
            pl.BlockSpec((None, CONV_K - 1, conv_ch), lambda b, s: (b, 0, 0)),
        ],
        out_shape=[
            jax.ShapeDtypeStruct((bsz * seq, d), F32),
            jax.ShapeDtypeStruct((bsz, WINDOW, kv_w), F32),
            jax.ShapeDtypeStruct((bsz, WINDOW, kv_w), F32),
            jax.ShapeDtypeStruct((bsz, CONV_K - 1, conv_ch), F32),
        ],
        scratch_shapes=[
            pltpu.VMEM((tq, attn_w), BF16),
            pltpu.VMEM((2 * N_KV_HEADS, LANES, WINDOW + tq), BF16),
            pltpu.VMEM((2 * N_KV_HEADS, WINDOW + tq, LANES), BF16),
            pltpu.VMEM((tq, attn_w), F32),
            pltpu.VMEM((SUBLANES, conv_ch), F32),
        ],
        compiler_params=pltpu.CompilerParams(dimension_semantics=("arbitrary", "arbitrary"),
                                             vmem_limit_bytes=VMEM_LIMIT_BYTES),
        name="prompt_mixer",
    )(x, win_b, *tabs, sinks, conv_w, g_attn, g_conv, wout_b, ln_g, ln_b)


def _sample_mixer_kernel(x_ref, ck_ref, cv_ref, st_ref, win_ref, c_ref, sa_ref, sb_ref, sinks_ref,
                         convw_ref, gattn_ref, gconv_ref, wout_ref, lng_ref, lnb_ref,
                         h_ref, ko_ref, vo_ref, co_ref, *, alpha, n_heads, attn_w, kv_w, conv_ch, dec_seq):
    nb, win = ck_ref.shape[0], ck_ref.shape[1]
    rows = nb * dec_seq
    q_per_kv = n_heads // N_KV_HEADS
    x = x_ref[...]
    xb = x.astype(BF16)
    c, sa, sb = c_ref[...], sa_ref[...], sb_ref[...]

    def proj(lo, width):
        return jnp.dot(xb, win_ref[:, lo:lo + width], preferred_element_type=F32)

    scale = HEAD_DIM ** -0.5
    k = _rope(proj(attn_w, kv_w), c, sa, sb)
    v = proj(attn_w + kv_w, kv_w)
    k3 = k.reshape(nb, dec_seq, kv_w)
    v3 = v.reshape(nb, dec_seq, kv_w)
    ck = ck_ref[...]
    cv = cv_ref[...]
    ko_ref[:, 0:win - dec_seq, :] = ck[:, dec_seq:, :]
    ko_ref[:, win - dec_seq:, :] = k3
    vo_ref[:, 0:win - dec_seq, :] = cv[:, dec_seq:, :]
    vo_ref[:, win - dec_seq:, :] = v3
    ckb, cvb, k3b, v3b = ck.astype(BF16), cv.astype(BF16), k3.astype(BF16), v3.astype(BF16)

    qrows = q_per_kv * dec_seq
    qi = lax.broadcasted_iota(jnp.int32, (nb, qrows, win), 1) % dec_seq
    mask_c = lax.broadcasted_iota(jnp.int32, (nb, qrows, win), 2) > qi + (win - WINDOW)
    qn = lax.broadcasted_iota(jnp.int32, (nb, qrows, dec_seq), 1) % dec_seq
    mask_n = lax.broadcasted_iota(jnp.int32, (nb, qrows, dec_seq), 2) <= qn
    sink_row = lax.broadcasted_iota(jnp.int32, (nb, qrows, 1), 1) // dec_seq

    q_chunks = [_rope(proj(j * LANES, LANES), c, sa, sb) * scale for j in range(attn_w // LANES)]
    heads_out = []
    for kvh in range(N_KV_HEADS):
        qs = []
        for g in range(q_per_kv):
            lo = (kvh * q_per_kv + g) * HEAD_DIM
            qh = q_chunks[lo // LANES][:, lo % LANES:lo % LANES + HEAD_DIM]
            qs.append(qh.reshape(nb, dec_seq, HEAD_DIM))
        qg = jnp.concatenate(qs, axis=1).astype(BF16)
        sl = slice(kvh * HEAD_DIM, (kvh + 1) * HEAD_DIM)
        sc_c = jnp.einsum("bqd,bkd->bqk", qg, ckb[:, :, sl], preferred_element_type=F32)
        sc_n = jnp.einsum("bqd,bkd->bqk", qg, k3b[:, :, sl], preferred_element_type=F32)
        sc_c = jnp.where(mask_c, sc_c, NEG_INF)
        sc_n = jnp.where(mask_n, sc_n, NEG_INF)
        sink = jnp.zeros((nb, qrows, 1), F32)
        for g in range(q_per_kv):
            sink = jnp.where(sink_row == g, sinks_ref[kvh * q_per_kv + g], sink)
        p_c, p_n = _sink_softmax([sc_c, sc_n], sink)
        og = (jnp.einsum("bqk,bkd->bqd", p_c, cvb[:, :, sl], preferred_element_type=F32)
              + jnp.einsum("bqk,bkd->bqd", p_n, v3b[:, :, sl], preferred_element_type=F32))
        for g in range(q_per_kv):
            heads_out.append(og[:, g * dec_seq:(g + 1) * dec_seq, :].reshape(rows, HEAD_DIM))
    attn_o = jnp.concatenate(heads_out, axis=-1)

    o3 = attn_w + 2 * kv_w
    gated = proj(o3 + 2 * conv_ch, conv_ch) * proj(o3, conv_ch)
    st = st_ref[...]
    prev2 = jnp.broadcast_to(st[:, 0:1, :], (nb, dec_seq, conv_ch)).reshape(rows, conv_ch)
    prev1 = jnp.broadcast_to(st[:, 1:2, :], (nb, dec_seq, conv_ch)).reshape(rows, conv_ch)
    row = lax.broadcasted_iota(jnp.int32, (rows, 1), 0) % dec_seq
    conv_o = _short_conv(gated, prev2, prev1, row, convw_ref[...], proj(o3 + conv_ch, conv_ch))
    co_ref[...] = gated.reshape(nb, dec_seq, conv_ch)[:, dec_seq - (CONV_K - 1):, :]

    h_ref[...] = _merge_norm(x, attn_o, conv_o, gattn_ref[...], gconv_ref[...], wout_ref,
                             lng_ref[...], lnb_ref[...], alpha)


def _sample_mixer(x, ck, cv, st, win_b, tabs, sinks, conv_w, g_attn, g_conv, wout_b, ln_g, ln_b, *, alpha):
    dec_b, dec_seq, d = x.shape
    assert dec_seq >= CONV_K - 1 and dec_seq % SUBLANES == 0
    nb = SAMPLE_SEQS
    rows = nb * dec_seq
    win = ck.shape[1]
    attn_w = g_attn.shape[-1]
    conv_ch = g_conv.shape[-1]
    n_heads = attn_w // HEAD_DIM
    kv_w = N_KV_HEADS * HEAD_DIM
    in_cols = win_b.shape[-1]
    const2 = lambda i: (0, 0)
    kern = functools.partial(_sample_mixer_kernel, alpha=alpha, n_heads=n_heads, attn_w=attn_w, kv_w=kv_w,
                             conv_ch=conv_ch, dec_seq=dec_seq)
    return pl.pallas_call(
        kern,
        grid=(dec_b // nb,),
        in_specs=[
            pl.BlockSpec((rows, d), lambda i: (i, 0)),
            pl.BlockSpec((nb, win, kv_w), lambda i: (i, 0, 0)),
            pl.BlockSpec((nb, win, kv_w), lambda i: (i, 0, 0)),
            pl.BlockSpec((nb, CONV_K - 1, conv_ch), lambda i: (i, 0, 0)),
            pl.BlockSpec((d, in_cols), const2),
            pl.BlockSpec((rows, LANES), const2),
            pl.BlockSpec((rows, LANES), const2),
            pl.BlockSpec((rows, LANES), const2),
            pl.BlockSpec(memory_space=pltpu.SMEM),
            pl.BlockSpec((CONV_K, conv_ch), const2),
            pl.BlockSpec((1, attn_w), const2),
            pl.BlockSpec((1, conv_ch), const2),
            pl.BlockSpec((attn_w + conv_ch, d), const2),
            pl.BlockSpec((1, d), const2),
            pl.BlockSpec((1, d), const2),
        ],
        out_specs=[
            pl.BlockSpec((rows, d), lambda i: (i, 0)),
            pl.BlockSpec((nb, win, kv_w), lambda i: (i, 0, 0)),
            pl.BlockSpec((nb, win, kv_w), lambda i: (i, 0, 0)),
            pl.BlockSpec((nb, CONV_K - 1, conv_ch), lambda i: (i, 0, 0)),
        ],
        out_shape=[
            jax.ShapeDtypeStruct((dec_b * dec_seq, d), F32),
            jax.ShapeDtypeStruct((dec_b, win, kv_w), F32),
            jax.ShapeDtypeStruct((dec_b, win, kv_w), F32),
            jax.ShapeDtypeStruct((dec_b, CONV_K - 1, conv_ch), F32),
        ],
        compiler_params=pltpu.CompilerParams(dimension_semantics=("arbitrary",),
                                             vmem_limit_bytes=VMEM_LIMIT_BYTES),
        name="sample_mixer",
    )(x.reshape(dec_b * dec_seq, d), ck, cv, st, win_b, *tabs, sinks, conv_w, g_attn, g_conv, wout_b,
      ln_g, ln_b)


def _over_experts(fn, x):
    return fn(fn(x, axis=0, keepdims=True), axis=1, keepdims=True)


def _two_group_specs(tm, d, n_prompt_tiles):
    return [pl.BlockSpec((tm, d), lambda i, *_: (jnp.minimum(i, n_prompt_tiles - 1), 0)),
            pl.BlockSpec((tm, d), lambda i, *_: (jnp.maximum(i - n_prompt_tiles, 0), 0))]


def _route_kernel(hp_ref, hs_ref, rwt_ref, bias_ref, eidx_ref, rank_ref, gate_ref, cnt_ref, earlier_s, *,
                  n_experts, n_prompt_tiles):
    i = pl.program_id(0)
    tm = hp_ref.shape[0]
    per_group = n_experts // N_EXPERT_GROUPS
    shape3 = (N_EXPERT_GROUPS, per_group, tm)

    @pl.when(i == 0)
    def _():
        t_from = lax.broadcasted_iota(jnp.int32, (tm, tm), 0)
        t_to = lax.broadcasted_iota(jnp.int32, (tm, tm), 1)
        earlier_s[...] = ((t_from < t_to) & (t_from // MOE_TILE == t_to // MOE_TILE)).astype(BF16)

    h = jnp.where(i < n_prompt_tiles, hp_ref[...], hs_ref[...])
    logits = lax.dot_general(rwt_ref[...], h.astype(BF16), (((1,), (1,)), ((), ())),
                             preferred_element_type=F32)
    scores = jax.nn.sigmoid(logits)
    sel = scores + bias_ref[...]
    scores3 = scores.reshape(shape3)
    grp = sel.reshape(shape3)
    member = lax.broadcasted_iota(jnp.int32, shape3, 1).astype(F32)
    group = lax.broadcasted_iota(jnp.int32, shape3, 0).astype(F32)
    expert = group * per_group + member

    m1 = jnp.max(grp, axis=1, keepdims=True)
    f1 = jnp.min(jnp.where(grp == m1, member, float(per_group)), axis=1, keepdims=True)
    m2 = jnp.max(jnp.where(member == f1, NEG_INF, grp), axis=1, keepdims=True)
    gscore = m1 + m2

    gid = lax.broadcasted_iota(jnp.int32, gscore.shape, 0).astype(F32)
    gmask = jnp.zeros(gscore.shape, F32)
    cur = gscore
    for _ in range(TOPK_GROUPS):
        mx = jnp.max(cur, axis=0, keepdims=True)
        pick = gid == jnp.min(jnp.where(cur == mx, gid, float(N_EXPERT_GROUPS)), axis=0, keepdims=True)
        gmask = jnp.where(pick, 1.0, gmask)
        cur = jnp.where(pick, NEG_INF, cur)

    cand = jnp.where(gmask > 0.0, grp, NEG_INF)
    chosen = jnp.zeros(shape3, F32)
    picks, firsts, weights = [], [], []
    for _ in range(TOP_K):
        mx = _over_experts(jnp.max, cand)
        first = _over_experts(jnp.min, jnp.where(cand == mx, expert, float(n_experts)))
        pick = expert == first
        picks.append(pick)
        firsts.append(first)
        weights.append(_over_experts(jnp.sum, jnp.where(pick, scores3, 0.0)))
        chosen = jnp.where(pick, 1.0, chosen)
        cand = jnp.where(pick, NEG_INF, cand)
    wsum = weights[0]
    for w in weights[1:]:
        wsum = wsum + w

    chosen_b = chosen.reshape(n_experts, tm).astype(BF16)
    before3 = jnp.dot(chosen_b, earlier_s[...], preferred_element_type=F32).reshape(shape3)

    pad = SUBLANES - TOP_K
    eidx = [f.reshape(1, tm).astype(jnp.int32) for f in firsts]
    rank = [_over_experts(jnp.sum, jnp.where(p, before3, 0.0)).reshape(1, tm).astype(jnp.int32) for p in picks]
    gate = [(w / wsum * ROUTED_SCALE).reshape(1, tm) for w in weights]
    eidx_ref[...] = jnp.concatenate(eidx + [jnp.zeros((pad, tm), jnp.int32)], axis=0)
    rank_ref[...] = jnp.concatenate(rank + [jnp.zeros((pad, tm), jnp.int32)], axis=0)
    gate_ref[...] = jnp.concatenate(gate + [jnp.zeros((pad, tm), F32)], axis=0)

    for sub in range(tm // MOE_TILE):
        cnt_ref[sub * SUBLANES:(sub + 1) * SUBLANES, :] = lax.dot_general(
            jnp.ones((SUBLANES, MOE_TILE), BF16), chosen_b[:, sub * MOE_TILE:(sub + 1) * MOE_TILE],
            (((1,), (1,)), ((), ())), preferred_element_type=F32)


def _route(h_p, h_s, rwt_b, bias_col):
    d = h_p.shape[1]
    t_all = h_p.shape[0] + h_s.shape[0]
    n_experts = rwt_b.shape[0]
    tm = ROUTE_TILE
    npt = h_p.shape[0] // tm
    moe_tiles = tm // MOE_TILE
    row_spec = pl.BlockSpec((SUBLANES, tm), lambda i: (0, i))
    return pl.pallas_call(
        functools.partial(_route_kernel, n_experts=n_experts, n_prompt_tiles=npt),
        grid=(t_all // tm,),
        in_specs=_two_group_specs(tm, d, npt) + [
            pl.BlockSpec((n_experts, d), lambda i: (0, 0)),
            pl.BlockSpec((n_experts, 1), lambda i: (0, 0)),
        ],
        out_specs=[row_spec, row_spec, row_spec,
                   pl.BlockSpec((moe_tiles * SUBLANES, n_experts), lambda i: (i, 0))],
        out_shape=[
            jax.ShapeDtypeStruct((SUBLANES, t_all), jnp.int32),
            jax.ShapeDtypeStruct((SUBLANES, t_all), jnp.int32),
            jax.ShapeDtypeStruct((SUBLANES, t_all), F32),
            jax.ShapeDtypeStruct((t_all // MOE_TILE * SUBLANES, n_experts), F32),
        ],
        scratch_shapes=[pltpu.VMEM((tm, tm), BF16)],
        compiler_params=pltpu.CompilerParams(dimension_semantics=("arbitrary",),
                                             vmem_limit_bytes=VMEM_LIMIT_BYTES),
        name="route",
    )(h_p, h_s, rwt_b, bias_col)


def _for_each_part(groups, max_part, fn):
    for size in [1 << b for b in range(max_part.bit_length())]:
        @pl.when((groups & size) != 0)
        def _(size=size):
            fn(groups & (size - 1), size)


def _start_run_copies(tile, cnt_ref, ls_ref, gs_ref, big_ref, n_experts, start_copy):
    def run(e):
        idx = tile * n_experts + e
        return cnt_ref[idx], ls_ref[idx], gs_ref[idx]

    for e in range(n_experts):
        groups, l0, g0 = run(e)
        _for_each_part(groups, RUN_CHUNK // 2, lambda off, size: start_copy(l0 + off, g0 + off, size))

    @pl.when(big_ref[tile] != 0)
    def _():
        def per_expert(e, c):
            groups, l0, g0 = run(e)
            base = groups & (RUN_CHUNK - 1)

            def chunk(j, c2):
                off = base + j * RUN_CHUNK
                start_copy(l0 + off, g0 + off, RUN_CHUNK)
                return c2

            lax.fori_loop(0, lax.shift_right_logical(groups, RUN_CHUNK.bit_length() - 1), chunk, 0)
            return c

        lax.fori_loop(0, n_experts, per_expert, 0)


def _top_bit(n):
    return 1 << (n.bit_length() - 1)


def _pack_rows(x):
    rows, two_w = x.shape
    w = two_w // 2
    x3 = x.reshape(rows // SUBLANES, SUBLANES, two_w)
    halves = jnp.concatenate([x3[:, :, :w], x3[:, :, w:]], axis=1).astype(BF16)
    return pltpu.bitcast(halves, jnp.uint32)


def _unpack_rows(p):
    groups, _, w = p.shape
    halves = pltpu.bitcast(p, BF16).astype(F32)
    return jnp.concatenate([halves[:, :SUBLANES, :], halves[:, SUBLANES:, :]], axis=-1).reshape(
        groups * SUBLANES, 2 * w)


def _local_groups(tm, n_experts):
    n = TOP_K * tm + n_experts * (SUBLANES - 1)
    return -(-n // SORT_CHUNK) * SORT_CHUNK // SUBLANES


def _dispatch_kernel(cnt_ref, ls_ref, gs_ref, tot_ref, big_ref, zstart_ref, zlen_ref, lp_ref, hp_ref, hs_ref,
                     xs_hbm, lbuf, zero_s, sem, zero_sem, *, n_experts, n_prompt_tiles):
    i = pl.program_id(0)
    tm, d = hp_ref.shape
    n = TOP_K * tm // SUBLANES
    chunk = SORT_CHUNK // SUBLANES
    slot = i % 2

    def tile_wait(tile):
        pltpu.make_async_copy(lbuf.at[0, pl.ds(0, n)], xs_hbm.at[pl.ds(0, n)], sem).wait()
        _for_each_part(tot_ref[tile] - n, _top_bit(n_experts * (SUBLANES - 1) // SUBLANES), lambda off, size:
                       pltpu.make_async_copy(lbuf.at[0, pl.ds(0, size)], xs_hbm.at[pl.ds(0, size)], sem).wait())

    def clear_padding(act):
        def per_expert(e, c):
            _for_each_part(zlen_ref[e], zero_s.shape[0], lambda off, size: act(pltpu.make_async_copy(
                zero_s.at[pl.ds(0, size)], xs_hbm.at[pl.ds(zstart_ref[e] + off, size)], zero_sem)))
            return c

        lax.fori_loop(0, n_experts, per_expert, 0)

    @pl.when(i == 0)
    def _():
        zero_s[...] = jnp.zeros(zero_s.shape, zero_s.dtype)
        clear_padding(lambda copy: copy.start())

    hb = jnp.where(i < n_prompt_tiles, hp_ref[...], hs_ref[...]).astype(BF16)
    lp = lp_ref[...]
    def sort_chunk(r):
        rows = r * SORT_CHUNK + lax.broadcasted_iota(jnp.int32, (SORT_CHUNK, tm), 0)
        hit = jnp.zeros((SORT_CHUNK, tm), F32)
        for k in range(TOP_K):
            hit = jnp.where(rows == lp[k:k + 1, :], 1.0, hit)
        lbuf[slot, pl.ds(r * chunk, chunk)] = _pack_rows(jnp.dot(hit.astype(BF16), hb, preferred_element_type=F32))

    last_chunk = lbuf.shape[1] // chunk - 1
    for r in range(last_chunk):
        sort_chunk(r)
    pl.when(tot_ref[i] > last_chunk * chunk)(functools.partial(sort_chunk, last_chunk))

    @pl.when(i > 0)
    def _():
        tile_wait(i - 1)

    _start_run_copies(i, cnt_ref, ls_ref, gs_ref, big_ref, n_experts, lambda lgrp, ggrp, size: pltpu.make_async_copy(
        lbuf.at[slot, pl.ds(lgrp, size)], xs_hbm.at[pl.ds(ggrp, size)], sem).start())

    @pl.when(i == pl.num_programs(0) - 1)
    def _():
        tile_wait(i)
        clear_padding(lambda copy: copy.wait())


def _dispatch(tables, zstart, zlen, lp, h_p, h_s, *, n_rows, blk):
    d = h_p.shape[1]
    t_all = h_p.shape[0] + h_s.shape[0]
    tm = MOE_TILE
    npt = h_p.shape[0] // tm
    n_experts = zstart.shape[0]
    return pl.pallas_call(
        functools.partial(_dispatch_kernel, n_experts=n_experts, n_prompt_tiles=npt),
        grid_spec=pltpu.PrefetchScalarGridSpec(
            num_scalar_prefetch=7,
            grid=(t_all // tm,),
            in_specs=[pl.BlockSpec((SUBLANES, tm), lambda i, *_: (0, i))] + _two_group_specs(tm, d, npt),
            out_specs=pl.BlockSpec(memory_space=pl.ANY),
            scratch_shapes=[
                pltpu.VMEM((2, _local_groups(tm, n_experts), SUBLANES, d // 2), jnp.uint32),
                pltpu.VMEM((_top_bit(blk // SUBLANES - 1), SUBLANES, d // 2), jnp.uint32),
                pltpu.SemaphoreType.DMA,
                pltpu.SemaphoreType.DMA,
            ],
        ),
        out_shape=jax.ShapeDtypeStruct((n_rows // SUBLANES, SUBLANES, d // 2), jnp.uint32),
        compiler_params=pltpu.CompilerParams(dimension_semantics=("arbitrary",),
                                             vmem_limit_bytes=VMEM_LIMIT_BYTES),
        name="dispatch",
    )(*tables, zstart, zlen, lp, h_p, h_s)


def _silu(x):
    return x * jax.nn.sigmoid(x)


def _expert_kernel(be_ref, nact_ref, xs_ref, wg_ref, wu_ref, wd_ref, ys_ref, wg_s, wu_s, wd_s):
    b = pl.program_id(0)

    @pl.when(b < nact_ref[0])
    def _():
        @pl.when((b == 0) | (be_ref[b] != be_ref[jnp.maximum(b - 1, 0)]))
        def _():
            wg_s[...] = wg_ref[...].astype(BF16)
            wu_s[...] = wu_ref[...].astype(BF16)
            wd_s[...] = wd_ref[...].astype(BF16)

        xb = _unpack_rows(xs_ref[...]).astype(BF16)
        hid = _silu(jnp.dot(xb, wg_s[...], preferred_element_type=F32)) * jnp.dot(
            xb, wu_s[...], preferred_element_type=F32)
        ys_ref[...] = _pack_rows(jnp.dot(hid.astype(BF16), wd_s[...], preferred_element_type=F32))


def _experts(block_expert, nact, xs, w_gate, w_up, w_down, *, blk):
    d = w_gate.shape[-2]
    ff = w_gate.shape[-1]
    block = (blk // SUBLANES,) + xs.shape[1:]
    n_blocks = xs.shape[0] // block[0]

    def active(b, be, na):
        return jnp.minimum(b, na[0] - 1)

    return pl.pallas_call(
        _expert_kernel,
        grid_spec=pltpu.PrefetchScalarGridSpec(
            num_scalar_prefetch=2,
            grid=(n_blocks,),
            in_specs=[
                pl.BlockSpec(block, lambda b, be, na: (active(b, be, na), 0, 0)),
                pl.BlockSpec((None, d, ff), lambda b, be, na: (be[active(b, be, na)], 0, 0)),
                pl.BlockSpec((None, d, ff), lambda b, be, na: (be[active(b, be, na)], 0, 0)),
                pl.BlockSpec((None, ff, d), lambda b, be, na: (be[active(b, be, na)], 0, 0)),
            ],
            out_specs=pl.BlockSpec(block, lambda b, be, na: (active(b, be, na), 0, 0)),
            scratch_shapes=[
                pltpu.VMEM((d, ff), BF16),
                pltpu.VMEM((d, ff), BF16),
                pltpu.VMEM((ff, d), BF16),
            ],
        ),
        out_shape=jax.ShapeDtypeStruct(xs.shape, jnp.uint32),
        compiler_params=pltpu.CompilerParams(dimension_semantics=("arbitrary",),
                                             vmem_limit_bytes=VMEM_LIMIT_BYTES),
        name="experts",
    )(block_expert, nact, xs, w_gate, w_up, w_down)


def _combine_kernel(cnt_ref, ls_ref, gs_ref, tot_ref, big_ref, ys_hbm, lpt_ref, gate_ref, hp_ref, hs_ref, wsg_ref, wsu_ref,
                    wsd_ref, lng_ref, lnb_ref, yp_ref, ysm_ref, ybuf, moe_s, sems, *, alpha, n_experts,
                    n_prompt_tiles):
    i = pl.program_id(0)
    tm, d = hp_ref.shape
    n = TOP_K * tm // SUBLANES
    chunk = SORT_CHUNK // SUBLANES
    slot = i % 2

    def fetch(tile, to):
        _start_run_copies(tile, cnt_ref, ls_ref, gs_ref, big_ref, n_experts, lambda lgrp, ggrp, size:
                          pltpu.make_async_copy(ys_hbm.at[pl.ds(ggrp, size)], ybuf.at[to, pl.ds(lgrp, size)],
                                                sems.at[to]).start())

    @pl.when(i == 0)
    def _():
        ybuf[...] = jnp.zeros(ybuf.shape, ybuf.dtype)
        fetch(0, 0)

    @pl.when(i + 1 < pl.num_programs(0))
    def _():
        fetch(i + 1, 1 - slot)

    h = jnp.where(i < n_prompt_tiles, hp_ref[...], hs_ref[...])
    hb = h.astype(BF16)
    hid = _silu(jnp.dot(hb, wsg_ref[...], preferred_element_type=F32)) * jnp.dot(
        hb, wsu_ref[...], preferred_element_type=F32)
    shared = jnp.dot(hid.astype(BF16), wsd_ref[...], preferred_element_type=F32)

    pltpu.make_async_copy(ys_hbm.at[pl.ds(0, n)], ybuf.at[slot, pl.ds(0, n)], sems.at[slot]).wait()
    _for_each_part(tot_ref[i] - n, _top_bit(n_experts * (SUBLANES - 1) // SUBLANES), lambda off, size:
                   pltpu.make_async_copy(ys_hbm.at[pl.ds(0, size)], ybuf.at[slot, pl.ds(0, size)],
                                         sems.at[slot]).wait())

    lpt = lpt_ref[...]
    gate = gate_ref[...]

    def chunk_sum(r):
        cols = r * SORT_CHUNK + lax.broadcasted_iota(jnp.int32, (tm, SORT_CHUNK), 1)
        g = jnp.zeros((tm, SORT_CHUNK), F32)
        for k in range(TOP_K):
            g = jnp.where(cols == lpt[:, k:k + 1], gate[:, k:k + 1], g)
        yb = _unpack_rows(ybuf[slot, pl.ds(r * chunk, chunk)]).astype(BF16)
        return jnp.dot(g.astype(BF16), yb, preferred_element_type=F32)

    last_chunk = ybuf.shape[1] // chunk - 1
    moe = shared
    for r in range(last_chunk):
        moe = moe + chunk_sum(r)
    moe_s[...] = moe

    @pl.when(tot_ref[i] > last_chunk * chunk)
    def _():
        moe_s[...] += chunk_sum(last_chunk)

    y = _layer_norm(alpha * h + moe_s[...], lng_ref[...], lnb_ref[...])

    @pl.when(i < n_prompt_tiles)
    def _():
        yp_ref[...] = y

    @pl.when(i >= n_prompt_tiles)
    def _():
        ysm_ref[...] = y


def _combine(tables, ys, lp_t, gates_t, h_p, h_s, wsg_b, wsu_b, wsd_b, ln_g, ln_b, *, alpha):
    t_prompt, d = h_p.shape
    t_all = t_prompt + h_s.shape[0]
    tm = MOE_TILE
    ff = wsg_b.shape[-1]
    npt = t_prompt // tm
    n_experts = tables[0].shape[0] // (t_all // tm)
    const2 = lambda i, *_: (0, 0)
    return pl.pallas_call(
        functools.partial(_combine_kernel, alpha=alpha, n_experts=n_experts, n_prompt_tiles=npt),
        grid_spec=pltpu.PrefetchScalarGridSpec(
            num_scalar_prefetch=5,
            grid=(t_all // tm,),
            in_specs=[
                pl.BlockSpec(memory_space=pl.ANY),
                pl.BlockSpec((tm, SUBLANES), lambda i, *_: (i, 0)),
                pl.BlockSpec((tm, SUBLANES), lambda i, *_: (i, 0)),
                *_two_group_specs(tm, d, npt),
                pl.BlockSpec((d, ff), const2),
                pl.BlockSpec((d, ff), const2),
                pl.BlockSpec((ff, d), const2),
                pl.BlockSpec((1, d), const2),
                pl.BlockSpec((1, d), const2),
            ],
            out_specs=[
                pl.BlockSpec((tm, d), lambda i, *_: (jnp.minimum(i, npt - 1), 0)),
                pl.BlockSpec((tm, d), lambda i, *_: (jnp.maximum(i - npt, 0), 0)),
            ],
            scratch_shapes=[
                pltpu.VMEM((2, _local_groups(tm, n_experts), SUBLANES, d // 2), jnp.uint32),
                pltpu.VMEM((tm, d), F32),
                pltpu.SemaphoreType.DMA((2,)),
            ],
        ),
        out_shape=[
            jax.ShapeDtypeStruct((t_prompt, d), F32),
            jax.ShapeDtypeStruct((t_all - t_prompt, d), F32),
        ],
        compiler_params=pltpu.CompilerParams(dimension_semantics=("arbitrary",),
                                             vmem_limit_bytes=VMEM_LIMIT_BYTES),
        name="combine",
    )(*tables, ys, lp_t, gates_t, h_p, h_s, wsg_b, wsu_b, wsd_b, ln_g, ln_b)


def _moe(h_p, h_s, router_w, router_bias, w_gate, w_up, w_down, ws_gate, ws_up, ws_down, ln_g, ln_b, *, alpha):
    t_all = h_p.shape[0] + h_s.shape[0]
    n_experts = router_w.shape[-1]
    blk = EXPERT_BLOCK
    tm = MOE_TILE
    nt = t_all // tm
    eidx8, lrank8, gate8, cnt8 = _route(h_p, h_s, router_w.T.astype(BF16), router_bias.reshape(n_experts, 1))

    bgrp = blk // SUBLANES
    cnt = cnt8.reshape(nt, SUBLANES, n_experts)[:, 0, :].astype(jnp.int32)
    cnt = (cnt + SUBLANES - 1) // SUBLANES
    counts = jnp.sum(cnt, axis=0)
    blocks_per_e = (counts + bgrp - 1) // bgrp
    block_end = jnp.cumsum(blocks_per_e)
    pad_start = (block_end - blocks_per_e) * bgrp
    n_blocks = -(-(t_all * TOP_K + nt * n_experts * (SUBLANES - 1)) // blk) + n_experts
    block_expert = jnp.minimum(jnp.sum(block_end[None, :] <= jnp.arange(n_blocks)[:, None], axis=1),
                               n_experts - 1).astype(jnp.int32)
    nact = block_end[-1:].astype(jnp.int32)
    zstart = (pad_start + counts).astype(jnp.int32)
    zlen = (blocks_per_e * bgrp - counts).astype(jnp.int32)
    gstart = pad_start[None, :] + jnp.cumsum(cnt, axis=0) - cnt
    lstart = jnp.cumsum(cnt, axis=1) - cnt
    tables = tuple(a.reshape(-1).astype(jnp.int32) for a in (
        cnt, lstart, gstart, jnp.sum(cnt, axis=1), jnp.max(cnt, axis=1) >= RUN_CHUNK))
    lstart_tok = jnp.repeat(lstart * SUBLANES, tm, axis=0)
    lp8 = jnp.sum(jnp.where(eidx8[..., None] == jnp.arange(n_experts), lstart_tok[None], 0), axis=-1) + lrank8
    lp8 = lp8.astype(jnp.int32)

    xs = _dispatch(tables, zstart, zlen, lp8, h_p, h_s, n_rows=n_blocks * blk, blk=blk)
    ys = _experts(block_expert, nact, xs, w_gate, w_up, w_down, blk=blk)
    return _combine(tables, ys, lp8.T, gate8.T, h_p, h_s, ws_gate.astype(BF16), ws_up.astype(BF16), ws_down.astype(BF16),
                    ln_g, ln_b, alpha=alpha)


def kernel(x_prompt, x_sample, cache_k, cache_v, state_conv, w_in, attn_sinks, conv_w, g_attn_out, g_conv_out, w_out, ln1_g, ln1_b, router_w, router_bias, w_gate, w_up, w_down, ws_gate, ws_up, ws_down, ln2_g, ln2_b):
    depth = w_in.shape[0]
    bsz, seq, d = x_prompt.shape
    dec_b, dec_seq, _ = x_sample.shape
    win = cache_k.shape[2]
    kv_w = N_KV_HEADS * HEAD_DIM
    t_prompt = bsz * seq
    t_all = t_prompt + dec_b * dec_seq
    alpha = (2.0 * depth) ** 0.25
    assert win == WINDOW and seq % PROMPT_TILE == 0 and dec_b % SAMPLE_SEQS == 0
    assert t_prompt % ROUTE_TILE == 0 and (t_all - t_prompt) % ROUTE_TILE == 0 and ROUTE_TILE % MOE_TILE == 0
    assert MOE_TILE & (MOE_TILE - 1) == 0 and (TOP_K * MOE_TILE) % SORT_CHUNK == 0

    tabs_p = _rope_tables(jnp.arange(seq))
    tabs_s = tuple(jnp.tile(t, (SAMPLE_SEQS, 1)) for t in _rope_tables(PAST_LEN + jnp.arange(dec_seq)))
    row = lambda a: a.reshape(1, -1)

    xp, xs = x_prompt, x_sample
    outs = [[] for _ in range(6)]
    for l in range(depth):
        win_b, wout_b = w_in[l].astype(BF16), w_out[l].astype(BF16)
        shared = (attn_sinks[l], conv_w[l], row(g_attn_out[l]), row(g_conv_out[l]), wout_b, row(ln1_g[l]),
                  row(ln1_b[l]))
        h_p, kp, vp, cp = _prompt_mixer(xp, win_b, tabs_p, *shared, alpha=alpha)
        h_s, kn, vn, cn = _sample_mixer(xs, cache_k[l].reshape(dec_b, win, kv_w),
                                        cache_v[l].reshape(dec_b, win, kv_w), state_conv[l], win_b, tabs_s,
                                        *shared, alpha=alpha)
        yp, ys = _moe(h_p, h_s, router_w[l], router_bias[l], w_gate[l], w_up[l], w_down[l], ws_gate[l], ws_up[l],
                      ws_down[l], row(ln2_g[l]), row(ln2_b[l]), alpha=alpha)
        xp, xs = yp.reshape(bsz, seq, d), ys.reshape(dec_b, dec_seq, d)
        heads = lambda a: a.reshape(a.shape[0], win, N_KV_HEADS, HEAD_DIM)
        for o, a in zip(outs, (heads(kp), heads(vp), cp, heads(kn), heads(vn), cn)):
            o.append(a)
    return (xp, xs) + tuple(jnp.stack(o, axis=0) for o in outs)
```

```python
import functools

import jax
import jax.numpy as jnp
from jax import lax
from jax.experimental import pallas as pl
from jax.experimental.pallas import tpu as pltpu

PAST_LEN = 16384
WINDOW = 128
HEAD_DIM = 64
N_KV_HEADS = 2
ROT_DIM = HEAD_DIM // 4
ROPE_THETA = 500000.0
CONV_K = 3
TOP_K = 6
N_EXPERT_GROUPS = 8
TOPK_GROUPS = 4
ROUTED_SCALE = 2.5
LN_EPS = 1e-5
RMS_EPS = 1e-6

LANES = 128
SUBLANES = 8
VMEM_LIMIT_BYTES = 56 * 1024 * 1024

PROMPT_TILE = 1024
SAMPLE_SEQS = 32
MOE_TILE = 256
SORT_CHUNK = 256
EXPERT_BLOCK = 2048
ROUTE_TILE = 1024
SHARED_TILES = 4
RUN_CHUNK = 8

F32 = jnp.float32
BF16 = jnp.bfloat16
NEG_INF = float("-inf")


def _rope_tables(positions):
    half = ROT_DIM // 2
    inv_freq = ROPE_THETA ** (-jnp.arange(0, ROT_DIM, 2, dtype=F32) / ROT_DIM)
    ang = positions.astype(F32)[:, None] * inv_freq[None, :]
    cos, sin = jnp.cos(ang), jnp.sin(ang)
    n = positions.shape[0]
    rest = HEAD_DIM - ROT_DIM
    c = jnp.concatenate([cos, cos, jnp.ones((n, rest), F32)], axis=-1)
    sa = jnp.concatenate([-sin, jnp.zeros((n, half + rest), F32)], axis=-1)
    sb = jnp.concatenate([jnp.zeros((n, half), F32), sin, jnp.zeros((n, rest), F32)], axis=-1)
    reps = LANES // HEAD_DIM
    return jnp.tile(c, (1, reps)), jnp.tile(sa, (1, reps)), jnp.tile(sb, (1, reps))


def _rope(x, c, sa, sb):
    half = ROT_DIM // 2
    return x * c + pltpu.roll(x, LANES - half, 1) * sa + pltpu.roll(x, half, 1) * sb


def _rms_norm(x, g):
    return x * lax.rsqrt(jnp.mean(jnp.square(x), axis=-1, keepdims=True) + RMS_EPS) * g


def _layer_norm(x, g, b):
    mu = jnp.mean(x, axis=-1, keepdims=True)
    var = jnp.mean(jnp.square(x - mu), axis=-1, keepdims=True)
    return (x - mu) * lax.rsqrt(var + LN_EPS) * g + b


def _short_conv(gated, prev2, prev1, row, conv_w, b_gate):
    g1 = pltpu.roll(gated, 1, 0)
    g2 = pltpu.roll(gated, 2, 0)
    g1 = jnp.where(row == 0, prev1, g1)
    g2 = jnp.where(row == 0, prev2, jnp.where(row == 1, prev1, g2))
    y = conv_w[0:1, :] * g2 + conv_w[1:2, :] * g1 + conv_w[2:3, :] * gated
    return b_gate * y


def _merge_norm(x, attn_o, conv_o, gattn, gconv, wout_ref, ln_g, ln_b, alpha):
    cat = jnp.concatenate([_rms_norm(attn_o, gattn), _rms_norm(conv_o, gconv)], axis=-1)
    mix = jnp.dot(cat.astype(BF16), wout_ref[...], preferred_element_type=F32)
    return _layer_norm(alpha * x + mix, ln_g, ln_b)


def _sink_softmax(parts, sink):
    m = sink
    for s in parts:
        m = jnp.maximum(m, jnp.max(s, axis=-1, keepdims=True))
    es = [jnp.exp(s - m) for s in parts]
    den = jnp.exp(sink - m)
    for e in es:
        den = den + jnp.sum(e, axis=-1, keepdims=True)
    return [(e / den).astype(BF16) for e in es]


def _prompt_mixer_kernel(x_ref, win_ref, c_ref, sa_ref, sb_ref, sinks_ref, convw_ref, gattn_ref, gconv_ref,
                         wout_ref, lng_ref, lnb_ref,
                         h_ref, ko_ref, vo_ref, co_ref,
                         q_s, kt_s, vm_s, o_s, gc_s, *, alpha, n_heads, attn_w, kv_w, conv_ch):
    s = pl.program_id(1)
    last = pl.num_programs(1) - 1
    tq = x_ref.shape[0]
    q_per_kv = n_heads // N_KV_HEADS
    heads_per_group = LANES // HEAD_DIM

    @pl.when(s == 0)
    def _():
        kt_s[:, :, 0:WINDOW] = jnp.zeros((kt_s.shape[0], LANES, WINDOW), BF16)
        vm_s[:, 0:WINDOW, :] = jnp.zeros((vm_s.shape[0], WINDOW, LANES), BF16)
        gc_s[...] = jnp.zeros(gc_s.shape, F32)

    @pl.when(s > 0)
    def _():
        kt_s[:, :, 0:WINDOW] = kt_s[:, :, tq:tq + WINDOW]
        vm_s[:, 0:WINDOW, :] = vm_s[:, tq:tq + WINDOW, :]

    x = x_ref[...]
    xb = x.astype(BF16)
    c, sa, sb = c_ref[...], sa_ref[...], sb_ref[...]

    def proj(lo, width):
        return jnp.dot(xb, win_ref[:, lo:lo + width], preferred_element_type=F32)

    scale = HEAD_DIM ** -0.5
    for j in range(attn_w // LANES):
        qj = _rope(proj(j * LANES, LANES), c, sa, sb)
        q_s[:, j * LANES:(j + 1) * LANES] = (qj * scale).astype(BF16)
    k = _rope(proj(attn_w, kv_w), c, sa, sb)
    v = proj(attn_w + kv_w, kv_w)

    @pl.when(s == last)
    def _():
        ko_ref[...] = k[tq - WINDOW:, :]
        vo_ref[...] = v[tq - WINDOW:, :]

    kt = k.T.astype(BF16)
    zeros_k = jnp.zeros((HEAD_DIM, tq), BF16)
    v_swapped = pltpu.roll(v, HEAD_DIM, 1)
    low_lanes = lax.broadcasted_iota(jnp.int32, (tq, LANES), 1) < HEAD_DIM
    for kvh in range(N_KV_HEADS):
        kt_h = kt[kvh * HEAD_DIM:(kvh + 1) * HEAD_DIM, :]
        v_lo = v if kvh == 0 else v_swapped
        v_hi = v_swapped if kvh == 0 else v
        kt_s[2 * kvh, :, WINDOW:] = jnp.concatenate([kt_h, zeros_k], axis=0)
        kt_s[2 * kvh + 1, :, WINDOW:] = jnp.concatenate([zeros_k, kt_h], axis=0)
        vm_s[2 * kvh, WINDOW:, :] = jnp.where(low_lanes, v_lo, 0.0).astype(BF16)
        vm_s[2 * kvh + 1, WINDOW:, :] = jnp.where(low_lanes, 0.0, v_hi).astype(BF16)

    qi = lax.broadcasted_iota(jnp.int32, (WINDOW, 2 * WINDOW), 0)
    ci = lax.broadcasted_iota(jnp.int32, (WINDOW, 2 * WINDOW), 1)
    band = (ci > qi) & (ci <= qi + WINDOW)
    for j in range(tq // WINDOW):
        r0 = j * WINDOW
        mask = band if j > 0 else band & ((ci >= WINDOW) | (s > 0))
        for grp in range(attn_w // LANES):
            q_grp = q_s[r0:r0 + WINDOW, grp * LANES:(grp + 1) * LANES]
            out = None
            for r in range(heads_per_group):
                hd = grp * heads_per_group + r
                src = 2 * (hd // q_per_kv) + r
                sink = sinks_ref[hd]
                sc = jnp.dot(q_grp, kt_s[src, :, r0:r0 + 2 * WINDOW], preferred_element_type=F32)
                sc = jnp.where(mask, sc, NEG_INF)
                m = jnp.maximum(jnp.max(sc, axis=-1, keepdims=True), sink)
                e = jnp.exp(sc - m)
                den = jnp.sum(e, axis=-1, keepdims=True) + jnp.exp(sink - m)
                o_h = jnp.dot(e.astype(BF16), vm_s[src, r0:r0 + 2 * WINDOW, :],
                              preferred_element_type=F32) * (1.0 / den)
                out = o_h if out is None else out + o_h
            o_s[r0:r0 + WINDOW, grp * LANES:(grp + 1) * LANES] = out

    o3 = attn_w + 2 * kv_w
    gated = proj(o3 + 2 * conv_ch, conv_ch) * proj(o3, conv_ch)
    row = lax.broadcasted_iota(jnp.int32, (tq, 1), 0)
    conv_o = _short_conv(gated, gc_s[0:1, :], gc_s[1:2, :], row, convw_ref[...], proj(o3 + conv_ch, conv_ch))
    gc_s[0:CONV_K - 1, :] = gated[tq - (CONV_K - 1):, :]

    @pl.when(s == last)
    def _():
        co_ref[...] = gated[tq - (CONV_K - 1):, :]

    h_ref[...] = _merge_norm(x, o_s[...], conv_o, gattn_ref[...], gconv_ref[...], wout_ref,
                             lng_ref[...], lnb_ref[...], alpha)


def _prompt_mixer(x, win_b, tabs, sinks, conv_w, g_attn, g_conv, wout_b, ln_g, ln_b, *, alpha):
    bsz, seq, d = x.shape
    tq = PROMPT_TILE
    ns = seq // tq
    attn_w = g_attn.shape[-1]
    conv_ch = g_conv.shape[-1]
    n_heads = attn_w // HEAD_DIM
    kv_w = N_KV_HEADS * HEAD_DIM
    assert kv_w == LANES and 2 * HEAD_DIM == LANES and (n_heads // N_KV_HEADS) % 2 == 0 and WINDOW == LANES
    in_cols = win_b.shape[-1]
    const2 = lambda b, s: (0, 0)
    kern = functools.partial(_prompt_mixer_kernel, alpha=alpha, n_heads=n_heads, attn_w=attn_w, kv_w=kv_w,
                             conv_ch=conv_ch)
    return pl.pallas_call(
        kern,
        grid=(bsz, ns),
        in_specs=[
            pl.BlockSpec((None, tq, d), lambda b, s: (b, s, 0)),
            pl.BlockSpec((d, in_cols), const2),
            pl.BlockSpec((tq, LANES), lambda b, s: (s, 0)),
            pl.BlockSpec((tq, LANES), lambda b, s: (s, 0)),
            pl.BlockSpec((tq, LANES), lambda b, s: (s, 0)),
            pl.BlockSpec(memory_space=pltpu.SMEM),
            pl.BlockSpec((CONV_K, conv_ch), const2),
            pl.BlockSpec((1, attn_w), const2),
            pl.BlockSpec((1, conv_ch), const2),
            pl.BlockSpec((attn_w + conv_ch, d), const2),
            pl.BlockSpec((1, d), const2),
            pl.BlockSpec((1, d), const2),
        ],
        out_specs=[
            pl.BlockSpec((tq, d), lambda b, s: (b * ns + s, 0)),
            pl.BlockSpec((None, WINDOW, kv_w), lambda b, s: (b, 0, 0)),
            pl.BlockSpec((None, WINDOW, kv_w), lambda b, s: (b, 0, 0)),
            pl.BlockSpec((None, CONV_K - 1, conv_ch), lambda b, s: (b, 0, 0)),
        ],
        out_shape=[
            jax.ShapeDtypeStruct((bsz * seq, d), F32),
            jax.ShapeDtypeStruct((bsz, WINDOW, kv_w), F32),
            jax.ShapeDtypeStruct((bsz, WINDOW, kv_w), F32),
            jax.ShapeDtypeStruct((bsz, CONV_K - 1, conv_ch), F32),
        ],
        scratch_shapes=[
            pltpu.VMEM((tq, attn_w), BF16),
            pltpu.VMEM((2 * N_KV_HEADS, LANES, WINDOW + tq), BF16),
            pltpu.VMEM((2 * N_KV_HEADS, WINDOW + tq, LANES), BF16),
            pltpu.VMEM((tq, attn_w), F32),
            pltpu.VMEM((SUBLANES, conv_ch), F32),
        ],
        compiler_params=pltpu.CompilerParams(dimension_semantics=("arbitrary", "arbitrary"),
                                             vmem_limit_bytes=VMEM_LIMIT_BYTES),
        name="prompt_mixer",
    )(x, win_b, *tabs, sinks, conv_w, g_attn, g_conv, wout_b, ln_g, ln_b)


def _sample_mixer_kernel(x_ref, ck_ref, cv_ref, st_ref, win_ref, c_ref, sa_ref, sb_ref, sinks_ref,
                         convw_ref, gattn_ref, gconv_ref, wout_ref, lng_ref, lnb_ref,
                         h_ref, ko_ref, vo_ref, co_ref, *, alpha, n_heads, attn_w, kv_w, conv_ch, dec_seq):
    nb, win = ck_ref.shape[0], ck_ref.shape[1]
    rows = nb * dec_seq
    q_per_kv = n_heads // N_KV_HEADS
    x = x_ref[...]
    xb = x.astype(BF16)
    c, sa, sb = c_ref[...], sa_ref[...], sb_ref[...]

    def proj(lo, width):
        return jnp.dot(xb, win_ref[:, lo:lo + width], preferred_element_type=F32)

    scale = HEAD_DIM ** -0.5
    k = _rope(proj(attn_w, kv_w), c, sa, sb)
    v = proj(attn_w + kv_w, kv_w)
    k3 = k.reshape(nb, dec_seq, kv_w)
    v3 = v.reshape(nb, dec_seq, kv_w)
    ck = ck_ref[...]
    cv = cv_ref[...]
    ko_ref[:, 0:win - dec_seq, :] = ck[:, dec_seq:, :]
    ko_ref[:, win - dec_seq:, :] = k3
    vo_ref[:, 0:win - dec_seq, :] = cv[:, dec_seq:, :]
    vo_ref[:, win - dec_seq:, :] = v3
    ckb, cvb, k3b, v3b = ck.astype(BF16), cv.astype(BF16), k3.astype(BF16), v3.astype(BF16)

    qrows = q_per_kv * dec_seq
    qi = lax.broadcasted_iota(jnp.int32, (nb, qrows, win), 1) % dec_seq
    mask_c = lax.broadcasted_iota(jnp.int32, (nb, qrows, win), 2) > qi + (win - WINDOW)
    qn = lax.broadcasted_iota(jnp.int32, (nb, qrows, dec_seq), 1) % dec_seq
    mask_n = lax.broadcasted_iota(jnp.int32, (nb, qrows, dec_seq), 2) <= qn
    sink_row = lax.broadcasted_iota(jnp.int32, (nb, qrows, 1), 1) // dec_seq

    q_chunks = [_rope(proj(j * LANES, LANES), c, sa, sb) * scale for j in range(attn_w // LANES)]
    heads_out = []
    for kvh in range(N_KV_HEADS):
        qs = []
        for g in range(q_per_kv):
            lo = (kvh * q_per_kv + g) * HEAD_DIM
            qh = q_chunks[lo // LANES][:, lo % LANES:lo % LANES + HEAD_DIM]
            qs.append(qh.reshape(nb, dec_seq, HEAD_DIM))
        qg = jnp.concatenate(qs, axis=1).astype(BF16)
        sl = slice(kvh * HEAD_DIM, (kvh + 1) * HEAD_DIM)
        sc_c = jnp.einsum("bqd,bkd->bqk", qg, ckb[:, :, sl], preferred_element_type=F32)
        sc_n = jnp.einsum("bqd,bkd->bqk", qg, k3b[:, :, sl], preferred_element_type=F32)
        sc_c = jnp.where(mask_c, sc_c, NEG_INF)
        sc_n = jnp.where(mask_n, sc_n, NEG_INF)
        sink = jnp.zeros((nb, qrows, 1), F32)
        for g in range(q_per_kv):
            sink = jnp.where(sink_row == g, sinks_ref[kvh * q_per_kv + g], sink)
        p_c, p_n = _sink_softmax([sc_c, sc_n], sink)
        og = (jnp.einsum("bqk,bkd->bqd", p_c, cvb[:, :, sl], preferred_element_type=F32)
              + jnp.einsum("bqk,bkd->bqd", p_n, v3b[:, :, sl], preferred_element_type=F32))
        for g in range(q_per_kv):
            heads_out.append(og[:, g * dec_seq:(g + 1) * dec_seq, :].reshape(rows, HEAD_DIM))
    attn_o = jnp.concatenate(heads_out, axis=-1)

    o3 = attn_w + 2 * kv_w
    gated = proj(o3 + 2 * conv_ch, conv_ch) * proj(o3, conv_ch)
    st = st_ref[...]
    prev2 = jnp.broadcast_to(st[:, 0:1, :], (nb, dec_seq, conv_ch)).reshape(rows, conv_ch)
    prev1 = jnp.broadcast_to(st[:, 1:2, :], (nb, dec_seq, conv_ch)).reshape(rows, conv_ch)
    row = lax.broadcasted_iota(jnp.int32, (rows, 1), 0) % dec_seq
    conv_o = _short_conv(gated, prev2, prev1, row, convw_ref[...], proj(o3 + conv_ch, conv_ch))
    co_ref[...] = gated.reshape(nb, dec_seq, conv_ch)[:, dec_seq - (CONV_K - 1):, :]

    h_ref[...] = _merge_norm(x, attn_o, conv_o, gattn_ref[...], gconv_ref[...], wout_ref,
                             lng_ref[...], lnb_ref[...], alpha)


def _sample_mixer(x, ck, cv, st, win_b, tabs, sinks, conv_w, g_attn, g_conv, wout_b, ln_g, ln_b, *, alpha):
    dec_b, dec_seq, d = x.shape
    assert dec_seq >= CONV_K - 1 and dec_seq % SUBLANES == 0
    nb = SAMPLE_SEQS
    rows = nb * dec_seq
    win = ck.shape[1]
    attn_w = g_attn.shape[-1]
    conv_ch = g_conv.shape[-1]
    n_heads = attn_w // HEAD_DIM
    kv_w = N_KV_HEADS * HEAD_DIM
    in_cols = win_b.shape[-1]
    const2 = lambda i: (0, 0)
    kern = functools.partial(_sample_mixer_kernel, alpha=alpha, n_heads=n_heads, attn_w=attn_w, kv_w=kv_w,
                             conv_ch=conv_ch, dec_seq=dec_seq)
    return pl.pallas_call(
        kern,
        grid=(dec_b // nb,),
        in_specs=[
            pl.BlockSpec((rows, d), lambda i: (i, 0)),
            pl.BlockSpec((nb, win, kv_w), lambda i: (i, 0, 0)),
            pl.BlockSpec((nb, win, kv_w), lambda i: (i, 0, 0)),
            pl.BlockSpec((nb, CONV_K - 1, conv_ch), lambda i: (i, 0, 0)),
            pl.BlockSpec((d, in_cols), const2),
            pl.BlockSpec((rows, LANES), const2),
            pl.BlockSpec((rows, LANES), const2),
            pl.BlockSpec((rows, LANES), const2),
            pl.BlockSpec(memory_space=pltpu.SMEM),
            pl.BlockSpec((CONV_K, conv_ch), const2),
            pl.BlockSpec((1, attn_w), const2),
            pl.BlockSpec((1, conv_ch), const2),
            pl.BlockSpec((attn_w + conv_ch, d), const2),
            pl.BlockSpec((1, d), const2),
            pl.BlockSpec((1, d), const2),
        ],
        out_specs=[
            pl.BlockSpec((rows, d), lambda i: (i, 0)),
            pl.BlockSpec((nb, win, kv_w), lambda i: (i, 0, 0)),
            pl.BlockSpec((nb, win, kv_w), lambda i: (i, 0, 0)),
            pl.BlockSpec((nb, CONV_K - 1, conv_ch), lambda i: (i, 0, 0)),
        ],
        out_shape=[
            jax.ShapeDtypeStruct((dec_b * dec_seq, d), F32),
            jax.ShapeDtypeStruct((dec_b, win, kv_w), F32),
            jax.ShapeDtypeStruct((dec_b, win, kv_w), F32),
            jax.ShapeDtypeStruct((dec_b, CONV_K - 1, conv_ch), F32),
        ],
        compiler_params=pltpu.CompilerParams(dimension_semantics=("arbitrary",),
                                             vmem_limit_bytes=VMEM_LIMIT_BYTES),
        name="sample_mixer",
    )(x.reshape(dec_b * dec_seq, d), ck, cv, st, win_b, *tabs, sinks, conv_w, g_attn, g_conv, wout_b,
      ln_g, ln_b)


def _over_experts(fn, x):
    return fn(fn(x, axis=0, keepdims=True), axis=1, keepdims=True)


def _two_group_specs(tm, d, n_prompt_tiles):
    return [pl.BlockSpec((tm, d), lambda i, *_: (jnp.minimum(i, n_prompt_tiles - 1), 0)),
            pl.BlockSpec((tm, d), lambda i, *_: (jnp.maximum(i - n_prompt_tiles, 0), 0))]


def _route_kernel(hp_ref, hs_ref, rwt_ref, bias_ref, eidx_ref, rank_ref, gate_ref, cnt_ref, earlier_s, *,
                  n_experts, n_prompt_tiles):
    i = pl.program_id(0)
    tm = hp_ref.shape[0]
    per_group = n_experts // N_EXPERT_GROUPS
    shape3 = (N_EXPERT_GROUPS, per_group, tm)

    @pl.when(i == 0)
    def _():
        t_from = lax.broadcasted_iota(jnp.int32, (tm, tm), 0)
        t_to = lax.broadcasted_iota(jnp.int32, (tm, tm), 1)
        earlier_s[...] = ((t_from < t_to) & (t_from // MOE_TILE == t_to // MOE_TILE)).astype(BF16)

    h = jnp.where(i < n_prompt_tiles, hp_ref[...], hs_ref[...])
    logits = lax.dot_general(rwt_ref[...], h.astype(BF16), (((1,), (1,)), ((), ())),
                             preferred_element_type=F32)
    scores = jax.nn.sigmoid(logits)
    sel = scores + bias_ref[...]
    scores3 = scores.reshape(shape3)
    grp = sel.reshape(shape3)
    member = lax.broadcasted_iota(jnp.int32, shape3, 1).astype(F32)
    group = lax.broadcasted_iota(jnp.int32, shape3, 0).astype(F32)
    expert = group * per_group + member

    m1 = jnp.max(grp, axis=1, keepdims=True)
    f1 = jnp.min(jnp.where(grp == m1, member, float(per_group)), axis=1, keepdims=True)
    m2 = jnp.max(jnp.where(member == f1, NEG_INF, grp), axis=1, keepdims=True)
    gscore = m1 + m2

    gid = lax.broadcasted_iota(jnp.int32, gscore.shape, 0).astype(F32)
    gmask = jnp.zeros(gscore.shape, F32)
    cur = gscore
    for _ in range(TOPK_GROUPS):
        mx = jnp.max(cur, axis=0, keepdims=True)
        pick = gid == jnp.min(jnp.where(cur == mx, gid, float(N_EXPERT_GROUPS)), axis=0, keepdims=True)
        gmask = jnp.where(pick, 1.0, gmask)
        cur = jnp.where(pick, NEG_INF, cur)

    cand = jnp.where(gmask > 0.0, grp, NEG_INF)
    chosen = jnp.zeros(shape3, F32)
    picks, firsts, weights = [], [], []
    for _ in range(TOP_K):
        mx = _over_experts(jnp.max, cand)
        first = _over_experts(jnp.min, jnp.where(cand == mx, expert, float(n_experts)))
        pick = expert == first
        picks.append(pick)
        firsts.append(first)
        weights.append(_over_experts(jnp.sum, jnp.where(pick, scores3, 0.0)))
        chosen = jnp.where(pick, 1.0, chosen)
        cand = jnp.where(pick, NEG_INF, cand)
    wsum = weights[0]
    for w in weights[1:]:
        wsum = wsum + w

    chosen_b = chosen.reshape(n_experts, tm).astype(BF16)
    before3 = jnp.dot(chosen_b, earlier_s[...], preferred_element_type=F32).reshape(shape3)

    pad = SUBLANES - TOP_K
    eidx = [f.reshape(1, tm).astype(jnp.int32) for f in firsts]
    rank = [_over_experts(jnp.sum, jnp.where(p, before3, 0.0)).reshape(1, tm).astype(jnp.int32) for p in picks]
    gate = [(w / wsum * ROUTED_SCALE).reshape(1, tm) for w in weights]
    eidx_ref[...] = jnp.concatenate(eidx + [jnp.zeros((pad, tm), jnp.int32)], axis=0)
    rank_ref[...] = jnp.concatenate(rank + [jnp.zeros((pad, tm), jnp.int32)], axis=0)
    gate_ref[...] = jnp.concatenate(gate + [jnp.zeros((pad, tm), F32)], axis=0)

    for sub in range(tm // MOE_TILE):
        cnt_ref[sub * SUBLANES:(sub + 1) * SUBLANES, :] = lax.dot_general(
            jnp.ones((SUBLANES, MOE_TILE), BF16), chosen_b[:, sub * MOE_TILE:(sub + 1) * MOE_TILE],
            (((1,), (1,)), ((), ())), preferred_element_type=F32)


def _route(h_p, h_s, rwt_b, bias_col):
    d = h_p.shape[1]
    t_all = h_p.shape[0] + h_s.shape[0]
    n_experts = rwt_b.shape[0]
    tm = ROUTE_TILE
    npt = h_p.shape[0] // tm
    moe_tiles = tm // MOE_TILE
    row_spec = pl.BlockSpec((SUBLANES, tm), lambda i: (0, i))
    return pl.pallas_call(
        functools.partial(_route_kernel, n_experts=n_experts, n_prompt_tiles=npt),
        grid=(t_all // tm,),
        in_specs=_two_group_specs(tm, d, npt) + [
            pl.BlockSpec((n_experts, d), lambda i: (0, 0)),
            pl.BlockSpec((n_experts, 1), lambda i: (0, 0)),
        ],
        out_specs=[row_spec, row_spec, row_spec,
                   pl.BlockSpec((moe_tiles * SUBLANES, n_experts), lambda i: (i, 0))],
        out_shape=[
            jax.ShapeDtypeStruct((SUBLANES, t_all), jnp.int32),
            jax.ShapeDtypeStruct((SUBLANES, t_all), jnp.int32),
            jax.ShapeDtypeStruct((SUBLANES, t_all), F32),
            jax.ShapeDtypeStruct((t_all // MOE_TILE * SUBLANES, n_experts), F32),
        ],
        scratch_shapes=[pltpu.VMEM((tm, tm), BF16)],
        compiler_params=pltpu.CompilerParams(dimension_semantics=("arbitrary",),
                                             vmem_limit_bytes=VMEM_LIMIT_BYTES),
        name="route",
    )(h_p, h_s, rwt_b, bias_col)


def _for_each_part(groups, max_part, fn):
    for size in [1 << b for b in range(max_part.bit_length())]:
        @pl.when((groups & size) != 0)
        def _(size=size):
            fn(groups & (size - 1), size)


def _start_run_copies(tile, cnt_ref, ls_ref, gs_ref, big_ref, n_experts, start_copy):
    def run(e):
        idx = tile * n_experts + e
        return cnt_ref[idx], ls_ref[idx], gs_ref[idx]

    for e in range(n_experts):
        groups, l0, g0 = run(e)
        _for_each_part(groups, RUN_CHUNK // 2, lambda off, size: start_copy(l0 + off, g0 + off, size))

    @pl.when(big_ref[tile] != 0)
    def _():
        def per_expert(e, c):
            groups, l0, g0 = run(e)
            base = groups & (RUN_CHUNK - 1)

            def chunk(j, c2):
                off = base + j * RUN_CHUNK
                start_copy(l0 + off, g0 + off, RUN_CHUNK)
                return c2

            lax.fori_loop(0, lax.shift_right_logical(groups, RUN_CHUNK.bit_length() - 1), chunk, 0)
            return c

        lax.fori_loop(0, n_experts, per_expert, 0)


def _top_bit(n):
    return 1 << (n.bit_length() - 1)


def _pack_rows(x):
    rows, two_w = x.shape
    w = two_w // 2
    x3 = x.reshape(rows // SUBLANES, SUBLANES, two_w)
    halves = jnp.concatenate([x3[:, :, :w], x3[:, :, w:]], axis=1).astype(BF16)
    return pltpu.bitcast(halves, jnp.uint32)


def _unpack_rows(p):
    groups, _, w = p.shape
    halves = pltpu.bitcast(p, BF16).astype(F32)
    return jnp.concatenate([halves[:, :SUBLANES, :], halves[:, SUBLANES:, :]], axis=-1).reshape(
        groups * SUBLANES, 2 * w)


def _local_groups(tm, n_experts):
    n = TOP_K * tm + n_experts * (SUBLANES - 1)
    return -(-n // SORT_CHUNK) * SORT_CHUNK // SUBLANES


def _dispatch_kernel(cnt_ref, ls_ref, gs_ref, tot_ref, big_ref, zstart_ref, zlen_ref, lp_ref, hp_ref, hs_ref,
                     xs_hbm, lbuf, zero_s, sem, zero_sem, *, n_experts, n_prompt_tiles):
    i = pl.program_id(0)
    tm, d = hp_ref.shape
    n = TOP_K * tm // SUBLANES
    chunk = SORT_CHUNK // SUBLANES
    slot = i % 2

    def tile_wait(tile):
        pltpu.make_async_copy(lbuf.at[0, pl.ds(0, n)], xs_hbm.at[pl.ds(0, n)], sem).wait()
        _for_each_part(tot_ref[tile] - n, _top_bit(n_experts * (SUBLANES - 1) // SUBLANES), lambda off, size:
                       pltpu.make_async_copy(lbuf.at[0, pl.ds(0, size)], xs_hbm.at[pl.ds(0, size)], sem).wait())

    def clear_padding(act):
        def per_expert(e, c):
            _for_each_part(zlen_ref[e], zero_s.shape[0], lambda off, size: act(pltpu.make_async_copy(
                zero_s.at[pl.ds(0, size)], xs_hbm.at[pl.ds(zstart_ref[e] + off, size)], zero_sem)))
            return c

        lax.fori_loop(0, n_experts, per_expert, 0)

    @pl.when(i == 0)
    def _():
        zero_s[...] = jnp.zeros(zero_s.shape, zero_s.dtype)
        clear_padding(lambda copy: copy.start())

    hb = jnp.where(i < n_prompt_tiles, hp_ref[...], hs_ref[...]).astype(BF16)
    lp = lp_ref[...]
    def sort_chunk(r):
        rows = r * SORT_CHUNK + lax.broadcasted_iota(jnp.int32, (SORT_CHUNK, tm), 0)
        hit = jnp.zeros((SORT_CHUNK, tm), F32)
        for k in range(TOP_K):
            hit = jnp.where(rows == lp[k:k + 1, :], 1.0, hit)
        lbuf[slot, pl.ds(r * chunk, chunk)] = _pack_rows(jnp.dot(hit.astype(BF16), hb, preferred_element_type=F32))

    last_chunk = lbuf.shape[1] // chunk - 1
    for r in range(last_chunk):
        sort_chunk(r)
    pl.when(tot_ref[i] > last_chunk * chunk)(functools.partial(sort_chunk, last_chunk))

    @pl.when(i > 0)
    def _():
        tile_wait(i - 1)

    _start_run_copies(i, cnt_ref, ls_ref, gs_ref, big_ref, n_experts, lambda lgrp, ggrp, size: pltpu.make_async_copy(
        lbuf.at[slot, pl.ds(lgrp, size)], xs_hbm.at[pl.ds(ggrp, size)], sem).start())

    @pl.when(i == pl.num_programs(0) - 1)
    def _():
        tile_wait(i)
        clear_padding(lambda copy: copy.wait())


def _dispatch(tables, zstart, zlen, lp, h_p, h_s, *, n_rows, blk):
    d = h_p.shape[1]
    t_all = h_p.shape[0] + h_s.shape[0]
    tm = MOE_TILE
    npt = h_p.shape[0] // tm
    n_experts = zstart.shape[0]
    return pl.pallas_call(
        functools.partial(_dispatch_kernel, n_experts=n_experts, n_prompt_tiles=npt),
        grid_spec=pltpu.PrefetchScalarGridSpec(
            num_scalar_prefetch=7,
            grid=(t_all // tm,),
            in_specs=[pl.BlockSpec((SUBLANES, tm), lambda i, *_: (0, i))] + _two_group_specs(tm, d, npt),
            out_specs=pl.BlockSpec(memory_space=pl.ANY),
            scratch_shapes=[
                pltpu.VMEM((2, _local_groups(tm, n_experts), SUBLANES, d // 2), jnp.uint32),
                pltpu.VMEM((_top_bit(blk // SUBLANES - 1), SUBLANES, d // 2), jnp.uint32),
                pltpu.SemaphoreType.DMA,
                pltpu.SemaphoreType.DMA,
            ],
        ),
        out_shape=jax.ShapeDtypeStruct((n_rows // SUBLANES, SUBLANES, d // 2), jnp.uint32),
        compiler_params=pltpu.CompilerParams(dimension_semantics=("arbitrary",),
                                             vmem_limit_bytes=VMEM_LIMIT_BYTES),
        name="dispatch",
    )(*tables, zstart, zlen, lp, h_p, h_s)


def _silu(x):
    return x * jax.nn.sigmoid(x)


def _expert_kernel(be_ref, nact_ref, xs_ref, wg_ref, wu_ref, wd_ref, ys_ref, wg_s, wu_s, wd_s):
    b = pl.program_id(0)

    @pl.when(b < nact_ref[0])
    def _():
        @pl.when((b == 0) | (be_ref[b] != be_ref[jnp.maximum(b - 1, 0)]))
        def _():
            wg_s[...] = wg_ref[...].astype(BF16)
            wu_s[...] = wu_ref[...].astype(BF16)
            wd_s[...] = wd_ref[...].astype(BF16)

        xb = _unpack_rows(xs_ref[...]).astype(BF16)
        hid = _silu(jnp.dot(xb, wg_s[...], preferred_element_type=F32)) * jnp.dot(
            xb, wu_s[...], preferred_element_type=F32)
        ys_ref[...] = _pack_rows(jnp.dot(hid.astype(BF16), wd_s[...], preferred_element_type=F32))


def _experts(block_expert, nact, xs, w_gate, w_up, w_down, *, blk):
    d = w_gate.shape[-2]
    ff = w_gate.shape[-1]
    block = (blk // SUBLANES,) + xs.shape[1:]
    n_blocks = xs.shape[0] // block[0]

    def active(b, be, na):
        return jnp.minimum(b, na[0] - 1)

    return pl.pallas_call(
        _expert_kernel,
        grid_spec=pltpu.PrefetchScalarGridSpec(
            num_scalar_prefetch=2,
            grid=(n_blocks,),
            in_specs=[
                pl.BlockSpec(block, lambda b, be, na: (active(b, be, na), 0, 0)),
                pl.BlockSpec((None, d, ff), lambda b, be, na: (be[active(b, be, na)], 0, 0)),
                pl.BlockSpec((None, d, ff), lambda b, be, na: (be[active(b, be, na)], 0, 0)),
                pl.BlockSpec((None, ff, d), lambda b, be, na: (be[active(b, be, na)], 0, 0)),
            ],
            out_specs=pl.BlockSpec(block, lambda b, be, na: (active(b, be, na), 0, 0)),
            scratch_shapes=[
                pltpu.VMEM((d, ff), BF16),
                pltpu.VMEM((d, ff), BF16),
                pltpu.VMEM((ff, d), BF16),
            ],
        ),
        out_shape=jax.ShapeDtypeStruct(xs.shape, jnp.uint32),
        compiler_params=pltpu.CompilerParams(dimension_semantics=("arbitrary",),
                                             vmem_limit_bytes=VMEM_LIMIT_BYTES),
        name="experts",
    )(block_expert, nact, xs, w_gate, w_up, w_down)


def _combine_kernel(cnt_ref, ls_ref, gs_ref, tot_ref, big_ref, ys_hbm, lpt_ref, gate_ref, hp_ref, hs_ref, wsg_ref, wsu_ref,
                    wsd_ref, lng_ref, lnb_ref, yp_ref, ysm_ref, ybuf, moe_s, shared_s, sems, *, alpha, n_experts,
                    n_prompt_tiles):
    i = pl.program_id(0)
    tm, d = yp_ref.shape
    n = TOP_K * tm // SUBLANES
    chunk = SORT_CHUNK // SUBLANES
    slot = i % 2

    def fetch(tile, to):
        _start_run_copies(tile, cnt_ref, ls_ref, gs_ref, big_ref, n_experts, lambda lgrp, ggrp, size:
                          pltpu.make_async_copy(ys_hbm.at[pl.ds(ggrp, size)], ybuf.at[to, pl.ds(lgrp, size)],
                                                sems.at[to]).start())

    @pl.when(i == 0)
    def _():
        ybuf[...] = jnp.zeros(ybuf.shape, ybuf.dtype)
        fetch(0, 0)

    @pl.when(i + 1 < pl.num_programs(0))
    def _():
        fetch(i + 1, 1 - slot)

    sub = i % SHARED_TILES

    @pl.when(sub == 0)
    def _():
        hb = jnp.where(i < n_prompt_tiles, hp_ref[...], hs_ref[...]).astype(BF16)
        hid = _silu(jnp.dot(hb, wsg_ref[...], preferred_element_type=F32)) * jnp.dot(
            hb, wsu_ref[...], preferred_element_type=F32)
        shared_s[...] = jnp.dot(hid.astype(BF16), wsd_ref[...], preferred_element_type=F32)

    rows = pl.ds(pl.multiple_of(sub * tm, tm), tm)
    h = jnp.where(i < n_prompt_tiles, hp_ref[rows, :], hs_ref[rows, :])
    shared = shared_s[rows, :]

    pltpu.make_async_copy(ys_hbm.at[pl.ds(0, n)], ybuf.at[slot, pl.ds(0, n)], sems.at[slot]).wait()
    _for_each_part(tot_ref[i] - n, _top_bit(n_experts * (SUBLANES - 1) // SUBLANES), lambda off, size:
                   pltpu.make_async_copy(ys_hbm.at[pl.ds(0, size)], ybuf.at[slot, pl.ds(0, size)],
                                         sems.at[slot]).wait())

    lpt = lpt_ref[...]
    gate = gate_ref[...]

    def chunk_sum(r):
        cols = r * SORT_CHUNK + lax.broadcasted_iota(jnp.int32, (tm, SORT_CHUNK), 1)
        g = jnp.zeros((tm, SORT_CHUNK), F32)
        for k in range(TOP_K):
            g = jnp.where(cols == lpt[:, k:k + 1], gate[:, k:k + 1], g)
        yb = _unpack_rows(ybuf[slot, pl.ds(r * chunk, chunk)]).astype(BF16)
        return jnp.dot(g.astype(BF16), yb, preferred_element_type=F32)

    last_chunk = ybuf.shape[1] // chunk - 1
    moe = shared
    for r in range(last_chunk):
        moe = moe + chunk_sum(r)
    moe_s[...] = moe

    @pl.when(tot_ref[i] > last_chunk * chunk)
    def _():
        moe_s[...] += chunk_sum(last_chunk)

    y = _layer_norm(alpha * h + moe_s[...], lng_ref[...], lnb_ref[...])

    @pl.when(i < n_prompt_tiles)
    def _():
        yp_ref[...] = y

    @pl.when(i >= n_prompt_tiles)
    def _():
        ysm_ref[...] = y


def _combine(tables, ys, lp_t, gates_t, h_p, h_s, wsg_b, wsu_b, wsd_b, ln_g, ln_b, *, alpha):
    t_prompt, d = h_p.shape
    t_all = t_prompt + h_s.shape[0]
    tm = MOE_TILE
    ff = wsg_b.shape[-1]
    npt = t_prompt // tm
    n_experts = tables[0].shape[0] // (t_all // tm)
    const2 = lambda i, *_: (0, 0)
    return pl.pallas_call(
        functools.partial(_combine_kernel, alpha=alpha, n_experts=n_experts, n_prompt_tiles=npt),
        grid_spec=pltpu.PrefetchScalarGridSpec(
            num_scalar_prefetch=5,
            grid=(t_all // tm,),
            in_specs=[
                pl.BlockSpec(memory_space=pl.ANY),
                pl.BlockSpec((tm, SUBLANES), lambda i, *_: (i, 0)),
                pl.BlockSpec((tm, SUBLANES), lambda i, *_: (i, 0)),
                pl.BlockSpec((SHARED_TILES * tm, d),
                             lambda i, *_: (jnp.minimum(i, npt - 1) // SHARED_TILES, 0)),
                pl.BlockSpec((SHARED_TILES * tm, d),
                             lambda i, *_: (jnp.maximum(i - npt, 0) // SHARED_TILES, 0)),
                pl.BlockSpec((d, ff), const2),
                pl.BlockSpec((d, ff), const2),
                pl.BlockSpec((ff, d), const2),
                pl.BlockSpec((1, d), const2),
                pl.BlockSpec((1, d), const2),
            ],
            out_specs=[
                pl.BlockSpec((tm, d), lambda i, *_: (jnp.minimum(i, npt - 1), 0)),
                pl.BlockSpec((tm, d), lambda i, *_: (jnp.maximum(i - npt, 0), 0)),
            ],
            scratch_shapes=[
                pltpu.VMEM((2, _local_groups(tm, n_experts), SUBLANES, d // 2), jnp.uint32),
                pltpu.VMEM((tm, d), F32),
                pltpu.VMEM((SHARED_TILES * tm, d), F32),
                pltpu.SemaphoreType.DMA((2,)),
            ],
        ),
        out_shape=[
            jax.ShapeDtypeStruct((t_prompt, d), F32),
            jax.ShapeDtypeStruct((t_all - t_prompt, d), F32),
        ],
        compiler_params=pltpu.CompilerParams(dimension_semantics=("arbitrary",),
                                             vmem_limit_bytes=VMEM_LIMIT_BYTES),
        name="combine",
    )(*tables, ys, lp_t, gates_t, h_p, h_s, wsg_b, wsu_b, wsd_b, ln_g, ln_b)


def _moe(h_p, h_s, router_w, router_bias, w_gate, w_up, w_down, ws_gate, ws_up, ws_down, ln_g, ln_b, *, alpha):
    t_all = h_p.shape[0] + h_s.shape[0]
    n_experts = router_w.shape[-1]
    blk = EXPERT_BLOCK
    tm = MOE_TILE
    nt = t_all // tm
    eidx8, lrank8, gate8, cnt8 = _route(h_p, h_s, router_w.T.astype(BF16), router_bias.reshape(n_experts, 1))

    bgrp = blk // SUBLANES
    cnt = cnt8.reshape(nt, SUBLANES, n_experts)[:, 0, :].astype(jnp.int32)
    cnt = (cnt + SUBLANES - 1) // SUBLANES
    counts = jnp.sum(cnt, axis=0)
    blocks_per_e = (counts + bgrp - 1) // bgrp
    block_end = jnp.cumsum(blocks_per_e)
    pad_start = (block_end - blocks_per_e) * bgrp
    n_blocks = -(-(t_all * TOP_K + nt * n_experts * (SUBLANES - 1)) // blk) + n_experts
    block_expert = jnp.minimum(jnp.sum(block_end[None, :] <= jnp.arange(n_blocks)[:, None], axis=1),
                               n_experts - 1).astype(jnp.int32)
    nact = block_end[-1:].astype(jnp.int32)
    zstart = (pad_start + counts).astype(jnp.int32)
    zlen = (blocks_per_e * bgrp - counts).astype(jnp.int32)
    gstart = pad_start[None, :] + jnp.cumsum(cnt, axis=0) - cnt
    lstart = jnp.cumsum(cnt, axis=1) - cnt
    tables = tuple(a.reshape(-1).astype(jnp.int32) for a in (
        cnt, lstart, gstart, jnp.sum(cnt, axis=1), jnp.max(cnt, axis=1) >= RUN_CHUNK))
    lstart_tok = jnp.repeat(lstart * SUBLANES, tm, axis=0)
    lp8 = jnp.sum(jnp.where(eidx8[..., None] == jnp.arange(n_experts), lstart_tok[None], 0), axis=-1) + lrank8
    lp8 = lp8.astype(jnp.int32)

    xs = _dispatch(tables, zstart, zlen, lp8, h_p, h_s, n_rows=n_blocks * blk, blk=blk)
    ys = _experts(block_expert, nact, xs, w_gate, w_up, w_down, blk=blk)
    return _combine(tables, ys, lp8.T, gate8.T, h_p, h_s, ws_gate.astype(BF16), ws_up.astype(BF16), ws_down.astype(BF16),
                    ln_g, ln_b, alpha=alpha)


def kernel(x_prompt, x_sample, cache_k, cache_v, state_conv, w_in, attn_sinks, conv_w, g_attn_out, g_conv_out, w_out, ln1_g, ln1_b, router_w, router_bias, w_gate, w_up, w_down, ws_gate, ws_up, ws_down, ln2_g, ln2_b):
    depth = w_in.shape[0]
    bsz, seq, d = x_prompt.shape
    dec_b, dec_seq, _ = x_sample.shape
    win = cache_k.shape[2]
    kv_w = N_KV_HEADS * HEAD_DIM
    t_prompt = bsz * seq
    t_all = t_prompt + dec_b * dec_seq
    alpha = (2.0 * depth) ** 0.25
    assert win == WINDOW and seq % PROMPT_TILE == 0 and dec_b % SAMPLE_SEQS == 0
    assert t_prompt % ROUTE_TILE == 0 and (t_all - t_prompt) % ROUTE_TILE == 0 and ROUTE_TILE % MOE_TILE == 0
    assert MOE_TILE & (MOE_TILE - 1) == 0 and (TOP_K * MOE_TILE) % SORT_CHUNK == 0
    assert t_prompt % (SHARED_TILES * MOE_TILE) == 0 and (t_all - t_prompt) % (SHARED_TILES * MOE_TILE) == 0

    tabs_p = _rope_tables(jnp.arange(seq))
    tabs_s = tuple(jnp.tile(t, (SAMPLE_SEQS, 1)) for t in _rope_tables(PAST_LEN + jnp.arange(dec_seq)))
    row = lambda a: a.reshape(1, -1)

    xp, xs = x_prompt, x_sample
    outs = [[] for _ in range(6)]
    for l in range(depth):
        win_b, wout_b = w_in[l].astype(BF16), w_out[l].astype(BF16)
        shared = (attn_sinks[l], conv_w[l], row(g_attn_out[l]), row(g_conv_out[l]), wout_b, row(ln1_g[l]),
                  row(ln1_b[l]))
        h_p, kp, vp, cp = _prompt_mixer(xp, win_b, tabs_p, *shared, alpha=alpha)
        h_s, kn, vn, cn = _sample_mixer(xs, cache_k[l].reshape(dec_b, win, kv_w),
                                        cache_v[l].reshape(dec_b, win, kv_w), state_conv[l], win_b, tabs_s,
                                        *shared, alpha=alpha)
        yp, ys = _moe(h_p, h_s, router_w[l], router_bias[l], w_gate[l], w_up[l], w_down[l], ws_gate[l], ws_up[l],
                      ws_down[l], row(ln2_g[l]), row(ln2_b[l]), alpha=alpha)
        xp, xs = yp.reshape(bsz, seq, d), ys.reshape(dec_b, dec_seq, d)
        heads = lambda a: a.reshape(a.shape[0], win, N_KV_HEADS, HEAD_DIM)
        for o, a in zip(outs, (heads(kp), heads(vp), cp, heads(kn), heads(vn), cn)):
            o.append(a)
    return (xp, xs) + tuple(jnp.stack(o, axis=0) for o in outs)
```

```python
import functools

import jax
import jax.numpy as jnp
from jax import lax
from jax.experimental import pallas as pl
from jax.experimental.pallas import tpu as pltpu

PAST_LEN = 16384
WINDOW = 128
HEAD_DIM = 64
N_KV_HEADS = 2
ROT_DIM = HEAD_DIM // 4
ROPE_THETA = 500000.0
CONV_K = 3
TOP_K = 6
N_EXPERT_GROUPS = 8
TOPK_GROUPS = 4
ROUTED_SCALE = 2.5
LN_EPS = 1e-5
RMS_EPS = 1e-6

LANES = 128
SUBLANES = 8
VMEM_LIMIT_BYTES = 56 * 1024 * 1024

PROMPT_TILE = 1024
SAMPLE_SEQS = 32
MOE_TILE = 256
SORT_CHUNK = 256
EXPERT_BLOCK = 2048
ROUTE_TILE = 1024
RUN_CHUNK = 8

F32 = jnp.float32
BF16 = jnp.bfloat16
NEG_INF = float("-inf")


def _rope_tables(positions):
    half = ROT_DIM // 2
    inv_freq = ROPE_THETA ** (-jnp.arange(0, ROT_DIM, 2, dtype=F32) / ROT_DIM)
    ang = positions.astype(F32)[:, None] * inv_freq[None, :]
    cos, sin = jnp.cos(ang), jnp.sin(ang)
    n = positions.shape[0]
    rest = HEAD_DIM - ROT_DIM
    c = jnp.concatenate([cos, cos, jnp.ones((n, rest), F32)], axis=-1)
    sa = jnp.concatenate([-sin, jnp.zeros((n, half + rest), F32)], axis=-1)
    sb = jnp.concatenate([jnp.zeros((n, half), F32), sin, jnp.zeros((n, rest), F32)], axis=-1)
    reps = LANES // HEAD_DIM
    return jnp.tile(c, (1, reps)), jnp.tile(sa, (1, reps)), jnp.tile(sb, (1, reps))


def _rope(x, c, sa, sb):
    half = ROT_DIM // 2
    return x * c + pltpu.roll(x, LANES - half, 1) * sa + pltpu.roll(x, half, 1) * sb


def _rms_norm(x, g):
    return x * lax.rsqrt(jnp.mean(jnp.square(x), axis=-1, keepdims=True) + RMS_EPS) * g


def _layer_norm(x, g, b):
    mu = jnp.mean(x, axis=-1, keepdims=True)
    var = jnp.mean(jnp.square(x - mu), axis=-1, keepdims=True)
    return (x - mu) * lax.rsqrt(var + LN_EPS) * g + b


def _short_conv(gated, prev2, prev1, row, conv_w, b_gate):
    g1 = pltpu.roll(gated, 1, 0)
    g2 = pltpu.roll(gated, 2, 0)
    g1 = jnp.where(row == 0, prev1, g1)
    g2 = jnp.where(row == 0, prev2, jnp.where(row == 1, prev1, g2))
    y = conv_w[0:1, :] * g2 + conv_w[1:2, :] * g1 + conv_w[2:3, :] * gated
    return b_gate * y


def _merge_norm(x, attn_o, conv_o, gattn, gconv, wout_ref, ln_g, ln_b, alpha):
    cat = jnp.concatenate([_rms_norm(attn_o, gattn), _rms_norm(conv_o, gconv)], axis=-1)
    mix = jnp.dot(cat.astype(BF16), wout_ref[...], preferred_element_type=F32)
    return _layer_norm(alpha * x + mix, ln_g, ln_b)


def _sink_softmax(parts, sink):
    m = sink
    for s in parts:
        m = jnp.maximum(m, jnp.max(s, axis=-1, keepdims=True))
    es = [jnp.exp(s - m) for s in parts]
    den = jnp.exp(sink - m)
    for e in es:
        den = den + jnp.sum(e, axis=-1, keepdims=True)
    return [(e / den).astype(BF16) for e in es]


def _prompt_mixer_kernel(x_ref, win_ref, c_ref, sa_ref, sb_ref, sinks_ref, convw_ref, gattn_ref, gconv_ref,
                         wout_ref, lng_ref, lnb_ref,
                         h_ref, ko_ref, vo_ref, co_ref,
                         q_s, kt_s, vm_s, o_s, gc_s, *, alpha, n_heads, attn_w, kv_w, conv_ch):
    s = pl.program_id(1)
    last = pl.num_programs(1) - 1
    tq = x_ref.shape[0]
    q_per_kv = n_heads // N_KV_HEADS
    heads_per_group = LANES // HEAD_DIM

    @pl.when(s == 0)
    def _():
        kt_s[:, :, 0:WINDOW] = jnp.zeros((kt_s.shape[0], LANES, WINDOW), BF16)
        vm_s[:, 0:WINDOW, :] = jnp.zeros((vm_s.shape[0], WINDOW, LANES), BF16)
        gc_s[...] = jnp.zeros(gc_s.shape, F32)

    @pl.when(s > 0)
    def _():
        kt_s[:, :, 0:WINDOW] = kt_s[:, :, tq:tq + WINDOW]
        vm_s[:, 0:WINDOW, :] = vm_s[:, tq:tq + WINDOW, :]

    x = x_ref[...]
    xb = x.astype(BF16)
    c, sa, sb = c_ref[...], sa_ref[...], sb_ref[...]

    def proj(lo, width):
        return jnp.dot(xb, win_ref[:, lo:lo + width], preferred_element_type=F32)

    scale = HEAD_DIM ** -0.5
    for j in range(attn_w // LANES):
        qj = _rope(proj(j * LANES, LANES), c, sa, sb)
        q_s[:, j * LANES:(j + 1) * LANES] = (qj * scale).astype(BF16)
    k = _rope(proj(attn_w, kv_w), c, sa, sb)
    v = proj(attn_w + kv_w, kv_w)

    @pl.when(s == last)
    def _():
        ko_ref[...] = k[tq - WINDOW:, :]
        vo_ref[...] = v[tq - WINDOW:, :]

    kt = k.T.astype(BF16)
    zeros_k = jnp.zeros((HEAD_DIM, tq), BF16)
    v_swapped = pltpu.roll(v, HEAD_DIM, 1)
    low_lanes = lax.broadcasted_iota(jnp.int32, (tq, LANES), 1) < HEAD_DIM
    for kvh in range(N_KV_HEADS):
        kt_h = kt[kvh * HEAD_DIM:(kvh + 1) * HEAD_DIM, :]
        v_lo = v if kvh == 0 else v_swapped
        v_hi = v_swapped if kvh == 0 else v
        kt_s[2 * kvh, :, WINDOW:] = jnp.concatenate([kt_h, zeros_k], axis=0)
        kt_s[2 * kvh + 1, :, WINDOW:] = jnp.concatenate([zeros_k, kt_h], axis=0)
        vm_s[2 * kvh, WINDOW:, :] = jnp.where(low_lanes, v_lo, 0.0).astype(BF16)
        vm_s[2 * kvh + 1, WINDOW:, :] = jnp.where(low_lanes, 0.0, v_hi).astype(BF16)

    qi = lax.broadcasted_iota(jnp.int32, (WINDOW, 2 * WINDOW), 0)
    ci = lax.broadcasted_iota(jnp.int32, (WINDOW, 2 * WINDOW), 1)
    band = (ci > qi) & (ci <= qi + WINDOW)
    for j in range(tq // WINDOW):
        r0 = j * WINDOW
        mask = band if j > 0 else band & ((ci >= WINDOW) | (s > 0))
        for grp in range(attn_w // LANES):
            q_grp = q_s[r0:r0 + WINDOW, grp * LANES:(grp + 1) * LANES]
            out = None
            for r in range(heads_per_group):
                hd = grp * heads_per_group + r
                src = 2 * (hd // q_per_kv) + r
                sink = sinks_ref[hd]
                sc = jnp.dot(q_grp, kt_s[src, :, r0:r0 + 2 * WINDOW], preferred_element_type=F32)
                sc = jnp.where(mask, sc, NEG_INF)
                m = jnp.maximum(jnp.max(sc, axis=-1, keepdims=True), sink)
                e = jnp.exp(sc - m)
                den = jnp.sum(e, axis=-1, keepdims=True) + jnp.exp(sink - m)
                o_h = jnp.dot(e.astype(BF16), vm_s[src, r0:r0 + 2 * WINDOW, :],
                              preferred_element_type=F32) * (1.0 / den)
                out = o_h if out is None else out + o_h
            o_s[r0:r0 + WINDOW, grp * LANES:(grp + 1) * LANES] = out

    o3 = attn_w + 2 * kv_w
    gated = proj(o3 + 2 * conv_ch, conv_ch) * proj(o3, conv_ch)
    row = lax.broadcasted_iota(jnp.int32, (tq, 1), 0)
    conv_o = _short_conv(gated, gc_s[0:1, :], gc_s[1:2, :], row, convw_ref[...], proj(o3 + conv_ch, conv_ch))
    gc_s[0:CONV_K - 1, :] = gated[tq - (CONV_K - 1):, :]

    @pl.when(s == last)
    def _():
        co_ref[...] = gated[tq - (CONV_K - 1):, :]

    h_ref[...] = _merge_norm(x, o_s[...], conv_o, gattn_ref[...], gconv_ref[...], wout_ref,
                             lng_ref[...], lnb_ref[...], alpha)


def _prompt_mixer(x, win_b, tabs, sinks, conv_w, g_attn, g_conv, wout_b, ln_g, ln_b, *, alpha):
    bsz, seq, d = x.shape
    tq = PROMPT_TILE
    ns = seq // tq
    attn_w = g_attn.shape[-1]
    conv_ch = g_conv.shape[-1]
    n_heads = attn_w // HEAD_DIM
    kv_w = N_KV_HEADS * HEAD_DIM
    assert kv_w == LANES and 2 * HEAD_DIM == LANES and (n_heads // N_KV_HEADS) % 2 == 0 and WINDOW == LANES
    in_cols = win_b.shape[-1]
    const2 = lambda b, s: (0, 0)
    kern = functools.partial(_prompt_mixer_kernel, alpha=alpha, n_heads=n_heads, attn_w=attn_w, kv_w=kv_w,
                             conv_ch=conv_ch)
    return pl.pallas_call(
        kern,
        grid=(bsz, ns),
        in_specs=[
            pl.BlockSpec((None, tq, d), lambda b, s: (b, s, 0)),
            pl.BlockSpec((d, in_cols), const2),
            pl.BlockSpec((tq, LANES), lambda b, s: (s, 0)),
            pl.BlockSpec((tq, LANES), lambda b, s: (s, 0)),
            pl.BlockSpec((tq, LANES), lambda b, s: (s, 0)),
            pl.BlockSpec(memory_space=pltpu.SMEM),
            pl.BlockSpec((CONV_K, conv_ch), const2),
            pl.BlockSpec((1, attn_w), const2),
            pl.BlockSpec((1, conv_ch), const2),
            pl.BlockSpec((attn_w + conv_ch, d), const2),
            pl.BlockSpec((1, d), const2),
            pl.BlockSpec((1, d), const2),
        ],
        out_specs=[
            pl.BlockSpec((tq, d), lambda b, s: (b * ns + s, 0)),
            pl.BlockSpec((None, WINDOW, kv_w), lambda b, s: (b, 0, 0)),
            pl.BlockSpec((None, WINDOW, kv_w), lambda b, s: (b, 0, 0)),
            pl.BlockSpec((None, CONV_K - 1, conv_ch), lambda b, s: (b, 0, 0)),
        ],
        out_shape=[
            jax.ShapeDtypeStruct((bsz * seq, d), F32),
            jax.ShapeDtypeStruct((bsz, WINDOW, kv_w), F32),
            jax.ShapeDtypeStruct((bsz, WINDOW, kv_w), F32),
            jax.ShapeDtypeStruct((bsz, CONV_K - 1, conv_ch), F32),
        ],
        scratch_shapes=[
            pltpu.VMEM((tq, attn_w), BF16),
            pltpu.VMEM((2 * N_KV_HEADS, LANES, WINDOW + tq), BF16),
            pltpu.VMEM((2 * N_KV_HEADS, WINDOW + tq, LANES), BF16),
            pltpu.VMEM((tq, attn_w), F32),
            pltpu.VMEM((SUBLANES, conv_ch), F32),
        ],
        compiler_params=pltpu.CompilerParams(dimension_semantics=("arbitrary", "arbitrary"),
                                             vmem_limit_bytes=VMEM_LIMIT_BYTES),
        name="prompt_mixer",
    )(x, win_b, *tabs, sinks, conv_w, g_attn, g_conv, wout_b, ln_g, ln_b)


def _sample_mixer_kernel(x_ref, ck_ref, cv_ref, st_ref, win_ref, c_ref, sa_ref, sb_ref, sinks_ref,
                         convw_ref, gattn_ref, gconv_ref, wout_ref, lng_ref, lnb_ref,
                         h_ref, ko_ref, vo_ref, co_ref, *, alpha, n_heads, attn_w, kv_w, conv_ch, dec_seq):
    nb, win = ck_ref.shape[0], ck_ref.shape[1]
    rows = nb * dec_seq
    q_per_kv = n_heads // N_KV_HEADS
    x = x_ref[...]
    xb = x.astype(BF16)
    c, sa, sb = c_ref[...], sa_ref[...], sb_ref[...]

    def proj(lo, width):
        return jnp.dot(xb, win_ref[:, lo:lo + width], preferred_element_type=F32)

    scale = HEAD_DIM ** -0.5
    k = _rope(proj(attn_w, kv_w), c, sa, sb)
    v = proj(attn_w + kv_w, kv_w)
    k3 = k.reshape(nb, dec_seq, kv_w)
    v3 = v.reshape(nb, dec_seq, kv_w)
    ck = ck_ref[...]
    cv = cv_ref[...]
    ko_ref[:, 0:win - dec_seq, :] = ck[:, dec_seq:, :]
    ko_ref[:, win - dec_seq:, :] = k3
    vo_ref[:, 0:win - dec_seq, :] = cv[:, dec_seq:, :]
    vo_ref[:, win - dec_seq:, :] = v3
    ckb, cvb, k3b, v3b = ck.astype(BF16), cv.astype(BF16), k3.astype(BF16), v3.astype(BF16)

    qrows = q_per_kv * dec_seq
    qi = lax.broadcasted_iota(jnp.int32, (nb, qrows, win), 1) % dec_seq
    mask_c = lax.broadcasted_iota(jnp.int32, (nb, qrows, win), 2) > qi + (win - WINDOW)
    qn = lax.broadcasted_iota(jnp.int32, (nb, qrows, dec_seq), 1) % dec_seq
    mask_n = lax.broadcasted_iota(jnp.int32, (nb, qrows, dec_seq), 2) <= qn
    sink_row = lax.broadcasted_iota(jnp.int32, (nb, qrows, 1), 1) // dec_seq

    q_chunks = [_rope(proj(j * LANES, LANES), c, sa, sb) * scale for j in range(attn_w // LANES)]
    heads_out = []
    for kvh in range(N_KV_HEADS):
        qs = []
        for g in range(q_per_kv):
            lo = (kvh * q_per_kv + g) * HEAD_DIM
            qh = q_chunks[lo // LANES][:, lo % LANES:lo % LANES + HEAD_DIM]
            qs.append(qh.reshape(nb, dec_seq, HEAD_DIM))
        qg = jnp.concatenate(qs, axis=1).astype(BF16)
        sl = slice(kvh * HEAD_DIM, (kvh + 1) * HEAD_DIM)
        sc_c = jnp.einsum("bqd,bkd->bqk", qg, ckb[:, :, sl], preferred_element_type=F32)
        sc_n = jnp.einsum("bqd,bkd->bqk", qg, k3b[:, :, sl], preferred_element_type=F32)
        sc_c = jnp.where(mask_c, sc_c, NEG_INF)
        sc_n = jnp.where(mask_n, sc_n, NEG_INF)
        sink = jnp.zeros((nb, qrows, 1), F32)
        for g in range(q_per_kv):
            sink = jnp.where(sink_row == g, sinks_ref[kvh * q_per_kv + g], sink)
        p_c, p_n = _sink_softmax([sc_c, sc_n], sink)
        og = (jnp.einsum("bqk,bkd->bqd", p_c, cvb[:, :, sl], preferred_element_type=F32)
              + jnp.einsum("bqk,bkd->bqd", p_n, v3b[:, :, sl], preferred_element_type=F32))
        for g in range(q_per_kv):
            heads_out.append(og[:, g * dec_seq:(g + 1) * dec_seq, :].reshape(rows, HEAD_DIM))
    attn_o = jnp.concatenate(heads_out, axis=-1)

    o3 = attn_w + 2 * kv_w
    gated = proj(o3 + 2 * conv_ch, conv_ch) * proj(o3, conv_ch)
    st = st_ref[...]
    prev2 = jnp.broadcast_to(st[:, 0:1, :], (nb, dec_seq, conv_ch)).reshape(rows, conv_ch)
    prev1 = jnp.broadcast_to(st[:, 1:2, :], (nb, dec_seq, conv_ch)).reshape(rows, conv_ch)
    row = lax.broadcasted_iota(jnp.int32, (rows, 1), 0) % dec_seq
    conv_o = _short_conv(gated, prev2, prev1, row, convw_ref[...], proj(o3 + conv_ch, conv_ch))
    co_ref[...] = gated.reshape(nb, dec_seq, conv_ch)[:, dec_seq - (CONV_K - 1):, :]

    h_ref[...] = _merge_norm(x, attn_o, conv_o, gattn_ref[...], gconv_ref[...], wout_ref,
                             lng_ref[...], lnb_ref[...], alpha)


def _sample_mixer(x, ck, cv, st, win_b, tabs, sinks, conv_w, g_attn, g_conv, wout_b, ln_g, ln_b, *, alpha):
    dec_b, dec_seq, d = x.shape
    assert dec_seq >= CONV_K - 1 and dec_seq % SUBLANES == 0
    nb = SAMPLE_SEQS
    rows = nb * dec_seq
    win = ck.shape[1]
    attn_w = g_attn.shape[-1]
    conv_ch = g_conv.shape[-1]
    n_heads = attn_w // HEAD_DIM
    kv_w = N_KV_HEADS * HEAD_DIM
    in_cols = win_b.shape[-1]
    const2 = lambda i: (0, 0)
    kern = functools.partial(_sample_mixer_kernel, alpha=alpha, n_heads=n_heads, attn_w=attn_w, kv_w=kv_w,
                             conv_ch=conv_ch, dec_seq=dec_seq)
    return pl.pallas_call(
        kern,
        grid=(dec_b // nb,),
        in_specs=[
            pl.BlockSpec((rows, d), lambda i: (i, 0)),
            pl.BlockSpec((nb, win, kv_w), lambda i: (i, 0, 0)),
            pl.BlockSpec((nb, win, kv_w), lambda i: (i, 0, 0)),
            pl.BlockSpec((nb, CONV_K - 1, conv_ch), lambda i: (i, 0, 0)),
            pl.BlockSpec((d, in_cols), const2),
            pl.BlockSpec((rows, LANES), const2),
            pl.BlockSpec((rows, LANES), const2),
            pl.BlockSpec((rows, LANES), const2),
            pl.BlockSpec(memory_space=pltpu.SMEM),
            pl.BlockSpec((CONV_K, conv_ch), const2),
            pl.BlockSpec((1, attn_w), const2),
            pl.BlockSpec((1, conv_ch), const2),
            pl.BlockSpec((attn_w + conv_ch, d), const2),
            pl.BlockSpec((1, d), const2),
            pl.BlockSpec((1, d), const2),
        ],
        out_specs=[
            pl.BlockSpec((rows, d), lambda i: (i, 0)),
            pl.BlockSpec((nb, win, kv_w), lambda i: (i, 0, 0)),
            pl.BlockSpec((nb, win, kv_w), lambda i: (i, 0, 0)),
            pl.BlockSpec((nb, CONV_K - 1, conv_ch), lambda i: (i, 0, 0)),
        ],
        out_shape=[
            jax.ShapeDtypeStruct((dec_b * dec_seq, d), F32),
            jax.ShapeDtypeStruct((dec_b, win, kv_w), F32),
            jax.ShapeDtypeStruct((dec_b, win, kv_w), F32),
            jax.ShapeDtypeStruct((dec_b, CONV_K - 1, conv_ch), F32),
        ],
        compiler_params=pltpu.CompilerParams(dimension_semantics=("arbitrary",),
                                             vmem_limit_bytes=VMEM_LIMIT_BYTES),
        name="sample_mixer",
    )(x.reshape(dec_b * dec_seq, d), ck, cv, st, win_b, *tabs, sinks, conv_w, g_attn, g_conv, wout_b,
      ln_g, ln_b)


def _over_experts(fn, x):
    return fn(fn(x, axis=0, keepdims=True), axis=1, keepdims=True)


def _two_group_specs(tm, d, n_prompt_tiles):
    return [pl.BlockSpec((tm, d), lambda i, *_: (jnp.minimum(i, n_prompt_tiles - 1), 0)),
            pl.BlockSpec((tm, d), lambda i, *_: (jnp.maximum(i - n_prompt_tiles, 0), 0))]


def _route_kernel(hp_ref, hs_ref, rwt_ref, bias_ref, eidx_ref, rank_ref, gate_ref, cnt_ref, earlier_s, *,
                  n_experts, n_prompt_tiles):
    i = pl.program_id(0)
    tm = hp_ref.shape[0]
    per_group = n_experts // N_EXPERT_GROUPS
    shape3 = (N_EXPERT_GROUPS, per_group, tm)

    @pl.when(i == 0)
    def _():
        t_from = lax.broadcasted_iota(jnp.int32, (tm, tm), 0)
        t_to = lax.broadcasted_iota(jnp.int32, (tm, tm), 1)
        earlier_s[...] = ((t_from < t_to) & (t_from // MOE_TILE == t_to // MOE_TILE)).astype(BF16)

    h = jnp.where(i < n_prompt_tiles, hp_ref[...], hs_ref[...])
    logits = lax.dot_general(rwt_ref[...], h.astype(BF16), (((1,), (1,)), ((), ())),
                             preferred_element_type=F32)
    scores = jax.nn.sigmoid(logits)
    sel = scores + bias_ref[...]
    scores3 = scores.reshape(shape3)
    grp = sel.reshape(shape3)
    member = lax.broadcasted_iota(jnp.int32, shape3, 1).astype(F32)
    group = lax.broadcasted_iota(jnp.int32, shape3, 0).astype(F32)
    expert = group * per_group + member

    m1 = jnp.max(grp, axis=1, keepdims=True)
    f1 = jnp.min(jnp.where(grp == m1, member, float(per_group)), axis=1, keepdims=True)
    m2 = jnp.max(jnp.where(member == f1, NEG_INF, grp), axis=1, keepdims=True)
    gscore = m1 + m2

    gid = lax.broadcasted_iota(jnp.int32, gscore.shape, 0).astype(F32)
    gmask = jnp.zeros(gscore.shape, F32)
    cur = gscore
    for _ in range(TOPK_GROUPS):
        mx = jnp.max(cur, axis=0, keepdims=True)
        pick = gid == jnp.min(jnp.where(cur == mx, gid, float(N_EXPERT_GROUPS)), axis=0, keepdims=True)
        gmask = jnp.where(pick, 1.0, gmask)
        cur = jnp.where(pick, NEG_INF, cur)

    cand = jnp.where(gmask > 0.0, grp, NEG_INF)
    chosen = jnp.zeros(shape3, F32)
    picks, firsts, weights = [], [], []
    for _ in range(TOP_K):
        mx = _over_experts(jnp.max, cand)
        first = _over_experts(jnp.min, jnp.where(cand == mx, expert, float(n_experts)))
        pick = expert == first
        picks.append(pick)
        firsts.append(first)
        weights.append(_over_experts(jnp.sum, jnp.where(pick, scores3, 0.0)))
        chosen = jnp.where(pick, 1.0, chosen)
        cand = jnp.where(pick, NEG_INF, cand)
    wsum = weights[0]
    for w in weights[1:]:
        wsum = wsum + w

    chosen_b = chosen.reshape(n_experts, tm).astype(BF16)
    before3 = jnp.dot(chosen_b, earlier_s[...], preferred_element_type=F32).reshape(shape3)

    pad = SUBLANES - TOP_K
    eidx = [f.reshape(1, tm).astype(jnp.int32) for f in firsts]
    rank = [_over_experts(jnp.sum, jnp.where(p, before3, 0.0)).reshape(1, tm).astype(jnp.int32) for p in picks]
    gate = [(w / wsum * ROUTED_SCALE).reshape(1, tm) for w in weights]
    eidx_ref[...] = jnp.concatenate(eidx + [jnp.zeros((pad, tm), jnp.int32)], axis=0)
    rank_ref[...] = jnp.concatenate(rank + [jnp.zeros((pad, tm), jnp.int32)], axis=0)
    gate_ref[...] = jnp.concatenate(gate + [jnp.zeros((pad, tm), F32)], axis=0)

    for sub in range(tm // MOE_TILE):
        cnt_ref[sub * SUBLANES:(sub + 1) * SUBLANES, :] = lax.dot_general(
            jnp.ones((SUBLANES, MOE_TILE), BF16), chosen_b[:, sub * MOE_TILE:(sub + 1) * MOE_TILE],
            (((1,), (1,)), ((), ())), preferred_element_type=F32)


def _route(h_p, h_s, rwt_b, bias_col):
    d = h_p.shape[1]
    t_all = h_p.shape[0] + h_s.shape[0]
    n_experts = rwt_b.shape[0]
    tm = ROUTE_TILE
    npt = h_p.shape[0] // tm
    moe_tiles = tm // MOE_TILE
    row_spec = pl.BlockSpec((SUBLANES, tm), lambda i: (0, i))
    return pl.pallas_call(
        functools.partial(_route_kernel, n_experts=n_experts, n_prompt_tiles=npt),
        grid=(t_all // tm,),
        in_specs=_two_group_specs(tm, d, npt) + [
            pl.BlockSpec((n_experts, d), lambda i: (0, 0)),
            pl.BlockSpec((n_experts, 1), lambda i: (0, 0)),
        ],
        out_specs=[row_spec, row_spec, row_spec,
                   pl.BlockSpec((moe_tiles * SUBLANES, n_experts), lambda i: (i, 0))],
        out_shape=[
            jax.ShapeDtypeStruct((SUBLANES, t_all), jnp.int32),
            jax.ShapeDtypeStruct((SUBLANES, t_all), jnp.int32),
            jax.ShapeDtypeStruct((SUBLANES, t_all), F32),
            jax.ShapeDtypeStruct((t_all // MOE_TILE * SUBLANES, n_experts), F32),
        ],
        scratch_shapes=[pltpu.VMEM((tm, tm), BF16)],
        compiler_params=pltpu.CompilerParams(dimension_semantics=("arbitrary",),
                                             vmem_limit_bytes=VMEM_LIMIT_BYTES),
        name="route",
    )(h_p, h_s, rwt_b, bias_col)


def _for_each_part(groups, max_part, fn):
    for size in [1 << b for b in range(max_part.bit_length())]:
        @pl.when((groups & size) != 0)
        def _(size=size):
            fn(groups & (size - 1), size)


def _start_run_copies(tile, cnt_ref, ls_ref, gs_ref, big_ref, n_experts, start_copy):
    def run(e):
        idx = tile * n_experts + e
        return cnt_ref[idx], ls_ref[idx], gs_ref[idx]

    for e in range(n_experts):
        groups, l0, g0 = run(e)
        _for_each_part(groups, RUN_CHUNK // 2, lambda off, size: start_copy(l0 + off, g0 + off, size))

    @pl.when(big_ref[tile] != 0)
    def _():
        def per_expert(e, c):
            groups, l0, g0 = run(e)
            base = groups & (RUN_CHUNK - 1)

            def chunk(j, c2):
                off = base + j * RUN_CHUNK
                start_copy(l0 + off, g0 + off, RUN_CHUNK)
                return c2

            lax.fori_loop(0, lax.shift_right_logical(groups, RUN_CHUNK.bit_length() - 1), chunk, 0)
            return c

        lax.fori_loop(0, n_experts, per_expert, 0)


def _top_bit(n):
    return 1 << (n.bit_length() - 1)


def _pack_rows(x):
    rows, two_w = x.shape
    w = two_w // 2
    x3 = x.reshape(rows // SUBLANES, SUBLANES, two_w)
    halves = jnp.concatenate([x3[:, :, :w], x3[:, :, w:]], axis=1).astype(BF16)
    return pltpu.bitcast(halves, jnp.uint32)


def _unpack_rows(p):
    groups, _, w = p.shape
    halves = pltpu.bitcast(p, BF16).astype(F32)
    return jnp.concatenate([halves[:, :SUBLANES, :], halves[:, SUBLANES:, :]], axis=-1).reshape(
        groups * SUBLANES, 2 * w)


def _local_groups(tm, n_experts):
    n = TOP_K * tm + n_experts * (SUBLANES - 1)
    return -(-n // SORT_CHUNK) * SORT_CHUNK // SUBLANES


def _dispatch_kernel(cnt_ref, ls_ref, gs_ref, tot_ref, big_ref, zstart_ref, zlen_ref, tail_ref, lp_ref, hp_ref,
                     hs_ref, xs_hbm, lbuf, zero_s, sem, zero_sem, *, n_experts, n_prompt_tiles):
    i = pl.program_id(0)
    tm, d = hp_ref.shape
    n = TOP_K * tm // SUBLANES
    chunk = SORT_CHUNK // SUBLANES
    slot = i % 2

    def tile_wait(tile):
        pltpu.make_async_copy(lbuf.at[0, pl.ds(0, n)], xs_hbm.at[pl.ds(0, n)], sem).wait()
        _for_each_part(tot_ref[tile] - n, _top_bit(n_experts * (SUBLANES - 1) // SUBLANES), lambda off, size:
                       pltpu.make_async_copy(lbuf.at[0, pl.ds(0, size)], xs_hbm.at[pl.ds(0, size)], sem).wait())

    def clear_padding(act):
        def per_expert(e, c):
            _for_each_part(zlen_ref[e], zero_s.shape[0], lambda off, size: act(pltpu.make_async_copy(
                zero_s.at[pl.ds(0, size)], xs_hbm.at[pl.ds(zstart_ref[e] + off, size)], zero_sem)))
            return c

        lax.fori_loop(0, n_experts, per_expert, 0)

    piece = zero_s.shape[0]
    n_pieces = (xs_hbm.shape[0] - tail_ref[0]) // piece
    per_step = (n_pieces + pl.num_programs(0) - 1) // pl.num_programs(0)

    def tail_piece(k):
        return pltpu.make_async_copy(zero_s, xs_hbm.at[pl.ds(pl.multiple_of(tail_ref[0] + k * piece, piece), piece)],
                                     zero_sem)

    @pl.when(i == 0)
    def _():
        zero_s[...] = jnp.zeros(zero_s.shape, zero_s.dtype)
        clear_padding(lambda copy: copy.start())

    def start_tail(j, c):
        k = i * per_step + j

        @pl.when(k < n_pieces)
        def _():
            tail_piece(k).start()
        return c

    lax.fori_loop(0, per_step, start_tail, 0)

    hb = jnp.where(i < n_prompt_tiles, hp_ref[...], hs_ref[...]).astype(BF16)
    lp = lp_ref[...]
    def sort_chunk(r):
        rows = r * SORT_CHUNK + lax.broadcasted_iota(jnp.int32, (SORT_CHUNK, tm), 0)
        hit = jnp.zeros((SORT_CHUNK, tm), F32)
        for k in range(TOP_K):
            hit = jnp.where(rows == lp[k:k + 1, :], 1.0, hit)
        lbuf[slot, pl.ds(r * chunk, chunk)] = _pack_rows(jnp.dot(hit.astype(BF16), hb, preferred_element_type=F32))

    last_chunk = lbuf.shape[1] // chunk - 1
    for r in range(last_chunk):
        sort_chunk(r)
    pl.when(tot_ref[i] > last_chunk * chunk)(functools.partial(sort_chunk, last_chunk))

    @pl.when(i > 0)
    def _():
        tile_wait(i - 1)

    _start_run_copies(i, cnt_ref, ls_ref, gs_ref, big_ref, n_experts, lambda lgrp, ggrp, size: pltpu.make_async_copy(
        lbuf.at[slot, pl.ds(lgrp, size)], xs_hbm.at[pl.ds(ggrp, size)], sem).start())

    @pl.when(i == pl.num_programs(0) - 1)
    def _():
        tile_wait(i)
        clear_padding(lambda copy: copy.wait())

        def wait_tail(k, c):
            tail_piece(k).wait()
            return c

        lax.fori_loop(0, n_pieces, wait_tail, 0)


def _dispatch(tables, zstart, zlen, tail, lp, h_p, h_s, *, n_rows, blk):
    d = h_p.shape[1]
    t_all = h_p.shape[0] + h_s.shape[0]
    tm = MOE_TILE
    npt = h_p.shape[0] // tm
    n_experts = zstart.shape[0]
    return pl.pallas_call(
        functools.partial(_dispatch_kernel, n_experts=n_experts, n_prompt_tiles=npt),
        grid_spec=pltpu.PrefetchScalarGridSpec(
            num_scalar_prefetch=8,
            grid=(t_all // tm,),
            in_specs=[pl.BlockSpec((SUBLANES, tm), lambda i, *_: (0, i))] + _two_group_specs(tm, d, npt),
            out_specs=pl.BlockSpec(memory_space=pl.ANY),
            scratch_shapes=[
                pltpu.VMEM((2, _local_groups(tm, n_experts), SUBLANES, d // 2), jnp.uint32),
                pltpu.VMEM((_top_bit(blk // SUBLANES - 1), SUBLANES, d // 2), jnp.uint32),
                pltpu.SemaphoreType.DMA,
                pltpu.SemaphoreType.DMA,
            ],
        ),
        out_shape=jax.ShapeDtypeStruct((n_rows // SUBLANES, SUBLANES, d // 2), jnp.uint32),
        compiler_params=pltpu.CompilerParams(dimension_semantics=("arbitrary",),
                                             vmem_limit_bytes=VMEM_LIMIT_BYTES),
        name="dispatch",
    )(*tables, zstart, zlen, tail, lp, h_p, h_s)


def _silu(x):
    return x * jax.nn.sigmoid(x)


def _expert_kernel(be_ref, nact_ref, xs_ref, wg_ref, wu_ref, wd_ref, ys_ref, wg_s, wu_s, wd_s):
    b = pl.program_id(0)

    @pl.when(b < nact_ref[0])
    def _():
        @pl.when((b == 0) | (be_ref[b] != be_ref[jnp.maximum(b - 1, 0)]))
        def _():
            wg_s[...] = wg_ref[...].astype(BF16)
            wu_s[...] = wu_ref[...].astype(BF16)
            wd_s[...] = wd_ref[...].astype(BF16)

        xb = _unpack_rows(xs_ref[...]).astype(BF16)
        hid = _silu(jnp.dot(xb, wg_s[...], preferred_element_type=F32)) * jnp.dot(
            xb, wu_s[...], preferred_element_type=F32)
        ys_ref[...] = _pack_rows(jnp.dot(hid.astype(BF16), wd_s[...], preferred_element_type=F32))


def _experts(block_expert, nact, xs, w_gate, w_up, w_down, *, blk):
    d = w_gate.shape[-2]
    ff = w_gate.shape[-1]
    block = (blk // SUBLANES,) + xs.shape[1:]
    n_blocks = xs.shape[0] // block[0]

    def active(b, be, na):
        return jnp.minimum(b, na[0] - 1)

    return pl.pallas_call(
        _expert_kernel,
        grid_spec=pltpu.PrefetchScalarGridSpec(
            num_scalar_prefetch=2,
            grid=(n_blocks,),
            in_specs=[
                pl.BlockSpec(block, lambda b, be, na: (active(b, be, na), 0, 0)),
                pl.BlockSpec((None, d, ff), lambda b, be, na: (be[active(b, be, na)], 0, 0)),
                pl.BlockSpec((None, d, ff), lambda b, be, na: (be[active(b, be, na)], 0, 0)),
                pl.BlockSpec((None, ff, d), lambda b, be, na: (be[active(b, be, na)], 0, 0)),
            ],
            out_specs=pl.BlockSpec(block, lambda b, be, na: (active(b, be, na), 0, 0)),
            scratch_shapes=[
                pltpu.VMEM((d, ff), BF16),
                pltpu.VMEM((d, ff), BF16),
                pltpu.VMEM((ff, d), BF16),
            ],
        ),
        out_shape=jax.ShapeDtypeStruct(xs.shape, jnp.uint32),
        input_output_aliases={2: 0},
        compiler_params=pltpu.CompilerParams(dimension_semantics=("arbitrary",),
                                             vmem_limit_bytes=VMEM_LIMIT_BYTES),
        name="experts",
    )(block_expert, nact, xs, w_gate, w_up, w_down)


def _combine_kernel(cnt_ref, ls_ref, gs_ref, tot_ref, big_ref, ys_hbm, lpt_ref, gate_ref, hp_ref, hs_ref, wsg_ref, wsu_ref,
                    wsd_ref, lng_ref, lnb_ref, yp_ref, ysm_ref, ybuf, moe_s, sems, *, alpha, n_experts,
                    n_prompt_tiles):
    i = pl.program_id(0)
    tm, d = hp_ref.shape
    n = TOP_K * tm // SUBLANES
    chunk = SORT_CHUNK // SUBLANES
    slot = i % 2

    def fetch(tile, to):
        _start_run_copies(tile, cnt_ref, ls_ref, gs_ref, big_ref, n_experts, lambda lgrp, ggrp, size:
                          pltpu.make_async_copy(ys_hbm.at[pl.ds(ggrp, size)], ybuf.at[to, pl.ds(lgrp, size)],
                                                sems.at[to]).start())

    @pl.when(i == 0)
    def _():
        ybuf[...] = jnp.zeros(ybuf.shape, ybuf.dtype)
        fetch(0, 0)

    @pl.when(i + 1 < pl.num_programs(0))
    def _():
        fetch(i + 1, 1 - slot)

    h = jnp.where(i < n_prompt_tiles, hp_ref[...], hs_ref[...])
    hb = h.astype(BF16)
    hid = _silu(jnp.dot(hb, wsg_ref[...], preferred_element_type=F32)) * jnp.dot(
        hb, wsu_ref[...], preferred_element_type=F32)
    shared = jnp.dot(hid.astype(BF16), wsd_ref[...], preferred_element_type=F32)

    pltpu.make_async_copy(ys_hbm.at[pl.ds(0, n)], ybuf.at[slot, pl.ds(0, n)], sems.at[slot]).wait()
    _for_each_part(tot_ref[i] - n, _top_bit(n_experts * (SUBLANES - 1) // SUBLANES), lambda off, size:
                   pltpu.make_async_copy(ys_hbm.at[pl.ds(0, size)], ybuf.at[slot, pl.ds(0, size)],
                                         sems.at[slot]).wait())

    lpt = lpt_ref[...]
    gate = gate_ref[...]

    def chunk_sum(r):
        cols = r * SORT_CHUNK + lax.broadcasted_iota(jnp.int32, (tm, SORT_CHUNK), 1)
        g = jnp.zeros((tm, SORT_CHUNK), F32)
        for k in range(TOP_K):
            g = jnp.where(cols == lpt[:, k:k + 1], gate[:, k:k + 1], g)
        yb = _unpack_rows(ybuf[slot, pl.ds(r * chunk, chunk)]).astype(BF16)
        return jnp.dot(g.astype(BF16), yb, preferred_element_type=F32)

    last_chunk = ybuf.shape[1] // chunk - 1
    moe = shared
    for r in range(last_chunk):
        moe = moe + chunk_sum(r)
    moe_s[...] = moe

    @pl.when(tot_ref[i] > last_chunk * chunk)
    def _():
        moe_s[...] += chunk_sum(last_chunk)

    y = _layer_norm(alpha * h + moe_s[...], lng_ref[...], lnb_ref[...])

    @pl.when(i < n_prompt_tiles)
    def _():
        yp_ref[...] = y

    @pl.when(i >= n_prompt_tiles)
    def _():
        ysm_ref[...] = y


def _combine(tables, ys, lp_t, gates_t, h_p, h_s, wsg_b, wsu_b, wsd_b, ln_g, ln_b, *, alpha):
    t_prompt, d = h_p.shape
    t_all = t_prompt + h_s.shape[0]
    tm = MOE_TILE
    ff = wsg_b.shape[-1]
    npt = t_prompt // tm
    n_experts = tables[0].shape[0] // (t_all // tm)
    const2 = lambda i, *_: (0, 0)
    return pl.pallas_call(
        functools.partial(_combine_kernel, alpha=alpha, n_experts=n_experts, n_prompt_tiles=npt),
        grid_spec=pltpu.PrefetchScalarGridSpec(
            num_scalar_prefetch=5,
            grid=(t_all // tm,),
            in_specs=[
                pl.BlockSpec(memory_space=pl.ANY),
                pl.BlockSpec((tm, SUBLANES), lambda i, *_: (i, 0)),
                pl.BlockSpec((tm, SUBLANES), lambda i, *_: (i, 0)),
                *_two_group_specs(tm, d, npt),
                pl.BlockSpec((d, ff), const2),
                pl.BlockSpec((d, ff), const2),
                pl.BlockSpec((ff, d), const2),
                pl.BlockSpec((1, d), const2),
                pl.BlockSpec((1, d), const2),
            ],
            out_specs=[
                pl.BlockSpec((tm, d), lambda i, *_: (jnp.minimum(i, npt - 1), 0)),
                pl.BlockSpec((tm, d), lambda i, *_: (jnp.maximum(i - npt, 0), 0)),
            ],
            scratch_shapes=[
                pltpu.VMEM((2, _local_groups(tm, n_experts), SUBLANES, d // 2), jnp.uint32),
                pltpu.VMEM((tm, d), F32),
                pltpu.SemaphoreType.DMA((2,)),
            ],
        ),
        out_shape=[
            jax.ShapeDtypeStruct((t_prompt, d), F32),
            jax.ShapeDtypeStruct((t_all - t_prompt, d), F32),
        ],
        compiler_params=pltpu.CompilerParams(dimension_semantics=("arbitrary",),
                                             vmem_limit_bytes=VMEM_LIMIT_BYTES),
        name="combine",
    )(*tables, ys, lp_t, gates_t, h_p, h_s, wsg_b, wsu_b, wsd_b, ln_g, ln_b)


def _moe(h_p, h_s, router_w, router_bias, w_gate, w_up, w_down, ws_gate, ws_up, ws_down, ln_g, ln_b, *, alpha):
    t_all = h_p.shape[0] + h_s.shape[0]
    n_experts = router_w.shape[-1]
    blk = EXPERT_BLOCK
    tm = MOE_TILE
    nt = t_all // tm
    eidx8, lrank8, gate8, cnt8 = _route(h_p, h_s, router_w.T.astype(BF16), router_bias.reshape(n_experts, 1))

    bgrp = blk // SUBLANES
    cnt = cnt8.reshape(nt, SUBLANES, n_experts)[:, 0, :].astype(jnp.int32)
    cnt = (cnt + SUBLANES - 1) // SUBLANES
    counts = jnp.sum(cnt, axis=0)
    blocks_per_e = (counts + bgrp - 1) // bgrp
    block_end = jnp.cumsum(blocks_per_e)
    pad_start = (block_end - blocks_per_e) * bgrp
    n_blocks = -(-(t_all * TOP_K + nt * n_experts * (SUBLANES - 1)) // blk) + n_experts
    block_expert = jnp.minimum(jnp.sum(block_end[None, :] <= jnp.arange(n_blocks)[:, None], axis=1),
                               n_experts - 1).astype(jnp.int32)
    nact = block_end[-1:].astype(jnp.int32)
    zstart = (pad_start + counts).astype(jnp.int32)
    zlen = (blocks_per_e * bgrp - counts).astype(jnp.int32)
    gstart = pad_start[None, :] + jnp.cumsum(cnt, axis=0) - cnt
    lstart = jnp.cumsum(cnt, axis=1) - cnt
    tables = tuple(a.reshape(-1).astype(jnp.int32) for a in (
        cnt, lstart, gstart, jnp.sum(cnt, axis=1), jnp.max(cnt, axis=1) >= RUN_CHUNK))
    lstart_tok = jnp.repeat(lstart * SUBLANES, tm, axis=0)
    lp8 = jnp.sum(jnp.where(eidx8[..., None] == jnp.arange(n_experts), lstart_tok[None], 0), axis=-1) + lrank8
    lp8 = lp8.astype(jnp.int32)

    xs = _dispatch(tables, zstart, zlen, nact * bgrp, lp8, h_p, h_s, n_rows=n_blocks * blk, blk=blk)
    ys = _experts(block_expert, nact, xs, w_gate, w_up, w_down, blk=blk)
    return _combine(tables, ys, lp8.T, gate8.T, h_p, h_s, ws_gate.astype(BF16), ws_up.astype(BF16), ws_down.astype(BF16),
                    ln_g, ln_b, alpha=alpha)


def kernel(x_prompt, x_sample, cache_k, cache_v, state_conv, w_in, attn_sinks, conv_w, g_attn_out, g_conv_out, w_out, ln1_g, ln1_b, router_w, router_bias, w_gate, w_up, w_down, ws_gate, ws_up, ws_down, ln2_g, ln2_b):
    depth = w_in.shape[0]
    bsz, seq, d = x_prompt.shape
    dec_b, dec_seq, _ = x_sample.shape
    win = cache_k.shape[2]
    kv_w = N_KV_HEADS * HEAD_DIM
    t_prompt = bsz * seq
    t_all = t_prompt + dec_b * dec_seq
    alpha = (2.0 * depth) ** 0.25
    assert win == WINDOW and seq % PROMPT_TILE == 0 and dec_b % SAMPLE_SEQS == 0
    assert t_prompt % ROUTE_TILE == 0 and (t_all - t_prompt) % ROUTE_TILE == 0 and ROUTE_TILE % MOE_TILE == 0
    assert MOE_TILE & (MOE_TILE - 1) == 0 and (TOP_K * MOE_TILE) % SORT_CHUNK == 0

    tabs_p = _rope_tables(jnp.arange(seq))
    tabs_s = tuple(jnp.tile(t, (SAMPLE_SEQS, 1)) for t in _rope_tables(PAST_LEN + jnp.arange(dec_seq)))
    row = lambda a: a.reshape(1, -1)

    xp, xs = x_prompt, x_sample
    outs = [[] for _ in range(6)]
    for l in range(depth):
        win_b, wout_b = w_in[l].astype(BF16), w_out[l].astype(BF16)
        shared = (attn_sinks[l], conv_w[l], row(g_attn_out[l]), row(g_conv_out[l]), wout_b, row(ln1_g[l]),
                  row(ln1_b[l]))
        h_p, kp, vp, cp = _prompt_mixer(xp, win_b, tabs_p, *shared, alpha=alpha)
        h_s, kn, vn, cn = _sample_mixer(xs, cache_k[l].reshape(dec_b, win, kv_w),
                                        cache_v[l].reshape(dec_b, win, kv_w), state_conv[l], win_b, tabs_s,
                                        *shared, alpha=alpha)
        yp, ys = _moe(h_p, h_s, router_w[l], router_bias[l], w_gate[l], w_up[l], w_down[l], ws_gate[l], ws_up[l],
                      ws_down[l], row(ln2_g[l]), row(ln2_b[l]), alpha=alpha)
        xp, xs = yp.reshape(bsz, seq, d), ys.reshape(dec_b, dec_seq, d)
        heads = lambda a: a.reshape(a.shape[0], win, N_KV_HEADS, HEAD_DIM)
        for o, a in zip(outs, (heads(kp), heads(vp), cp, heads(kn), heads(vn), cn)):
            o.append(a)
    return (xp, xs) + tuple(jnp.stack(o, axis=0) for o in outs)
```

```python
import functools

import jax
import jax.numpy as jnp
from jax import lax
from jax.experimental import pallas as pl
from jax.experimental.pallas import tpu as pltpu

PAST_LEN = 16384
WINDOW = 128
HEAD_DIM = 64
N_KV_HEADS = 2
ROT_DIM = HEAD_DIM // 4
ROPE_THETA = 500000.0
CONV_K = 3
TOP_K = 6
N_EXPERT_GROUPS = 8
TOPK_GROUPS = 4
ROUTED_SCALE = 2.5
LN_EPS = 1e-5
RMS_EPS = 1e-6

LANES = 128
SUBLANES = 8
VMEM_LIMIT_BYTES = 56 * 1024 * 1024

PROMPT_TILE = 1024
SAMPLE_SEQS = 32
MOE_TILE = 256
SORT_CHUNK = 256
EXPERT_BLOCK = 2048
ROUTE_TILE = 1024
RUN_CHUNK = 8

F32 = jnp.float32
BF16 = jnp.bfloat16
NEG_INF = float("-inf")


def _rope_tables(positions):
    half = ROT_DIM // 2
    inv_freq = ROPE_THETA ** (-jnp.arange(0, ROT_DIM, 2, dtype=F32) / ROT_DIM)
    ang = positions.astype(F32)[:, None] * inv_freq[None, :]
    cos, sin = jnp.cos(ang), jnp.sin(ang)
    n = positions.shape[0]
    rest = HEAD_DIM - ROT_DIM
    c = jnp.concatenate([cos, cos, jnp.ones((n, rest), F32)], axis=-1)
    sa = jnp.concatenate([-sin, jnp.zeros((n, half + rest), F32)], axis=-1)
    sb = jnp.concatenate([jnp.zeros((n, half), F32), sin, jnp.zeros((n, rest), F32)], axis=-1)
    reps = LANES // HEAD_DIM
    return jnp.tile(c, (1, reps)), jnp.tile(sa, (1, reps)), jnp.tile(sb, (1, reps))


def _rope(x, c, sa, sb):
    half = ROT_DIM // 2
    return x * c + pltpu.roll(x, LANES - half, 1) * sa + pltpu.roll(x, half, 1) * sb


def _rms_norm(x, g):
    return x * lax.rsqrt(jnp.mean(jnp.square(x), axis=-1, keepdims=True) + RMS_EPS) * g


def _layer_norm(x, g, b):
    mu = jnp.mean(x, axis=-1, keepdims=True)
    var = jnp.mean(jnp.square(x - mu), axis=-1, keepdims=True)
    return (x - mu) * lax.rsqrt(var + LN_EPS) * g + b


def _short_conv(gated, prev2, prev1, row, conv_w, b_gate):
    g1 = pltpu.roll(gated, 1, 0)
    g2 = pltpu.roll(gated, 2, 0)
    g1 = jnp.where(row == 0, prev1, g1)
    g2 = jnp.where(row == 0, prev2, jnp.where(row == 1, prev1, g2))
    y = conv_w[0:1, :] * g2 + conv_w[1:2, :] * g1 + conv_w[2:3, :] * gated
    return b_gate * y


def _merge_norm(x, attn_o, conv_o, gattn, gconv, wout_ref, ln_g, ln_b, alpha):
    cat = jnp.concatenate([_rms_norm(attn_o, gattn), _rms_norm(conv_o, gconv)], axis=-1)
    mix = jnp.dot(cat.astype(BF16), wout_ref[...], preferred_element_type=F32)
    return _layer_norm(alpha * x + mix, ln_g, ln_b)


def _sink_softmax(parts, sink):
    m = sink
    for s in parts:
        m = jnp.maximum(m, jnp.max(s, axis=-1, keepdims=True))
    es = [jnp.exp(s - m) for s in parts]
    den = jnp.exp(sink - m)
    for e in es:
        den = den + jnp.sum(e, axis=-1, keepdims=True)
    return [(e / den).astype(BF16) for e in es]


def _prompt_mixer_kernel(x_ref, win_ref, c_ref, sa_ref, sb_ref, sinks_ref, convw_ref, gattn_ref, gconv_ref,
                         wout_ref, lng_ref, lnb_ref,
                         h_ref, ko_ref, vo_ref, co_ref,
                         q_s, kt_s, vm_s, o_s, gc_s, *, alpha, n_heads, attn_w, kv_w, conv_ch):
    s = pl.program_id(1)
    last = pl.num_programs(1) - 1
    tq = x_ref.shape[0]
    q_per_kv = n_heads // N_KV_HEADS
    heads_per_group = LANES // HEAD_DIM

    @pl.when(s == 0)
    def _():
        kt_s[:, :, 0:WINDOW] = jnp.zeros((kt_s.shape[0], LANES, WINDOW), BF16)
        vm_s[:, 0:WINDOW, :] = jnp.zeros((vm_s.shape[0], WINDOW, LANES), BF16)
        gc_s[...] = jnp.zeros(gc_s.shape, F32)

    @pl.when(s > 0)
    def _():
        kt_s[:, :, 0:WINDOW] = kt_s[:, :, tq:tq + WINDOW]
        vm_s[:, 0:WINDOW, :] = vm_s[:, tq:tq + WINDOW, :]

    x = x_ref[...]
    xb = x.astype(BF16)
    c, sa, sb = c_ref[...], sa_ref[...], sb_ref[...]

    def proj(lo, width):
        return jnp.dot(xb, win_ref[:, lo:lo + width], preferred_element_type=F32)

    scale = HEAD_DIM ** -0.5
    for j in range(attn_w // LANES):
        qj = _rope(proj(j * LANES, LANES), c, sa, sb)
        q_s[:, j * LANES:(j + 1) * LANES] = (qj * scale).astype(BF16)
    k = _rope(proj(attn_w, kv_w), c, sa, sb)
    v = proj(attn_w + kv_w, kv_w)

    @pl.when(s == last)
    def _():
        ko_ref[...] = k[tq - WINDOW:, :]
        vo_ref[...] = v[tq - WINDOW:, :]

    kt = k.T.astype(BF16)
    zeros_k = jnp.zeros((HEAD_DIM, tq), BF16)
    v_swapped = pltpu.roll(v, HEAD_DIM, 1)
    low_lanes = lax.broadcasted_iota(jnp.int32, (tq, LANES), 1) < HEAD_DIM
    for kvh in range(N_KV_HEADS):
        kt_h = kt[kvh * HEAD_DIM:(kvh + 1) * HEAD_DIM, :]
        v_lo = v if kvh == 0 else v_swapped
        v_hi = v_swapped if kvh == 0 else v
        kt_s[2 * kvh, :, WINDOW:] = jnp.concatenate([kt_h, zeros_k], axis=0)
        kt_s[2 * kvh + 1, :, WINDOW:] = jnp.concatenate([zeros_k, kt_h], axis=0)
        vm_s[2 * kvh, WINDOW:, :] = jnp.where(low_lanes, v_lo, 0.0).astype(BF16)
        vm_s[2 * kvh + 1, WINDOW:, :] = jnp.where(low_lanes, 0.0, v_hi).astype(BF16)

    qi = lax.broadcasted_iota(jnp.int32, (WINDOW, 2 * WINDOW), 0)
    ci = lax.broadcasted_iota(jnp.int32, (WINDOW, 2 * WINDOW), 1)
    band = (ci > qi) & (ci <= qi + WINDOW)
    for j in range(tq // WINDOW):
        r0 = j * WINDOW
        mask = band if j > 0 else band & ((ci >= WINDOW) | (s > 0))
        for grp in range(attn_w // LANES):
            q_grp = q_s[r0:r0 + WINDOW, grp * LANES:(grp + 1) * LANES]
            out = None
            for r in range(heads_per_group):
                hd = grp * heads_per_group + r
                src = 2 * (hd // q_per_kv) + r
                sink = sinks_ref[hd]
                sc = jnp.dot(q_grp, kt_s[src, :, r0:r0 + 2 * WINDOW], preferred_element_type=F32)
                sc = jnp.where(mask, sc, NEG_INF)
                m = jnp.maximum(jnp.max(sc, axis=-1, keepdims=True), sink)
                e = jnp.exp(sc - m)
                den = jnp.sum(e, axis=-1, keepdims=True) + jnp.exp(sink - m)
                o_h = jnp.dot(e.astype(BF16), vm_s[src, r0:r0 + 2 * WINDOW, :],
                              preferred_element_type=F32) * (1.0 / den)
                out = o_h if out is None else out + o_h
            o_s[r0:r0 + WINDOW, grp * LANES:(grp + 1) * LANES] = out

    o3 = attn_w + 2 * kv_w
    gated = proj(o3 + 2 * conv_ch, conv_ch) * proj(o3, conv_ch)
    row = lax.broadcasted_iota(jnp.int32, (tq, 1), 0)
    conv_o = _short_conv(gated, gc_s[0:1, :], gc_s[1:2, :], row, convw_ref[...], proj(o3 + conv_ch, conv_ch))
    gc_s[0:CONV_K - 1, :] = gated[tq - (CONV_K - 1):, :]

    @pl.when(s == last)
    def _():
        co_ref[...] = gated[tq - (CONV_K - 1):, :]

    h_ref[...] = _merge_norm(x, o_s[...], conv_o, gattn_ref[...], gconv_ref[...], wout_ref,
                             lng_ref[...], lnb_ref[...], alpha)


def _prompt_mixer(x, win_b, tabs, sinks, conv_w, g_attn, g_conv, wout_b, ln_g, ln_b, *, alpha):
    bsz, seq, d = x.shape
    tq = PROMPT_TILE
    ns = seq // tq
    attn_w = g_attn.shape[-1]
    conv_ch = g_conv.shape[-1]
    n_heads = attn_w // HEAD_DIM
    kv_w = N_KV_HEADS * HEAD_DIM
    assert kv_w == LANES and 2 * HEAD_DIM == LANES and (n_heads // N_KV_HEADS) % 2 == 0 and WINDOW == LANES
    in_cols = win_b.shape[-1]
    const2 = lambda b, s: (0, 0)
    kern = functools.partial(_prompt_mixer_kernel, alpha=alpha, n_heads=n_heads, attn_w=attn_w, kv_w=kv_w,
                             conv_ch=conv_ch)
    return pl.pallas_call(
        kern,
        grid=(bsz, ns),
        in_specs=[
            pl.BlockSpec((None, tq, d), lambda b, s: (b, s, 0)),
            pl.BlockSpec((d, in_cols), const2),
            pl.BlockSpec((tq, LANES), lambda b, s: (s, 0)),
            pl.BlockSpec((tq, LANES), lambda b, s: (s, 0)),
            pl.BlockSpec((tq, LANES), lambda b, s: (s, 0)),
            pl.BlockSpec(memory_space=pltpu.SMEM),
            pl.BlockSpec((CONV_K, conv_ch), const2),
            pl.BlockSpec((1, attn_w), const2),
            pl.BlockSpec((1, conv_ch), const2),
            pl.BlockSpec((attn_w + conv_ch, d), const2),
            pl.BlockSpec((1, d), const2),
            pl.BlockSpec((1, d), const2),
        ],
        out_specs=[
            pl.BlockSpec((tq, d), lambda b, s: (b * ns + s, 0)),
            pl.BlockSpec((None, WINDOW, kv_w), lambda b, s: (b, 0, 0)),
            pl.BlockSpec((None, WINDOW, kv_w), lambda b, s: (b, 0, 0)),
            pl.BlockSpec((None, CONV_K - 1, conv_ch), lambda b, s: (b, 0, 0)),
        ],
        out_shape=[
            jax.ShapeDtypeStruct((bsz * seq, d), F32),
            jax.ShapeDtypeStruct((bsz, WINDOW, kv_w), F32),
            jax.ShapeDtypeStruct((bsz, WINDOW, kv_w), F32),
            jax.ShapeDtypeStruct((bsz, CONV_K - 1, conv_ch), F32),
        ],
        scratch_shapes=[
            pltpu.VMEM((tq, attn_w), BF16),
            pltpu.VMEM((2 * N_KV_HEADS, LANES, WINDOW + tq), BF16),
            pltpu.VMEM((2 * N_KV_HEADS, WINDOW + tq, LANES), BF16),
            pltpu.VMEM((tq, attn_w), F32),
            pltpu.VMEM((SUBLANES, conv_ch), F32),
        ],
        compiler_params=pltpu.CompilerParams(dimension_semantics=("arbitrary", "arbitrary"),
                                             vmem_limit_bytes=VMEM_LIMIT_BYTES),
        name="prompt_mixer",
    )(x, win_b, *tabs, sinks, conv_w, g_attn, g_conv, wout_b, ln_g, ln_b)


def _sample_mixer_kernel(x_ref, ck_ref, cv_ref, st_ref, win_ref, c_ref, sa_ref, sb_ref, sinks_ref,
                         convw_ref, gattn_ref, gconv_ref, wout_ref, lng_ref, lnb_ref,
                         h_ref, ko_ref, vo_ref, co_ref, *, alpha, n_heads, attn_w, kv_w, conv_ch, dec_seq):
    nb, win = ck_ref.shape[0], ck_ref.shape[1]
    rows = nb * dec_seq
    q_per_kv = n_heads // N_KV_HEADS
    x = x_ref[...]
    xb = x.astype(BF16)
    c, sa, sb = c_ref[...], sa_ref[...], sb_ref[...]

    def proj(lo, width):
        return jnp.dot(xb, win_ref[:, lo:lo + width], preferred_element_type=F32)

    scale = HEAD_DIM ** -0.5
    k = _rope(proj(attn_w, kv_w), c, sa, sb)
    v = proj(attn_w + kv_w, kv_w)
    k3 = k.reshape(nb, dec_seq, kv_w)
    v3 = v.reshape(nb, dec_seq, kv_w)
    ck = ck_ref[...]
    cv = cv_ref[...]
    ko_ref[:, 0:win - dec_seq, :] = ck[:, dec_seq:, :]
    ko_ref[:, win - dec_seq:, :] = k3
    vo_ref[:, 0:win - dec_seq, :] = cv[:, dec_seq:, :]
    vo_ref[:, win - dec_seq:, :] = v3
    ckb, cvb, k3b, v3b = ck.astype(BF16), cv.astype(BF16), k3.astype(BF16), v3.astype(BF16)

    qrows = q_per_kv * dec_seq
    qi = lax.broadcasted_iota(jnp.int32, (nb, qrows, win), 1) % dec_seq
    mask_c = lax.broadcasted_iota(jnp.int32, (nb, qrows, win), 2) > qi + (win - WINDOW)
    qn = lax.broadcasted_iota(jnp.int32, (nb, qrows, dec_seq), 1) % dec_seq
    mask_n = lax.broadcasted_iota(jnp.int32, (nb, qrows, dec_seq), 2) <= qn
    sink_row = lax.broadcasted_iota(jnp.int32, (nb, qrows, 1), 1) // dec_seq

    q_chunks = [_rope(proj(j * LANES, LANES), c, sa, sb) * scale for j in range(attn_w // LANES)]
    heads_out = []
    for kvh in range(N_KV_HEADS):
        qs = []
        for g in range(q_per_kv):
            lo = (kvh * q_per_kv + g) * HEAD_DIM
            qh = q_chunks[lo // LANES][:, lo % LANES:lo % LANES + HEAD_DIM]
            qs.append(qh.reshape(nb, dec_seq, HEAD_DIM))
        qg = jnp.concatenate(qs, axis=1).astype(BF16)
        sl = slice(kvh * HEAD_DIM, (kvh + 1) * HEAD_DIM)
        sc_c = jnp.einsum("bqd,bkd->bqk", qg, ckb[:, :, sl], preferred_element_type=F32)
        sc_n = jnp.einsum("bqd,bkd->bqk", qg, k3b[:, :, sl], preferred_element_type=F32)
        sc_c = jnp.where(mask_c, sc_c, NEG_INF)
        sc_n = jnp.where(mask_n, sc_n, NEG_INF)
        sink = jnp.zeros((nb, qrows, 1), F32)
        for g in range(q_per_kv):
            sink = jnp.where(sink_row == g, sinks_ref[kvh * q_per_kv + g], sink)
        p_c, p_n = _sink_softmax([sc_c, sc_n], sink)
        og = (jnp.einsum("bqk,bkd->bqd", p_c, cvb[:, :, sl], preferred_element_type=F32)
              + jnp.einsum("bqk,bkd->bqd", p_n, v3b[:, :, sl], preferred_element_type=F32))
        for g in range(q_per_kv):
            heads_out.append(og[:, g * dec_seq:(g + 1) * dec_seq, :].reshape(rows, HEAD_DIM))
    attn_o = jnp.concatenate(heads_out, axis=-1)

    o3 = attn_w + 2 * kv_w
    gated = proj(o3 + 2 * conv_ch, conv_ch) * proj(o3, conv_ch)
    st = st_ref[...]
    prev2 = jnp.broadcast_to(st[:, 0:1, :], (nb, dec_seq, conv_ch)).reshape(rows, conv_ch)
    prev1 = jnp.broadcast_to(st[:, 1:2, :], (nb, dec_seq, conv_ch)).reshape(rows, conv_ch)
    row = lax.broadcasted_iota(jnp.int32, (rows, 1), 0) % dec_seq
    conv_o = _short_conv(gated, prev2, prev1, row, convw_ref[...], proj(o3 + conv_ch, conv_ch))
    co_ref[...] = gated.reshape(nb, dec_seq, conv_ch)[:, dec_seq - (CONV_K - 1):, :]

    h_ref[...] = _merge_norm(x, attn_o, conv_o, gattn_ref[...], gconv_ref[...], wout_ref,
                             lng_ref[...], lnb_ref[...], alpha)


def _sample_mixer(x, ck, cv, st, win_b, tabs, sinks, conv_w, g_attn, g_conv, wout_b, ln_g, ln_b, *, alpha):
    dec_b, dec_seq, d = x.shape
    assert dec_seq >= CONV_K - 1 and dec_seq % SUBLANES == 0
    nb = SAMPLE_SEQS
    rows = nb * dec_seq
    win = ck.shape[1]
    attn_w = g_attn.shape[-1]
    conv_ch = g_conv.shape[-1]
    n_heads = attn_w // HEAD_DIM
    kv_w = N_KV_HEADS * HEAD_DIM
    in_cols = win_b.shape[-1]
    const2 = lambda i: (0, 0)
    kern = functools.partial(_sample_mixer_kernel, alpha=alpha, n_heads=n_heads, attn_w=attn_w, kv_w=kv_w,
                             conv_ch=conv_ch, dec_seq=dec_seq)
    return pl.pallas_call(
        kern,
        grid=(dec_b // nb,),
        in_specs=[
            pl.BlockSpec((rows, d), lambda i: (i, 0)),
            pl.BlockSpec((nb, win, kv_w), lambda i: (i, 0, 0)),
            pl.BlockSpec((nb, win, kv_w), lambda i: (i, 0, 0)),
            pl.BlockSpec((nb, CONV_K - 1, conv_ch), lambda i: (i, 0, 0)),
            pl.BlockSpec((d, in_cols), const2),
            pl.BlockSpec((rows, LANES), const2),
            pl.BlockSpec((rows, LANES), const2),
            pl.BlockSpec((rows, LANES), const2),
            pl.BlockSpec(memory_space=pltpu.SMEM),
            pl.BlockSpec((CONV_K, conv_ch), const2),
            pl.BlockSpec((1, attn_w), const2),
            pl.BlockSpec((1, conv_ch), const2),
            pl.BlockSpec((attn_w + conv_ch, d), const2),
            pl.BlockSpec((1, d), const2),
            pl.BlockSpec((1, d), const2),
        ],
        out_specs=[
            pl.BlockSpec((rows, d), lambda i: (i, 0)),
            pl.BlockSpec((nb, win, kv_w), lambda i: (i, 0, 0)),
            pl.BlockSpec((nb, win, kv_w), lambda i: (i, 0, 0)),
            pl.BlockSpec((nb, CONV_K - 1, conv_ch), lambda i: (i, 0, 0)),
        ],
        out_shape=[
            jax.ShapeDtypeStruct((dec_b * dec_seq, d), F32),
            jax.ShapeDtypeStruct((dec_b, win, kv_w), F32),
            jax.ShapeDtypeStruct((dec_b, win, kv_w), F32),
            jax.ShapeDtypeStruct((dec_b, CONV_K - 1, conv_ch), F32),
        ],
        compiler_params=pltpu.CompilerParams(dimension_semantics=("arbitrary",),
                                             vmem_limit_bytes=VMEM_LIMIT_BYTES),
        name="sample_mixer",
    )(x.reshape(dec_b * dec_seq, d), ck, cv, st, win_b, *tabs, sinks, conv_w, g_attn, g_conv, wout_b,
      ln_g, ln_b)


def _over_experts(fn, x):
    return fn(fn(x, axis=0, keepdims=True), axis=1, keepdims=True)


def _two_group_specs(tm, d, n_prompt_tiles):
    return [pl.BlockSpec((tm, d), lambda i, *_: (jnp.minimum(i, n_prompt_tiles - 1), 0)),
            pl.BlockSpec((tm, d), lambda i, *_: (jnp.maximum(i - n_prompt_tiles, 0), 0))]


def _route_kernel(hp_ref, hs_ref, rwt_ref, bias_ref, eidx_ref, rank_ref, gate_ref, cnt_ref, earlier_s, *,
                  n_experts, n_prompt_tiles):
    i = pl.program_id(0)
    tm = hp_ref.shape[0]
    per_group = n_experts // N_EXPERT_GROUPS
    shape3 = (N_EXPERT_GROUPS, per_group, tm)

    @pl.when(i == 0)
    def _():
        t_from = lax.broadcasted_iota(jnp.int32, (tm, tm), 0)
        t_to = lax.broadcasted_iota(jnp.int32, (tm, tm), 1)
        earlier_s[...] = ((t_from < t_to) & (t_from // MOE_TILE == t_to // MOE_TILE)).astype(BF16)

    h = jnp.where(i < n_prompt_tiles, hp_ref[...], hs_ref[...])
    logits = lax.dot_general(rwt_ref[...], h.astype(BF16), (((1,), (1,)), ((), ())),
                             preferred_element_type=F32)
    scores = jax.nn.sigmoid(logits)
    sel = scores + bias_ref[...]
    scores3 = scores.reshape(shape3)
    grp = sel.reshape(shape3)
    member = lax.broadcasted_iota(jnp.int32, shape3, 1).astype(F32)
    group = lax.broadcasted_iota(jnp.int32, shape3, 0).astype(F32)
    expert = group * per_group + member

    m1 = jnp.max(grp, axis=1, keepdims=True)
    f1 = jnp.min(jnp.where(grp == m1, member, float(per_group)), axis=1, keepdims=True)
    m2 = jnp.max(jnp.where(member == f1, NEG_INF, grp), axis=1, keepdims=True)
    gscore = m1 + m2

    gid = lax.broadcasted_iota(jnp.int32, gscore.shape, 0).astype(F32)
    gmask = jnp.zeros(gscore.shape, F32)
    cur = gscore
    for _ in range(TOPK_GROUPS):
        mx = jnp.max(cur, axis=0, keepdims=True)
        pick = gid == jnp.min(jnp.where(cur == mx, gid, float(N_EXPERT_GROUPS)), axis=0, keepdims=True)
        gmask = jnp.where(pick, 1.0, gmask)
        cur = jnp.where(pick, NEG_INF, cur)

    cand = jnp.where(gmask > 0.0, grp, NEG_INF)
    chosen = jnp.zeros(shape3, F32)
    picks, firsts, weights = [], [], []
    for _ in range(TOP_K):
        mx = _over_experts(jnp.max, cand)
        first = _over_experts(jnp.min, jnp.where(cand == mx, expert, float(n_experts)))
        pick = expert == first
        picks.append(pick)
        firsts.append(first)
        weights.append(_over_experts(jnp.sum, jnp.where(pick, scores3, 0.0)))
        chosen = jnp.where(pick, 1.0, chosen)
        cand = jnp.where(pick, NEG_INF, cand)
    wsum = weights[0]
    for w in weights[1:]:
        wsum = wsum + w

    chosen_b = chosen.reshape(n_experts, tm).astype(BF16)
    before3 = jnp.dot(chosen_b, earlier_s[...], preferred_element_type=F32).reshape(shape3)

    pad = SUBLANES - TOP_K
    eidx = [f.reshape(1, tm).astype(jnp.int32) for f in firsts]
    rank = [_over_experts(jnp.sum, jnp.where(p, before3, 0.0)).reshape(1, tm).astype(jnp.int32) for p in picks]
    gate = [(w / wsum * ROUTED_SCALE).reshape(1, tm) for w in weights]
    eidx_ref[...] = jnp.concatenate(eidx + [jnp.zeros((pad, tm), jnp.int32)], axis=0)
    rank_ref[...] = jnp.concatenate(rank + [jnp.zeros((pad, tm), jnp.int32)], axis=0)
    gate_ref[...] = jnp.concatenate(gate + [jnp.zeros((pad, tm), F32)], axis=0)

    for sub in range(tm // MOE_TILE):
        cnt_ref[sub * SUBLANES:(sub + 1) * SUBLANES, :] = lax.dot_general(
            jnp.ones((SUBLANES, MOE_TILE), BF16), chosen_b[:, sub * MOE_TILE:(sub + 1) * MOE_TILE],
            (((1,), (1,)), ((), ())), preferred_element_type=F32)


def _route(h_p, h_s, rwt_b, bias_col):
    d = h_p.shape[1]
    t_all = h_p.shape[0] + h_s.shape[0]
    n_experts = rwt_b.shape[0]
    tm = ROUTE_TILE
    npt = h_p.shape[0] // tm
    moe_tiles = tm // MOE_TILE
    row_spec = pl.BlockSpec((SUBLANES, tm), lambda i: (0, i))
    return pl.pallas_call(
        functools.partial(_route_kernel, n_experts=n_experts, n_prompt_tiles=npt),
        grid=(t_all // tm,),
        in_specs=_two_group_specs(tm, d, npt) + [
            pl.BlockSpec((n_experts, d), lambda i: (0, 0)),
            pl.BlockSpec((n_experts, 1), lambda i: (0, 0)),
        ],
        out_specs=[row_spec, row_spec, row_spec,
                   pl.BlockSpec((moe_tiles * SUBLANES, n_experts), lambda i: (i, 0))],
        out_shape=[
            jax.ShapeDtypeStruct((SUBLANES, t_all), jnp.int32),
            jax.ShapeDtypeStruct((SUBLANES, t_all), jnp.int32),
            jax.ShapeDtypeStruct((SUBLANES, t_all), F32),
            jax.ShapeDtypeStruct((t_all // MOE_TILE * SUBLANES, n_experts), F32),
        ],
        scratch_shapes=[pltpu.VMEM((tm, tm), BF16)],
        compiler_params=pltpu.CompilerParams(dimension_semantics=("arbitrary",),
                                             vmem_limit_bytes=VMEM_LIMIT_BYTES),
        name="route",
    )(h_p, h_s, rwt_b, bias_col)


def _for_each_part(groups, max_part, fn):
    for size in [1 << b for b in range(max_part.bit_length())]:
        @pl.when((groups & size) != 0)
        def _(size=size):
            fn(groups & (size - 1), size)


def _start_run_copies(tile, cnt_ref, ls_ref, gs_ref, big_ref, n_experts, start_copy):
    def run(e):
        idx = tile * n_experts + e
        return cnt_ref[idx], ls_ref[idx], gs_ref[idx]

    for e in range(n_experts):
        groups, l0, g0 = run(e)
        _for_each_part(groups, RUN_CHUNK // 2, lambda off, size: start_copy(l0 + off, g0 + off, size, e % 2))

    @pl.when(big_ref[tile] != 0)
    def _():
        def per_expert(e, c):
            groups, l0, g0 = run(e)
            base = groups & (RUN_CHUNK - 1)

            def chunk(j, c2):
                off = base + j * RUN_CHUNK
                start_copy(l0 + off, g0 + off, RUN_CHUNK, 0)
                return c2

            lax.fori_loop(0, lax.shift_right_logical(groups, RUN_CHUNK.bit_length() - 1), chunk, 0)
            return c

        lax.fori_loop(0, n_experts, per_expert, 0)


def _top_bit(n):
    return 1 << (n.bit_length() - 1)


def _pack_rows(x):
    rows, two_w = x.shape
    w = two_w // 2
    x3 = x.reshape(rows // SUBLANES, SUBLANES, two_w)
    halves = jnp.concatenate([x3[:, :, :w], x3[:, :, w:]], axis=1).astype(BF16)
    return pltpu.bitcast(halves, jnp.uint32)


def _unpack_rows(p):
    groups, _, w = p.shape
    halves = pltpu.bitcast(p, BF16).astype(F32)
    return jnp.concatenate([halves[:, :SUBLANES, :], halves[:, SUBLANES:, :]], axis=-1).reshape(
        groups * SUBLANES, 2 * w)


def _local_groups(tm, n_experts):
    n = TOP_K * tm + n_experts * (SUBLANES - 1)
    return -(-n // SORT_CHUNK) * SORT_CHUNK // SUBLANES


def _dispatch_kernel(cnt_ref, ls_ref, gs_ref, tot_ref, big_ref, zstart_ref, zlen_ref, tail_ref, lp_ref, hp_ref,
                     hs_ref, xs_hbm, lbuf, zero_s, sem, zero_sem, *, n_experts, n_prompt_tiles):
    i = pl.program_id(0)
    tm, d = hp_ref.shape
    n = TOP_K * tm // SUBLANES
    chunk = SORT_CHUNK // SUBLANES
    slot = i % 2

    def tile_wait(tile):
        pltpu.make_async_copy(lbuf.at[0, pl.ds(0, n)], xs_hbm.at[pl.ds(0, n)], sem).wait()
        _for_each_part(tot_ref[tile] - n, _top_bit(n_experts * (SUBLANES - 1) // SUBLANES), lambda off, size:
                       pltpu.make_async_copy(lbuf.at[0, pl.ds(0, size)], xs_hbm.at[pl.ds(0, size)], sem).wait())

    def clear_padding(act):
        def per_expert(e, c):
            _for_each_part(zlen_ref[e], zero_s.shape[0], lambda off, size: act(pltpu.make_async_copy(
                zero_s.at[pl.ds(0, size)], xs_hbm.at[pl.ds(zstart_ref[e] + off, size)], zero_sem)))
            return c

        lax.fori_loop(0, n_experts, per_expert, 0)

    piece = zero_s.shape[0]
    n_pieces = (xs_hbm.shape[0] - tail_ref[0]) // piece
    per_step = (n_pieces + pl.num_programs(0) - 1) // pl.num_programs(0)

    def tail_piece(k):
        return pltpu.make_async_copy(zero_s, xs_hbm.at[pl.ds(pl.multiple_of(tail_ref[0] + k * piece, piece), piece)],
                                     zero_sem)

    @pl.when(i == 0)
    def _():
        zero_s[...] = jnp.zeros(zero_s.shape, zero_s.dtype)
        clear_padding(lambda copy: copy.start())

    def start_tail(j, c):
        k = i * per_step + j

        @pl.when(k < n_pieces)
        def _():
            tail_piece(k).start()
        return c

    lax.fori_loop(0, per_step, start_tail, 0)

    hb = jnp.where(i < n_prompt_tiles, hp_ref[...], hs_ref[...]).astype(BF16)
    lp = lp_ref[...]
    def sort_chunk(r):
        rows = r * SORT_CHUNK + lax.broadcasted_iota(jnp.int32, (SORT_CHUNK, tm), 0)
        hit = jnp.zeros((SORT_CHUNK, tm), F32)
        for k in range(TOP_K):
            hit = jnp.where(rows == lp[k:k + 1, :], 1.0, hit)
        lbuf[slot, pl.ds(r * chunk, chunk)] = _pack_rows(jnp.dot(hit.astype(BF16), hb, preferred_element_type=F32))

    last_chunk = lbuf.shape[1] // chunk - 1
    for r in range(last_chunk):
        sort_chunk(r)
    pl.when(tot_ref[i] > last_chunk * chunk)(functools.partial(sort_chunk, last_chunk))

    @pl.when(i > 0)
    def _():
        tile_wait(i - 1)

    _start_run_copies(i, cnt_ref, ls_ref, gs_ref, big_ref, n_experts,
                      lambda lgrp, ggrp, size, prio: pltpu.make_async_copy(
                          lbuf.at[slot, pl.ds(lgrp, size)], xs_hbm.at[pl.ds(ggrp, size)], sem).start(priority=prio))

    @pl.when(i == pl.num_programs(0) - 1)
    def _():
        tile_wait(i)
        clear_padding(lambda copy: copy.wait())

        def wait_tail(k, c):
            tail_piece(k).wait()
            return c

        lax.fori_loop(0, n_pieces, wait_tail, 0)


def _dispatch(tables, zstart, zlen, tail, lp, h_p, h_s, *, n_rows, blk):
    d = h_p.shape[1]
    t_all = h_p.shape[0] + h_s.shape[0]
    tm = MOE_TILE
    npt = h_p.shape[0] // tm
    n_experts = zstart.shape[0]
    return pl.pallas_call(
        functools.partial(_dispatch_kernel, n_experts=n_experts, n_prompt_tiles=npt),
        grid_spec=pltpu.PrefetchScalarGridSpec(
            num_scalar_prefetch=8,
            grid=(t_all // tm,),
            in_specs=[pl.BlockSpec((SUBLANES, tm), lambda i, *_: (0, i))] + _two_group_specs(tm, d, npt),
            out_specs=pl.BlockSpec(memory_space=pl.ANY),
            scratch_shapes=[
                pltpu.VMEM((2, _local_groups(tm, n_experts), SUBLANES, d // 2), jnp.uint32),
                pltpu.VMEM((_top_bit(blk // SUBLANES - 1), SUBLANES, d // 2), jnp.uint32),
                pltpu.SemaphoreType.DMA,
                pltpu.SemaphoreType.DMA,
            ],
        ),
        out_shape=jax.ShapeDtypeStruct((n_rows // SUBLANES, SUBLANES, d // 2), jnp.uint32),
        compiler_params=pltpu.CompilerParams(dimension_semantics=("arbitrary",),
                                             vmem_limit_bytes=VMEM_LIMIT_BYTES),
        name="dispatch",
    )(*tables, zstart, zlen, tail, lp, h_p, h_s)


def _silu(x):
    return x * jax.nn.sigmoid(x)


def _expert_kernel(be_ref, nact_ref, xs_ref, wg_ref, wu_ref, wd_ref, ys_ref, wg_s, wu_s, wd_s):
    b = pl.program_id(0)

    @pl.when(b < nact_ref[0])
    def _():
        @pl.when((b == 0) | (be_ref[b] != be_ref[jnp.maximum(b - 1, 0)]))
        def _():
            wg_s[...] = wg_ref[...].astype(BF16)
            wu_s[...] = wu_ref[...].astype(BF16)
            wd_s[...] = wd_ref[...].astype(BF16)

        xb = _unpack_rows(xs_ref[...]).astype(BF16)
        hid = _silu(jnp.dot(xb, wg_s[...], preferred_element_type=F32)) * jnp.dot(
            xb, wu_s[...], preferred_element_type=F32)
        ys_ref[...] = _pack_rows(jnp.dot(hid.astype(BF16), wd_s[...], preferred_element_type=F32))


def _experts(block_expert, nact, xs, w_gate, w_up, w_down, *, blk):
    d = w_gate.shape[-2]
    ff = w_gate.shape[-1]
    block = (blk // SUBLANES,) + xs.shape[1:]
    n_blocks = xs.shape[0] // block[0]

    def active(b, be, na):
        return jnp.minimum(b, na[0] - 1)

    return pl.pallas_call(
        _expert_kernel,
        grid_spec=pltpu.PrefetchScalarGridSpec(
            num_scalar_prefetch=2,
            grid=(n_blocks,),
            in_specs=[
                pl.BlockSpec(block, lambda b, be, na: (active(b, be, na), 0, 0)),
                pl.BlockSpec((None, d, ff), lambda b, be, na: (be[active(b, be, na)], 0, 0)),
                pl.BlockSpec((None, d, ff), lambda b, be, na: (be[active(b, be, na)], 0, 0)),
                pl.BlockSpec((None, ff, d), lambda b, be, na: (be[active(b, be, na)], 0, 0)),
            ],
            out_specs=pl.BlockSpec(block, lambda b, be, na: (active(b, be, na), 0, 0)),
            scratch_shapes=[
                pltpu.VMEM((d, ff), BF16),
                pltpu.VMEM((d, ff), BF16),
                pltpu.VMEM((ff, d), BF16),
            ],
        ),
        out_shape=jax.ShapeDtypeStruct(xs.shape, jnp.uint32),
        input_output_aliases={2: 0},
        compiler_params=pltpu.CompilerParams(dimension_semantics=("arbitrary",),
                                             vmem_limit_bytes=VMEM_LIMIT_BYTES),
        name="experts",
    )(block_expert, nact, xs, w_gate, w_up, w_down)


def _combine_kernel(cnt_ref, ls_ref, gs_ref, tot_ref, big_ref, ys_hbm, lpt_ref, gate_ref, hp_ref, hs_ref, wsg_ref, wsu_ref,
                    wsd_ref, lng_ref, lnb_ref, yp_ref, ysm_ref, ybuf, moe_s, sems, *, alpha, n_experts,
                    n_prompt_tiles):
    i = pl.program_id(0)
    tm, d = hp_ref.shape
    n = TOP_K * tm // SUBLANES
    chunk = SORT_CHUNK // SUBLANES
    slot = i % 2

    def fetch(tile, to):
        _start_run_copies(tile, cnt_ref, ls_ref, gs_ref, big_ref, n_experts, lambda lgrp, ggrp, size, prio:
                          pltpu.make_async_copy(ys_hbm.at[pl.ds(ggrp, size)], ybuf.at[to, pl.ds(lgrp, size)],
                                                sems.at[to]).start(priority=prio))

    @pl.when(i == 0)
    def _():
        ybuf[...] = jnp.zeros(ybuf.shape, ybuf.dtype)
        fetch(0, 0)

    @pl.when(i + 1 < pl.num_programs(0))
    def _():
        fetch(i + 1, 1 - slot)

    h = jnp.where(i < n_prompt_tiles, hp_ref[...], hs_ref[...])
    hb = h.astype(BF16)
    hid = _silu(jnp.dot(hb, wsg_ref[...], preferred_element_type=F32)) * jnp.dot(
        hb, wsu_ref[...], preferred_element_type=F32)
    shared = jnp.dot(hid.astype(BF16), wsd_ref[...], preferred_element_type=F32)

    pltpu.make_async_copy(ys_hbm.at[pl.ds(0, n)], ybuf.at[slot, pl.ds(0, n)], sems.at[slot]).wait()
    _for_each_part(tot_ref[i] - n, _top_bit(n_experts * (SUBLANES - 1) // SUBLANES), lambda off, size:
                   pltpu.make_async_copy(ys_hbm.at[pl.ds(0, size)], ybuf.at[slot, pl.ds(0, size)],
                                         sems.at[slot]).wait())

    lpt = lpt_ref[...]
    gate = gate_ref[...]

    def chunk_sum(r):
        cols = r * SORT_CHUNK + lax.broadcasted_iota(jnp.int32, (tm, SORT_CHUNK), 1)
        g = jnp.zeros((tm, SORT_CHUNK), F32)
        for k in range(TOP_K):
            g = jnp.where(cols == lpt[:, k:k + 1], gate[:, k:k + 1], g)
        yb = _unpack_rows(ybuf[slot, pl.ds(r * chunk, chunk)]).astype(BF16)
        return jnp.dot(g.astype(BF16), yb, preferred_element_type=F32)

    last_chunk = ybuf.shape[1] // chunk - 1
    moe = shared
    for r in range(last_chunk):
        moe = moe + chunk_sum(r)
    moe_s[...] = moe

    @pl.when(tot_ref[i] > last_chunk * chunk)
    def _():
        moe_s[...] += chunk_sum(last_chunk)

    y = _layer_norm(alpha * h + moe_s[...], lng_ref[...], lnb_ref[...])

    @pl.when(i < n_prompt_tiles)
    def _():
        yp_ref[...] = y

    @pl.when(i >= n_prompt_tiles)
    def _():
        ysm_ref[...] = y


def _combine(tables, ys, lp_t, gates_t, h_p, h_s, wsg_b, wsu_b, wsd_b, ln_g, ln_b, *, alpha):
    t_prompt, d = h_p.shape
    t_all = t_prompt + h_s.shape[0]
    tm = MOE_TILE
    ff = wsg_b.shape[-1]
    npt = t_prompt // tm
    n_experts = tables[0].shape[0] // (t_all // tm)
    const2 = lambda i, *_: (0, 0)
    return pl.pallas_call(
        functools.partial(_combine_kernel, alpha=alpha, n_experts=n_experts, n_prompt_tiles=npt),
        grid_spec=pltpu.PrefetchScalarGridSpec(
            num_scalar_prefetch=5,
            grid=(t_all // tm,),
            in_specs=[
                pl.BlockSpec(memory_space=pl.ANY),
                pl.BlockSpec((tm, SUBLANES), lambda i, *_: (i, 0)),
                pl.BlockSpec((tm, SUBLANES), lambda i, *_: (i, 0)),
                *_two_group_specs(tm, d, npt),
                pl.BlockSpec((d, ff), const2),
                pl.BlockSpec((d, ff), const2),
                pl.BlockSpec((ff, d), const2),
                pl.BlockSpec((1, d), const2),
                pl.BlockSpec((1, d), const2),
            ],
            out_specs=[
                pl.BlockSpec((tm, d), lambda i, *_: (jnp.minimum(i, npt - 1), 0)),
                pl.BlockSpec((tm, d), lambda i, *_: (jnp.maximum(i - npt, 0), 0)),
            ],
            scratch_shapes=[
                pltpu.VMEM((2, _local_groups(tm, n_experts), SUBLANES, d // 2), jnp.uint32),
                pltpu.VMEM((tm, d), F32),
                pltpu.SemaphoreType.DMA((2,)),
            ],
        ),
        out_shape=[
            jax.ShapeDtypeStruct((t_prompt, d), F32),
            jax.ShapeDtypeStruct((t_all - t_prompt, d), F32),
        ],
        compiler_params=pltpu.CompilerParams(dimension_semantics=("arbitrary",),
                                             vmem_limit_bytes=VMEM_LIMIT_BYTES),
        name="combine",
    )(*tables, ys, lp_t, gates_t, h_p, h_s, wsg_b, wsu_b, wsd_b, ln_g, ln_b)


def _moe(h_p, h_s, router_w, router_bias, w_gate, w_up, w_down, ws_gate, ws_up, ws_down, ln_g, ln_b, *, alpha):
    t_all = h_p.shape[0] + h_s.shape[0]
    n_experts = router_w.shape[-1]
    blk = EXPERT_BLOCK
    tm = MOE_TILE
    nt = t_all // tm
    eidx8, lrank8, gate8, cnt8 = _route(h_p, h_s, router_w.T.astype(BF16), router_bias.reshape(n_experts, 1))

    bgrp = blk // SUBLANES
    cnt = cnt8.reshape(nt, SUBLANES, n_experts)[:, 0, :].astype(jnp.int32)
    cnt = (cnt + SUBLANES - 1) // SUBLANES
    counts = jnp.sum(cnt, axis=0)
    blocks_per_e = (counts + bgrp - 1) // bgrp
    block_end = jnp.cumsum(blocks_per_e)
    pad_start = (block_end - blocks_per_e) * bgrp
    n_blocks = -(-(t_all * TOP_K + nt * n_experts * (SUBLANES - 1)) // blk) + n_experts
    block_expert = jnp.minimum(jnp.sum(block_end[None, :] <= jnp.arange(n_blocks)[:, None], axis=1),
                               n_experts - 1).astype(jnp.int32)
    nact = block_end[-1:].astype(jnp.int32)
    zstart = (pad_start + counts).astype(jnp.int32)
    zlen = (blocks_per_e * bgrp - counts).astype(jnp.int32)
    gstart = pad_start[None, :] + jnp.cumsum(cnt, axis=0) - cnt
    lstart = jnp.cumsum(cnt, axis=1) - cnt
    tables = tuple(a.reshape(-1).astype(jnp.int32) for a in (
        cnt, lstart, gstart, jnp.sum(cnt, axis=1), jnp.max(cnt, axis=1) >= RUN_CHUNK))
    lstart_tok = jnp.repeat(lstart * SUBLANES, tm, axis=0)
    lp8 = jnp.sum(jnp.where(eidx8[..., None] == jnp.arange(n_experts), lstart_tok[None], 0), axis=-1) + lrank8
    lp8 = lp8.astype(jnp.int32)

    xs = _dispatch(tables, zstart, zlen, nact * bgrp, lp8, h_p, h_s, n_rows=n_blocks * blk, blk=blk)
    ys = _experts(block_expert, nact, xs, w_gate, w_up, w_down, blk=blk)
    return _combine(tables, ys, lp8.T, gate8.T, h_p, h_s, ws_gate.astype(BF16), ws_up.astype(BF16), ws_down.astype(BF16),
                    ln_g, ln_b, alpha=alpha)


def kernel(x_prompt, x_sample, cache_k, cache_v, state_conv, w_in, attn_sinks, conv_w, g_attn_out, g_conv_out, w_out, ln1_g, ln1_b, router_w, router_bias, w_gate, w_up, w_down, ws_gate, ws_up, ws_down, ln2_g, ln2_b):
    depth = w_in.shape[0]
    bsz, seq, d = x_prompt.shape
    dec_b, dec_seq, _ = x_sample.shape
    win = cache_k.shape[2]
    kv_w = N_KV_HEADS * HEAD_DIM
    t_prompt = bsz * seq
    t_all = t_prompt + dec_b * dec_seq
    alpha = (2.0 * depth) ** 0.25
    assert win == WINDOW and seq % PROMPT_TILE == 0 and dec_b % SAMPLE_SEQS == 0
    assert t_prompt % ROUTE_TILE == 0 and (t_all - t_prompt) % ROUTE_TILE == 0 and ROUTE_TILE % MOE_TILE == 0
    assert MOE_TILE & (MOE_TILE - 1) == 0 and (TOP_K * MOE_TILE) % SORT_CHUNK == 0

    tabs_p = _rope_tables(jnp.arange(seq))
    tabs_s = tuple(jnp.tile(t, (SAMPLE_SEQS, 1)) for t in _rope_tables(PAST_LEN + jnp.arange(dec_seq)))
    row = lambda a: a.reshape(1, -1)

    xp, xs = x_prompt, x_sample
    outs = [[] for _ in range(6)]
    for l in range(depth):
        win_b, wout_b = w_in[l].astype(BF16), w_out[l].astype(BF16)
        shared = (attn_sinks[l], conv_w[l], row(g_attn_out[l]), row(g_conv_out[l]), wout_b, row(ln1_g[l]),
                  row(ln1_b[l]))
        h_p, kp, vp, cp = _prompt_mixer(xp, win_b, tabs_p, *shared, alpha=alpha)
        h_s, kn, vn, cn = _sample_mixer(xs, cache_k[l].reshape(dec_b, win, kv_w),
                                        cache_v[l].reshape(dec_b, win, kv_w), state_conv[l], win_b, tabs_s,
                                        *shared, alpha=alpha)
        yp, ys = _moe(h_p, h_s, router_w[l], router_bias[l], w_gate[l], w_up[l], w_down[l], ws_gate[l], ws_up[l],
                      ws_down[l], row(ln2_g[l]), row(ln2_b[l]), alpha=alpha)
        xp, xs = yp.reshape(bsz, seq, d), ys.reshape(dec_b, dec_seq, d)
        heads = lambda a: a.reshape(a.shape[0], win, N_KV_HEADS, HEAD_DIM)
        for o, a in zip(outs, (heads(kp), heads(vp), cp, heads(kn), heads(vn), cn)):
            o.append(a)
    return (xp, xs) + tuple(jnp.stack(o, axis=0) for o in outs)
```
